```python
import math
import jax, jax.numpy as jnp
from jax import lax
import numpy as np

D_MODEL = 1024
BATCH = 4
SEQ = 4096
DEPTH = 2
DEC_BATCH = 128
DEC_SEQ = 4
PAST_LEN = 2048
PAGE_SIZE = 128

HEAD_DIM = 64
D_MIX = D_MODEL
H_ATT = (D_MIX // 2) // HEAD_DIM
H_GDN = (D_MIX // 2) // HEAD_DIM
D_ATT = H_ATT * HEAD_DIM
D_GDN = H_GDN * HEAD_DIM
D_CONV = 3 * D_GDN
D_IN = 3 * D_ATT + D_CONV + D_GDN + 2 * H_GDN
MOBA_BLOCK = 256
MOBA_TOPK = 3
MOBA_QROWS = 128
GDN_CONV = 4
GDN_CHUNK = 64
N_BUCKETS = 32
MAX_DISTANCE = 128
D_FF = 2816
N_EXPERTS = 8
TOP_K = 2
D_FF_EXPERT = 3584
ALPHA = (2 * DEPTH) ** 0.25
BETA_INIT = (8 * DEPTH) ** -0.25
LN_EPS = 1e-5

kernel_name = 'moba_gdn_hybrid_step'


def layer_norm(x, g, b):
    xf = x.astype(jnp.float32)
    mu = xf.mean(-1, keepdims=True)
    var = jnp.square(xf - mu).mean(-1, keepdims=True)
    return ((xf - mu) * lax.rsqrt(var + LN_EPS) * g + b).astype(x.dtype)


def rms_norm(x, w):
    xf = x.astype(jnp.float32)
    return (xf * lax.rsqrt(jnp.square(xf).mean(-1, keepdims=True) + 1e-6) * w).astype(x.dtype)


def l2norm(x):
    xf = x.astype(jnp.float32)
    return (xf * lax.rsqrt(jnp.square(xf).sum(-1, keepdims=True) + 1e-6)).astype(x.dtype)


def rel_bucket(dist):
    n = jnp.maximum(dist, 0)
    max_exact = N_BUCKETS // 2
    large = max_exact + (jnp.log(jnp.maximum(n, 1).astype(jnp.float32) / max_exact)
                         / math.log(MAX_DISTANCE / max_exact) * (N_BUCKETS - max_exact)).astype(jnp.int32)
    large = jnp.minimum(large, N_BUCKETS - 1)
    return jnp.where(n < max_exact, n, large)


def moba_attention(q, k, v, q_pos, rel_table):
    B, H, Lq, hd = q.shape
    nb = k.shape[2] // MOBA_BLOCK
    ksel = min(MOBA_TOPK, nb)
    k_blocks = k.reshape(B, H, nb, MOBA_BLOCK, hd)
    v_blocks = v.reshape(B, H, nb, MOBA_BLOCK, hd)
    k_mean = k_blocks.astype(jnp.float32).mean(axis=3).astype(q.dtype)
    qb = max(1, min(Lq, MOBA_QROWS // B))
    n_qb = -(-Lq // qb)
    pad = n_qb * qb - Lq
    q_p = jnp.pad(q, ((0, 0), (0, 0), (0, pad), (0, 0)))
    pos_p = jnp.pad(q_pos, (0, pad), mode='edge')
    q_blk = q_p.reshape(B, H, n_qb, qb, hd).transpose(2, 0, 1, 3, 4)
    pos_blk = pos_p.reshape(n_qb, qb)
    b_idx = jnp.arange(B)[:, None, None, None]
    h_idx = jnp.arange(H)[None, :, None, None]
    rel_h = rel_table.T
    offs = jnp.arange(MOBA_BLOCK, dtype=jnp.int32)
    scale = hd ** -0.5
    block_ids = jnp.arange(nb, dtype=jnp.int32)

    def one_block(args):
        qq, pos = args
        own = pos // MOBA_BLOCK
        gate = jnp.einsum('bhqd,bhnd->bhqn', qq, k_mean).astype(jnp.float32)
        gate = jnp.where(block_ids[None, :] < own[:, None], gate, -jnp.inf)
        _, top = lax.top_k(gate, ksel)
        valid = top < own[:, None]
        blocks = jnp.concatenate([top, jnp.broadcast_to(own[:, None], (B, H, qb, 1))], -1)
        slot_ok = jnp.concatenate([valid, jnp.ones((B, H, qb, 1), bool)], -1)
        kg = k_blocks[b_idx, h_idx, blocks]
        vg = v_blocks[b_idx, h_idx, blocks]
        kpos = blocks[..., None] * MOBA_BLOCK + offs
        dist = pos[:, None, None] - kpos
        allowed = slot_ok[..., None] & (dist >= 0)
        bias = rel_h[h_idx[..., None], rel_bucket(dist)]
        logits = jnp.einsum('bhqd,bhqskd->bhqsk', qq, kg).astype(jnp.float32) * scale + bias
        logits = jnp.where(allowed, logits, -jnp.inf)
        p = jax.nn.softmax(logits.reshape(B, H, qb, -1), axis=-1).reshape(logits.shape)
        return jnp.einsum('bhqsk,bhqskd->bhqd', p.astype(v.dtype), vg)

    out = lax.map(one_block, (q_blk, pos_blk))
    return out.transpose(1, 2, 0, 3, 4).reshape(B, H, n_qb * qb, hd)[:, :, :Lq]


def causal_short_conv(x, buf, w):
    L = x.shape[1]
    xp = jnp.concatenate([buf.astype(x.dtype), x], axis=1)
    y = xp[:, 0:L] * w[0]
    for i in range(1, GDN_CONV):
        y = y + xp[:, i:i + L] * w[i]
    return jax.nn.silu(y), xp[:, L:]


def gated_delta_rule(q, k, v, g, beta, s0):
    B, H, L, dk = q.shape
    dv = v.shape[-1]
    out_dtype = v.dtype
    f32 = jnp.float32
    C = min(GDN_CHUNK, L)
    pad = (-L) % C
    n = (L + pad) // C

    def prep(t):
        t = jnp.pad(t.astype(f32), [(0, 0), (0, 0), (0, pad)] + [(0, 0)] * (t.ndim - 3))
        return t.reshape((B, H, n, C) + t.shape[3:])

    qc, kc, vc, gc, bc = prep(q), prep(k), prep(v), prep(g), prep(beta)
    qc = qc * dk ** -0.5
    gcum = jnp.cumsum(gc, axis=-1)
    incl = jnp.tril(jnp.ones((C, C), bool))
    strict = jnp.tril(jnp.ones((C, C), bool), -1)
    diff = gcum[..., :, None] - gcum[..., None, :]
    decay = jnp.where(incl, jnp.exp(jnp.where(incl, diff, 0.0)), 0.0)
    a_mat = jnp.where(strict, jnp.einsum('bhnid,bhnjd->bhnij', kc, kc) * decay * bc[..., :, None], 0.0)
    rhs = jnp.concatenate([vc * bc[..., None], kc * (bc * jnp.exp(gcum))[..., None]], -1)
    sol = lax.linalg.triangular_solve(a_mat + jnp.eye(C, dtype=f32), rhs,
                                      left_side=True, lower=True, unit_diagonal=True)
    u_c, w_c = sol[..., :dv], sol[..., dv:]
    qk = jnp.where(incl, jnp.einsum('bhnid,bhnjd->bhnij', qc, kc) * decay, 0.0)
    q_dec = qc * jnp.exp(gcum)[..., None]
    k_dec = kc * jnp.exp(gcum[..., -1:] - gcum)[..., None]
    chunk_decay = jnp.exp(gcum[..., -1])

    def step(S, xs):
        u_i, w_i, qk_i, qd_i, kd_i, cd_i = xs
        v_new = u_i - jnp.einsum('bhck,bhkv->bhcv', w_i, S)
        o_i = jnp.einsum('bhck,bhkv->bhcv', qd_i, S) + jnp.einsum('bhij,bhjv->bhiv', qk_i, v_new)
        S = S * cd_i[..., None, None] + jnp.einsum('bhck,bhcv->bhkv', kd_i, v_new)
        return S, o_i

    xs = tuple(jnp.moveaxis(t, 2, 0) for t in (u_c, w_c, qk, q_dec, k_dec, chunk_decay))
    s_fin, o = lax.scan(step, s0.astype(f32), xs)
    o = jnp.moveaxis(o, 0, 2).reshape(B, H, n * C, dv)[:, :, :L]
    return o.astype(out_dtype), s_fin.astype(s0.dtype)


def mixing_sublayer(u, pos0, past_k, past_v, conv_buf, s0,
                    w_in, conv_w, a_log, dt_bias, gdn_norm_w, w_out, rel_table):
    B, L, _ = u.shape
    proj = u @ w_in
    cuts = [D_ATT, 2 * D_ATT, 3 * D_ATT, 3 * D_ATT + D_CONV,
            3 * D_ATT + D_CONV + D_GDN, 3 * D_ATT + D_CONV + D_GDN + H_GDN]
    aq, ak, av, gqkv, gz, gb, ga = jnp.split(proj, cuts, axis=-1)

    def heads(t):
        return t.reshape(B, L, -1, HEAD_DIM)

    k_rows, v_rows = heads(ak), heads(av)
    k_all = k_rows if past_k is None else jnp.concatenate([past_k, k_rows], axis=1)
    v_all = v_rows if past_v is None else jnp.concatenate([past_v, v_rows], axis=1)
    kpad = (-k_all.shape[1]) % MOBA_BLOCK
    k_all = jnp.pad(k_all, ((0, 0), (0, kpad), (0, 0), (0, 0))).transpose(0, 2, 1, 3)
    v_all = jnp.pad(v_all, ((0, 0), (0, kpad), (0, 0), (0, 0))).transpose(0, 2, 1, 3)
    q_pos = pos0 + jnp.arange(L, dtype=jnp.int32)
    att = moba_attention(heads(aq).transpose(0, 2, 1, 3), k_all, v_all, q_pos, rel_table)
    att = att.transpose(0, 2, 1, 3).reshape(B, L, D_ATT)

    gqkv, conv_new = causal_short_conv(gqkv, conv_buf, conv_w)
    gq, gk, gv = jnp.split(gqkv, 3, axis=-1)
    gq = l2norm(heads(gq)).transpose(0, 2, 1, 3)
    gk = l2norm(heads(gk)).transpose(0, 2, 1, 3)
    gv = heads(gv).transpose(0, 2, 1, 3)
    g = (-jnp.exp(a_log.astype(jnp.float32))
         * jax.nn.softplus(ga.astype(jnp.float32) + dt_bias.astype(jnp.float32))).transpose(0, 2, 1)
    beta = jax.nn.sigmoid(gb.astype(jnp.float32)).transpose(0, 2, 1)
    o, s_new = gated_delta_rule(gq, gk, gv, g, beta, s0)
    o = rms_norm(o.transpose(0, 2, 1, 3), gdn_norm_w) * jax.nn.silu(heads(gz))

    mix = jnp.concatenate([att, o.reshape(B, L, D_GDN).astype(att.dtype)], axis=-1)
    return mix @ w_out, k_rows, v_rows, conv_new, s_new


def swiglu(u, w_gate, w_up, w_down):
    return (jax.nn.silu(u @ w_gate) * (u @ w_up)) @ w_down


def moe_swiglu(u, w_router, we_gate, we_up, we_down):
    logits = (u @ w_router).astype(jnp.float32)
    top_v, top_i = lax.top_k(logits, TOP_K)
    gates = jax.nn.softmax(top_v, axis=-1)
    dense_gate = jnp.sum(jax.nn.one_hot(top_i, N_EXPERTS, dtype=jnp.float32) * gates[..., None], axis=-2)
    out = jnp.zeros_like(u)
    for e in range(N_EXPERTS):
        h = jax.nn.silu(u @ we_gate[e]) * (u @ we_up[e])
        out = out + dense_gate[..., e:e + 1].astype(u.dtype) * (h @ we_down[e])
    return out


def trunk(x, c, past, p):
    B, L, _ = x.shape
    if past is None:
        pos0 = 0
    else:
        cache_k, cache_v, state_conv, state_gdn, page_table = past
        pos0 = page_table.shape[1] * PAGE_SIZE
    x = layer_norm(x, p['ln_in_g'], p['ln_in_b'])
    c_act = jax.nn.silu(c)
    rows_k, rows_v, convs, states = [], [], [], []
    for layer in range(DEPTH):
        mod = (c_act @ p['w_mod'][layer] + p['b_mod'][layer])[:, None, :]
        sh_m, sc_m, g_m, sh_f, sc_f, g_f = jnp.split(mod, 6, axis=-1)
        if past is None:
            pk = pv = None
            buf = jnp.zeros((B, GDN_CONV - 1, D_CONV), x.dtype)
            s0 = jnp.zeros((B, H_GDN, HEAD_DIM, HEAD_DIM), x.dtype)
        else:
            pk = cache_k[layer][page_table].reshape(B, pos0, H_ATT, HEAD_DIM)
            pv = cache_v[layer][page_table].reshape(B, pos0, H_ATT, HEAD_DIM)
            buf = state_conv[layer]
            s0 = state_gdn[layer]
        h, kr, vr, buf_new, s_new = mixing_sublayer(
            x * (1 + sc_m) + sh_m, pos0, pk, pv, buf, s0,
            p['w_in'][layer], p['conv_w'][layer], p['a_log'][layer], p['dt_bias'][layer],
            p['gdn_norm_w'][layer], p['w_out'][layer], p['rel_table'])
        x = layer_norm(ALPHA * x + (1 + g_m) * h, p['ln_g'][layer, 0], p['ln_b'][layer, 0])
        u = x * (1 + sc_f) + sh_f
        i = layer // 2
        if layer % 2 == 0:
            f = swiglu(u, p['ffn_w_gate'][i], p['ffn_w_up'][i], p['ffn_w_down'][i])
        else:
            f = moe_swiglu(u, p['moe_router'][i], p['moe_w_gate'][i], p['moe_w_up'][i], p['moe_w_down'][i])
        x = layer_norm(ALPHA * x + (1 + g_f) * f, p['ln_g'][layer, 1], p['ln_b'][layer, 1])
        rows_k.append(kr)
        rows_v.append(vr)
        convs.append(buf_new)
        states.append(s_new)
    return x, jnp.stack(rows_k), jnp.stack(rows_v), jnp.stack(convs), jnp.stack(states)


def setup_inputs(seed: int = 0) -> dict:
    key = jax.random.key(seed)
    ks = jax.random.split(key, 32)
    f32 = jnp.float32
    n_pages = PAST_LEN // PAGE_SIZE
    n_pool = (DEC_BATCH * n_pages * 5) // 4
    n_dense = (DEPTH + 1) // 2
    n_moe = DEPTH // 2

    def nrm(k, shape, scale):
        return jax.random.normal(k, shape, f32) * scale

    col_scale = (jnp.ones((D_IN,), f32)
                 .at[2 * D_ATT:3 * D_ATT].set(BETA_INIT)
                 .at[3 * D_ATT + 2 * D_GDN:3 * D_ATT + 3 * D_GDN].set(BETA_INIT))
    dt = jnp.exp(jax.random.uniform(ks[14], (DEPTH, H_GDN), f32, math.log(1e-3), math.log(1e-1)))
    return {
        'x_prompt': nrm(ks[0], (BATCH, SEQ, D_MODEL), 1.0),
        'x_sample': nrm(ks[1], (DEC_BATCH, DEC_SEQ, D_MODEL), 1.0),
        'cache_k': nrm(ks[2], (DEPTH, n_pool, PAGE_SIZE, H_ATT, HEAD_DIM), 1.0),
        'cache_v': nrm(ks[3], (DEPTH, n_pool, PAGE_SIZE, H_ATT, HEAD_DIM), 1.0),
        'state_conv': nrm(ks[4], (DEPTH, DEC_BATCH, GDN_CONV - 1, D_CONV), 1.0),
        'state_gdn': nrm(ks[5], (DEPTH, DEC_BATCH, H_GDN, HEAD_DIM, HEAD_DIM), 0.1),
        'page_table': jax.random.permutation(ks[6], n_pool)[:DEC_BATCH * n_pages]
                      .reshape(DEC_BATCH, n_pages).astype(jnp.int32),
        'c_prompt': nrm(ks[7], (BATCH, D_MODEL), 1.0),
        'c_sample': nrm(ks[8], (DEC_BATCH, D_MODEL), 1.0),
        'ln_in_g': 1.0 + nrm(ks[9], (D_MODEL,), 0.02),
        'ln_in_b': nrm(ks[10], (D_MODEL,), 0.02),
        'w_mod': nrm(ks[11], (DEPTH, D_MODEL, 6 * D_MODEL), 0.2 * D_MODEL ** -0.5),
        'b_mod': nrm(ks[12], (DEPTH, 6 * D_MODEL), 0.02),
        'w_in': nrm(ks[13], (DEPTH, D_MODEL, D_IN), D_MODEL ** -0.5) * col_scale,
        'conv_w': nrm(ks[15], (DEPTH, GDN_CONV, D_CONV), GDN_CONV ** -0.5),
        'a_log': jnp.log(jax.random.uniform(ks[16], (DEPTH, H_GDN), f32, 1.0, 16.0)),
        'dt_bias': dt + jnp.log(-jnp.expm1(-dt)),
        'gdn_norm_w': 1.0 + nrm(ks[17], (DEPTH, HEAD_DIM), 0.02),
        'w_out': nrm(ks[18], (DEPTH, D_MIX, D_MODEL), D_MIX ** -0.5 * BETA_INIT),
        'rel_table': nrm(ks[19], (N_BUCKETS, H_ATT), 0.5),
        'ln_g': 1.0 + nrm(ks[20], (DEPTH, 2, D_MODEL), 0.02),
        'ln_b': nrm(ks[21], (DEPTH, 2, D_MODEL), 0.02),
        'ffn_w_gate': nrm(ks[22], (n_dense, D_MODEL, D_FF), D_MODEL ** -0.5),
        'ffn_w_up': nrm(ks[23], (n_dense, D_MODEL, D_FF), D_MODEL ** -0.5),
        'ffn_w_down': nrm(ks[24], (n_dense, D_FF, D_MODEL), D_FF ** -0.5 * BETA_INIT),
        'moe_router': nrm(ks[25], (n_moe, D_MODEL, N_EXPERTS), D_MODEL ** -0.5),
        'moe_w_gate': nrm(ks[26], (n_moe, N_EXPERTS, D_MODEL, D_FF_EXPERT), D_MODEL ** -0.5),
        'moe_w_up': nrm(ks[27], (n_moe, N_EXPERTS, D_MODEL, D_FF_EXPERT), D_MODEL ** -0.5),
        'moe_w_down': nrm(ks[28], (n_moe, N_EXPERTS, D_FF_EXPERT, D_MODEL), D_FF_EXPERT ** -0.5 * BETA_INIT),
    }


def reference(x_prompt, x_sample, cache_k, cache_v, state_conv, state_gdn, page_table, c_prompt, c_sample,
              ln_in_g, ln_in_b, w_mod, b_mod, w_in, conv_w, a_log, dt_bias, gdn_norm_w, w_out, rel_table,
              ln_g, ln_b, ffn_w_gate, ffn_w_up, ffn_w_down, moe_router, moe_w_gate, moe_w_up, moe_w_down):
    params = dict(ln_in_g=ln_in_g, ln_in_b=ln_in_b, w_mod=w_mod, b_mod=b_mod, w_in=w_in, conv_w=conv_w,
                  a_log=a_log, dt_bias=dt_bias, gdn_norm_w=gdn_norm_w, w_out=w_out, rel_table=rel_table,
                  ln_g=ln_g, ln_b=ln_b, ffn_w_gate=ffn_w_gate, ffn_w_up=ffn_w_up, ffn_w_down=ffn_w_down,
                  moe_router=moe_router, moe_w_gate=moe_w_gate, moe_w_up=moe_w_up, moe_w_down=moe_w_down)
    y_prompt, k_prompt, v_prompt, conv_prompt, gdn_prompt = trunk(x_prompt, c_prompt, None, params)
    y_sample, k_sample, v_sample, conv_sample, gdn_sample = trunk(
        x_sample, c_sample, (cache_k, cache_v, state_conv, state_gdn, page_table), params)
    return (y_prompt, y_sample, k_prompt, v_prompt, conv_prompt, gdn_prompt,
            k_sample, v_sample, conv_sample, gdn_sample)
```

```python
import functools
import math

import numpy as np
import jax
import jax.numpy as jnp
from jax import lax
from jax.experimental import pallas as pl
from jax.experimental.pallas import tpu as pltpu

F32 = jnp.float32
BF16 = jnp.bfloat16

D_MODEL = 1024
HEAD_DIM = 64
N_HEADS = 8
D_HEADS = N_HEADS * HEAD_DIM
D_CONV = 3 * D_HEADS
D_IN = 3 * D_HEADS + D_CONV + D_HEADS + 2 * N_HEADS
D_IN_MAIN = D_IN - 2 * N_HEADS
LANES = 128
D_IN_PAD = D_IN_MAIN + LANES
MOBA_BLOCK = 256
MOBA_TOPK = 3
PAGE_SIZE = 128
GDN_CONV = 4
GDN_CHUNK = 64
N_BUCKETS = 32
MAX_DISTANCE = 128
N_EXPERTS = 8
TOP_K = 2
LN_EPS = 1e-5
NEG = -1e30
MOD_ROWS = 128
VMEM_LIMIT = 56 * 1024 * 1024

HIGHEST = lax.Precision.HIGHEST


def _cparams(n_axes):
    return pltpu.CompilerParams(dimension_semantics=("arbitrary",) * n_axes,
                                vmem_limit_bytes=VMEM_LIMIT)


def _silu(x):
    return x * (1.0 / (1.0 + jnp.exp(-x)))


def _layer_norm(x, g, b):
    mu = jnp.mean(x, axis=-1, keepdims=True)
    xc = x - mu
    var = jnp.mean(xc * xc, axis=-1, keepdims=True)
    return xc * lax.rsqrt(var + LN_EPS) * g + b


def _modulate(x, scale, shift):
    tm, d = x.shape
    x3 = x.reshape(tm // MOD_ROWS, MOD_ROWS, d)
    return (x3 * (1.0 + scale[None]) + shift[None]).reshape(tm, d)


def _gated(x, gate, h):
    tm, d = x.shape
    x3 = x.reshape(tm // MOD_ROWS, MOD_ROWS, d)
    h3 = h.reshape(tm // MOD_ROWS, MOD_ROWS, d)
    return (x3 + (1.0 + gate[None]) * h3).reshape(tm, d)


def _mod_kernel(c_ref, w_ref, b_ref, o_ref):
    a = _silu(c_ref[...])
    o_ref[...] = jnp.dot(a, w_ref[...], precision=HIGHEST, preferred_element_type=F32) + b_ref[...]


def _mod_vectors(c_all, w_mod, b_mod, tn=1536):
    depth, d, n = w_mod.shape
    rows = c_all.shape[0]
    return pl.pallas_call(
        _mod_kernel,
        grid=(depth, n // tn),
        in_specs=[pl.BlockSpec((rows, d), lambda l, j: (0, 0)),
                  pl.BlockSpec((None, d, tn), lambda l, j: (l, 0, j)),
                  pl.BlockSpec((None, 1, tn), lambda l, j: (l, 0, j))],
        out_specs=pl.BlockSpec((None, rows, tn), lambda l, j: (l, 0, j)),
        out_shape=jax.ShapeDtypeStruct((depth, rows, n), F32),
        compiler_params=_cparams(2),
        name="mod_vectors",
    )(c_all, w_mod, b_mod.reshape(depth, 1, n))


def _ln_mod_kernel(x_ref, g_ref, b_ref, mod_ref, xn_ref, u_ref):
    xn = _layer_norm(x_ref[...], g_ref[...], b_ref[...])
    xn_ref[...] = xn
    u_ref[...] = _modulate(xn, mod_ref[1], mod_ref[0]).astype(BF16)


def _mod_spec(layer, group_of_tile):
    return pl.BlockSpec((None, 6, None, MOD_ROWS, D_MODEL),
                        lambda i: (layer, 0, group_of_tile(i), 0, 0))


def _ln_mod(x, g, b, mod_all, layer, tm, group_of_tile):
    t, d = x.shape
    row = pl.BlockSpec((tm, d), lambda i: (i, 0))
    vec = pl.BlockSpec((1, d), lambda i: (0, 0))
    return pl.pallas_call(
        _ln_mod_kernel,
        grid=(t // tm,),
        in_specs=[row, vec, vec, _mod_spec(layer, group_of_tile)],
        out_specs=[row, row],
        out_shape=[jax.ShapeDtypeStruct((t, d), F32), jax.ShapeDtypeStruct((t, d), BF16)],
        compiler_params=_cparams(1),
        name="ln_mod",
    )(x, g.reshape(1, d), b.reshape(1, d), mod_all)


def _proj_kernel(u_ref, w_ref, proj_ref, qkv_ref, *, chunk):
    u = u_ref[...]
    n = w_ref.shape[1]
    n_qkv = qkv_ref.shape[1]
    for c0 in range(0, n, chunk):
        c1 = min(c0 + chunk, n)
        r = jnp.dot(u, w_ref[:, c0:c1], preferred_element_type=F32)
        proj_ref[:, c0:c1] = r
        if c1 <= n_qkv:
            qkv_ref[:, c0:c1] = r.astype(BF16)


def _proj_in(u, w, tm, chunk=512):
    t, d = u.shape
    n = w.shape[1]
    return pl.pallas_call(
        functools.partial(_proj_kernel, chunk=chunk),
        grid=(t // tm,),
        in_specs=[pl.BlockSpec((tm, d), lambda i: (i, 0)),
                  pl.BlockSpec((d, n), lambda i: (0, 0))],
        out_specs=[pl.BlockSpec((tm, n), lambda i: (i, 0)),
                   pl.BlockSpec((tm, 3 * D_HEADS), lambda i: (i, 0))],
        out_shape=[jax.ShapeDtypeStruct((t, n), F32),
                   jax.ShapeDtypeStruct((t, 3 * D_HEADS), BF16)],
        compiler_params=_cparams(1),
        name="proj_in",
    )(u, w)


def _top2_gates(logits):
    lane = lax.broadcasted_iota(jnp.int32, logits.shape, 1)
    v1 = jnp.max(logits, axis=-1, keepdims=True)
    i1 = jnp.min(jnp.where(logits == v1, lane, LANES), axis=-1, keepdims=True)
    rest = jnp.where(lane == i1, -jnp.inf, logits)
    v2 = jnp.max(rest, axis=-1, keepdims=True)
    i2 = jnp.min(jnp.where(rest == v2, lane, LANES), axis=-1, keepdims=True)
    e2 = jnp.exp(v2 - v1)
    inv = 1.0 / (1.0 + e2)
    return jnp.where(lane == i1, inv, 0.0) + jnp.where(lane == i2, e2 * inv, 0.0)


def _res_ln_kernel(*refs, n_in, alpha, next_rows, with_router):
    a_refs = refs[:n_in]
    w_refs = refs[n_in:2 * n_in]
    x_ref, modg_ref, modn_ref, g_ref, b_ref = refs[2 * n_in:2 * n_in + 5]
    pos = 2 * n_in + 5
    if with_router:
        wr_ref = refs[pos]
        pos += 1
    outs = refs[pos:]
    h = jnp.dot(a_refs[0][...], w_refs[0][...], preferred_element_type=F32)
    for a_ref, w_ref in zip(a_refs[1:], w_refs[1:]):
        h = h + jnp.dot(a_ref[...], w_ref[...], preferred_element_type=F32)
    xn = _layer_norm(_gated(alpha * x_ref[...], modg_ref[...], h), g_ref[...], b_ref[...])
    outs[0][...] = xn
    if next_rows is not None:
        u = _modulate(xn, modn_ref[next_rows[0]], modn_ref[next_rows[1]])
        outs[1][...] = u.astype(BF16)
        if with_router:
            logits = jnp.dot(u, wr_ref[...], precision=HIGHEST, preferred_element_type=F32)
            lane = lax.broadcasted_iota(jnp.int32, logits.shape, 1)
            outs[2][...] = _top2_gates(jnp.where(lane < N_EXPERTS, logits, -jnp.inf))


def _res_ln(a_list, w_list, x, mod_all, gate_layer, gate_row, next_layer, next_rows, ln_g, ln_b,
            alpha, tm, group_of_tile, w_router=None):
    t, d = x.shape
    n_in = len(a_list)
    row = pl.BlockSpec((tm, d), lambda i: (i, 0))
    vec = pl.BlockSpec((1, d), lambda i: (0, 0))
    in_specs = [pl.BlockSpec((tm, a.shape[1]), lambda i: (i, 0)) for a in a_list]
    in_specs += [pl.BlockSpec(w.shape, lambda i: (0, 0)) for w in w_list]
    in_specs += [row,
                 pl.BlockSpec((None, None, None, MOD_ROWS, d),
                              lambda i: (gate_layer, gate_row, group_of_tile(i), 0, 0)),
                 _mod_spec(next_layer if next_rows is not None else gate_layer, group_of_tile),
                 vec, vec]
    args = list(a_list) + list(w_list) + [x, mod_all, mod_all, ln_g.reshape(1, d), ln_b.reshape(1, d)]
    out_specs = [row]
    out_shape = [jax.ShapeDtypeStruct((t, d), F32)]
    if next_rows is not None:
        out_specs.append(row)
        out_shape.append(jax.ShapeDtypeStruct((t, d), BF16))
    with_router = w_router is not None
    if with_router:
        in_specs.append(pl.BlockSpec(w_router.shape, lambda i: (0, 0)))
        args.append(w_router)
        out_specs.append(pl.BlockSpec((tm, LANES), lambda i: (i, 0)))
        out_shape.append(jax.ShapeDtypeStruct((t, LANES), F32))
    return pl.pallas_call(
        functools.partial(_res_ln_kernel, n_in=n_in, alpha=alpha, next_rows=next_rows,
                          with_router=with_router),
        grid=(t // tm,),
        in_specs=in_specs,
        out_specs=out_specs,
        out_shape=out_shape,
        compiler_params=_cparams(1),
        name="res_ln",
    )(*args)


def _ffn_up_kernel(u_ref, wg_ref, wu_ref, o_ref, *, chunk):
    u = u_ref[...]
    n = o_ref.shape[1]
    for c0 in range(0, n, chunk):
        c1 = min(c0 + chunk, n)
        hg = jnp.dot(u, wg_ref[:, c0:c1], preferred_element_type=F32)
        hu = jnp.dot(u, wu_ref[:, c0:c1], preferred_element_type=F32)
        o_ref[:, c0:c1] = (_silu(hg) * hu).astype(BF16)


def _ffn_up(u, wg, wu, tm, chunk=256):
    t, d = u.shape
    n = wg.shape[1]
    wspec = pl.BlockSpec((d, n), lambda i: (0, 0))
    return pl.pallas_call(
        functools.partial(_ffn_up_kernel, chunk=chunk),
        grid=(t // tm,),
        in_specs=[pl.BlockSpec((tm, d), lambda i: (i, 0)), wspec, wspec],
        out_specs=pl.BlockSpec((tm, n), lambda i: (i, 0)),
        out_shape=jax.ShapeDtypeStruct((t, n), BF16),
        compiler_params=_cparams(1),
        name="ffn_up",
    )(u, wg, wu)


def _moe_kernel(u_ref, gates_ref, wg_ref, wu_ref, wd_ref, x_ref, modg_ref, g_ref, b_ref,
                o_ref, acc_ref, *, alpha):
    e = pl.program_id(1)
    j = pl.program_id(2)

    @pl.when((e == 0) & (j == 0))
    def _():
        acc_ref[...] = jnp.zeros_like(acc_ref)

    u = u_ref[...]
    hg = jnp.dot(u, wg_ref[...].astype(BF16), preferred_element_type=F32)
    hu = jnp.dot(u, wu_ref[...].astype(BF16), preferred_element_type=F32)
    gates = gates_ref[...]
    lane = lax.broadcasted_iota(jnp.int32, gates.shape, 1)
    gate_e = jnp.sum(jnp.where(lane == e, gates, 0.0), axis=-1, keepdims=True)
    hh = (_silu(hg) * hu * gate_e).astype(BF16)
    acc_ref[...] += jnp.dot(hh, wd_ref[...].astype(BF16), preferred_element_type=F32)

    @pl.when((e == pl.num_programs(1) - 1) & (j == pl.num_programs(2) - 1))
    def _():
        o_ref[...] = _layer_norm(_gated(alpha * x_ref[...], modg_ref[...], acc_ref[...]),
                                 g_ref[...], b_ref[...])


def _moe_dense(u, gates, wg, wu, wd, x, mod_all, gate_layer, gate_row, ln_g, ln_b, alpha,
               tm, tj, group_of_tile):
    t, d = x.shape
    n_e, _, f = wg.shape
    row = pl.BlockSpec((tm, d), lambda i, e, j: (i, 0))
    vec = pl.BlockSpec((1, d), lambda i, e, j: (0, 0))
    return pl.pallas_call(
        functools.partial(_moe_kernel, alpha=alpha),
        grid=(t // tm, n_e, f // tj),
        in_specs=[row,
                  pl.BlockSpec((tm, LANES), lambda i, e, j: (i, 0)),
                  pl.BlockSpec((None, d, tj), lambda i, e, j: (e, 0, j)),
                  pl.BlockSpec((None, d, tj), lambda i, e, j: (e, 0, j)),
                  pl.BlockSpec((None, tj, d), lambda i, e, j: (e, j, 0)),
                  row,
                  pl.BlockSpec((None, None, None, MOD_ROWS, d),
                               lambda i, e, j: (gate_layer, gate_row, group_of_tile(i), 0, 0)),
                  vec, vec],
        out_specs=row,
        out_shape=jax.ShapeDtypeStruct((t, d), F32),
        scratch_shapes=[pltpu.VMEM((tm, d), F32)],
        compiler_params=_cparams(3),
        name="moe_dense",
    )(u, gates, wg, wu, wd, x, mod_all, ln_g.reshape(1, d), ln_b.reshape(1, d))


def _rel_bucket_table(max_dist):
    n = np.arange(max_dist + 1)
    max_exact = N_BUCKETS // 2
    ratio = np.log(np.maximum(n, 1).astype(np.float32) / np.float32(max_exact))
    large = max_exact + (ratio / np.float32(math.log(MAX_DISTANCE / max_exact))
                         * np.float32(N_BUCKETS - max_exact)).astype(np.int32)
    large = np.minimum(large, N_BUCKETS - 1)
    return np.where(n < max_exact, n, large).astype(np.int32)


def _select_top_blocks(gate, n_valid, ksel):
    lane = lax.broadcasted_iota(jnp.int32, gate.shape, 1)
    nb = gate.shape[1]
    g = jnp.where(lane < n_valid, gate, -jnp.inf)
    keep = lane == n_valid
    for _ in range(ksel):
        mx = jnp.max(g, axis=-1, keepdims=True)
        first = jnp.min(jnp.where(g == mx, lane, nb), axis=-1, keepdims=True)
        hit = lane == first
        keep = keep | (hit & (lane < n_valid))
        g = jnp.where(hit, -jnp.inf, g)
    return jnp.where(keep, 0.0, NEG)


def _moba_prompt_kernel(q_ref, k_ref, v_ref, bias_ref, o_ref,
                        kmean_ref, qm_ref, sel_ref, m_ref, l_ref, acc_ref, *, ksel):
    qt = pl.program_id(1)
    seq = k_ref.shape[0]
    nb = seq // MOBA_BLOCK
    gw = 4 * HEAD_DIM
    n_groups = N_HEADS // 4

    @pl.when(qt == 0)
    def _():
        row = lax.broadcasted_iota(jnp.int32, (nb, seq), 0)
        col = lax.broadcasted_iota(jnp.int32, (nb, seq), 1)
        pool = jnp.where(col // MOBA_BLOCK == row, 1.0, 0.0).astype(BF16)
        kmean_ref[...] = jnp.dot(pool, k_ref[...], preferred_element_type=F32) * (1.0 / MOBA_BLOCK)

    head_of_lane = lax.broadcasted_iota(jnp.int32, (1, gw), 1) // HEAD_DIM
    q = q_ref[...] * (HEAD_DIM ** -0.5)
    kmean = kmean_ref[...].astype(BF16)
    for h in range(N_HEADS):
        g, hh = divmod(h, 4)
        qm = jnp.where(head_of_lane == hh, q[:, g * gw:(g + 1) * gw], 0.0).astype(BF16)
        qm_ref[h] = qm
        gate = lax.dot_general(qm, kmean[:, g * gw:(g + 1) * gw], (((1,), (1,)), ((), ())),
                               preferred_element_type=F32)
        sel_ref[h] = _select_top_blocks(gate, qt, ksel)
        m_ref[h] = jnp.full(m_ref.shape[1:], NEG, F32)
        l_ref[h] = jnp.zeros(l_ref.shape[1:], F32)
    acc_ref[...] = jnp.zeros_like(acc_ref)

    def body(step, carry):
        n = qt - step
        start = pl.multiple_of(n * MOBA_BLOCK, MOBA_BLOCK)
        kb = k_ref[pl.ds(start, MOBA_BLOCK), :]
        vb = v_ref[pl.ds(start, MOBA_BLOCK), :]
        bias_idx = jnp.minimum(step, 2)
        for h in range(N_HEADS):
            g, hh = divmod(h, 4)
            s = lax.dot_general(qm_ref[h], kb[:, g * gw:(g + 1) * gw], (((1,), (1,)), ((), ())),
                                preferred_element_type=F32)
            sel = sel_ref[h]
            lane = lax.broadcasted_iota(jnp.int32, sel.shape, 1)
            sel_col = jnp.sum(jnp.where(lane == n, sel, 0.0), axis=-1, keepdims=True)
            s = s + bias_ref[h, bias_idx] + sel_col
            m_old = m_ref[h]
            m_new = jnp.maximum(m_old, jnp.max(s, axis=-1, keepdims=True))
            alpha = jnp.exp(m_old - m_new)
            p = jnp.exp(s - m_new)
            l_ref[h] = alpha * l_ref[h] + jnp.sum(p, axis=-1, keepdims=True)
            m_ref[h] = m_new
            pv = jnp.dot(p.astype(BF16), vb[:, g * gw:(g + 1) * gw], preferred_element_type=F32)
            acc = acc_ref[g]
            acc_ref[g] = jnp.where(head_of_lane == hh, alpha * acc + pv, acc)
        return carry

    lax.fori_loop(0, qt + 1, body, 0)

    for g in range(n_groups):
        inv = jnp.zeros((q.shape[0], gw), F32)
        for hh in range(4):
            inv = jnp.where(head_of_lane == hh, 1.0 / l_ref[4 * g + hh], inv)
        o_ref[:, g * gw:(g + 1) * gw] = (acc_ref[g] * inv).astype(o_ref.dtype)


def _moba_bias_tables(rel_table):
    bucket = _rel_bucket_table(2 * MOBA_BLOCK)
    i = np.arange(MOBA_BLOCK)[:, None]
    j = np.arange(MOBA_BLOCK)[None, :]
    rel_h = rel_table.T
    own = jnp.where(jnp.asarray(i >= j), rel_h[:, bucket[np.maximum(i - j, 0)]], NEG)
    prev = rel_h[:, bucket[MOBA_BLOCK + i - j]]
    far = jnp.broadcast_to(rel_h[:, N_BUCKETS - 1][:, None, None], prev.shape)
    return jnp.stack([own, prev, far], axis=1).astype(F32)


def _moba_prompt(qkv, bias_tab, batch, seq):
    nb = seq // MOBA_BLOCK
    ksel = min(MOBA_TOPK, nb)
    stat = pltpu.VMEM((N_HEADS, MOBA_BLOCK, 1), F32)
    return pl.pallas_call(
        functools.partial(_moba_prompt_kernel, ksel=ksel),
        grid=(batch, nb),
        in_specs=[pl.BlockSpec((MOBA_BLOCK, D_HEADS), lambda b, t: (b * nb + t, 0)),
                  pl.BlockSpec((seq, D_HEADS), lambda b, t: (b, 1)),
                  pl.BlockSpec((seq, D_HEADS), lambda b, t: (b, 2)),
                  pl.BlockSpec(bias_tab.shape, lambda b, t: (0, 0, 0, 0))],
        out_specs=pl.BlockSpec((MOBA_BLOCK, D_HEADS), lambda b, t: (b * nb + t, 0)),
        out_shape=jax.ShapeDtypeStruct((batch * seq, D_HEADS), BF16),
        scratch_shapes=[pltpu.VMEM((nb, D_HEADS), F32),
                        pltpu.VMEM((N_HEADS, MOBA_BLOCK, 4 * HEAD_DIM), BF16),
                        pltpu.VMEM((N_HEADS, MOBA_BLOCK, nb), F32),
                        stat, stat,
                        pltpu.VMEM((N_HEADS // 4, MOBA_BLOCK, 4 * HEAD_DIM), F32)],
        compiler_params=_cparams(2),
        name="moba_prompt",
    )(qkv, qkv, qkv, bias_tab)


def _moba_sample_kernel(pt_ref, q_ref, kn_ref, vn_ref, ka_ref, kb_ref, va_ref, vb_ref,
                        bias_ref, bown_ref, o_ref, m_ref, l_ref, g_ref, acc_ref, *, ksel, n_q):
    del pt_ref
    j = pl.program_id(1)
    nbp = pl.num_programs(1)
    nt = (((1,), (1,)), ((), ()))
    q = q_ref[...]
    qs = (q * (HEAD_DIM ** -0.5)).astype(BF16)
    kk = jnp.concatenate([ka_ref[...], kb_ref[...]], axis=0)
    vv = jnp.concatenate([va_ref[...], vb_ref[...]], axis=0)
    s = lax.dot_general(qs, kk.astype(BF16), nt, preferred_element_type=F32)
    s = s + bias_ref[jnp.where(j == nbp - 1, 0, 1)]
    m = jnp.max(s, axis=-1, keepdims=True)
    p = jnp.exp(s - m)
    m_ref[j] = m
    l_ref[j] = jnp.sum(p, axis=-1, keepdims=True)
    acc_ref[j] = jnp.dot(p.astype(BF16), vv.astype(BF16), preferred_element_type=F32)
    kmean = jnp.sum(kk.reshape(MOBA_BLOCK, N_HEADS, HEAD_DIM), axis=0) * (1.0 / MOBA_BLOCK)
    gfull = lax.dot_general(q, kmean, nt, precision=HIGHEST, preferred_element_type=F32)
    lane = lax.broadcasted_iota(jnp.int32, gfull.shape, 1)
    row_head = lax.broadcasted_iota(jnp.int32, gfull.shape, 0) // n_q
    g_ref[j] = jnp.sum(jnp.where(lane == row_head, gfull, 0.0), axis=-1, keepdims=True)

    @pl.when(j == nbp - 1)
    def _():
        s_o = lax.dot_general(qs, kn_ref[...].astype(BF16), nt, preferred_element_type=F32)
        s_o = s_o + bown_ref[...]
        m_o = jnp.max(s_o, axis=-1, keepdims=True)
        p_o = jnp.exp(s_o - m_o)
        l_o = jnp.sum(p_o, axis=-1, keepdims=True)
        acc_o = jnp.dot(p_o.astype(BF16), vn_ref[...].astype(BF16), preferred_element_type=F32)

        gg = g_ref[...]
        blk = lax.broadcasted_iota(jnp.int32, gg.shape, 0)
        keep = blk < 0
        for _ in range(ksel):
            mx = jnp.max(gg, axis=0, keepdims=True)
            first = jnp.min(jnp.where(gg == mx, blk, nbp), axis=0, keepdims=True)
            hit = (blk == first) & (gg > -jnp.inf)
            keep = keep | hit
            gg = jnp.where(blk == first, -jnp.inf, gg)
        m_all = m_ref[...]
        m_tot = jnp.maximum(jnp.max(jnp.where(keep, m_all, NEG), axis=0), m_o)
        w = jnp.where(keep, jnp.exp(m_all - m_tot[None]), 0.0)
        w_o = jnp.exp(m_o - m_tot)
        l_tot = jnp.sum(w * l_ref[...], axis=0) + w_o * l_o
        acc = jnp.sum(w * acc_ref[...], axis=0) + w_o * acc_o
        o_ref[...] = acc * (1.0 / l_tot)


def _moba_sample_bias(rel_table, past_len, n_q):
    rel_h = rel_table.T
    bucket = _rel_bucket_table(past_len + n_q)
    rows_h = np.repeat(np.arange(N_HEADS), n_q)[:, None]
    rows_i = np.tile(np.arange(n_q), N_HEADS)[:, None]
    cols = np.arange(MOBA_BLOCK * N_HEADS)[None, :]
    col_pos, col_h = cols // N_HEADS, cols % N_HEADS
    match = jnp.asarray(rows_h == col_h)
    near = rel_h[rows_h, bucket[MOBA_BLOCK + rows_i - col_pos]]
    far = jnp.broadcast_to(rel_h[rows_h, N_BUCKETS - 1], near.shape)
    past = jnp.stack([jnp.where(match, near, NEG), jnp.where(match, far, NEG)]).astype(F32)
    ocols = np.arange(n_q * N_HEADS)[None, :]
    o_pos, o_h = ocols // N_HEADS, ocols % N_HEADS
    ok = jnp.asarray((rows_h == o_h) & (o_pos <= rows_i))
    own = jnp.where(ok, rel_h[rows_h, bucket[np.maximum(rows_i - o_pos, 0)]], NEG).astype(F32)
    return past, own


def _moba_sample(q_rows, k_new, v_new, cache_k, cache_v, page_table, layer, bias_past, bias_own):
    bsz, rows, hd = q_rows.shape
    n_q = rows // N_HEADS
    pages_per_block = MOBA_BLOCK // PAGE_SIZE
    nbp = page_table.shape[1] // pages_per_block
    ksel = min(MOBA_TOPK, nbp + 1)
    page_rows = PAGE_SIZE * N_HEADS

    def page_spec(which):
        return pl.BlockSpec((None, None, page_rows, hd),
                            lambda b, j, pt: (layer, pt[b, pages_per_block * j + which], 0, 0))

    per_seq = lambda b, j, pt: (b, 0, 0)
    stat = pltpu.VMEM((nbp, rows, 1), F32)
    grid_spec = pltpu.PrefetchScalarGridSpec(
        num_scalar_prefetch=1,
        grid=(bsz, nbp),
        in_specs=[pl.BlockSpec((None, rows, hd), per_seq),
                  pl.BlockSpec((None, rows, hd), per_seq),
                  pl.BlockSpec((None, rows, hd), per_seq),
                  page_spec(0), page_spec(1), page_spec(0), page_spec(1),
                  pl.BlockSpec(bias_past.shape, lambda b, j, pt: (0, 0, 0)),
                  pl.BlockSpec(bias_own.shape, lambda b, j, pt: (0, 0))],
        out_specs=pl.BlockSpec((None, rows, hd), per_seq),
        scratch_shapes=[stat, stat, stat, pltpu.VMEM((nbp, rows, hd), F32)])
    return pl.pallas_call(
        functools.partial(_moba_sample_kernel, ksel=ksel, n_q=n_q),
        grid_spec=grid_spec,
        out_shape=jax.ShapeDtypeStruct((bsz, rows, hd), F32),
        compiler_params=_cparams(2),
        name="moba_sample",
    )(page_table, q_rows, k_new, v_new, cache_k, cache_k, cache_v, cache_v, bias_past, bias_own)


def _softplus(x):
    return jnp.maximum(x, 0.0) + jnp.log1p(jnp.exp(-jnp.abs(x)))


def _dot_bf16(a, b):
    return jnp.dot(a.astype(BF16), b.astype(BF16), preferred_element_type=F32)


def _dot_nt_bf16(a, b):
    return lax.dot_general(a.astype(BF16), b.astype(BF16), (((1,), (1,)), ((), ())),
                           preferred_element_type=F32)


def _dot_f32(a, b):
    return jnp.dot(a, b, precision=HIGHEST, preferred_element_type=F32)


def _unit_lower_inverse(a):
    c = a.shape[0]
    row = lax.broadcasted_iota(jnp.int32, (c, c), 0)
    col = lax.broadcasted_iota(jnp.int32, (c, c), 1)
    eye = jnp.where(row == col, 1.0, 0.0)
    base = 16
    in_base = (row // base) == (col // base)
    nil = jnp.where(in_base, -a, 0.0)
    x = eye + nil
    p = nil
    width = 2
    while width < base:
        p = _dot_f32(p, p)
        x = x + _dot_f32(x, p)
        width *= 2
    size = base
    while size < c:
        inner = (row // size) == (col // size)
        outer = (row // (2 * size)) == (col // (2 * size))
        off = jnp.where(outer & jnp.logical_not(inner), a, 0.0)
        x = x - _dot_f32(x, _dot_f32(off, x))
        size *= 2
    return x


def _gdn_chunk(q, k, v, k_t, g_col, g_row, beta, state):
    c = q.shape[0]
    row = lax.broadcasted_iota(jnp.int32, (c, c), 0)
    col = lax.broadcasted_iota(jnp.int32, (c, c), 1)
    incl = row >= col
    strict = row > col
    decay = jnp.where(incl, jnp.exp(jnp.where(incl, g_col - g_row, 0.0)), 0.0)
    a_mat = jnp.where(strict, _dot_nt_bf16(k, k) * decay * beta, 0.0)
    e_g = jnp.exp(g_col)
    rhs = jnp.concatenate([v * beta, k * beta * e_g], axis=1)
    sol = _dot_f32(_unit_lower_inverse(a_mat), rhs)
    dv = v.shape[1]
    u_c, w_c = sol[:, :dv], sol[:, dv:]
    qk = jnp.where(incl, _dot_nt_bf16(q, k) * decay, 0.0)
    g_last = g_col[c - 1:c, :]
    k_dec_t = k_t * jnp.exp(g_last - g_row)
    v_new = u_c - _dot_bf16(w_c, state)
    o = _dot_bf16(q * e_g, state) + _dot_bf16(qk, v_new)
    new_state = state * jnp.exp(g_last) + _dot_bf16(k_dec_t, v_new)
    return o, new_state


def _gdn_kernel(x_ref, z_ref, gab_ref, conv0_ref, s0_ref, cw_ref, alog_ref, dtb_ref, nw_ref, seg_ref,
                o_ref, conv_ref, sout_ref,
                xbuf_ref, halo_ref, s_ref, qh_ref, kh_ref, vh_ref, kt_ref, gc_ref, gr_ref, bb_ref,
                oh_ref, *, n_valid):
    t = pl.program_id(1)
    rb = x_ref.shape[0]
    tl = qh_ref.shape[1]
    cs = GDN_CHUNK
    n_chunks = tl // cs
    hd = HEAD_DIM

    @pl.when(t == 0)
    def _():
        halo_ref[...] = conv0_ref[...]
        s_ref[...] = s0_ref[...]

    x = x_ref[...]
    xbuf_ref[0:8, :] = halo_ref[...]
    xbuf_ref[8:8 + rb, :] = x
    if rb < tl:
        xbuf_ref[8 + rb:8 + tl, :] = jnp.zeros((tl - rb, x.shape[1]), F32)
    y = cw_ref[0:1, :] * xbuf_ref[5:5 + tl, :]
    for i in range(1, GDN_CONV):
        y = y + cw_ref[i:i + 1, :] * xbuf_ref[5 + i:5 + i + tl, :]
    y = _silu(y)
    tail = xbuf_ref[n_valid:n_valid + 8, :]
    halo_ref[...] = tail
    conv_ref[...] = tail[8 - (GDN_CONV - 1):, :]

    dh = N_HEADS * hd
    seg = seg_ref[...]

    def l2n(a):
        return a * lax.rsqrt(_dot_f32(a * a, seg) + 1e-6)

    qn = l2n(y[:, :dh]) * (hd ** -0.5)
    kn = l2n(y[:, dh:2 * dh])
    vv = y[:, 2 * dh:]
    kn_t = kn.T

    gab = gab_ref[...]
    if rb < tl:
        gab = jnp.concatenate([gab, jnp.zeros((tl - rb, gab.shape[1]), F32)], axis=0)
    beta = 1.0 / (1.0 + jnp.exp(-gab))
    glog = -jnp.exp(alog_ref[...]) * _softplus(gab + dtb_ref[...])
    if n_valid < tl:
        live = lax.broadcasted_iota(jnp.int32, gab.shape, 0) < n_valid
        beta = jnp.where(live, beta, 0.0)
        glog = jnp.where(live, glog, 0.0)

    r_i = lax.broadcasted_iota(jnp.int32, (cs, cs), 0)
    c_i = lax.broadcasted_iota(jnp.int32, (cs, cs), 1)
    tril = jnp.where(r_i >= c_i, 1.0, 0.0)
    pick = jnp.where(lax.broadcasted_iota(jnp.int32, (N_HEADS, LANES), 1)
                     == lax.broadcasted_iota(jnp.int32, (N_HEADS, LANES), 0) + N_HEADS, 1.0, 0.0)
    for c in range(n_chunks):
        rows = slice(c * cs, (c + 1) * cs)
        gcum = _dot_f32(tril, glog[rows])
        g_rows = lax.dot_general(pick, gcum, (((1,), (1,)), ((), ())), precision=HIGHEST,
                                 preferred_element_type=F32)
        for h in range(N_HEADS):
            gc_ref[h, rows, :] = jnp.broadcast_to(gcum[:, N_HEADS + h:N_HEADS + h + 1], (cs, cs))
            bb_ref[h, rows, :] = jnp.broadcast_to(beta[rows, h:h + 1], (cs, cs))
            gr_ref[c, h] = g_rows[h:h + 1, :]
            kt_ref[h, c] = kn_t[h * hd:(h + 1) * hd, rows]
    for h in range(N_HEADS):
        lanes = slice(h * hd, (h + 1) * hd)
        qh_ref[h] = qn[:, lanes]
        kh_ref[h] = kn[:, lanes]
        vh_ref[h] = vv[:, lanes]

    def chunk_body(c, carry):
        start = pl.multiple_of(c * cs, cs)
        rows = pl.ds(start, cs)
        for h in range(N_HEADS):
            o, s_new = _gdn_chunk(qh_ref[h, rows, :], kh_ref[h, rows, :], vh_ref[h, rows, :],
                                  kt_ref[h, c], gc_ref[h, rows, :], gr_ref[c, h],
                                  bb_ref[h, rows, :], s_ref[h])
            oh_ref[h, rows, :] = o
            s_ref[h] = s_new
        return carry

    lax.fori_loop(0, n_chunks, chunk_body, 0)

    z = z_ref[...]
    if rb < tl:
        z = jnp.concatenate([z, jnp.zeros((tl - rb, z.shape[1]), F32)], axis=0)
    nw = nw_ref[...]
    for h in range(N_HEADS):
        lanes = slice(h * hd, (h + 1) * hd)
        o = oh_ref[h]
        o = o * lax.rsqrt(jnp.mean(o * o, axis=-1, keepdims=True) + 1e-6) * nw
        o = o * _silu(z[:, lanes])
        o_ref[:, lanes] = o[:rb].astype(o_ref.dtype)

    @pl.when(t == pl.num_programs(1) - 1)
    def _():
        sout_ref[...] = s_ref[...]


def _gdn(x_arr, x_col, z_arr, z_col, gab_arr, gab_col, row_block_of, n_seq, n_steps, rb, tl, n_valid,
         conv0, s0, conv_w, a_log, dt_bias, norm_w, out_rows):
    hd = HEAD_DIM
    cs = GDN_CHUNK
    lane = np.arange(D_HEADS)
    seg = jnp.asarray((lane[:, None] // hd == lane[None, :] // hd).astype(np.float32))
    alog_row = jnp.zeros((1, LANES), F32).at[0, N_HEADS:2 * N_HEADS].set(a_log)
    dtb_row = jnp.zeros((1, LANES), F32).at[0, N_HEADS:2 * N_HEADS].set(dt_bias)
    const2 = lambda b, t: (0, 0)
    per_head = pltpu.VMEM((N_HEADS, tl, hd), F32)
    return pl.pallas_call(
        functools.partial(_gdn_kernel, n_valid=n_valid),
        grid=(n_seq, n_steps),
        in_specs=[pl.BlockSpec((rb, D_CONV), lambda b, t: (row_block_of(b, t), x_col)),
                  pl.BlockSpec((rb, D_HEADS), lambda b, t: (row_block_of(b, t), z_col)),
                  pl.BlockSpec((rb, LANES), lambda b, t: (row_block_of(b, t), gab_col)),
                  pl.BlockSpec((None, 8, D_CONV), lambda b, t: (b, 0, 0)),
                  pl.BlockSpec((None, N_HEADS, hd, hd), lambda b, t: (b, 0, 0, 0)),
                  pl.BlockSpec((GDN_CONV, D_CONV), const2),
                  pl.BlockSpec((1, LANES), const2),
                  pl.BlockSpec((1, LANES), const2),
                  pl.BlockSpec((1, hd), const2),
                  pl.BlockSpec((D_HEADS, D_HEADS), const2)],
        out_specs=[pl.BlockSpec((rb, D_HEADS), lambda b, t: (row_block_of(b, t), 0)),
                   pl.BlockSpec((None, GDN_CONV - 1, D_CONV), lambda b, t: (b, 0, 0)),
                   pl.BlockSpec((None, N_HEADS, hd, hd), lambda b, t: (b, 0, 0, 0))],
        out_shape=[jax.ShapeDtypeStruct((out_rows, D_HEADS), BF16),
                   jax.ShapeDtypeStruct((n_seq, GDN_CONV - 1, D_CONV), F32),
                   jax.ShapeDtypeStruct((n_seq, N_HEADS, hd, hd), F32)],
        scratch_shapes=[pltpu.VMEM((8 + tl + 8, D_CONV), F32),
                        pltpu.VMEM((8, D_CONV), F32),
                        pltpu.VMEM((N_HEADS, hd, hd), F32),
                        per_head, per_head, per_head,
                        pltpu.VMEM((N_HEADS, tl // cs, hd, cs), F32),
                        per_head,
                        pltpu.VMEM((tl // cs, N_HEADS, 1, cs), F32),
                        per_head, per_head],
        compiler_params=_cparams(2),
        name="gdn",
    )(x_arr, z_arr, gab_arr, conv0, s0, conv_w, alog_row, dtb_row, norm_w.reshape(1, hd), seg)


TM = 256
TM_MOE = 512
TJ_MOE = 512
GDN_ROWS = 256
SAMPLE_PAD = 8


def kernel(x_prompt, x_sample, cache_k, cache_v, state_conv, state_gdn, page_table, c_prompt, c_sample,
           ln_in_g, ln_in_b, w_mod, b_mod, w_in, conv_w, a_log, dt_bias, gdn_norm_w, w_out, rel_table,
           ln_g, ln_b, ffn_w_gate, ffn_w_up, ffn_w_down, moe_router, moe_w_gate, moe_w_up, moe_w_down):
    bp, seq, d = x_prompt.shape
    bs, n_q, _ = x_sample.shape
    depth = w_in.shape[0]
    tp, ts = bp * seq, bs * n_q
    h, hd = N_HEADS, HEAD_DIM
    assert bs == MOD_ROWS and seq % TM_MOE == 0 and ts % TM_MOE == 0 and seq % MOBA_BLOCK == 0
    assert depth == 2 and n_q <= SAMPLE_PAD and GDN_CHUNK == HEAD_DIM
    alpha = (2 * depth) ** 0.25

    def groups(tm):
        return lambda i: jnp.minimum(i // (seq // tm), bp)

    n_c = bp + bs
    c_all = jnp.pad(jnp.concatenate([c_prompt, c_sample]), ((0, (-n_c) % 8), (0, 0)))
    mod = _mod_vectors(c_all, w_mod, b_mod).reshape(depth, -1, 6, d)
    mod_p = jnp.broadcast_to(mod[:, :bp].transpose(0, 2, 1, 3)[:, :, :, None, :],
                             (depth, 6, bp, MOD_ROWS, d))
    mod_s = mod[:, bp:n_c].transpose(0, 2, 1, 3)[:, :, None]
    mod_all = jnp.concatenate([mod_p, mod_s], axis=2)

    x_all = jnp.concatenate([x_prompt.reshape(tp, d), x_sample.transpose(1, 0, 2).reshape(ts, d)])
    x, u = _ln_mod(x_all, ln_in_g, ln_in_b, mod_all, 0, TM, groups(TM))

    bias_tab = _moba_bias_tables(rel_table)
    past_len = page_table.shape[1] * PAGE_SIZE
    bias_past, bias_own = _moba_sample_bias(rel_table, past_len, n_q)
    n_pool = cache_k.shape[1]
    cache_k4 = cache_k.reshape(depth, n_pool, PAGE_SIZE * h, hd)
    cache_v4 = cache_v.reshape(depth, n_pool, PAGE_SIZE * h, hd)
    zero_conv = jnp.zeros((bp, 8, D_CONV), F32)
    zero_state = jnp.zeros((bp, h, hd, hd), F32)
    steps = seq // GDN_ROWS
    col_x, col_z, col_gab = 1, D_IN_MAIN // D_HEADS - 1, D_IN_MAIN // LANES

    k_p, v_p, conv_p, gdn_p, k_s, v_s, conv_s, gdn_s = [], [], [], [], [], [], [], []
    for layer in range(depth):
        w_l = w_in[layer]
        w_cat = jnp.concatenate(
            [w_l[:, :D_IN_MAIN], jnp.pad(w_l[:, D_IN_MAIN:], ((0, 0), (0, LANES - 2 * h)))],
            axis=1).astype(BF16)
        proj, qkv = _proj_in(u, w_cat, TM)

        def heads_s(cols):
            return proj[tp:, cols].reshape(n_q, bs, h, hd)

        q_sr = heads_s(slice(0, D_HEADS)).transpose(1, 2, 0, 3).reshape(bs, h * n_q, hd)
        k_sr = heads_s(slice(D_HEADS, 2 * D_HEADS)).transpose(1, 0, 2, 3)
        v_sr = heads_s(slice(2 * D_HEADS, 3 * D_HEADS)).transpose(1, 0, 2, 3)
        att_p = _moba_prompt(qkv, bias_tab, bp, seq)
        att_s = _moba_sample(q_sr, k_sr.reshape(bs, n_q * h, hd), v_sr.reshape(bs, n_q * h, hd),
                             cache_k4, cache_v4, page_table, layer, bias_past, bias_own)
        att_s = att_s.reshape(bs, h, n_q, hd).transpose(2, 0, 1, 3).reshape(ts, D_HEADS)
        att = jnp.concatenate([att_p, att_s.astype(BF16)])

        o_p, conv_new_p, s_new_p = _gdn(
            proj, col_x, proj, col_z, proj, col_gab, lambda b, t: b * steps + t, bp, steps,
            GDN_ROWS, GDN_ROWS, GDN_ROWS, zero_conv, zero_state, conv_w[layer], a_log[layer],
            dt_bias[layer], gdn_norm_w[layer], tp)
        proj_s = jnp.pad(proj[tp:].reshape(n_q, bs, -1).transpose(1, 0, 2),
                         ((0, 0), (0, SAMPLE_PAD - n_q), (0, 0))).reshape(bs * SAMPLE_PAD, -1)
        conv0_s = jnp.pad(state_conv[layer], ((0, 0), (8 - (GDN_CONV - 1), 0), (0, 0)))
        o_s, conv_new_s, s_new_s = _gdn(
            proj_s, col_x, proj_s, col_z, proj_s, col_gab, lambda b, t: b, bs, 1,
            SAMPLE_PAD, GDN_CHUNK, n_q, conv0_s, state_gdn[layer], conv_w[layer], a_log[layer],
            dt_bias[layer], gdn_norm_w[layer], bs * SAMPLE_PAD)
        o_s = o_s.reshape(bs, SAMPLE_PAD, D_HEADS)[:, :n_q].transpose(1, 0, 2).reshape(ts, D_HEADS)
        gdn_o = jnp.concatenate([o_p, o_s])

        w_o = w_out[layer].astype(BF16)
        i = layer // 2
        if layer % 2 == 0:
            x, u = _res_ln([att, gdn_o], [w_o[:D_HEADS], w_o[D_HEADS:]], x, mod_all, layer, 2, layer,
                           (4, 3), ln_g[layer, 0], ln_b[layer, 0], alpha, TM, groups(TM))
            hdn = _ffn_up(u, ffn_w_gate[i].astype(BF16), ffn_w_up[i].astype(BF16), TM)
            x, u = _res_ln([hdn], [ffn_w_down[i].astype(BF16)], x, mod_all, layer, 5, layer + 1,
                           (1, 0), ln_g[layer, 1], ln_b[layer, 1], alpha, TM, groups(TM))
        else:
            w_r = jnp.pad(moe_router[i], ((0, 0), (0, LANES - N_EXPERTS)))
            x, u, gates = _res_ln([att, gdn_o], [w_o[:D_HEADS], w_o[D_HEADS:]], x, mod_all, layer, 2,
                                  layer, (4, 3), ln_g[layer, 0], ln_b[layer, 0], alpha, TM,
                                  groups(TM), w_router=w_r)
            x = _moe_dense(u, gates, moe_w_gate[i], moe_w_up[i], moe_w_down[i], x, mod_all, layer, 5,
                           ln_g[layer, 1], ln_b[layer, 1], alpha, TM_MOE, TJ_MOE, groups(TM_MOE))

        k_p.append(proj[:tp, D_HEADS:2 * D_HEADS].reshape(bp, seq, h, hd))
        v_p.append(proj[:tp, 2 * D_HEADS:3 * D_HEADS].reshape(bp, seq, h, hd))
        conv_p.append(conv_new_p)
        gdn_p.append(s_new_p)
        k_s.append(k_sr)
        v_s.append(v_sr)
        conv_s.append(conv_new_s)
        gdn_s.append(s_new_s)

    y_prompt = x[:tp].reshape(bp, seq, d)
    y_sample = x[tp:].reshape(n_q, bs, d).transpose(1, 0, 2)
    return (y_prompt, y_sample, jnp.stack(k_p), jnp.stack(v_p), jnp.stack(conv_p), jnp.stack(gdn_p),
            jnp.stack(k_s), jnp.stack(v_s), jnp.stack(conv_s), jnp.stack(gdn_s))
```

```python
import functools
import math

import numpy as np
import jax
import jax.numpy as jnp
from jax import lax
from jax.experimental import pallas as pl
from jax.experimental.pallas import tpu as pltpu

F32 = jnp.float32
BF16 = jnp.bfloat16

D_MODEL = 1024
HEAD_DIM = 64
N_HEADS = 8
D_HEADS = N_HEADS * HEAD_DIM
D_CONV = 3 * D_HEADS
D_IN = 3 * D_HEADS + D_CONV + D_HEADS + 2 * N_HEADS
D_IN_MAIN = D_IN - 2 * N_HEADS
LANES = 128
D_IN_PAD = D_IN_MAIN + LANES
MOBA_BLOCK = 256
MOBA_TOPK = 3
PAGE_SIZE = 128
GDN_CONV = 4
GDN_CHUNK = 64
N_BUCKETS = 32
MAX_DISTANCE = 128
N_EXPERTS = 8
TOP_K = 2
LN_EPS = 1e-5
NEG = -1e30
MOD_ROWS = 128
VMEM_LIMIT = 56 * 1024 * 1024

HIGHEST = lax.Precision.HIGHEST


def _cparams(n_axes):
    return pltpu.CompilerParams(dimension_semantics=("arbitrary",) * n_axes,
                                vmem_limit_bytes=VMEM_LIMIT)


def _silu(x):
    return x * (1.0 / (1.0 + jnp.exp(-x)))


def _layer_norm(x, g, b):
    mu = jnp.mean(x, axis=-1, keepdims=True)
    xc = x - mu
    var = jnp.mean(xc * xc, axis=-1, keepdims=True)
    return xc * lax.rsqrt(var + LN_EPS) * g + b


def _modulate(x, scale, shift):
    tm, d = x.shape
    x3 = x.reshape(tm // MOD_ROWS, MOD_ROWS, d)
    return (x3 * (1.0 + scale[None]) + shift[None]).reshape(tm, d)


def _gated(x, gate, h):
    tm, d = x.shape
    x3 = x.reshape(tm // MOD_ROWS, MOD_ROWS, d)
    h3 = h.reshape(tm // MOD_ROWS, MOD_ROWS, d)
    return (x3 + (1.0 + gate[None]) * h3).reshape(tm, d)


def _mod_kernel(c_ref, w_ref, b_ref, o_ref):
    a = _silu(c_ref[...])
    o_ref[...] = jnp.dot(a, w_ref[...], precision=HIGHEST, preferred_element_type=F32) + b_ref[...]


def _mod_vectors(c_all, w_mod, b_mod, tn=1536):
    depth, d, n = w_mod.shape
    rows = c_all.shape[0]
    return pl.pallas_call(
        _mod_kernel,
        grid=(depth, n // tn),
        in_specs=[pl.BlockSpec((rows, d), lambda l, j: (0, 0)),
                  pl.BlockSpec((None, d, tn), lambda l, j: (l, 0, j)),
                  pl.BlockSpec((None, 1, tn), lambda l, j: (l, 0, j))],
        out_specs=pl.BlockSpec((None, rows, tn), lambda l, j: (l, 0, j)),
        out_shape=jax.ShapeDtypeStruct((depth, rows, n), F32),
        compiler_params=_cparams(2),
        name="mod_vectors",
    )(c_all, w_mod, b_mod.reshape(depth, 1, n))


def _ln_mod_kernel(x_ref, g_ref, b_ref, mod_ref, xn_ref, u_ref):
    xn = _layer_norm(x_ref[...], g_ref[...], b_ref[...])
    xn_ref[...] = xn
    u_ref[...] = _modulate(xn, mod_ref[1], mod_ref[0]).astype(BF16)


def _mod_spec(layer, group_of_tile):
    return pl.BlockSpec((None, 6, None, MOD_ROWS, D_MODEL),
                        lambda i: (layer, 0, group_of_tile(i), 0, 0))


def _ln_mod(x, g, b, mod_all, layer, tm, group_of_tile):
    t, d = x.shape
    row = pl.BlockSpec((tm, d), lambda i: (i, 0))
    vec = pl.BlockSpec((1, d), lambda i: (0, 0))
    return pl.pallas_call(
        _ln_mod_kernel,
        grid=(t // tm,),
        in_specs=[row, vec, vec, _mod_spec(layer, group_of_tile)],
        out_specs=[row, row],
        out_shape=[jax.ShapeDtypeStruct((t, d), F32), jax.ShapeDtypeStruct((t, d), BF16)],
        compiler_params=_cparams(1),
        name="ln_mod",
    )(x, g.reshape(1, d), b.reshape(1, d), mod_all)


def _proj_kernel(u_ref, w_ref, proj_ref, qkv_ref, *, chunk):
    u = u_ref[...]
    n = w_ref.shape[1]
    n_qkv = qkv_ref.shape[1]
    for c0 in range(0, n, chunk):
        c1 = min(c0 + chunk, n)
        r = jnp.dot(u, w_ref[:, c0:c1], preferred_element_type=F32)
        proj_ref[:, c0:c1] = r
        if c1 <= n_qkv:
            qkv_ref[:, c0:c1] = r.astype(BF16)


def _proj_in(u, w, tm, chunk=512):
    t, d = u.shape
    n = w.shape[1]
    return pl.pallas_call(
        functools.partial(_proj_kernel, chunk=chunk),
        grid=(t // tm,),
        in_specs=[pl.BlockSpec((tm, d), lambda i: (i, 0)),
                  pl.BlockSpec((d, n), lambda i: (0, 0))],
        out_specs=[pl.BlockSpec((tm, n), lambda i: (i, 0)),
                   pl.BlockSpec((tm, 3 * D_HEADS), lambda i: (i, 0))],
        out_shape=[jax.ShapeDtypeStruct((t, n), F32),
                   jax.ShapeDtypeStruct((t, 3 * D_HEADS), BF16)],
        compiler_params=_cparams(1),
        name="proj_in",
    )(u, w)


def _top2_gates(logits):
    lane = lax.broadcasted_iota(jnp.int32, logits.shape, 1)
    v1 = jnp.max(logits, axis=-1, keepdims=True)
    i1 = jnp.min(jnp.where(logits == v1, lane, LANES), axis=-1, keepdims=True)
    rest = jnp.where(lane == i1, -jnp.inf, logits)
    v2 = jnp.max(rest, axis=-1, keepdims=True)
    i2 = jnp.min(jnp.where(rest == v2, lane, LANES), axis=-1, keepdims=True)
    e2 = jnp.exp(v2 - v1)
    inv = 1.0 / (1.0 + e2)
    return jnp.where(lane == i1, inv, 0.0) + jnp.where(lane == i2, e2 * inv, 0.0)


def _res_ln_kernel(*refs, n_in, alpha, next_rows, with_router):
    a_refs = refs[:n_in]
    w_refs = refs[n_in:2 * n_in]
    x_ref, modg_ref, modn_ref, g_ref, b_ref = refs[2 * n_in:2 * n_in + 5]
    pos = 2 * n_in + 5
    if with_router:
        wr_ref = refs[pos]
        pos += 1
    outs = refs[pos:]
    h = jnp.dot(a_refs[0][...], w_refs[0][...], preferred_element_type=F32)
    for a_ref, w_ref in zip(a_refs[1:], w_refs[1:]):
        h = h + jnp.dot(a_ref[...], w_ref[...], preferred_element_type=F32)
    xn = _layer_norm(_gated(alpha * x_ref[...], modg_ref[...], h), g_ref[...], b_ref[...])
    outs[0][...] = xn
    if next_rows is not None:
        u = _modulate(xn, modn_ref[next_rows[0]], modn_ref[next_rows[1]])
        outs[1][...] = u.astype(BF16)
        if with_router:
            logits = jnp.dot(u, wr_ref[...], precision=HIGHEST, preferred_element_type=F32)
            lane = lax.broadcasted_iota(jnp.int32, logits.shape, 1)
            outs[2][...] = _top2_gates(jnp.where(lane < N_EXPERTS, logits, -jnp.inf))


def _res_ln(a_list, w_list, x, mod_all, gate_layer, gate_row, next_layer, next_rows, ln_g, ln_b,
            alpha, tm, group_of_tile, w_router=None):
    t, d = x.shape
    n_in = len(a_list)
    row = pl.BlockSpec((tm, d), lambda i: (i, 0))
    vec = pl.BlockSpec((1, d), lambda i: (0, 0))
    in_specs = [pl.BlockSpec((tm, a.shape[1]), lambda i: (i, 0)) for a in a_list]
    in_specs += [pl.BlockSpec(w.shape, lambda i: (0, 0)) for w in w_list]
    in_specs += [row,
                 pl.BlockSpec((None, None, None, MOD_ROWS, d),
                              lambda i: (gate_layer, gate_row, group_of_tile(i), 0, 0)),
                 _mod_spec(next_layer if next_rows is not None else gate_layer, group_of_tile),
                 vec, vec]
    args = list(a_list) + list(w_list) + [x, mod_all, mod_all, ln_g.reshape(1, d), ln_b.reshape(1, d)]
    out_specs = [row]
    out_shape = [jax.ShapeDtypeStruct((t, d), F32)]
    if next_rows is not None:
        out_specs.append(row)
        out_shape.append(jax.ShapeDtypeStruct((t, d), BF16))
    with_router = w_router is not None
    if with_router:
        in_specs.append(pl.BlockSpec(w_router.shape, lambda i: (0, 0)))
        args.append(w_router)
        out_specs.append(pl.BlockSpec((tm, LANES), lambda i: (i, 0)))
        out_shape.append(jax.ShapeDtypeStruct((t, LANES), F32))
    return pl.pallas_call(
        functools.partial(_res_ln_kernel, n_in=n_in, alpha=alpha, next_rows=next_rows,
                          with_router=with_router),
        grid=(t // tm,),
        in_specs=in_specs,
        out_specs=out_specs,
        out_shape=out_shape,
        compiler_params=_cparams(1),
        name="res_ln",
    )(*args)


def _ffn_up_kernel(u_ref, wg_ref, wu_ref, o_ref, *, chunk):
    u = u_ref[...]
    n = o_ref.shape[1]
    for c0 in range(0, n, chunk):
        c1 = min(c0 + chunk, n)
        hg = jnp.dot(u, wg_ref[:, c0:c1], preferred_element_type=F32)
        hu = jnp.dot(u, wu_ref[:, c0:c1], preferred_element_type=F32)
        o_ref[:, c0:c1] = (_silu(hg) * hu).astype(BF16)


def _ffn_up(u, wg, wu, tm, chunk=256):
    t, d = u.shape
    n = wg.shape[1]
    wspec = pl.BlockSpec((d, n), lambda i: (0, 0))
    return pl.pallas_call(
        functools.partial(_ffn_up_kernel, chunk=chunk),
        grid=(t // tm,),
        in_specs=[pl.BlockSpec((tm, d), lambda i: (i, 0)), wspec, wspec],
        out_specs=pl.BlockSpec((tm, n), lambda i: (i, 0)),
        out_shape=jax.ShapeDtypeStruct((t, n), BF16),
        compiler_params=_cparams(1),
        name="ffn_up",
    )(u, wg, wu)


def _moe_kernel(u_ref, gates_ref, wg_ref, wu_ref, wd_ref, x_ref, modg_ref, g_ref, b_ref,
                o_ref, acc_ref, *, alpha):
    e = pl.program_id(1)
    j = pl.program_id(2)

    @pl.when((e == 0) & (j == 0))
    def _():
        acc_ref[...] = jnp.zeros_like(acc_ref)

    u = u_ref[...]
    hg = jnp.dot(u, wg_ref[...].astype(BF16), preferred_element_type=F32)
    hu = jnp.dot(u, wu_ref[...].astype(BF16), preferred_element_type=F32)
    gates = gates_ref[...]
    lane = lax.broadcasted_iota(jnp.int32, gates.shape, 1)
    gate_e = jnp.sum(jnp.where(lane == e, gates, 0.0), axis=-1, keepdims=True)
    hh = (_silu(hg) * hu * gate_e).astype(BF16)
    acc_ref[...] += jnp.dot(hh, wd_ref[...].astype(BF16), preferred_element_type=F32)

    @pl.when((e == pl.num_programs(1) - 1) & (j == pl.num_programs(2) - 1))
    def _():
        o_ref[...] = _layer_norm(_gated(alpha * x_ref[...], modg_ref[...], acc_ref[...]),
                                 g_ref[...], b_ref[...])


def _moe_dense(u, gates, wg, wu, wd, x, mod_all, gate_layer, gate_row, ln_g, ln_b, alpha,
               tm, tj, group_of_tile):
    t, d = x.shape
    n_e, _, f = wg.shape
    row = pl.BlockSpec((tm, d), lambda i, e, j: (i, 0))
    vec = pl.BlockSpec((1, d), lambda i, e, j: (0, 0))
    return pl.pallas_call(
        functools.partial(_moe_kernel, alpha=alpha),
        grid=(t // tm, n_e, f // tj),
        in_specs=[row,
                  pl.BlockSpec((tm, LANES), lambda i, e, j: (i, 0)),
                  pl.BlockSpec((None, d, tj), lambda i, e, j: (e, 0, j)),
                  pl.BlockSpec((None, d, tj), lambda i, e, j: (e, 0, j)),
                  pl.BlockSpec((None, tj, d), lambda i, e, j: (e, j, 0)),
                  row,
                  pl.BlockSpec((None, None, None, MOD_ROWS, d),
                               lambda i, e, j: (gate_layer, gate_row, group_of_tile(i), 0, 0)),
                  vec, vec],
        out_specs=row,
        out_shape=jax.ShapeDtypeStruct((t, d), F32),
        scratch_shapes=[pltpu.VMEM((tm, d), F32)],
        compiler_params=_cparams(3),
        name="moe_dense",
    )(u, gates, wg, wu, wd, x, mod_all, ln_g.reshape(1, d), ln_b.reshape(1, d))


def _rel_bucket_table(max_dist):
    n = np.arange(max_dist + 1)
    max_exact = N_BUCKETS // 2
    ratio = np.log(np.maximum(n, 1).astype(np.float32) / np.float32(max_exact))
    large = max_exact + (ratio / np.float32(math.log(MAX_DISTANCE / max_exact))
                         * np.float32(N_BUCKETS - max_exact)).astype(np.int32)
    large = np.minimum(large, N_BUCKETS - 1)
    return np.where(n < max_exact, n, large).astype(np.int32)


def _bias_lookup(rel_table, bucket_idx):
    onehot = jnp.asarray(np.eye(N_BUCKETS, dtype=np.float32)[bucket_idx.reshape(-1)])
    vals = jnp.dot(onehot, rel_table, precision=HIGHEST)
    return vals.T.reshape((rel_table.shape[1],) + bucket_idx.shape)


def _select_top_blocks(gate, n_valid, ksel):
    lane = lax.broadcasted_iota(jnp.int32, gate.shape, 1)
    nb = gate.shape[1]
    g = jnp.where(lane < n_valid, gate, -jnp.inf)
    keep = lane == n_valid
    for _ in range(ksel):
        mx = jnp.max(g, axis=-1, keepdims=True)
        first = jnp.min(jnp.where(g == mx, lane, nb), axis=-1, keepdims=True)
        hit = lane == first
        keep = keep | (hit & (lane < n_valid))
        g = jnp.where(hit, -jnp.inf, g)
    return jnp.where(keep, 0.0, NEG)


def _moba_prompt_kernel(q_ref, k_ref, v_ref, bias_ref, o_ref,
                        kmean_ref, qm_ref, sel_ref, m_ref, l_ref, acc_ref, *, ksel):
    qt = pl.program_id(1)
    seq = k_ref.shape[0]
    nb = seq // MOBA_BLOCK
    gw = 4 * HEAD_DIM
    n_groups = N_HEADS // 4

    @pl.when(qt == 0)
    def _():
        row = lax.broadcasted_iota(jnp.int32, (nb, seq), 0)
        col = lax.broadcasted_iota(jnp.int32, (nb, seq), 1)
        pool = jnp.where(col // MOBA_BLOCK == row, 1.0, 0.0).astype(BF16)
        kmean_ref[...] = jnp.dot(pool, k_ref[...], preferred_element_type=F32) * (1.0 / MOBA_BLOCK)

    head_of_lane = lax.broadcasted_iota(jnp.int32, (1, gw), 1) // HEAD_DIM
    q = q_ref[...] * (HEAD_DIM ** -0.5)
    kmean = kmean_ref[...].astype(BF16)
    for h in range(N_HEADS):
        g, hh = divmod(h, 4)
        qm = jnp.where(head_of_lane == hh, q[:, g * gw:(g + 1) * gw], 0.0).astype(BF16)
        qm_ref[h] = qm
        gate = lax.dot_general(qm, kmean[:, g * gw:(g + 1) * gw], (((1,), (1,)), ((), ())),
                               preferred_element_type=F32)
        sel_ref[h] = _select_top_blocks(gate, qt, ksel)
        m_ref[h] = jnp.full(m_ref.shape[1:], NEG, F32)
        l_ref[h] = jnp.zeros(l_ref.shape[1:], F32)
    acc_ref[...] = jnp.zeros_like(acc_ref)

    def body(step, carry):
        n = qt - step
        start = pl.multiple_of(n * MOBA_BLOCK, MOBA_BLOCK)
        kb = k_ref[pl.ds(start, MOBA_BLOCK), :]
        vb = v_ref[pl.ds(start, MOBA_BLOCK), :]
        bias_idx = jnp.minimum(step, 2)
        for h in range(N_HEADS):
            g, hh = divmod(h, 4)
            s = lax.dot_general(qm_ref[h], kb[:, g * gw:(g + 1) * gw], (((1,), (1,)), ((), ())),
                                preferred_element_type=F32)
            sel = sel_ref[h]
            lane = lax.broadcasted_iota(jnp.int32, sel.shape, 1)
            sel_col = jnp.sum(jnp.where(lane == n, sel, 0.0), axis=-1, keepdims=True)
            s = s + bias_ref[h, bias_idx] + sel_col
            m_old = m_ref[h]
            m_new = jnp.maximum(m_old, jnp.max(s, axis=-1, keepdims=True))
            alpha = jnp.exp(m_old - m_new)
            p = jnp.exp(s - m_new)
            l_ref[h] = alpha * l_ref[h] + jnp.sum(p, axis=-1, keepdims=True)
            m_ref[h] = m_new
            pv = jnp.dot(p.astype(BF16), vb[:, g * gw:(g + 1) * gw], preferred_element_type=F32)
            acc = acc_ref[g]
            acc_ref[g] = jnp.where(head_of_lane == hh, alpha * acc + pv, acc)
        return carry

    lax.fori_loop(0, qt + 1, body, 0)

    for g in range(n_groups):
        inv = jnp.zeros((q.shape[0], gw), F32)
        for hh in range(4):
            inv = jnp.where(head_of_lane == hh, 1.0 / l_ref[4 * g + hh], inv)
        o_ref[:, g * gw:(g + 1) * gw] = (acc_ref[g] * inv).astype(o_ref.dtype)


def _moba_bias_tables(rel_table):
    bucket = _rel_bucket_table(2 * MOBA_BLOCK)
    i = np.arange(MOBA_BLOCK)[:, None]
    j = np.arange(MOBA_BLOCK)[None, :]
    own = jnp.where(jnp.asarray(i >= j), _bias_lookup(rel_table, bucket[np.maximum(i - j, 0)]), NEG)
    prev = _bias_lookup(rel_table, bucket[MOBA_BLOCK + i - j])
    far = jnp.broadcast_to(rel_table[N_BUCKETS - 1][:, None, None], prev.shape)
    return jnp.stack([own, prev, far], axis=1).astype(F32)


def _moba_prompt(qkv, bias_tab, batch, seq):
    nb = seq // MOBA_BLOCK
    ksel = min(MOBA_TOPK, nb)
    stat = pltpu.VMEM((N_HEADS, MOBA_BLOCK, 1), F32)
    return pl.pallas_call(
        functools.partial(_moba_prompt_kernel, ksel=ksel),
        grid=(batch, nb),
        in_specs=[pl.BlockSpec((MOBA_BLOCK, D_HEADS), lambda b, t: (b * nb + t, 0)),
                  pl.BlockSpec((seq, D_HEADS), lambda b, t: (b, 1)),
                  pl.BlockSpec((seq, D_HEADS), lambda b, t: (b, 2)),
                  pl.BlockSpec(bias_tab.shape, lambda b, t: (0, 0, 0, 0))],
        out_specs=pl.BlockSpec((MOBA_BLOCK, D_HEADS), lambda b, t: (b * nb + t, 0)),
        out_shape=jax.ShapeDtypeStruct((batch * seq, D_HEADS), BF16),
        scratch_shapes=[pltpu.VMEM((nb, D_HEADS), F32),
                        pltpu.VMEM((N_HEADS, MOBA_BLOCK, 4 * HEAD_DIM), BF16),
                        pltpu.VMEM((N_HEADS, MOBA_BLOCK, nb), F32),
                        stat, stat,
                        pltpu.VMEM((N_HEADS // 4, MOBA_BLOCK, 4 * HEAD_DIM), F32)],
        compiler_params=_cparams(2),
        name="moba_prompt",
    )(qkv, qkv, qkv, bias_tab)


def _moba_sample_kernel(pt_ref, q_ref, kn_ref, vn_ref, ka_ref, kb_ref, va_ref, vb_ref,
                        bias_ref, bown_ref, o_ref, m_ref, l_ref, g_ref, acc_ref, *, ksel, n_q):
    del pt_ref
    j = pl.program_id(1)
    nbp = pl.num_programs(1)
    nt = (((1,), (1,)), ((), ()))
    dh = N_HEADS * HEAD_DIM
    q8 = q_ref[...]
    slots = q8.shape[0]
    lane_head = lax.broadcasted_iota(jnp.int32, (1, dh), 1) // HEAD_DIM
    qbd = jnp.concatenate([jnp.where(lane_head == hp, q8, 0.0) for hp in range(N_HEADS)], axis=0)
    qs = (qbd * (HEAD_DIM ** -0.5)).astype(BF16)

    ones = jnp.ones((8, PAGE_SIZE), BF16)
    s_parts, acc, ksum = [], None, None
    for k_ref, v_ref in ((ka_ref, va_ref), (kb_ref, vb_ref)):
        kt = k_ref[...].reshape(dh, PAGE_SIZE)
        kt_hi = kt.astype(BF16)
        kt_lo = (kt - kt_hi.astype(F32)).astype(BF16)
        s_parts.append(jnp.dot(qs, kt_hi, preferred_element_type=F32))
        part = (lax.dot_general(ones, kt_hi, nt, preferred_element_type=F32)
                + lax.dot_general(ones, kt_lo, nt, preferred_element_type=F32))
        ksum = part if ksum is None else ksum + part
    s = jnp.concatenate(s_parts, axis=1) + bias_ref[jnp.where(j == nbp - 1, 0, 1)]
    m = jnp.max(s, axis=-1, keepdims=True)
    p32 = jnp.exp(s - m)
    p = p32.astype(BF16)
    for idx, v_ref in enumerate((va_ref, vb_ref)):
        vt = v_ref[...].reshape(dh, PAGE_SIZE).astype(BF16)
        part = lax.dot_general(p[:, idx * PAGE_SIZE:(idx + 1) * PAGE_SIZE], vt, nt,
                               preferred_element_type=F32)
        acc = part if acc is None else acc + part
    m_ref[j] = m
    l_ref[j] = jnp.sum(p32, axis=-1, keepdims=True)
    acc_ref[j] = acc
    g_ref[j] = jnp.sum(qbd * ksum[0:1, :], axis=-1, keepdims=True) * (1.0 / MOBA_BLOCK)

    @pl.when(j == nbp - 1)
    def _():
        s_own = [jnp.sum(qbd * kn_ref[i:i + 1, :], axis=-1, keepdims=True) * (HEAD_DIM ** -0.5)
                 + bown_ref[:, i:i + 1] for i in range(n_q)]
        m_o = functools.reduce(jnp.maximum, s_own)
        p_own = [jnp.exp(s_i - m_o) for s_i in s_own]
        l_o = functools.reduce(jnp.add, p_own)
        acc_o = functools.reduce(jnp.add, [p_i * vn_ref[i:i + 1, :] for i, p_i in enumerate(p_own)])

        gg = g_ref[...]
        blk = lax.broadcasted_iota(jnp.int32, gg.shape, 0)
        keep = blk < 0
        for _ in range(ksel):
            mx = jnp.max(gg, axis=0, keepdims=True)
            first = jnp.min(jnp.where(gg == mx, blk, nbp), axis=0, keepdims=True)
            hit = (blk == first) & (gg > -jnp.inf)
            keep = keep | hit
            gg = jnp.where(blk == first, -jnp.inf, gg)
        m_all = m_ref[...]
        m_tot = jnp.maximum(jnp.max(jnp.where(keep, m_all, NEG), axis=0), m_o)
        w = jnp.where(keep, jnp.exp(m_all - m_tot[None]), 0.0)
        w_o = jnp.exp(m_o - m_tot)
        l_tot = jnp.sum(w * l_ref[...], axis=0) + w_o * l_o
        acc_tot = (jnp.sum(w * acc_ref[...], axis=0) + w_o * acc_o) * (1.0 / l_tot)
        row_head = lax.broadcasted_iota(jnp.int32, (N_HEADS * slots, 1), 0) // slots
        acc_tot = jnp.where(row_head == lane_head, acc_tot, 0.0)
        o_ref[...] = functools.reduce(
            jnp.add, [acc_tot[:, hp * HEAD_DIM:(hp + 1) * HEAD_DIM] for hp in range(N_HEADS)])


def _moba_sample_bias(rel_table, past_len, n_q):
    del past_len
    slots = SAMPLE_PAD
    bucket = _rel_bucket_table(MOBA_BLOCK + slots)
    rows_h = np.repeat(np.arange(N_HEADS), slots)
    rows_i = np.minimum(np.tile(np.arange(slots), N_HEADS), n_q - 1)
    pos = np.arange(MOBA_BLOCK)[None, :]
    by_head = _bias_lookup(rel_table, bucket[MOBA_BLOCK + rows_i[:, None] - pos])
    pick = jnp.asarray(np.arange(N_HEADS)[:, None, None] == rows_h[None, :, None])
    near = jnp.sum(jnp.where(pick, by_head, 0.0), axis=0)
    far = jnp.broadcast_to(jnp.repeat(rel_table[N_BUCKETS - 1], slots)[:, None], near.shape)
    past = jnp.stack([near, far]).astype(F32)
    new = np.arange(slots)[None, :]
    own_h = _bias_lookup(rel_table, bucket[np.maximum(rows_i[:, None] - new, 0)])
    own = jnp.sum(jnp.where(pick, own_h, 0.0), axis=0)
    ok = jnp.asarray((new <= rows_i[:, None]) & (new < n_q))
    return past, jnp.where(ok, own, NEG).astype(F32)


def _moba_sample(q_rows, k_new, v_new, cache_k, cache_v, page_table, layer, bias_past, bias_own, n_q):
    bsz, slots, dh = q_rows.shape
    hd = HEAD_DIM
    rows = N_HEADS * slots
    pages_per_block = MOBA_BLOCK // PAGE_SIZE
    nbp = page_table.shape[1] // pages_per_block
    ksel = min(MOBA_TOPK, nbp + 1)

    def page_spec(which):
        return pl.BlockSpec((None, None, N_HEADS, hd, PAGE_SIZE),
                            lambda b, j, pt: (layer, pt[b, pages_per_block * j + which], 0, 0, 0))

    per_seq = lambda b, j, pt: (b, 0, 0)
    stat = pltpu.VMEM((nbp, rows, 1), F32)
    grid_spec = pltpu.PrefetchScalarGridSpec(
        num_scalar_prefetch=1,
        grid=(bsz, nbp),
        in_specs=[pl.BlockSpec((None, slots, dh), per_seq),
                  pl.BlockSpec((None, slots, dh), per_seq),
                  pl.BlockSpec((None, slots, dh), per_seq),
                  page_spec(0), page_spec(1), page_spec(0), page_spec(1),
                  pl.BlockSpec(bias_past.shape, lambda b, j, pt: (0, 0, 0)),
                  pl.BlockSpec(bias_own.shape, lambda b, j, pt: (0, 0))],
        out_specs=pl.BlockSpec((None, rows, hd), per_seq),
        scratch_shapes=[stat, stat, stat, pltpu.VMEM((nbp, rows, dh), F32)])
    return pl.pallas_call(
        functools.partial(_moba_sample_kernel, ksel=ksel, n_q=n_q),
        grid_spec=grid_spec,
        out_shape=jax.ShapeDtypeStruct((bsz, rows, hd), F32),
        compiler_params=_cparams(2),
        name="moba_sample",
    )(page_table, q_rows, k_new, v_new, cache_k, cache_k, cache_v, cache_v, bias_past, bias_own)


def _softplus(x):
    return jnp.maximum(x, 0.0) + jnp.log1p(jnp.exp(-jnp.abs(x)))


def _dot_bf16(a, b):
    return jnp.dot(a.astype(BF16), b.astype(BF16), preferred_element_type=F32)


def _dot_nt_bf16(a, b):
    return lax.dot_general(a.astype(BF16), b.astype(BF16), (((1,), (1,)), ((), ())),
                           preferred_element_type=F32)


def _dot_f32(a, b):
    return jnp.dot(a, b, precision=HIGHEST, preferred_element_type=F32)


def _split_bf16(a):
    hi = a.astype(BF16)
    return hi, (a - hi.astype(F32)).astype(BF16)


def _dot_3pass(a, b):
    a_hi, a_lo = _split_bf16(a)
    b_hi, b_lo = _split_bf16(b)
    return (jnp.dot(a_hi, b_hi, preferred_element_type=F32)
            + jnp.dot(a_hi, b_lo, preferred_element_type=F32)
            + jnp.dot(a_lo, b_hi, preferred_element_type=F32))


def _unit_lower_inverse(a, row, col):
    c = a.shape[0]
    eye = jnp.where(row == col, 1.0, 0.0)
    in16 = (row // 16) == (col // 16)
    in32 = (row // 32) == (col // 32)
    nil = jnp.where(in16, -a, 0.0)
    x = eye + nil
    p = _dot_3pass(nil, nil)
    for _ in range(2):
        xp = _dot_3pass(jnp.concatenate([x, p], axis=0), p)
        x = x + xp[:c]
        p = xp[c:]
    x = x + _dot_3pass(x, p)
    for off in (jnp.where(in32 & jnp.logical_not(in16), a, 0.0),
                jnp.where(in32, 0.0, a)):
        x = x - _dot_bf16(x, _dot_bf16(off, x))
    return x


def _gdn_prepare(q, k, v, k_t, g_col, g_row, beta, row, col):
    c = q.shape[0]
    incl = row >= col
    decay = jnp.where(incl, jnp.exp(jnp.where(incl, g_col - g_row, 0.0)), 0.0)
    qk_kk = _dot_nt_bf16(jnp.concatenate([q, k], axis=0), k)
    a_mat = jnp.where(row > col, qk_kk[c:] * decay * beta, 0.0)
    e_g = jnp.exp(g_col)
    rhs = jnp.concatenate([v * beta, k * beta * e_g], axis=1)
    sol = _dot_bf16(_unit_lower_inverse(a_mat, row, col), rhs)
    dv = v.shape[1]
    g_last = g_col[c - 1:c, :]
    w_qd = jnp.concatenate([sol[:, dv:], q * e_g], axis=0).astype(BF16)
    qk = jnp.where(incl, qk_kk[:c] * decay, 0.0).astype(BF16)
    k_dec_t = (k_t * jnp.exp(g_last - g_row)).astype(BF16)
    return sol[:, :dv], w_qd, qk, k_dec_t, jnp.exp(g_last)


def _gdn_step(u, w_qd, qk, k_dec_t, chunk_decay, state):
    c = u.shape[0]
    ws_qs = jnp.dot(w_qd, state.astype(BF16), preferred_element_type=F32)
    v_new = (u - ws_qs[:c]).astype(BF16)
    o = ws_qs[c:] + jnp.dot(qk, v_new, preferred_element_type=F32)
    new_state = state * chunk_decay + jnp.dot(k_dec_t, v_new, preferred_element_type=F32)
    return o, new_state


def _gdn_prompt_kernel(x_ref, z_ref, gab_ref, cw_ref, alog_ref, dtb_ref, nw_ref, seg_ref,
                       o_ref, conv_ref, sout_ref,
                       xbuf_ref, halo_ref, s_ref, u_ref, wqd_ref, qk_ref, kdt_ref, cd_ref, oh_ref):
    t = pl.program_id(1)
    tl = x_ref.shape[0]
    cs = GDN_CHUNK
    n_chunks = tl // cs
    hd = HEAD_DIM

    @pl.when(t == 0)
    def _():
        halo_ref[...] = jnp.zeros_like(halo_ref)
        s_ref[...] = jnp.zeros_like(s_ref)

    x = x_ref[...]
    xbuf_ref[0:8, :] = halo_ref[...]
    xbuf_ref[8:8 + tl, :] = x
    y = cw_ref[0:1, :] * xbuf_ref[5:5 + tl, :]
    for i in range(1, GDN_CONV):
        y = y + cw_ref[i:i + 1, :] * xbuf_ref[5 + i:5 + i + tl, :]
    y = _silu(y)
    tail = x[tl - 8:, :]
    halo_ref[...] = tail
    conv_ref[...] = tail[8 - (GDN_CONV - 1):, :]

    dh = N_HEADS * hd
    seg = seg_ref[...]

    def l2n(a):
        sq_hi, sq_lo = _split_bf16(a * a)
        ss = (jnp.dot(sq_hi, seg, preferred_element_type=F32)
              + jnp.dot(sq_lo, seg, preferred_element_type=F32))
        return a * lax.rsqrt(ss + 1e-6)

    qn = l2n(y[:, :dh]) * (hd ** -0.5)
    kn = l2n(y[:, dh:2 * dh])
    vv = y[:, 2 * dh:]
    kn_t = kn.T

    gab = gab_ref[...]
    beta = 1.0 / (1.0 + jnp.exp(-gab))
    glog = -jnp.exp(alog_ref[...]) * _softplus(gab + dtb_ref[...])

    row = lax.broadcasted_iota(jnp.int32, (cs, cs), 0)
    col = lax.broadcasted_iota(jnp.int32, (cs, cs), 1)
    tril = jnp.where(row >= col, 1.0, 0.0)
    pick = jnp.where(lax.broadcasted_iota(jnp.int32, (N_HEADS, LANES), 1)
                     == lax.broadcasted_iota(jnp.int32, (N_HEADS, LANES), 0) + N_HEADS, 1.0, 0.0)
    for c in range(n_chunks):
        rows = slice(c * cs, (c + 1) * cs)
        gcum = _dot_f32(tril, glog[rows])
        g_rows = lax.dot_general(pick, gcum, (((1,), (1,)), ((), ())), precision=HIGHEST,
                                 preferred_element_type=F32)
        for h in range(N_HEADS):
            lanes = slice(h * hd, (h + 1) * hd)
            g_col = jnp.broadcast_to(gcum[:, N_HEADS + h:N_HEADS + h + 1], (cs, cs))
            beta_b = jnp.broadcast_to(beta[rows, h:h + 1], (cs, cs))
            u_c, w_qd, qk, k_dec_t, chunk_decay = _gdn_prepare(
                qn[rows, lanes], kn[rows, lanes], vv[rows, lanes], kn_t[lanes, rows],
                g_col, g_rows[h:h + 1, :], beta_b, row, col)
            u_ref[h, rows, :] = u_c
            wqd_ref[h, c] = w_qd
            qk_ref[h, c] = qk
            kdt_ref[h, c] = k_dec_t
            cd_ref[c, h] = chunk_decay

    def chunk_body(c, carry):
        rows = pl.ds(pl.multiple_of(c * cs, cs), cs)
        for h in range(N_HEADS):
            o, s_new = _gdn_step(u_ref[h, rows, :], wqd_ref[h, c], qk_ref[h, c], kdt_ref[h, c],
                                 cd_ref[c, h], s_ref[h])
            oh_ref[h, rows, :] = o
            s_ref[h] = s_new
        return carry

    lax.fori_loop(0, n_chunks, chunk_body, 0)

    z = z_ref[...]
    nw = nw_ref[...]
    for h in range(N_HEADS):
        lanes = slice(h * hd, (h + 1) * hd)
        o = oh_ref[h]
        o = o * lax.rsqrt(jnp.mean(o * o, axis=-1, keepdims=True) + 1e-6) * nw
        o_ref[:, lanes] = (o * _silu(z[:, lanes])).astype(o_ref.dtype)

    @pl.when(t == pl.num_programs(1) - 1)
    def _():
        sout_ref[...] = s_ref[...]


def _gdn_prompt(proj, batch, seq, tl, conv_w, a_log, dt_bias, norm_w):
    hd = HEAD_DIM
    cs = GDN_CHUNK
    n_steps = seq // tl
    n_chunks = tl // cs
    col_x, col_z, col_gab = 1, D_IN_MAIN // D_HEADS - 1, D_IN_MAIN // LANES
    lane = np.arange(D_HEADS)
    seg = jnp.asarray((lane[:, None] // hd == lane[None, :] // hd).astype(np.float32)).astype(BF16)
    alog_row = jnp.zeros((1, LANES), F32).at[0, N_HEADS:2 * N_HEADS].set(a_log)
    dtb_row = jnp.zeros((1, LANES), F32).at[0, N_HEADS:2 * N_HEADS].set(dt_bias)
    const2 = lambda b, t: (0, 0)
    per_head = pltpu.VMEM((N_HEADS, tl, hd), F32)
    per_chunk = pltpu.VMEM((N_HEADS, n_chunks, hd, cs), BF16)
    return pl.pallas_call(
        _gdn_prompt_kernel,
        grid=(batch, n_steps),
        in_specs=[pl.BlockSpec((tl, D_CONV), lambda b, t: (b * n_steps + t, col_x)),
                  pl.BlockSpec((tl, D_HEADS), lambda b, t: (b * n_steps + t, col_z)),
                  pl.BlockSpec((tl, LANES), lambda b, t: (b * n_steps + t, col_gab)),
                  pl.BlockSpec((GDN_CONV, D_CONV), const2),
                  pl.BlockSpec((1, LANES), const2),
                  pl.BlockSpec((1, LANES), const2),
                  pl.BlockSpec((1, hd), const2),
                  pl.BlockSpec((D_HEADS, D_HEADS), const2)],
        out_specs=[pl.BlockSpec((tl, D_HEADS), lambda b, t: (b * n_steps + t, 0)),
                   pl.BlockSpec((None, GDN_CONV - 1, D_CONV), lambda b, t: (b, 0, 0)),
                   pl.BlockSpec((None, N_HEADS, hd, hd), lambda b, t: (b, 0, 0, 0))],
        out_shape=[jax.ShapeDtypeStruct((batch * seq, D_HEADS), BF16),
                   jax.ShapeDtypeStruct((batch, GDN_CONV - 1, D_CONV), F32),
                   jax.ShapeDtypeStruct((batch, N_HEADS, hd, hd), F32)],
        scratch_shapes=[pltpu.VMEM((8 + tl + 8, D_CONV), F32),
                        pltpu.VMEM((8, D_CONV), F32),
                        pltpu.VMEM((N_HEADS, hd, hd), F32),
                        per_head,
                        pltpu.VMEM((N_HEADS, n_chunks, 2 * cs, hd), BF16),
                        per_chunk, per_chunk,
                        pltpu.VMEM((n_chunks, N_HEADS, 1, hd), F32),
                        per_head],
        compiler_params=_cparams(2),
        name="gdn_prompt",
    )(proj, proj, proj, conv_w, alog_row, dtb_row, norm_w.reshape(1, hd), seg)


def _gdn_sample_kernel(xq_ref, xk_ref, xv_ref, cq_ref, ck_ref, cv_ref, wq_ref, wk_ref, wv_ref,
                       z_ref, gab_ref, alog_ref, dtb_ref, nw_ref, s_ref,
                       o_ref, sout_ref, kq_ref, gt_ref, ot_ref):
    pair = pl.program_id(0)
    n_q = xq_ref.shape[0]
    hd = HEAD_DIM

    def conv_t(x_ref, c_ref, w_ref):
        xp = [c_ref[i] for i in range(GDN_CONV - 1)] + [x_ref[i] for i in range(n_q)]
        out = []
        for i in range(n_q):
            y = w_ref[0:1, :] * xp[i]
            for m in range(1, GDN_CONV):
                y = y + w_ref[m:m + 1, :] * xp[i + m]
            out.append(_silu(y).T)
        return out

    q_t = conv_t(xq_ref, cq_ref, wq_ref)
    k_t = conv_t(xk_ref, ck_ref, wk_ref)
    v_t = conv_t(xv_ref, cv_ref, wv_ref)

    def l2n(a):
        return a * lax.rsqrt(jnp.sum(a * a, axis=0, keepdims=True) + 1e-6)

    for i in range(n_q):
        gab_t = gab_ref[i].T
        gt_ref[0, i] = 1.0 / (1.0 + jnp.exp(-gab_t))
        gt_ref[1, i] = jnp.exp(-jnp.exp(alog_ref[...]) * _softplus(gab_t + dtb_ref[...]))

    for hh in range(2):
        rows = slice(hh * hd, (hh + 1) * hd)
        head = 2 * pair + hh
        for i in range(n_q):
            kq_ref[0] = l2n(k_t[i][rows])
            kq_ref[1] = l2n(q_t[i][rows]) * (hd ** -0.5)
            beta = gt_ref[0, i, pl.ds(head, 1), :]
            decay = gt_ref[1, i, pl.ds(N_HEADS + head, 1), :]
            src = s_ref if i == 0 else sout_ref

            def ks_body(kk, acc):
                return acc + kq_ref[0, pl.ds(kk, 1), :] * src[hh, kk]

            k_s = lax.fori_loop(0, hd, ks_body, jnp.zeros((hd, k_t[i].shape[1]), F32), unroll=8)
            r = beta * (v_t[i][rows] - decay * k_s)

            def upd_body(kk, acc):
                s_new = decay * src[hh, kk] + kq_ref[0, pl.ds(kk, 1), :] * r
                sout_ref[hh, kk] = s_new
                return acc + kq_ref[1, pl.ds(kk, 1), :] * s_new

            o = lax.fori_loop(0, hd, upd_body, jnp.zeros_like(r), unroll=8)
            ot_ref[i, rows, :] = o * lax.rsqrt(jnp.mean(o * o, axis=0, keepdims=True) + 1e-6) * nw_ref[...]

    for i in range(n_q):
        o_ref[i] = (ot_ref[i].T * _silu(z_ref[i])).astype(o_ref.dtype)


def _gdn_sample(proj_s, conv_state, state_t, conv_w, a_log, dt_bias, norm_w):
    n_q, bsz, _ = proj_s.shape
    hd = HEAD_DIM
    pw = 2 * hd
    n_pairs = N_HEADS // 2
    base = D_CONV // pw
    col = jnp.zeros((LANES, 1), F32)
    alog_col = col.at[N_HEADS:2 * N_HEADS, 0].set(a_log)
    dtb_col = col.at[N_HEADS:2 * N_HEADS, 0].set(dt_bias)
    nw_col = norm_w.reshape(hd, 1)

    def xspec(part):
        return pl.BlockSpec((n_q, bsz, pw), lambda p: (0, 0, base + part * n_pairs + p))

    def cspec(rows, part):
        return pl.BlockSpec((rows, bsz, pw) if rows else (GDN_CONV, pw),
                            (lambda p: (0, 0, part * n_pairs + p)) if rows
                            else (lambda p: (0, part * n_pairs + p)))

    const = lambda p: (0, 0)
    state_spec = pl.BlockSpec((2, hd, hd, bsz), lambda p: (p, 0, 0, 0))
    return pl.pallas_call(
        _gdn_sample_kernel,
        grid=(n_pairs,),
        in_specs=[xspec(0), xspec(1), xspec(2),
                  cspec(GDN_CONV - 1, 0), cspec(GDN_CONV - 1, 1), cspec(GDN_CONV - 1, 2),
                  cspec(0, 0), cspec(0, 1), cspec(0, 2),
                  pl.BlockSpec((n_q, bsz, pw), lambda p: (0, 0, D_IN_MAIN // pw - n_pairs + p)),
                  pl.BlockSpec((n_q, bsz, LANES), lambda p: (0, 0, D_IN_MAIN // LANES)),
                  pl.BlockSpec((LANES, 1), const), pl.BlockSpec((LANES, 1), const),
                  pl.BlockSpec((hd, 1), const),
                  state_spec],
        out_specs=[pl.BlockSpec((n_q, bsz, pw), lambda p: (0, 0, p)), state_spec],
        out_shape=[jax.ShapeDtypeStruct((n_q, bsz, D_HEADS), BF16),
                   jax.ShapeDtypeStruct(state_t.shape, F32)],
        scratch_shapes=[pltpu.VMEM((2, hd, bsz), F32),
                        pltpu.VMEM((2, n_q, LANES, bsz), F32),
                        pltpu.VMEM((n_q, pw, bsz), F32)],
        compiler_params=_cparams(1),
        name="gdn_sample",
    )(proj_s, proj_s, proj_s, conv_state, conv_state, conv_state, conv_w, conv_w, conv_w,
      proj_s, proj_s, alog_col, dtb_col, nw_col, state_t)


TM = 256
TM_MOE = 512
TJ_MOE = 512
GDN_ROWS = 256
SAMPLE_PAD = 8


def kernel(x_prompt, x_sample, cache_k, cache_v, state_conv, state_gdn, page_table, c_prompt, c_sample,
           ln_in_g, ln_in_b, w_mod, b_mod, w_in, conv_w, a_log, dt_bias, gdn_norm_w, w_out, rel_table,
           ln_g, ln_b, ffn_w_gate, ffn_w_up, ffn_w_down, moe_router, moe_w_gate, moe_w_up, moe_w_down):
    bp, seq, d = x_prompt.shape
    bs, n_q, _ = x_sample.shape
    depth = w_in.shape[0]
    tp, ts = bp * seq, bs * n_q
    h, hd = N_HEADS, HEAD_DIM
    assert bs == MOD_ROWS and seq % TM_MOE == 0 and ts % TM_MOE == 0 and seq % MOBA_BLOCK == 0
    assert depth == 2 and GDN_CONV - 1 <= n_q <= SAMPLE_PAD and GDN_CHUNK == HEAD_DIM
    alpha = (2 * depth) ** 0.25

    def groups(tm):
        return lambda i: jnp.minimum(i // (seq // tm), bp)

    n_c = bp + bs
    c_all = jnp.pad(jnp.concatenate([c_prompt, c_sample]), ((0, (-n_c) % 8), (0, 0)))
    mod = _mod_vectors(c_all, w_mod, b_mod).reshape(depth, -1, 6, d)
    mod_p = jnp.broadcast_to(mod[:, :bp].transpose(0, 2, 1, 3)[:, :, :, None, :],
                             (depth, 6, bp, MOD_ROWS, d))
    mod_s = mod[:, bp:n_c].transpose(0, 2, 1, 3)[:, :, None]
    mod_all = jnp.concatenate([mod_p, mod_s], axis=2)

    x_all = jnp.concatenate([x_prompt.reshape(tp, d), x_sample.transpose(1, 0, 2).reshape(ts, d)])
    x, u = _ln_mod(x_all, ln_in_g, ln_in_b, mod_all, 0, TM, groups(TM))

    bias_tab = _moba_bias_tables(rel_table)
    past_len = page_table.shape[1] * PAGE_SIZE
    bias_past, bias_own = _moba_sample_bias(rel_table, past_len, n_q)
    cache_kt = cache_k.transpose(0, 1, 3, 4, 2)
    cache_vt = cache_v.transpose(0, 1, 3, 4, 2)
    state_t = state_gdn.transpose(0, 2, 3, 4, 1)
    conv_t = state_conv.transpose(0, 2, 1, 3)

    k_p, v_p, conv_p, gdn_p, k_s, v_s, conv_s, gdn_s = [], [], [], [], [], [], [], []
    for layer in range(depth):
        w_l = w_in[layer]
        w_cat = jnp.concatenate(
            [w_l[:, :D_IN_MAIN], jnp.pad(w_l[:, D_IN_MAIN:], ((0, 0), (0, LANES - 2 * h)))],
            axis=1).astype(BF16)
        proj, qkv = _proj_in(u, w_cat, TM)
        proj_s = proj[tp:].reshape(n_q, bs, -1)

        def slots_s(cols):
            return jnp.pad(proj_s[:, :, cols].transpose(1, 0, 2), ((0, 0), (0, SAMPLE_PAD - n_q), (0, 0)))

        k_sr = proj_s[:, :, D_HEADS:2 * D_HEADS].transpose(1, 0, 2).reshape(bs, n_q, h, hd)
        v_sr = proj_s[:, :, 2 * D_HEADS:3 * D_HEADS].transpose(1, 0, 2).reshape(bs, n_q, h, hd)
        att_p = _moba_prompt(qkv, bias_tab, bp, seq)
        att_s = _moba_sample(slots_s(slice(0, D_HEADS)), slots_s(slice(D_HEADS, 2 * D_HEADS)),
                             slots_s(slice(2 * D_HEADS, 3 * D_HEADS)), cache_kt, cache_vt, page_table,
                             layer, bias_past, bias_own, n_q)
        att_s = att_s.reshape(bs, h, SAMPLE_PAD, hd)[:, :, :n_q].transpose(2, 0, 1, 3)
        att = jnp.concatenate([att_p, att_s.reshape(ts, D_HEADS).astype(BF16)])

        o_p, conv_new_p, s_new_p = _gdn_prompt(proj, bp, seq, GDN_ROWS, conv_w[layer], a_log[layer],
                                               dt_bias[layer], gdn_norm_w[layer])
        o_s, s_new_s = _gdn_sample(proj_s, conv_t[layer], state_t[layer], conv_w[layer], a_log[layer],
                                   dt_bias[layer], gdn_norm_w[layer])
        s_new_s = s_new_s.transpose(3, 0, 1, 2)
        conv_new_s = proj_s[n_q - (GDN_CONV - 1):, :, 3 * D_HEADS:3 * D_HEADS + D_CONV].transpose(1, 0, 2)
        gdn_o = jnp.concatenate([o_p, o_s.reshape(ts, D_HEADS)])

        w_o = w_out[layer].astype(BF16)
        i = layer // 2
        if layer % 2 == 0:
            x, u = _res_ln([att, gdn_o], [w_o[:D_HEADS], w_o[D_HEADS:]], x, mod_all, layer, 2, layer,
                           (4, 3), ln_g[layer, 0], ln_b[layer, 0], alpha, TM, groups(TM))
            hdn = _ffn_up(u, ffn_w_gate[i].astype(BF16), ffn_w_up[i].astype(BF16), TM)
            x, u = _res_ln([hdn], [ffn_w_down[i].astype(BF16)], x, mod_all, layer, 5, layer + 1,
                           (1, 0), ln_g[layer, 1], ln_b[layer, 1], alpha, TM, groups(TM))
        else:
            w_r = jnp.pad(moe_router[i], ((0, 0), (0, LANES - N_EXPERTS)))
            x, u, gates = _res_ln([att, gdn_o], [w_o[:D_HEADS], w_o[D_HEADS:]], x, mod_all, layer, 2,
                                  layer, (4, 3), ln_g[layer, 0], ln_b[layer, 0], alpha, TM,
                                  groups(TM), w_router=w_r)
            x = _moe_dense(u, gates, moe_w_gate[i], moe_w_up[i], moe_w_down[i], x, mod_all, layer, 5,
                           ln_g[layer, 1], ln_b[layer, 1], alpha, TM_MOE, TJ_MOE, groups(TM_MOE))

        k_p.append(proj[:tp, D_HEADS:2 * D_HEADS].reshape(bp, seq, h, hd))
        v_p.append(proj[:tp, 2 * D_HEADS:3 * D_HEADS].reshape(bp, seq, h, hd))
        conv_p.append(conv_new_p)
        gdn_p.append(s_new_p)
        k_s.append(k_sr)
        v_s.append(v_sr)
        conv_s.append(conv_new_s)
        gdn_s.append(s_new_s)

    y_prompt = x[:tp].reshape(bp, seq, d)
    y_sample = x[tp:].reshape(n_q, bs, d).transpose(1, 0, 2)
    return (y_prompt, y_sample, jnp.stack(k_p), jnp.stack(v_p), jnp.stack(conv_p), jnp.stack(gdn_p),
            jnp.stack(k_s), jnp.stack(v_s), jnp.stack(conv_s), jnp.stack(gdn_s))
```

```python
import functools
import math

import numpy as np
import jax
import jax.numpy as jnp
from jax import lax
from jax.experimental import pallas as pl
from jax.experimental.pallas import tpu as pltpu

F32 = jnp.float32
BF16 = jnp.bfloat16

D_MODEL = 1024
HEAD_DIM = 64
N_HEADS = 8
D_HEADS = N_HEADS * HEAD_DIM
D_CONV = 3 * D_HEADS
D_IN = 3 * D_HEADS + D_CONV + D_HEADS + 2 * N_HEADS
D_IN_MAIN = D_IN - 2 * N_HEADS
LANES = 128
D_IN_PAD = D_IN_MAIN + LANES
MOBA_BLOCK = 256
MOBA_TOPK = 3
PAGE_SIZE = 128
GDN_CONV = 4
GDN_CHUNK = 64
N_BUCKETS = 32
MAX_DISTANCE = 128
N_EXPERTS = 8
TOP_K = 2
LN_EPS = 1e-5
NEG = -1e30
MOD_ROWS = 128
VMEM_LIMIT = 56 * 1024 * 1024

HIGHEST = lax.Precision.HIGHEST


def _cparams(n_axes):
    return pltpu.CompilerParams(dimension_semantics=("arbitrary",) * n_axes,
                                vmem_limit_bytes=VMEM_LIMIT)


def _silu(x):
    return x * (1.0 / (1.0 + jnp.exp(-x)))


def _layer_norm(x, g, b):
    mu = jnp.mean(x, axis=-1, keepdims=True)
    xc = x - mu
    var = jnp.mean(xc * xc, axis=-1, keepdims=True)
    return xc * lax.rsqrt(var + LN_EPS) * g + b


def _modulate(x, scale, shift):
    tm, d = x.shape
    x3 = x.reshape(tm // MOD_ROWS, MOD_ROWS, d)
    return (x3 * (1.0 + scale[None]) + shift[None]).reshape(tm, d)


def _gated(x, gate, h):
    tm, d = x.shape
    x3 = x.reshape(tm // MOD_ROWS, MOD_ROWS, d)
    h3 = h.reshape(tm // MOD_ROWS, MOD_ROWS, d)
    return (x3 + (1.0 + gate[None]) * h3).reshape(tm, d)


def _mod_kernel(c_ref, w_ref, b_ref, o_ref):
    a = _silu(c_ref[...])
    o_ref[...] = jnp.dot(a, w_ref[...], precision=HIGHEST, preferred_element_type=F32) + b_ref[...]


def _mod_vectors(c_all, w_mod, b_mod, tn=1536):
    depth, d, n = w_mod.shape
    rows = c_all.shape[0]
    return pl.pallas_call(
        _mod_kernel,
        grid=(depth, n // tn),
        in_specs=[pl.BlockSpec((rows, d), lambda l, j: (0, 0)),
                  pl.BlockSpec((None, d, tn), lambda l, j: (l, 0, j)),
                  pl.BlockSpec((None, 1, tn), lambda l, j: (l, 0, j))],
        out_specs=pl.BlockSpec((None, rows, tn), lambda l, j: (l, 0, j)),
        out_shape=jax.ShapeDtypeStruct((depth, rows, n), F32),
        compiler_params=_cparams(2),
        name="mod_vectors",
    )(c_all, w_mod, b_mod.reshape(depth, 1, n))


def _ln_mod_kernel(x_ref, g_ref, b_ref, mod_ref, xn_ref, u_ref):
    xn = _layer_norm(x_ref[...], g_ref[...], b_ref[...])
    xn_ref[...] = xn
    u_ref[...] = _modulate(xn, mod_ref[1], mod_ref[0]).astype(BF16)


def _mod_spec(layer, group_of_tile):
    return pl.BlockSpec((None, 6, None, MOD_ROWS, D_MODEL),
                        lambda i: (layer, 0, group_of_tile(i), 0, 0))


def _ln_mod(x, g, b, mod_all, layer, tm, group_of_tile):
    t, d = x.shape
    row = pl.BlockSpec((tm, d), lambda i: (i, 0))
    vec = pl.BlockSpec((1, d), lambda i: (0, 0))
    return pl.pallas_call(
        _ln_mod_kernel,
        grid=(t // tm,),
        in_specs=[row, vec, vec, _mod_spec(layer, group_of_tile)],
        out_specs=[row, row],
        out_shape=[jax.ShapeDtypeStruct((t, d), F32), jax.ShapeDtypeStruct((t, d), BF16)],
        compiler_params=_cparams(1),
        name="ln_mod",
    )(x, g.reshape(1, d), b.reshape(1, d), mod_all)


def _proj_kernel(u_ref, w_ref, proj_ref, qkv_ref, *, chunk):
    u = u_ref[...]
    n = w_ref.shape[1]
    n_qkv = qkv_ref.shape[1]
    for c0 in range(0, n, chunk):
        c1 = min(c0 + chunk, n)
        r = jnp.dot(u, w_ref[:, c0:c1], preferred_element_type=F32)
        proj_ref[:, c0:c1] = r
        if c1 <= n_qkv:
            qkv_ref[:, c0:c1] = r.astype(BF16)


def _proj_in(u, w, tm, chunk=512):
    t, d = u.shape
    n = w.shape[1]
    return pl.pallas_call(
        functools.partial(_proj_kernel, chunk=chunk),
        grid=(t // tm,),
        in_specs=[pl.BlockSpec((tm, d), lambda i: (i, 0)),
                  pl.BlockSpec((d, n), lambda i: (0, 0))],
        out_specs=[pl.BlockSpec((tm, n), lambda i: (i, 0)),
                   pl.BlockSpec((tm, 3 * D_HEADS), lambda i: (i, 0))],
        out_shape=[jax.ShapeDtypeStruct((t, n), F32),
                   jax.ShapeDtypeStruct((t, 3 * D_HEADS), BF16)],
        compiler_params=_cparams(1),
        name="proj_in",
    )(u, w)


def _top2_gates(logits):
    lane = lax.broadcasted_iota(jnp.int32, logits.shape, 1)
    v1 = jnp.max(logits, axis=-1, keepdims=True)
    i1 = jnp.min(jnp.where(logits == v1, lane, LANES), axis=-1, keepdims=True)
    rest = jnp.where(lane == i1, -jnp.inf, logits)
    v2 = jnp.max(rest, axis=-1, keepdims=True)
    i2 = jnp.min(jnp.where(rest == v2, lane, LANES), axis=-1, keepdims=True)
    e2 = jnp.exp(v2 - v1)
    inv = 1.0 / (1.0 + e2)
    route = jnp.where(lane == 0, i1.astype(F32), 0.0) + jnp.where(lane == 1, i2.astype(F32), 0.0)
    return route + jnp.where(lane == 2, inv, 0.0) + jnp.where(lane == 3, e2 * inv, 0.0)


def _res_ln_kernel(*refs, n_in, alpha, next_rows, with_router):
    a_refs = refs[:n_in]
    w_refs = refs[n_in:2 * n_in]
    x_ref, modg_ref, modn_ref, g_ref, b_ref = refs[2 * n_in:2 * n_in + 5]
    pos = 2 * n_in + 5
    if with_router:
        wr_ref = refs[pos]
        pos += 1
    outs = refs[pos:]
    h = jnp.dot(a_refs[0][...], w_refs[0][...], preferred_element_type=F32)
    for a_ref, w_ref in zip(a_refs[1:], w_refs[1:]):
        h = h + jnp.dot(a_ref[...], w_ref[...], preferred_element_type=F32)
    xn = _layer_norm(_gated(alpha * x_ref[...], modg_ref[...], h), g_ref[...], b_ref[...])
    outs[0][...] = xn
    if next_rows is not None:
        u = _modulate(xn, modn_ref[next_rows[0]], modn_ref[next_rows[1]])
        outs[1][...] = u.astype(outs[1].dtype)
        if with_router:
            logits = jnp.dot(u, wr_ref[...], precision=HIGHEST, preferred_element_type=F32)
            lane = lax.broadcasted_iota(jnp.int32, logits.shape, 1)
            outs[2][...] = _top2_gates(jnp.where(lane < N_EXPERTS, logits, -jnp.inf))


def _res_ln(a_list, w_list, x, mod_all, gate_layer, gate_row, next_layer, next_rows, ln_g, ln_b,
            alpha, tm, group_of_tile, w_router=None):
    t, d = x.shape
    n_in = len(a_list)
    row = pl.BlockSpec((tm, d), lambda i: (i, 0))
    vec = pl.BlockSpec((1, d), lambda i: (0, 0))
    in_specs = [pl.BlockSpec((tm, a.shape[1]), lambda i: (i, 0)) for a in a_list]
    in_specs += [pl.BlockSpec(w.shape, lambda i: (0, 0)) for w in w_list]
    in_specs += [row,
                 pl.BlockSpec((None, None, None, MOD_ROWS, d),
                              lambda i: (gate_layer, gate_row, group_of_tile(i), 0, 0)),
                 _mod_spec(next_layer if next_rows is not None else gate_layer, group_of_tile),
                 vec, vec]
    args = list(a_list) + list(w_list) + [x, mod_all, mod_all, ln_g.reshape(1, d), ln_b.reshape(1, d)]
    out_specs = [row]
    out_shape = [jax.ShapeDtypeStruct((t, d), F32)]
    with_router = w_router is not None
    if next_rows is not None:
        out_specs.append(row)
        out_shape.append(jax.ShapeDtypeStruct((t, d), F32 if with_router else BF16))
    if with_router:
        in_specs.append(pl.BlockSpec(w_router.shape, lambda i: (0, 0)))
        args.append(w_router)
        out_specs.append(pl.BlockSpec((tm, LANES), lambda i: (i, 0)))
        out_shape.append(jax.ShapeDtypeStruct((t, LANES), F32))
    return pl.pallas_call(
        functools.partial(_res_ln_kernel, n_in=n_in, alpha=alpha, next_rows=next_rows,
                          with_router=with_router),
        grid=(t // tm,),
        in_specs=in_specs,
        out_specs=out_specs,
        out_shape=out_shape,
        compiler_params=_cparams(1),
        name="res_ln",
    )(*args)


def _ffn_up_kernel(u_ref, wg_ref, wu_ref, o_ref, *, chunk):
    u = u_ref[...]
    n = o_ref.shape[1]
    for c0 in range(0, n, chunk):
        c1 = min(c0 + chunk, n)
        hg = jnp.dot(u, wg_ref[:, c0:c1], preferred_element_type=F32)
        hu = jnp.dot(u, wu_ref[:, c0:c1], preferred_element_type=F32)
        o_ref[:, c0:c1] = (_silu(hg) * hu).astype(BF16)


def _ffn_up(u, wg, wu, tm, chunk=256):
    t, d = u.shape
    n = wg.shape[1]
    wspec = pl.BlockSpec((d, n), lambda i: (0, 0))
    return pl.pallas_call(
        functools.partial(_ffn_up_kernel, chunk=chunk),
        grid=(t // tm,),
        in_specs=[pl.BlockSpec((tm, d), lambda i: (i, 0)), wspec, wspec],
        out_specs=pl.BlockSpec((tm, n), lambda i: (i, 0)),
        out_shape=jax.ShapeDtypeStruct((t, n), BF16),
        compiler_params=_cparams(1),
        name="ffn_up",
    )(u, wg, wu)


def _row_copy(src_hbm, src_row, dst_ref, r, sem):
    return pltpu.make_async_copy(src_hbm.at[pl.ds(src_row, 1)], dst_ref.at[pl.ds(r, 1)], sem)


def _gather_start(idx_ref, idx_base, src_hbm, dst_ref, sem, n_rows):
    def body(r, carry):
        _row_copy(src_hbm, idx_ref[0, idx_base + r], dst_ref, r, sem).start()
        return carry

    lax.fori_loop(0, n_rows, body, 0)


def _gather_wait(src_hbm, dst_ref, sem, n_rows):
    def body(r, carry):
        _row_copy(src_hbm, 0, dst_ref, r, sem).wait()
        return carry

    lax.fori_loop(0, n_rows, body, 0)


def _moe_ffn_kernel(te_ref, nv_ref, tos_ref, u_hbm, wg_ref, wu_ref, wd_ref, y_ref,
                    xg_ref, xb_ref, acc_ref, sem):
    del te_ref
    i = pl.program_id(0)
    j = pl.program_id(1)
    valid = i < nv_ref[0]

    @pl.when(valid & (j == 0))
    def _():
        _gather_start(tos_ref, 0, u_hbm, xg_ref, sem, xg_ref.shape[0])
        _gather_wait(u_hbm, xg_ref, sem, xg_ref.shape[0])
        xb_ref[...] = xg_ref[...].astype(BF16)
        acc_ref[...] = jnp.zeros_like(acc_ref)

    @pl.when(valid)
    def _():
        x = xb_ref[...]
        hg = jnp.dot(x, wg_ref[...].astype(BF16), preferred_element_type=F32)
        hu = jnp.dot(x, wu_ref[...].astype(BF16), preferred_element_type=F32)
        acc_ref[...] += jnp.dot((_silu(hg) * hu).astype(BF16), wd_ref[...].astype(BF16),
                                preferred_element_type=F32)

    last = j == pl.num_programs(1) - 1

    @pl.when(valid & last)
    def _():
        y_ref[...] = acc_ref[...]

    @pl.when(jnp.logical_not(valid) & last)
    def _():
        y_ref[...] = jnp.zeros_like(y_ref)


def _moe_ffn(u, tile_expert, n_valid, token_of_slot, wg, wu, wd, tm, tj):
    n_tiles = tile_expert.shape[0]
    d = u.shape[1]
    f = wg.shape[2]
    nj = f // tj

    def jj(i, j, nv):
        return jnp.where(i < nv[0], j, nj - 1)

    grid_spec = pltpu.PrefetchScalarGridSpec(
        num_scalar_prefetch=2,
        grid=(n_tiles, nj),
        in_specs=[pl.BlockSpec((None, 1, tm), lambda i, j, te, nv: (i, 0, 0), memory_space=pltpu.SMEM),
                  pl.BlockSpec(memory_space=pl.ANY),
                  pl.BlockSpec((None, d, tj), lambda i, j, te, nv: (te[i], 0, jj(i, j, nv))),
                  pl.BlockSpec((None, d, tj), lambda i, j, te, nv: (te[i], 0, jj(i, j, nv))),
                  pl.BlockSpec((None, tj, d), lambda i, j, te, nv: (te[i], jj(i, j, nv), 0))],
        out_specs=pl.BlockSpec((tm, d), lambda i, j, te, nv: (i, 0)),
        scratch_shapes=[pltpu.VMEM((tm, d), F32), pltpu.VMEM((tm, d), BF16),
                        pltpu.VMEM((tm, d), F32), pltpu.SemaphoreType.DMA(())])
    return pl.pallas_call(
        _moe_ffn_kernel,
        grid_spec=grid_spec,
        out_shape=jax.ShapeDtypeStruct((n_tiles * tm, d), F32),
        compiler_params=_cparams(2),
        name="moe_ffn",
    )(tile_expert, n_valid, token_of_slot, u, wg, wu, wd)


def _moe_combine_kernel(slots_ref, y_hbm, route_ref, x_ref, modg_ref, g_ref, b_ref, o_ref,
                        buf_ref, sem, *, alpha):
    rows = x_ref.shape[0]
    for k in range(TOP_K):
        _gather_start(slots_ref, k * rows, y_hbm, buf_ref.at[k], sem, rows)
    for k in range(TOP_K):
        _gather_wait(y_hbm, buf_ref.at[k], sem, rows)
    route = route_ref[...]
    f = route[:, 2:3] * buf_ref[0] + route[:, 3:4] * buf_ref[1]
    o_ref[...] = _layer_norm(_gated(alpha * x_ref[...], modg_ref[...], f), g_ref[...], b_ref[...])


def _moe_combine(y, slots, route, x, mod_all, gate_layer, gate_row, ln_g, ln_b, alpha, tm,
                 group_of_tile):
    t, d = x.shape
    row = pl.BlockSpec((tm, d), lambda i: (i, 0))
    vec = pl.BlockSpec((1, d), lambda i: (0, 0))
    return pl.pallas_call(
        functools.partial(_moe_combine_kernel, alpha=alpha),
        grid=(t // tm,),
        in_specs=[pl.BlockSpec((None, 1, 2 * tm), lambda i: (i, 0, 0), memory_space=pltpu.SMEM),
                  pl.BlockSpec(memory_space=pl.ANY),
                  pl.BlockSpec((tm, LANES), lambda i: (i, 0)),
                  row,
                  pl.BlockSpec((None, None, None, MOD_ROWS, d),
                               lambda i: (gate_layer, gate_row, group_of_tile(i), 0, 0)),
                  vec, vec],
        out_specs=row,
        out_shape=jax.ShapeDtypeStruct((t, d), F32),
        scratch_shapes=[pltpu.VMEM((2, tm, d), F32), pltpu.SemaphoreType.DMA(())],
        compiler_params=_cparams(1),
        name="moe_combine",
    )(slots, y, route, x, mod_all, ln_g.reshape(1, d), ln_b.reshape(1, d))


def _moe_routing(route, tm):
    t = route.shape[0]
    n_assign = TOP_K * t
    n_tiles = -(-n_assign // tm) + N_EXPERTS
    experts = route[:, :TOP_K].astype(jnp.int32).reshape(n_assign)
    onehot = (experts[:, None] == jnp.arange(N_EXPERTS, dtype=jnp.int32)[None, :]).astype(jnp.int32)
    rank = jnp.sum((jnp.cumsum(onehot, axis=0) - onehot) * onehot, axis=1)
    count = jnp.sum(onehot, axis=0)
    padded = (count + tm - 1) // tm * tm
    ends = jnp.cumsum(padded)
    slot = jnp.sum(onehot * (ends - padded)[None, :], axis=1) + rank
    n_valid = ends[-1] // tm
    starts = jnp.arange(n_tiles, dtype=jnp.int32) * tm
    tile_expert = jnp.minimum(jnp.sum((starts[:, None] >= ends[None, :]).astype(jnp.int32), axis=1),
                              N_EXPERTS - 1)
    tile_expert = jnp.where(starts < ends[-1], tile_expert, tile_expert[jnp.maximum(n_valid - 1, 0)])
    token_of_slot = jnp.zeros((n_tiles * tm,), jnp.int32).at[slot].set(
        jnp.arange(n_assign, dtype=jnp.int32) // TOP_K)
    return (tile_expert.astype(jnp.int32), n_valid.reshape(1).astype(jnp.int32),
            token_of_slot.reshape(n_tiles, 1, tm), slot.reshape(t, TOP_K).astype(jnp.int32))


def _rel_bucket_table(max_dist):
    n = np.arange(max_dist + 1)
    max_exact = N_BUCKETS // 2
    ratio = np.log(np.maximum(n, 1).astype(np.float32) / np.float32(max_exact))
    large = max_exact + (ratio / np.float32(math.log(MAX_DISTANCE / max_exact))
                         * np.float32(N_BUCKETS - max_exact)).astype(np.int32)
    large = np.minimum(large, N_BUCKETS - 1)
    return np.where(n < max_exact, n, large).astype(np.int32)


def _bias_lookup(rel_table, bucket_idx):
    onehot = jnp.asarray(np.eye(N_BUCKETS, dtype=np.float32)[bucket_idx.reshape(-1)])
    vals = jnp.dot(onehot, rel_table, precision=HIGHEST)
    return vals.T.reshape((rel_table.shape[1],) + bucket_idx.shape)


def _select_top_blocks(gate, n_valid, ksel):
    lane = lax.broadcasted_iota(jnp.int32, gate.shape, 1)
    nb = gate.shape[1]
    g = jnp.where(lane < n_valid, gate, -jnp.inf)
    keep = lane == n_valid
    for _ in range(ksel):
        mx = jnp.max(g, axis=-1, keepdims=True)
        first = jnp.min(jnp.where(g == mx, lane, nb), axis=-1, keepdims=True)
        hit = lane == first
        keep = keep | (hit & (lane < n_valid))
        g = jnp.where(hit, -jnp.inf, g)
    return jnp.where(keep, 0.0, NEG)


def _moba_prompt_kernel(q_ref, k_ref, v_ref, bias_ref, o_ref,
                        kmean_ref, qm_ref, sel_ref, m_ref, l_ref, acc_ref, *, ksel):
    qt = pl.program_id(1)
    seq = k_ref.shape[0]
    nb = seq // MOBA_BLOCK
    gw = 4 * HEAD_DIM
    n_groups = N_HEADS // 4

    @pl.when(qt == 0)
    def _():
        row = lax.broadcasted_iota(jnp.int32, (nb, seq), 0)
        col = lax.broadcasted_iota(jnp.int32, (nb, seq), 1)
        pool = jnp.where(col // MOBA_BLOCK == row, 1.0, 0.0).astype(BF16)
        kmean_ref[...] = jnp.dot(pool, k_ref[...], preferred_element_type=F32) * (1.0 / MOBA_BLOCK)

    head_of_lane = lax.broadcasted_iota(jnp.int32, (1, gw), 1) // HEAD_DIM
    q = q_ref[...] * (HEAD_DIM ** -0.5)
    kmean = kmean_ref[...].astype(BF16)
    for h in range(N_HEADS):
        g, hh = divmod(h, 4)
        qm = jnp.where(head_of_lane == hh, q[:, g * gw:(g + 1) * gw], 0.0).astype(BF16)
        qm_ref[h] = qm
        gate = lax.dot_general(qm, kmean[:, g * gw:(g + 1) * gw], (((1,), (1,)), ((), ())),
                               preferred_element_type=F32)
        sel_ref[h] = _select_top_blocks(gate, qt, ksel)
        m_ref[h] = jnp.full(m_ref.shape[1:], NEG, F32)
        l_ref[h] = jnp.zeros(l_ref.shape[1:], F32)
    acc_ref[...] = jnp.zeros_like(acc_ref)

    def body(step, carry):
        n = qt - step
        start = pl.multiple_of(n * MOBA_BLOCK, MOBA_BLOCK)
        kb = k_ref[pl.ds(start, MOBA_BLOCK), :]
        vb = v_ref[pl.ds(start, MOBA_BLOCK), :]
        bias_idx = jnp.minimum(step, 2)
        for h in range(N_HEADS):
            g, hh = divmod(h, 4)
            s = lax.dot_general(qm_ref[h], kb[:, g * gw:(g + 1) * gw], (((1,), (1,)), ((), ())),
                                preferred_element_type=F32)
            sel = sel_ref[h]
            lane = lax.broadcasted_iota(jnp.int32, sel.shape, 1)
            sel_col = jnp.sum(jnp.where(lane == n, sel, 0.0), axis=-1, keepdims=True)
            s = s + bias_ref[h, bias_idx] + sel_col
            m_old = m_ref[h]
            m_new = jnp.maximum(m_old, jnp.max(s, axis=-1, keepdims=True))
            alpha = jnp.exp(m_old - m_new)
            p = jnp.exp(s - m_new)
            l_ref[h] = alpha * l_ref[h] + jnp.sum(p, axis=-1, keepdims=True)
            m_ref[h] = m_new
            pv = jnp.dot(p.astype(BF16), vb[:, g * gw:(g + 1) * gw], preferred_element_type=F32)
            acc = acc_ref[g]
            acc_ref[g] = jnp.where(head_of_lane == hh, alpha * acc + pv, acc)
        return carry

    lax.fori_loop(0, qt + 1, body, 0)

    for g in range(n_groups):
        inv = jnp.zeros((q.shape[0], gw), F32)
        for hh in range(4):
            inv = jnp.where(head_of_lane == hh, 1.0 / l_ref[4 * g + hh], inv)
        o_ref[:, g * gw:(g + 1) * gw] = (acc_ref[g] * inv).astype(o_ref.dtype)


def _moba_bias_tables(rel_table):
    bucket = _rel_bucket_table(2 * MOBA_BLOCK)
    i = np.arange(MOBA_BLOCK)[:, None]
    j = np.arange(MOBA_BLOCK)[None, :]
    own = jnp.where(jnp.asarray(i >= j), _bias_lookup(rel_table, bucket[np.maximum(i - j, 0)]), NEG)
    prev = _bias_lookup(rel_table, bucket[MOBA_BLOCK + i - j])
    far = jnp.broadcast_to(rel_table[N_BUCKETS - 1][:, None, None], prev.shape)
    return jnp.stack([own, prev, far], axis=1).astype(F32)


def _moba_prompt(qkv, bias_tab, batch, seq):
    nb = seq // MOBA_BLOCK
    ksel = min(MOBA_TOPK, nb)
    stat = pltpu.VMEM((N_HEADS, MOBA_BLOCK, 1), F32)
    return pl.pallas_call(
        functools.partial(_moba_prompt_kernel, ksel=ksel),
        grid=(batch, nb),
        in_specs=[pl.BlockSpec((MOBA_BLOCK, D_HEADS), lambda b, t: (b * nb + t, 0)),
                  pl.BlockSpec((seq, D_HEADS), lambda b, t: (b, 1)),
                  pl.BlockSpec((seq, D_HEADS), lambda b, t: (b, 2)),
                  pl.BlockSpec(bias_tab.shape, lambda b, t: (0, 0, 0, 0))],
        out_specs=pl.BlockSpec((MOBA_BLOCK, D_HEADS), lambda b, t: (b * nb + t, 0)),
        out_shape=jax.ShapeDtypeStruct((batch * seq, D_HEADS), BF16),
        scratch_shapes=[pltpu.VMEM((nb, D_HEADS), F32),
                        pltpu.VMEM((N_HEADS, MOBA_BLOCK, 4 * HEAD_DIM), BF16),
                        pltpu.VMEM((N_HEADS, MOBA_BLOCK, nb), F32),
                        stat, stat,
                        pltpu.VMEM((N_HEADS // 4, MOBA_BLOCK, 4 * HEAD_DIM), F32)],
        compiler_params=_cparams(2),
        name="moba_prompt",
    )(qkv, qkv, qkv, bias_tab)


def _moba_sample_kernel(pt_ref, q_ref, kn_ref, vn_ref, ka_ref, kb_ref, va_ref, vb_ref,
                        bias_ref, bown_ref, o_ref, m_ref, l_ref, g_ref, acc_ref, *, ksel, n_q):
    del pt_ref
    j = pl.program_id(1)
    nbp = pl.num_programs(1)
    nt = (((1,), (1,)), ((), ()))
    dh = N_HEADS * HEAD_DIM
    q8 = q_ref[...]
    slots = q8.shape[0]
    lane_head = lax.broadcasted_iota(jnp.int32, (1, dh), 1) // HEAD_DIM
    qbd = jnp.concatenate([jnp.where(lane_head == hp, q8, 0.0) for hp in range(N_HEADS)], axis=0)
    qs = (qbd * (HEAD_DIM ** -0.5)).astype(BF16)

    ones = jnp.ones((8, PAGE_SIZE), BF16)
    s_parts, acc, ksum = [], None, None
    for k_ref, v_ref in ((ka_ref, va_ref), (kb_ref, vb_ref)):
        kt = k_ref[...].reshape(dh, PAGE_SIZE)
        kt_hi = kt.astype(BF16)
        kt_lo = (kt - kt_hi.astype(F32)).astype(BF16)
        s_parts.append(jnp.dot(qs, kt_hi, preferred_element_type=F32))
        part = (lax.dot_general(ones, kt_hi, nt, preferred_element_type=F32)
                + lax.dot_general(ones, kt_lo, nt, preferred_element_type=F32))
        ksum = part if ksum is None else ksum + part
    s = jnp.concatenate(s_parts, axis=1) + bias_ref[jnp.where(j == nbp - 1, 0, 1)]
    m = jnp.max(s, axis=-1, keepdims=True)
    p32 = jnp.exp(s - m)
    p = p32.astype(BF16)
    for idx, v_ref in enumerate((va_ref, vb_ref)):
        vt = v_ref[...].reshape(dh, PAGE_SIZE).astype(BF16)
        part = lax.dot_general(p[:, idx * PAGE_SIZE:(idx + 1) * PAGE_SIZE], vt, nt,
                               preferred_element_type=F32)
        acc = part if acc is None else acc + part
    m_ref[j] = m
    l_ref[j] = jnp.sum(p32, axis=-1, keepdims=True)
    acc_ref[j] = acc
    g_ref[j] = jnp.sum(qbd * ksum[0:1, :], axis=-1, keepdims=True) * (1.0 / MOBA_BLOCK)

    @pl.when(j == nbp - 1)
    def _():
        s_own = [jnp.sum(qbd * kn_ref[i:i + 1, :], axis=-1, keepdims=True) * (HEAD_DIM ** -0.5)
                 + bown_ref[:, i:i + 1] for i in range(n_q)]
        m_o = functools.reduce(jnp.maximum, s_own)
        p_own = [jnp.exp(s_i - m_o) for s_i in s_own]
        l_o = functools.reduce(jnp.add, p_own)
        acc_o = functools.reduce(jnp.add, [p_i * vn_ref[i:i + 1, :] for i, p_i in enumerate(p_own)])

        gg = g_ref[...]
        blk = lax.broadcasted_iota(jnp.int32, gg.shape, 0)
        keep = blk < 0
        for _ in range(ksel):
            mx = jnp.max(gg, axis=0, keepdims=True)
            first = jnp.min(jnp.where(gg == mx, blk, nbp), axis=0, keepdims=True)
            hit = (blk == first) & (gg > -jnp.inf)
            keep = keep | hit
            gg = jnp.where(blk == first, -jnp.inf, gg)
        m_all = m_ref[...]
        m_tot = jnp.maximum(jnp.max(jnp.where(keep, m_all, NEG), axis=0), m_o)
        w = jnp.where(keep, jnp.exp(m_all - m_tot[None]), 0.0)
        w_o = jnp.exp(m_o - m_tot)
        l_tot = jnp.sum(w * l_ref[...], axis=0) + w_o * l_o
        acc_tot = (jnp.sum(w * acc_ref[...], axis=0) + w_o * acc_o) * (1.0 / l_tot)
        row_head = lax.broadcasted_iota(jnp.int32, (N_HEADS * slots, 1), 0) // slots
        acc_tot = jnp.where(row_head == lane_head, acc_tot, 0.0)
        o_ref[...] = functools.reduce(
            jnp.add, [acc_tot[:, hp * HEAD_DIM:(hp + 1) * HEAD_DIM] for hp in range(N_HEADS)])


def _moba_sample_bias(rel_table, past_len, n_q):
    del past_len
    slots = SAMPLE_PAD
    bucket = _rel_bucket_table(MOBA_BLOCK + slots)
    rows_h = np.repeat(np.arange(N_HEADS), slots)
    rows_i = np.minimum(np.tile(np.arange(slots), N_HEADS), n_q - 1)
    pos = np.arange(MOBA_BLOCK)[None, :]
    by_head = _bias_lookup(rel_table, bucket[MOBA_BLOCK + rows_i[:, None] - pos])
    pick = jnp.asarray(np.arange(N_HEADS)[:, None, None] == rows_h[None, :, None])
    near = jnp.sum(jnp.where(pick, by_head, 0.0), axis=0)
    far = jnp.broadcast_to(jnp.repeat(rel_table[N_BUCKETS - 1], slots)[:, None], near.shape)
    past = jnp.stack([near, far]).astype(F32)
    new = np.arange(slots)[None, :]
    own_h = _bias_lookup(rel_table, bucket[np.maximum(rows_i[:, None] - new, 0)])
    own = jnp.sum(jnp.where(pick, own_h, 0.0), axis=0)
    ok = jnp.asarray((new <= rows_i[:, None]) & (new < n_q))
    return past, jnp.where(ok, own, NEG).astype(F32)


def _moba_sample(q_rows, k_new, v_new, cache_k, cache_v, page_table, layer, bias_past, bias_own, n_q):
    bsz, slots, dh = q_rows.shape
    hd = HEAD_DIM
    rows = N_HEADS * slots
    pages_per_block = MOBA_BLOCK // PAGE_SIZE
    nbp = page_table.shape[1] // pages_per_block
    ksel = min(MOBA_TOPK, nbp + 1)

    def page_spec(which):
        return pl.BlockSpec((None, None, N_HEADS, hd, PAGE_SIZE),
                            lambda b, j, pt: (layer, pt[b, pages_per_block * j + which], 0, 0, 0))

    per_seq = lambda b, j, pt: (b, 0, 0)
    stat = pltpu.VMEM((nbp, rows, 1), F32)
    grid_spec = pltpu.PrefetchScalarGridSpec(
        num_scalar_prefetch=1,
        grid=(bsz, nbp),
        in_specs=[pl.BlockSpec((None, slots, dh), per_seq),
                  pl.BlockSpec((None, slots, dh), per_seq),
                  pl.BlockSpec((None, slots, dh), per_seq),
                  page_spec(0), page_spec(1), page_spec(0), page_spec(1),
                  pl.BlockSpec(bias_past.shape, lambda b, j, pt: (0, 0, 0)),
                  pl.BlockSpec(bias_own.shape, lambda b, j, pt: (0, 0))],
        out_specs=pl.BlockSpec((None, rows, hd), per_seq),
        scratch_shapes=[stat, stat, stat, pltpu.VMEM((nbp, rows, dh), F32)])
    return pl.pallas_call(
        functools.partial(_moba_sample_kernel, ksel=ksel, n_q=n_q),
        grid_spec=grid_spec,
        out_shape=jax.ShapeDtypeStruct((bsz, rows, hd), F32),
        compiler_params=_cparams(2),
        name="moba_sample",
    )(page_table, q_rows, k_new, v_new, cache_k, cache_k, cache_v, cache_v, bias_past, bias_own)


def _softplus(x):
    return jnp.maximum(x, 0.0) + jnp.log1p(jnp.exp(-jnp.abs(x)))


def _bmm(a, b):
    return jnp.matmul(a.astype(BF16), b.astype(BF16), preferred_element_type=F32)


def _bmm_nt(a, b):
    return jnp.einsum("nid,njd->nij", a.astype(BF16), b.astype(BF16), preferred_element_type=F32)


def _dot_f32(a, b):
    return jnp.dot(a, b, precision=HIGHEST, preferred_element_type=F32)


def _split_bf16(a):
    hi = a.astype(BF16)
    return hi, (a - hi.astype(F32)).astype(BF16)


def _bmm_3pass(a, b):
    a_hi, a_lo = _split_bf16(a)
    b_hi, b_lo = _split_bf16(b)
    return (jnp.matmul(a_hi, b_hi, preferred_element_type=F32)
            + jnp.matmul(a_hi, b_lo, preferred_element_type=F32)
            + jnp.matmul(a_lo, b_hi, preferred_element_type=F32))


def _unit_lower_inverse(a, row, col):
    c = a.shape[1]
    eye = jnp.where(row == col, 1.0, 0.0)
    in16 = (row // 16) == (col // 16)
    in32 = (row // 32) == (col // 32)
    nil = jnp.where(in16, -a, 0.0)
    x = eye + nil
    p = _bmm_3pass(nil, nil)
    for _ in range(2):
        xp = _bmm_3pass(jnp.concatenate([x, p], axis=1), p)
        x = x + xp[:, :c]
        p = xp[:, c:]
    x = x + _bmm_3pass(x, p)
    for off in (jnp.where(in32 & jnp.logical_not(in16), a, 0.0),
                jnp.where(in32, 0.0, a)):
        x = x - _bmm(x, _bmm(off, x))
    return x


def _gdn_prepare(q, k, v, k_t, g_col, g_row, beta, row, col):
    c = q.shape[1]
    incl = row >= col
    decay = jnp.where(incl, jnp.exp(jnp.where(incl, g_col - g_row, 0.0)), 0.0)
    qk_kk = _bmm_nt(jnp.concatenate([q, k], axis=1), k)
    a_mat = jnp.where(row > col, qk_kk[:, c:] * decay * beta, 0.0)
    e_g = jnp.exp(g_col)
    rhs = jnp.concatenate([v * beta, k * beta * e_g], axis=2)
    sol = _bmm(_unit_lower_inverse(a_mat, row, col), rhs)
    dv = v.shape[2]
    g_last = g_col[:, c - 1:c, :]
    w_qd = jnp.concatenate([sol[:, :, dv:], q * e_g], axis=1).astype(BF16)
    qk = jnp.where(incl, qk_kk[:, :c] * decay, 0.0).astype(BF16)
    k_dec_t = (k_t * jnp.exp(g_last - g_row)).astype(BF16)
    return sol[:, :, :dv], w_qd, qk, k_dec_t, jnp.exp(g_last)


def _gdn_step(u, w_qd, qk, k_dec_t, chunk_decay, state):
    c = u.shape[1]
    ws_qs = jnp.matmul(w_qd, state.astype(BF16), preferred_element_type=F32)
    v_new = (u - ws_qs[:, :c]).astype(BF16)
    o = ws_qs[:, c:] + jnp.matmul(qk, v_new, preferred_element_type=F32)
    new_state = state * chunk_decay + jnp.matmul(k_dec_t, v_new, preferred_element_type=F32)
    return o, new_state


def _gdn_prompt_kernel(x_ref, z_ref, gab_ref, cw_ref, alog_ref, dtb_ref, nw_ref, seg_ref,
                       o_ref, conv_ref, sout_ref,
                       xbuf_ref, halo_ref, s_ref, u_ref, wqd_ref, qk_ref, kdt_ref, cd_ref, oh_ref):
    t = pl.program_id(1)
    tl = x_ref.shape[0]
    cs = GDN_CHUNK
    n_chunks = tl // cs
    hd = HEAD_DIM

    @pl.when(t == 0)
    def _():
        halo_ref[...] = jnp.zeros_like(halo_ref)
        s_ref[...] = jnp.zeros_like(s_ref)

    x = x_ref[...]
    xbuf_ref[0:8, :] = halo_ref[...]
    xbuf_ref[8:8 + tl, :] = x
    y = cw_ref[0:1, :] * xbuf_ref[5:5 + tl, :]
    for i in range(1, GDN_CONV):
        y = y + cw_ref[i:i + 1, :] * xbuf_ref[5 + i:5 + i + tl, :]
    y = _silu(y)
    tail = x[tl - 8:, :]
    halo_ref[...] = tail
    conv_ref[...] = tail[8 - (GDN_CONV - 1):, :]

    dh = N_HEADS * hd
    seg = seg_ref[...]

    def l2n(a):
        sq_hi, sq_lo = _split_bf16(a * a)
        ss = (jnp.dot(sq_hi, seg, preferred_element_type=F32)
              + jnp.dot(sq_lo, seg, preferred_element_type=F32))
        return a * lax.rsqrt(ss + 1e-6)

    qn = l2n(y[:, :dh]) * (hd ** -0.5)
    kn = l2n(y[:, dh:2 * dh])
    vv = y[:, 2 * dh:]
    kn_t = kn.T

    gab = gab_ref[...]
    beta = 1.0 / (1.0 + jnp.exp(-gab))
    glog = -jnp.exp(alog_ref[...]) * _softplus(gab + dtb_ref[...])

    row = lax.broadcasted_iota(jnp.int32, (cs, cs), 0)
    col = lax.broadcasted_iota(jnp.int32, (cs, cs), 1)
    tril = jnp.where(row >= col, 1.0, 0.0)
    pick = jnp.where(lax.broadcasted_iota(jnp.int32, (N_HEADS, LANES), 1)
                     == lax.broadcasted_iota(jnp.int32, (N_HEADS, LANES), 0) + N_HEADS, 1.0, 0.0)
    parts = {name: [] for name in ("q", "k", "v", "kt", "gc", "gr", "bb")}
    for c in range(n_chunks):
        rows = slice(c * cs, (c + 1) * cs)
        gcum = _dot_f32(tril, glog[rows])
        g_rows = lax.dot_general(pick, gcum, (((1,), (1,)), ((), ())), precision=HIGHEST,
                                 preferred_element_type=F32)
        for h in range(N_HEADS):
            lanes = slice(h * hd, (h + 1) * hd)
            parts["q"].append(qn[rows, lanes])
            parts["k"].append(kn[rows, lanes])
            parts["v"].append(vv[rows, lanes])
            parts["kt"].append(kn_t[lanes, rows])
            parts["gc"].append(jnp.broadcast_to(gcum[:, N_HEADS + h:N_HEADS + h + 1], (cs, cs)))
            parts["gr"].append(g_rows[h:h + 1, :])
            parts["bb"].append(jnp.broadcast_to(beta[rows, h:h + 1], (cs, cs)))
    st = {name: jnp.stack(vals) for name, vals in parts.items()}
    u_all, w_qd, qk, k_dec_t, chunk_decay = _gdn_prepare(
        st["q"], st["k"], st["v"], st["kt"], st["gc"], st["gr"], st["bb"], row, col)
    u_ref[...] = u_all.reshape(u_ref.shape)
    wqd_ref[...] = w_qd.reshape(wqd_ref.shape)
    qk_ref[...] = qk.reshape(qk_ref.shape)
    kdt_ref[...] = k_dec_t.reshape(kdt_ref.shape)
    cd_ref[...] = chunk_decay.reshape(cd_ref.shape)

    def chunk_body(c, carry):
        o, s_new = _gdn_step(u_ref[c], wqd_ref[c], qk_ref[c], kdt_ref[c], cd_ref[c], s_ref[...])
        oh_ref[c] = o
        s_ref[...] = s_new
        return carry

    lax.fori_loop(0, n_chunks, chunk_body, 0)

    z = z_ref[...]
    nw = nw_ref[...]
    for h in range(N_HEADS):
        lanes = slice(h * hd, (h + 1) * hd)
        o = jnp.concatenate([oh_ref[c, h] for c in range(n_chunks)], axis=0)
        o = o * lax.rsqrt(jnp.mean(o * o, axis=-1, keepdims=True) + 1e-6) * nw
        o_ref[:, lanes] = (o * _silu(z[:, lanes])).astype(o_ref.dtype)

    @pl.when(t == pl.num_programs(1) - 1)
    def _():
        sout_ref[...] = s_ref[...]


def _gdn_prompt(proj, batch, seq, tl, conv_w, a_log, dt_bias, norm_w):
    hd = HEAD_DIM
    cs = GDN_CHUNK
    n_steps = seq // tl
    n_chunks = tl // cs
    col_x, col_z, col_gab = 1, D_IN_MAIN // D_HEADS - 1, D_IN_MAIN // LANES
    lane = np.arange(D_HEADS)
    seg = jnp.asarray((lane[:, None] // hd == lane[None, :] // hd).astype(np.float32)).astype(BF16)
    alog_row = jnp.zeros((1, LANES), F32).at[0, N_HEADS:2 * N_HEADS].set(a_log)
    dtb_row = jnp.zeros((1, LANES), F32).at[0, N_HEADS:2 * N_HEADS].set(dt_bias)
    const2 = lambda b, t: (0, 0)
    per_inst = pltpu.VMEM((n_chunks, N_HEADS, cs, hd), F32)
    per_inst_bf16 = pltpu.VMEM((n_chunks, N_HEADS, hd, cs), BF16)
    return pl.pallas_call(
        _gdn_prompt_kernel,
        grid=(batch, n_steps),
        in_specs=[pl.BlockSpec((tl, D_CONV), lambda b, t: (b * n_steps + t, col_x)),
                  pl.BlockSpec((tl, D_HEADS), lambda b, t: (b * n_steps + t, col_z)),
                  pl.BlockSpec((tl, LANES), lambda b, t: (b * n_steps + t, col_gab)),
                  pl.BlockSpec((GDN_CONV, D_CONV), const2),
                  pl.BlockSpec((1, LANES), const2),
                  pl.BlockSpec((1, LANES), const2),
                  pl.BlockSpec((1, hd), const2),
                  pl.BlockSpec((D_HEADS, D_HEADS), const2)],
        out_specs=[pl.BlockSpec((tl, D_HEADS), lambda b, t: (b * n_steps + t, 0)),
                   pl.BlockSpec((None, GDN_CONV - 1, D_CONV), lambda b, t: (b, 0, 0)),
                   pl.BlockSpec((None, N_HEADS, hd, hd), lambda b, t: (b, 0, 0, 0))],
        out_shape=[jax.ShapeDtypeStruct((batch * seq, D_HEADS), BF16),
                   jax.ShapeDtypeStruct((batch, GDN_CONV - 1, D_CONV), F32),
                   jax.ShapeDtypeStruct((batch, N_HEADS, hd, hd), F32)],
        scratch_shapes=[pltpu.VMEM((8 + tl + 8, D_CONV), F32),
                        pltpu.VMEM((8, D_CONV), F32),
                        pltpu.VMEM((N_HEADS, hd, hd), F32),
                        per_inst,
                        pltpu.VMEM((n_chunks, N_HEADS, 2 * cs, hd), BF16),
                        per_inst_bf16, per_inst_bf16,
                        pltpu.VMEM((n_chunks, N_HEADS, 1, hd), F32),
                        per_inst],
        compiler_params=_cparams(2),
        name="gdn_prompt",
    )(proj, proj, proj, conv_w, alog_row, dtb_row, norm_w.reshape(1, hd), seg)


def _gdn_sample_kernel(xq_ref, xk_ref, xv_ref, cq_ref, ck_ref, cv_ref, wq_ref, wk_ref, wv_ref,
                       z_ref, gab_ref, alog_ref, dtb_ref, nw_ref, s_ref,
                       o_ref, sout_ref, kq_ref, gt_ref, ot_ref):
    pair = pl.program_id(0)
    n_q = xq_ref.shape[0]
    hd = HEAD_DIM

    def conv_t(x_ref, c_ref, w_ref):
        xp = [c_ref[i] for i in range(GDN_CONV - 1)] + [x_ref[i] for i in range(n_q)]
        out = []
        for i in range(n_q):
            y = w_ref[0:1, :] * xp[i]
            for m in range(1, GDN_CONV):
                y = y + w_ref[m:m + 1, :] * xp[i + m]
            out.append(_silu(y).T)
        return out

    q_t = conv_t(xq_ref, cq_ref, wq_ref)
    k_t = conv_t(xk_ref, ck_ref, wk_ref)
    v_t = conv_t(xv_ref, cv_ref, wv_ref)

    def l2n(a):
        return a * lax.rsqrt(jnp.sum(a * a, axis=0, keepdims=True) + 1e-6)

    for i in range(n_q):
        gab_t = gab_ref[i].T
        gt_ref[0, i] = 1.0 / (1.0 + jnp.exp(-gab_t))
        gt_ref[1, i] = jnp.exp(-jnp.exp(alog_ref[...]) * _softplus(gab_t + dtb_ref[...]))

    for hh in range(2):
        rows = slice(hh * hd, (hh + 1) * hd)
        head = 2 * pair + hh
        for i in range(n_q):
            kq_ref[0] = l2n(k_t[i][rows])
            kq_ref[1] = l2n(q_t[i][rows]) * (hd ** -0.5)
            beta = gt_ref[0, i, pl.ds(head, 1), :]
            decay = gt_ref[1, i, pl.ds(N_HEADS + head, 1), :]
            src = s_ref if i == 0 else sout_ref

            def ks_body(kk, acc):
                return acc + kq_ref[0, pl.ds(kk, 1), :] * src[hh, kk]

            k_s = lax.fori_loop(0, hd, ks_body, jnp.zeros((hd, k_t[i].shape[1]), F32), unroll=8)
            r = beta * (v_t[i][rows] - decay * k_s)

            def upd_body(kk, acc):
                s_new = decay * src[hh, kk] + kq_ref[0, pl.ds(kk, 1), :] * r
                sout_ref[hh, kk] = s_new
                return acc + kq_ref[1, pl.ds(kk, 1), :] * s_new

            o = lax.fori_loop(0, hd, upd_body, jnp.zeros_like(r), unroll=8)
            ot_ref[i, rows, :] = o * lax.rsqrt(jnp.mean(o * o, axis=0, keepdims=True) + 1e-6) * nw_ref[...]

    for i in range(n_q):
        o_ref[i] = (ot_ref[i].T * _silu(z_ref[i])).astype(o_ref.dtype)


def _gdn_sample(proj_s, conv_state, state_t, conv_w, a_log, dt_bias, norm_w):
    n_q, bsz, _ = proj_s.shape
    hd = HEAD_DIM
    pw = 2 * hd
    n_pairs = N_HEADS // 2
    base = D_CONV // pw
    col = jnp.zeros((LANES, 1), F32)
    alog_col = col.at[N_HEADS:2 * N_HEADS, 0].set(a_log)
    dtb_col = col.at[N_HEADS:2 * N_HEADS, 0].set(dt_bias)
    nw_col = norm_w.reshape(hd, 1)

    def xspec(part):
        return pl.BlockSpec((n_q, bsz, pw), lambda p: (0, 0, base + part * n_pairs + p))

    def cspec(rows, part):
        return pl.BlockSpec((rows, bsz, pw) if rows else (GDN_CONV, pw),
                            (lambda p: (0, 0, part * n_pairs + p)) if rows
                            else (lambda p: (0, part * n_pairs + p)))

    const = lambda p: (0, 0)
    state_spec = pl.BlockSpec((2, hd, hd, bsz), lambda p: (p, 0, 0, 0))
    return pl.pallas_call(
        _gdn_sample_kernel,
        grid=(n_pairs,),
        in_specs=[xspec(0), xspec(1), xspec(2),
                  cspec(GDN_CONV - 1, 0), cspec(GDN_CONV - 1, 1), cspec(GDN_CONV - 1, 2),
                  cspec(0, 0), cspec(0, 1), cspec(0, 2),
                  pl.BlockSpec((n_q, bsz, pw), lambda p: (0, 0, D_IN_MAIN // pw - n_pairs + p)),
                  pl.BlockSpec((n_q, bsz, LANES), lambda p: (0, 0, D_IN_MAIN // LANES)),
                  pl.BlockSpec((LANES, 1), const), pl.BlockSpec((LANES, 1), const),
                  pl.BlockSpec((hd, 1), const),
                  state_spec],
        out_specs=[pl.BlockSpec((n_q, bsz, pw), lambda p: (0, 0, p)), state_spec],
        out_shape=[jax.ShapeDtypeStruct((n_q, bsz, D_HEADS), BF16),
                   jax.ShapeDtypeStruct(state_t.shape, F32)],
        scratch_shapes=[pltpu.VMEM((2, hd, bsz), F32),
                        pltpu.VMEM((2, n_q, LANES, bsz), F32),
                        pltpu.VMEM((n_q, pw, bsz), F32)],
        compiler_params=_cparams(1),
        name="gdn_sample",
    )(proj_s, proj_s, proj_s, conv_state, conv_state, conv_state, conv_w, conv_w, conv_w,
      proj_s, proj_s, alog_col, dtb_col, nw_col, state_t)


TM = 256
TM_MOE = 1024
TJ_MOE = 512
GDN_ROWS = 256
SAMPLE_PAD = 8


def kernel(x_prompt, x_sample, cache_k, cache_v, state_conv, state_gdn, page_table, c_prompt, c_sample,
           ln_in_g, ln_in_b, w_mod, b_mod, w_in, conv_w, a_log, dt_bias, gdn_norm_w, w_out, rel_table,
           ln_g, ln_b, ffn_w_gate, ffn_w_up, ffn_w_down, moe_router, moe_w_gate, moe_w_up, moe_w_down):
    bp, seq, d = x_prompt.shape
    bs, n_q, _ = x_sample.shape
    depth = w_in.shape[0]
    tp, ts = bp * seq, bs * n_q
    h, hd = N_HEADS, HEAD_DIM
    assert bs == MOD_ROWS and seq % TM == 0 and ts % TM == 0 and seq % MOBA_BLOCK == 0
    assert depth == 2 and GDN_CONV - 1 <= n_q <= SAMPLE_PAD and GDN_CHUNK == HEAD_DIM
    alpha = (2 * depth) ** 0.25

    def groups(tm):
        return lambda i: jnp.minimum(i // (seq // tm), bp)

    n_c = bp + bs
    c_all = jnp.pad(jnp.concatenate([c_prompt, c_sample]), ((0, (-n_c) % 8), (0, 0)))
    mod = _mod_vectors(c_all, w_mod, b_mod).reshape(depth, -1, 6, d)
    mod_p = jnp.broadcast_to(mod[:, :bp].transpose(0, 2, 1, 3)[:, :, :, None, :],
                             (depth, 6, bp, MOD_ROWS, d))
    mod_s = mod[:, bp:n_c].transpose(0, 2, 1, 3)[:, :, None]
    mod_all = jnp.concatenate([mod_p, mod_s], axis=2)

    x_all = jnp.concatenate([x_prompt.reshape(tp, d), x_sample.transpose(1, 0, 2).reshape(ts, d)])
    x, u = _ln_mod(x_all, ln_in_g, ln_in_b, mod_all, 0, TM, groups(TM))

    bias_tab = _moba_bias_tables(rel_table)
    past_len = page_table.shape[1] * PAGE_SIZE
    bias_past, bias_own = _moba_sample_bias(rel_table, past_len, n_q)
    cache_kt = cache_k.transpose(0, 1, 3, 4, 2)
    cache_vt = cache_v.transpose(0, 1, 3, 4, 2)
    state_t = state_gdn.transpose(0, 2, 3, 4, 1)
    conv_t = state_conv.transpose(0, 2, 1, 3)

    k_p, v_p, conv_p, gdn_p, k_s, v_s, conv_s, gdn_s = [], [], [], [], [], [], [], []
    for layer in range(depth):
        w_l = w_in[layer]
        w_cat = jnp.concatenate(
            [w_l[:, :D_IN_MAIN], jnp.pad(w_l[:, D_IN_MAIN:], ((0, 0), (0, LANES - 2 * h)))],
            axis=1).astype(BF16)
        proj, qkv = _proj_in(u, w_cat, TM)
        proj_s = proj[tp:].reshape(n_q, bs, -1)

        def slots_s(cols):
            return jnp.pad(proj_s[:, :, cols].transpose(1, 0, 2), ((0, 0), (0, SAMPLE_PAD - n_q), (0, 0)))

        k_sr = proj_s[:, :, D_HEADS:2 * D_HEADS].transpose(1, 0, 2).reshape(bs, n_q, h, hd)
        v_sr = proj_s[:, :, 2 * D_HEADS:3 * D_HEADS].transpose(1, 0, 2).reshape(bs, n_q, h, hd)
        att_p = _moba_prompt(qkv, bias_tab, bp, seq)
        att_s = _moba_sample(slots_s(slice(0, D_HEADS)), slots_s(slice(D_HEADS, 2 * D_HEADS)),
                             slots_s(slice(2 * D_HEADS, 3 * D_HEADS)), cache_kt, cache_vt, page_table,
                             layer, bias_past, bias_own, n_q)
        att_s = att_s.reshape(bs, h, SAMPLE_PAD, hd)[:, :, :n_q].transpose(2, 0, 1, 3)
        att = jnp.concatenate([att_p, att_s.reshape(ts, D_HEADS).astype(BF16)])

        o_p, conv_new_p, s_new_p = _gdn_prompt(proj, bp, seq, GDN_ROWS, conv_w[layer], a_log[layer],
                                               dt_bias[layer], gdn_norm_w[layer])
        o_s, s_new_s = _gdn_sample(proj_s, conv_t[layer], state_t[layer], conv_w[layer], a_log[layer],
                                   dt_bias[layer], gdn_norm_w[layer])
        s_new_s = s_new_s.transpose(3, 0, 1, 2)
        conv_new_s = proj_s[n_q - (GDN_CONV - 1):, :, 3 * D_HEADS:3 * D_HEADS + D_CONV].transpose(1, 0, 2)
        gdn_o = jnp.concatenate([o_p, o_s.reshape(ts, D_HEADS)])

        w_o = w_out[layer].astype(BF16)
        i = layer // 2
        if layer % 2 == 0:
            x, u = _res_ln([att, gdn_o], [w_o[:D_HEADS], w_o[D_HEADS:]], x, mod_all, layer, 2, layer,
                           (4, 3), ln_g[layer, 0], ln_b[layer, 0], alpha, TM, groups(TM))
            hdn = _ffn_up(u, ffn_w_gate[i].astype(BF16), ffn_w_up[i].astype(BF16), TM)
            x, u = _res_ln([hdn], [ffn_w_down[i].astype(BF16)], x, mod_all, layer, 5, layer + 1,
                           (1, 0), ln_g[layer, 1], ln_b[layer, 1], alpha, TM, groups(TM))
        else:
            w_r = jnp.pad(moe_router[i], ((0, 0), (0, LANES - N_EXPERTS)))
            x, u, route = _res_ln([att, gdn_o], [w_o[:D_HEADS], w_o[D_HEADS:]], x, mod_all, layer, 2,
                                  layer, (4, 3), ln_g[layer, 0], ln_b[layer, 0], alpha, TM,
                                  groups(TM), w_router=w_r)
            tile_expert, n_valid, token_of_slot, slot = _moe_routing(route, TM_MOE)
            y = _moe_ffn(u, tile_expert, n_valid, token_of_slot, moe_w_gate[i], moe_w_up[i],
                         moe_w_down[i], TM_MOE, TJ_MOE)
            n_t = (tp + ts) // TM
            slots = slot.reshape(n_t, TM, TOP_K).transpose(0, 2, 1).reshape(n_t, 1, TOP_K * TM)
            x = _moe_combine(y, slots, route, x, mod_all, layer, 5, ln_g[layer, 1], ln_b[layer, 1],
                             alpha, TM, groups(TM))

        k_p.append(proj[:tp, D_HEADS:2 * D_HEADS].reshape(bp, seq, h, hd))
        v_p.append(proj[:tp, 2 * D_HEADS:3 * D_HEADS].reshape(bp, seq, h, hd))
        conv_p.append(conv_new_p)
        gdn_p.append(s_new_p)
        k_s.append(k_sr)
        v_s.append(v_sr)
        conv_s.append(conv_new_s)
        gdn_s.append(s_new_s)

    y_prompt = x[:tp].reshape(bp, seq, d)
    y_sample = x[tp:].reshape(n_q, bs, d).transpose(1, 0, 2)
    return (y_prompt, y_sample, jnp.stack(k_p), jnp.stack(v_p), jnp.stack(conv_p), jnp.stack(gdn_p),
            jnp.stack(k_s), jnp.stack(v_s), jnp.stack(conv_s), jnp.stack(gdn_s))
```

```python
import functools
import math

import numpy as np
import jax
import jax.numpy as jnp
from jax import lax
from jax.experimental import pallas as pl
from jax.experimental.pallas import tpu as pltpu

F32 = jnp.float32
BF16 = jnp.bfloat16

D_MODEL = 1024
HEAD_DIM = 64
N_HEADS = 8
D_HEADS = N_HEADS * HEAD_DIM
D_CONV = 3 * D_HEADS
D_IN = 3 * D_HEADS + D_CONV + D_HEADS + 2 * N_HEADS
D_IN_MAIN = D_IN - 2 * N_HEADS
LANES = 128
D_IN_PAD = D_IN_MAIN + LANES
MOBA_BLOCK = 256
MOBA_TOPK = 3
PAGE_SIZE = 128
GDN_CONV = 4
GDN_CHUNK = 64
N_BUCKETS = 32
MAX_DISTANCE = 128
N_EXPERTS = 8
TOP_K = 2
LN_EPS = 1e-5
NEG = -1e30
MOD_ROWS = 128
VMEM_LIMIT = 56 * 1024 * 1024

HIGHEST = lax.Precision.HIGHEST


def _cparams(n_axes):
    return pltpu.CompilerParams(dimension_semantics=("arbitrary",) * n_axes,
                                vmem_limit_bytes=VMEM_LIMIT)


def _silu(x):
    return x * (1.0 / (1.0 + jnp.exp(-x)))


def _layer_norm(x, g, b):
    mu = jnp.mean(x, axis=-1, keepdims=True)
    xc = x - mu
    var = jnp.mean(xc * xc, axis=-1, keepdims=True)
    return xc * lax.rsqrt(var + LN_EPS) * g + b


def _modulate(x, scale, shift):
    tm, d = x.shape
    x3 = x.reshape(tm // MOD_ROWS, MOD_ROWS, d)
    return (x3 * (1.0 + scale[None]) + shift[None]).reshape(tm, d)


def _gated(x, gate, h):
    tm, d = x.shape
    x3 = x.reshape(tm // MOD_ROWS, MOD_ROWS, d)
    h3 = h.reshape(tm // MOD_ROWS, MOD_ROWS, d)
    return (x3 + (1.0 + gate[None]) * h3).reshape(tm, d)


def _mod_kernel(c_ref, w_ref, b_ref, o_ref):
    a = _silu(c_ref[...])
    o_ref[...] = jnp.dot(a, w_ref[...], precision=HIGHEST, preferred_element_type=F32) + b_ref[...]


def _mod_vectors(c_all, w_mod, b_mod, tn=1536):
    depth, d, n = w_mod.shape
    rows = c_all.shape[0]
    return pl.pallas_call(
        _mod_kernel,
        grid=(depth, n // tn),
        in_specs=[pl.BlockSpec((rows, d), lambda l, j: (0, 0)),
                  pl.BlockSpec((None, d, tn), lambda l, j: (l, 0, j)),
                  pl.BlockSpec((None, 1, tn), lambda l, j: (l, 0, j))],
        out_specs=pl.BlockSpec((None, rows, tn), lambda l, j: (l, 0, j)),
        out_shape=jax.ShapeDtypeStruct((depth, rows, n), F32),
        compiler_params=_cparams(2),
        name="mod_vectors",
    )(c_all, w_mod, b_mod.reshape(depth, 1, n))


def _ln_mod_kernel(x_ref, g_ref, b_ref, mod_ref, xn_ref, u_ref):
    xn = _layer_norm(x_ref[...], g_ref[...], b_ref[...])
    xn_ref[...] = xn
    u_ref[...] = _modulate(xn, mod_ref[1], mod_ref[0]).astype(BF16)


def _mod_spec(layer, group_of_tile):
    return pl.BlockSpec((None, 6, None, MOD_ROWS, D_MODEL),
                        lambda i: (layer, 0, group_of_tile(i), 0, 0))


def _ln_mod(x, g, b, mod_all, layer, tm, group_of_tile):
    t, d = x.shape
    row = pl.BlockSpec((tm, d), lambda i: (i, 0))
    vec = pl.BlockSpec((1, d), lambda i: (0, 0))
    return pl.pallas_call(
        _ln_mod_kernel,
        grid=(t // tm,),
        in_specs=[row, vec, vec, _mod_spec(layer, group_of_tile)],
        out_specs=[row, row],
        out_shape=[jax.ShapeDtypeStruct((t, d), F32), jax.ShapeDtypeStruct((t, d), BF16)],
        compiler_params=_cparams(1),
        name="ln_mod",
    )(x, g.reshape(1, d), b.reshape(1, d), mod_all)


def _proj_kernel(u_ref, w_ref, proj_ref, qkv_ref, *, chunk):
    u = u_ref[...]
    n = w_ref.shape[1]
    n_qkv = qkv_ref.shape[1]
    for c0 in range(0, n, chunk):
        c1 = min(c0 + chunk, n)
        r = jnp.dot(u, w_ref[:, c0:c1], preferred_element_type=F32)
        proj_ref[:, c0:c1] = r
        if c1 <= n_qkv:
            qkv_ref[:, c0:c1] = r.astype(BF16)


def _proj_in(u, w, tm, chunk=512):
    t, d = u.shape
    n = w.shape[1]
    return pl.pallas_call(
        functools.partial(_proj_kernel, chunk=chunk),
        grid=(t // tm,),
        in_specs=[pl.BlockSpec((tm, d), lambda i: (i, 0)),
                  pl.BlockSpec((d, n), lambda i: (0, 0))],
        out_specs=[pl.BlockSpec((tm, n), lambda i: (i, 0)),
                   pl.BlockSpec((tm, 3 * D_HEADS), lambda i: (i, 0))],
        out_shape=[jax.ShapeDtypeStruct((t, n), F32),
                   jax.ShapeDtypeStruct((t, 3 * D_HEADS), BF16)],
        compiler_params=_cparams(1),
        name="proj_in",
    )(u, w)


def _top2_gates(logits):
    lane = lax.broadcasted_iota(jnp.int32, logits.shape, 1)
    v1 = jnp.max(logits, axis=-1, keepdims=True)
    i1 = jnp.min(jnp.where(logits == v1, lane, LANES), axis=-1, keepdims=True)
    rest = jnp.where(lane == i1, -jnp.inf, logits)
    v2 = jnp.max(rest, axis=-1, keepdims=True)
    i2 = jnp.min(jnp.where(rest == v2, lane, LANES), axis=-1, keepdims=True)
    e2 = jnp.exp(v2 - v1)
    inv = 1.0 / (1.0 + e2)
    route = jnp.where(lane == 0, i1.astype(F32), 0.0) + jnp.where(lane == 1, i2.astype(F32), 0.0)
    return route + jnp.where(lane == 2, inv, 0.0) + jnp.where(lane == 3, e2 * inv, 0.0)


def _res_ln_kernel(*refs, n_in, alpha, next_rows, with_router):
    a_refs = refs[:n_in]
    w_refs = refs[n_in:2 * n_in]
    x_ref, modg_ref, modn_ref, g_ref, b_ref = refs[2 * n_in:2 * n_in + 5]
    pos = 2 * n_in + 5
    if with_router:
        wr_ref = refs[pos]
        pos += 1
    outs = refs[pos:]
    h = jnp.dot(a_refs[0][...], w_refs[0][...], preferred_element_type=F32)
    for a_ref, w_ref in zip(a_refs[1:], w_refs[1:]):
        h = h + jnp.dot(a_ref[...], w_ref[...], preferred_element_type=F32)
    xn = _layer_norm(_gated(alpha * x_ref[...], modg_ref[...], h), g_ref[...], b_ref[...])
    outs[0][...] = xn
    if next_rows is not None:
        u = _modulate(xn, modn_ref[next_rows[0]], modn_ref[next_rows[1]])
        outs[1][...] = u.astype(outs[1].dtype)
        if with_router:
            logits = jnp.dot(u, wr_ref[...], precision=HIGHEST, preferred_element_type=F32)
            lane = lax.broadcasted_iota(jnp.int32, logits.shape, 1)
            outs[2][...] = _top2_gates(jnp.where(lane < N_EXPERTS, logits, -jnp.inf))


def _res_ln(a_list, w_list, x, mod_all, gate_layer, gate_row, next_layer, next_rows, ln_g, ln_b,
            alpha, tm, group_of_tile, w_router=None):
    t, d = x.shape
    n_in = len(a_list)
    row = pl.BlockSpec((tm, d), lambda i: (i, 0))
    vec = pl.BlockSpec((1, d), lambda i: (0, 0))
    in_specs = [pl.BlockSpec((tm, a.shape[1]), lambda i: (i, 0)) for a in a_list]
    in_specs += [pl.BlockSpec(w.shape, lambda i: (0, 0)) for w in w_list]
    in_specs += [row,
                 pl.BlockSpec((None, None, None, MOD_ROWS, d),
                              lambda i: (gate_layer, gate_row, group_of_tile(i), 0, 0)),
                 _mod_spec(next_layer if next_rows is not None else gate_layer, group_of_tile),
                 vec, vec]
    args = list(a_list) + list(w_list) + [x, mod_all, mod_all, ln_g.reshape(1, d), ln_b.reshape(1, d)]
    out_specs = [row]
    out_shape = [jax.ShapeDtypeStruct((t, d), F32)]
    with_router = w_router is not None
    if next_rows is not None:
        out_specs.append(row)
        out_shape.append(jax.ShapeDtypeStruct((t, d), F32 if with_router else BF16))
    if with_router:
        in_specs.append(pl.BlockSpec(w_router.shape, lambda i: (0, 0)))
        args.append(w_router)
        out_specs.append(pl.BlockSpec((tm, LANES), lambda i: (i, 0)))
        out_shape.append(jax.ShapeDtypeStruct((t, LANES), F32))
    return pl.pallas_call(
        functools.partial(_res_ln_kernel, n_in=n_in, alpha=alpha, next_rows=next_rows,
                          with_router=with_router),
        grid=(t // tm,),
        in_specs=in_specs,
        out_specs=out_specs,
        out_shape=out_shape,
        compiler_params=_cparams(1),
        name="res_ln",
    )(*args)


def _ffn_up_kernel(u_ref, wg_ref, wu_ref, o_ref, *, chunk):
    u = u_ref[...]
    n = o_ref.shape[1]
    for c0 in range(0, n, chunk):
        c1 = min(c0 + chunk, n)
        hg = jnp.dot(u, wg_ref[:, c0:c1], preferred_element_type=F32)
        hu = jnp.dot(u, wu_ref[:, c0:c1], preferred_element_type=F32)
        o_ref[:, c0:c1] = (_silu(hg) * hu).astype(BF16)


def _ffn_up(u, wg, wu, tm, chunk=256):
    t, d = u.shape
    n = wg.shape[1]
    wspec = pl.BlockSpec((d, n), lambda i: (0, 0))
    return pl.pallas_call(
        functools.partial(_ffn_up_kernel, chunk=chunk),
        grid=(t // tm,),
        in_specs=[pl.BlockSpec((tm, d), lambda i: (i, 0)), wspec, wspec],
        out_specs=pl.BlockSpec((tm, n), lambda i: (i, 0)),
        out_shape=jax.ShapeDtypeStruct((t, n), BF16),
        compiler_params=_cparams(1),
        name="ffn_up",
    )(u, wg, wu)


def _row_copy(src_hbm, src_row, dst_ref, r, sem):
    return pltpu.make_async_copy(src_hbm.at[pl.ds(src_row, 1)], dst_ref.at[pl.ds(r, 1)], sem)


def _gather_start(idx_ref, idx_base, src_hbm, dst_ref, sem, n_rows):
    def body(r, carry):
        _row_copy(src_hbm, idx_ref[0, idx_base + r], dst_ref, r, sem).start()
        return carry

    lax.fori_loop(0, n_rows, body, 0, unroll=8)


def _gather_wait(src_hbm, dst_ref, sem):
    pltpu.make_async_copy(src_hbm.at[pl.ds(0, dst_ref.shape[0])], dst_ref, sem).wait()


def _moe_ffn_kernel(te_ref, nv_ref, tos_ref, u_hbm, wg_ref, wu_ref, wd_ref, y_ref,
                    xg_ref, xb_ref, acc_ref, sem):
    del te_ref
    i = pl.program_id(0)
    j = pl.program_id(1)
    valid = i < nv_ref[0]

    @pl.when(valid & (j == 0))
    def _():
        _gather_start(tos_ref, 0, u_hbm, xg_ref, sem, xg_ref.shape[0])
        _gather_wait(u_hbm, xg_ref, sem)
        xb_ref[...] = xg_ref[...].astype(BF16)
        acc_ref[...] = jnp.zeros_like(acc_ref)

    @pl.when(valid)
    def _():
        x = xb_ref[...]
        hg = jnp.dot(x, wg_ref[...].astype(BF16), preferred_element_type=F32)
        hu = jnp.dot(x, wu_ref[...].astype(BF16), preferred_element_type=F32)
        acc_ref[...] += jnp.dot((_silu(hg) * hu).astype(BF16), wd_ref[...].astype(BF16),
                                preferred_element_type=F32)

    last = j == pl.num_programs(1) - 1

    @pl.when(valid & last)
    def _():
        y_ref[...] = acc_ref[...]

    @pl.when(jnp.logical_not(valid) & last)
    def _():
        y_ref[...] = jnp.zeros_like(y_ref)


def _moe_ffn(u, tile_expert, n_valid, token_of_slot, wg, wu, wd, tm, tj):
    n_tiles = tile_expert.shape[0]
    d = u.shape[1]
    f = wg.shape[2]
    nj = f // tj

    def jj(i, j, nv):
        return jnp.where(i < nv[0], j, nj - 1)

    grid_spec = pltpu.PrefetchScalarGridSpec(
        num_scalar_prefetch=2,
        grid=(n_tiles, nj),
        in_specs=[pl.BlockSpec((None, 1, tm), lambda i, j, te, nv: (i, 0, 0), memory_space=pltpu.SMEM),
                  pl.BlockSpec(memory_space=pl.ANY),
                  pl.BlockSpec((None, d, tj), lambda i, j, te, nv: (te[i], 0, jj(i, j, nv))),
                  pl.BlockSpec((None, d, tj), lambda i, j, te, nv: (te[i], 0, jj(i, j, nv))),
                  pl.BlockSpec((None, tj, d), lambda i, j, te, nv: (te[i], jj(i, j, nv), 0))],
        out_specs=pl.BlockSpec((tm, d), lambda i, j, te, nv: (i, 0)),
        scratch_shapes=[pltpu.VMEM((tm, d), F32), pltpu.VMEM((tm, d), BF16),
                        pltpu.VMEM((tm, d), F32), pltpu.SemaphoreType.DMA(())])
    return pl.pallas_call(
        _moe_ffn_kernel,
        grid_spec=grid_spec,
        out_shape=jax.ShapeDtypeStruct((n_tiles * tm, d), F32),
        compiler_params=_cparams(2),
        name="moe_ffn",
    )(tile_expert, n_valid, token_of_slot, u, wg, wu, wd)


def _moe_combine_kernel(slots_ref, y_hbm, route_ref, x_ref, modg_ref, g_ref, b_ref, o_ref,
                        buf_ref, sem, *, alpha):
    rows = x_ref.shape[0]
    for k in range(TOP_K):
        _gather_start(slots_ref, k * rows, y_hbm, buf_ref.at[k], sem, rows)
    for k in range(TOP_K):
        _gather_wait(y_hbm, buf_ref.at[k], sem)
    route = route_ref[...]
    f = route[:, 2:3] * buf_ref[0] + route[:, 3:4] * buf_ref[1]
    o_ref[...] = _layer_norm(_gated(alpha * x_ref[...], modg_ref[...], f), g_ref[...], b_ref[...])


def _moe_combine(y, slots, route, x, mod_all, gate_layer, gate_row, ln_g, ln_b, alpha, tm,
                 group_of_tile):
    t, d = x.shape
    row = pl.BlockSpec((tm, d), lambda i: (i, 0))
    vec = pl.BlockSpec((1, d), lambda i: (0, 0))
    return pl.pallas_call(
        functools.partial(_moe_combine_kernel, alpha=alpha),
        grid=(t // tm,),
        in_specs=[pl.BlockSpec((None, 1, 2 * tm), lambda i: (i, 0, 0), memory_space=pltpu.SMEM),
                  pl.BlockSpec(memory_space=pl.ANY),
                  pl.BlockSpec((tm, LANES), lambda i: (i, 0)),
                  row,
                  pl.BlockSpec((None, None, None, MOD_ROWS, d),
                               lambda i: (gate_layer, gate_row, group_of_tile(i), 0, 0)),
                  vec, vec],
        out_specs=row,
        out_shape=jax.ShapeDtypeStruct((t, d), F32),
        scratch_shapes=[pltpu.VMEM((2, tm, d), F32), pltpu.SemaphoreType.DMA(())],
        compiler_params=_cparams(1),
        name="moe_combine",
    )(slots, y, route, x, mod_all, ln_g.reshape(1, d), ln_b.reshape(1, d))


def _moe_routing(route, tm):
    t = route.shape[0]
    n_assign = TOP_K * t
    n_tiles = -(-n_assign // tm) + N_EXPERTS
    experts = route[:, :TOP_K].astype(jnp.int32).reshape(n_assign)
    onehot = (experts[:, None] == jnp.arange(N_EXPERTS, dtype=jnp.int32)[None, :]).astype(jnp.int32)
    rank = jnp.sum((jnp.cumsum(onehot, axis=0) - onehot) * onehot, axis=1)
    count = jnp.sum(onehot, axis=0)
    padded = (count + tm - 1) // tm * tm
    ends = jnp.cumsum(padded)
    slot = jnp.sum(onehot * (ends - padded)[None, :], axis=1) + rank
    n_valid = ends[-1] // tm
    starts = jnp.arange(n_tiles, dtype=jnp.int32) * tm
    tile_expert = jnp.minimum(jnp.sum((starts[:, None] >= ends[None, :]).astype(jnp.int32), axis=1),
                              N_EXPERTS - 1)
    tile_expert = jnp.where(starts < ends[-1], tile_expert, tile_expert[jnp.maximum(n_valid - 1, 0)])
    token_of_slot = jnp.zeros((n_tiles * tm,), jnp.int32).at[slot].set(
        jnp.arange(n_assign, dtype=jnp.int32) // TOP_K)
    return (tile_expert.astype(jnp.int32), n_valid.reshape(1).astype(jnp.int32),
            token_of_slot.reshape(n_tiles, 1, tm), slot.reshape(t, TOP_K).astype(jnp.int32))


def _rel_bucket_table(max_dist):
    n = np.arange(max_dist + 1)
    max_exact = N_BUCKETS // 2
    ratio = np.log(np.maximum(n, 1).astype(np.float32) / np.float32(max_exact))
    large = max_exact + (ratio / np.float32(math.log(MAX_DISTANCE / max_exact))
                         * np.float32(N_BUCKETS - max_exact)).astype(np.int32)
    large = np.minimum(large, N_BUCKETS - 1)
    return np.where(n < max_exact, n, large).astype(np.int32)


def _bias_lookup(rel_table, bucket_idx):
    onehot = jnp.asarray(np.eye(N_BUCKETS, dtype=np.float32)[bucket_idx.reshape(-1)])
    vals = jnp.dot(onehot, rel_table, precision=HIGHEST)
    return vals.T.reshape((rel_table.shape[1],) + bucket_idx.shape)


def _select_top_blocks(gate, n_valid, ksel):
    blk = lax.broadcasted_iota(jnp.int32, gate.shape, 0)
    nb = gate.shape[0]
    g = jnp.where(blk < n_valid, gate, -jnp.inf)
    keep = blk == n_valid
    for _ in range(ksel):
        mx = jnp.max(g, axis=0, keepdims=True)
        first = jnp.min(jnp.where(g == mx, blk, nb), axis=0, keepdims=True)
        hit = blk == first
        keep = keep | (hit & (blk < n_valid))
        g = jnp.where(hit, -jnp.inf, g)
    return jnp.where(keep, 0.0, NEG)


def _moba_prompt_kernel(q_ref, k_ref, v_ref, bias_ref, o_ref,
                        kmean_ref, vt_ref, qm_ref, sel_ref, m_ref, l_ref, acc_ref, *, ksel):
    qt = pl.program_id(1)
    seq = k_ref.shape[0]
    nb = seq // MOBA_BLOCK
    gw = 4 * HEAD_DIM
    n_groups = N_HEADS // 4
    nt = (((1,), (1,)), ((), ()))

    @pl.when(qt == 0)
    def _():
        row = lax.broadcasted_iota(jnp.int32, (nb, seq), 0)
        col = lax.broadcasted_iota(jnp.int32, (nb, seq), 1)
        pool = jnp.where(col // MOBA_BLOCK == row, 1.0, 0.0).astype(BF16)
        kmean_ref[...] = jnp.dot(pool, k_ref[...], preferred_element_type=F32) * (1.0 / MOBA_BLOCK)
        for n in range(nb):
            vt_ref[n] = v_ref[n * MOBA_BLOCK:(n + 1) * MOBA_BLOCK, :].astype(F32).T.astype(BF16)

    head_of_lane = lax.broadcasted_iota(jnp.int32, (1, gw), 1) // HEAD_DIM
    q = q_ref[...] * (HEAD_DIM ** -0.5)
    kmean = kmean_ref[...].astype(BF16)
    for h in range(N_HEADS):
        g, hh = divmod(h, 4)
        qm = jnp.where(head_of_lane == hh, q[:, g * gw:(g + 1) * gw], 0.0).astype(BF16)
        qm_ref[h] = qm
        gate = lax.dot_general(kmean[:, g * gw:(g + 1) * gw], qm, nt, preferred_element_type=F32)
        sel_ref[h] = _select_top_blocks(gate, qt, ksel)
        m_ref[h] = jnp.full(m_ref.shape[1:], NEG, F32)
        l_ref[h] = jnp.zeros(l_ref.shape[1:], F32)
    acc_ref[...] = jnp.zeros_like(acc_ref)

    def body(step, carry):
        n = qt - step
        start = pl.multiple_of(n * MOBA_BLOCK, MOBA_BLOCK)
        kb = k_ref[pl.ds(start, MOBA_BLOCK), :]
        vt = vt_ref[n]
        bias_idx = jnp.minimum(step, 2)
        scores = [lax.dot_general(kb[:, (h // 4) * gw:(h // 4 + 1) * gw], qm_ref[h], nt,
                                  preferred_element_type=F32) for h in range(N_HEADS)]
        probs, alphas = [], []
        for h in range(N_HEADS):
            s = scores[h] + bias_ref[h, bias_idx] + sel_ref[h, pl.ds(n, 1), :]
            m_old = m_ref[h]
            m_new = jnp.maximum(m_old, jnp.max(s, axis=0, keepdims=True))
            alpha = jnp.exp(m_old - m_new)
            p = jnp.exp(s - m_new)
            l_ref[h] = alpha * l_ref[h] + jnp.sum(p, axis=0, keepdims=True)
            m_ref[h] = m_new
            probs.append(p.astype(BF16))
            alphas.append(alpha)
        outs = [jnp.dot(vt[h * HEAD_DIM:(h + 1) * HEAD_DIM, :], probs[h],
                        preferred_element_type=F32) for h in range(N_HEADS)]
        for h in range(N_HEADS):
            g, hh = divmod(h, 4)
            rows = slice(hh * HEAD_DIM, (hh + 1) * HEAD_DIM)
            acc_ref[g, rows, :] = alphas[h] * acc_ref[g, rows, :] + outs[h]
        return carry

    lax.fori_loop(0, qt + 1, body, 0)

    for g in range(n_groups):
        o_t = jnp.concatenate(
            [acc_ref[g, hh * HEAD_DIM:(hh + 1) * HEAD_DIM, :] * (1.0 / l_ref[4 * g + hh])
             for hh in range(4)], axis=0)
        o_ref[:, g * gw:(g + 1) * gw] = o_t.T.astype(o_ref.dtype)


def _moba_bias_tables(rel_table):
    bucket = _rel_bucket_table(2 * MOBA_BLOCK)
    i = np.arange(MOBA_BLOCK)[None, :]
    j = np.arange(MOBA_BLOCK)[:, None]
    own = jnp.where(jnp.asarray(i >= j), _bias_lookup(rel_table, bucket[np.maximum(i - j, 0)]), NEG)
    prev = _bias_lookup(rel_table, bucket[MOBA_BLOCK + i - j])
    far = jnp.broadcast_to(rel_table[N_BUCKETS - 1][:, None, None], prev.shape)
    return jnp.stack([own, prev, far], axis=1).astype(F32)


def _moba_prompt(qkv, bias_tab, batch, seq):
    nb = seq // MOBA_BLOCK
    ksel = min(MOBA_TOPK, nb)
    stat = pltpu.VMEM((N_HEADS, 1, MOBA_BLOCK), F32)
    return pl.pallas_call(
        functools.partial(_moba_prompt_kernel, ksel=ksel),
        grid=(batch, nb),
        in_specs=[pl.BlockSpec((MOBA_BLOCK, D_HEADS), lambda b, t: (b * nb + t, 0)),
                  pl.BlockSpec((seq, D_HEADS), lambda b, t: (b, 1)),
                  pl.BlockSpec((seq, D_HEADS), lambda b, t: (b, 2)),
                  pl.BlockSpec(bias_tab.shape, lambda b, t: (0, 0, 0, 0))],
        out_specs=pl.BlockSpec((MOBA_BLOCK, D_HEADS), lambda b, t: (b * nb + t, 0)),
        out_shape=jax.ShapeDtypeStruct((batch * seq, D_HEADS), BF16),
        scratch_shapes=[pltpu.VMEM((nb, D_HEADS), F32),
                        pltpu.VMEM((nb, D_HEADS, MOBA_BLOCK), BF16),
                        pltpu.VMEM((N_HEADS, MOBA_BLOCK, 4 * HEAD_DIM), BF16),
                        pltpu.VMEM((N_HEADS, nb, MOBA_BLOCK), F32),
                        stat, stat,
                        pltpu.VMEM((N_HEADS // 4, 4 * HEAD_DIM, MOBA_BLOCK), F32)],
        compiler_params=_cparams(2),
        name="moba_prompt",
    )(qkv, qkv, qkv, bias_tab)


def _moba_sample_kernel(pt_ref, q_ref, kn_ref, vn_ref, *refs, ksel, n_q, nbp, pages_per_block):
    del pt_ref
    n_pages = nbp * pages_per_block
    k_refs, v_refs = refs[:n_pages], refs[n_pages:2 * n_pages]
    bias_ref, bown_ref, o_ref, m_ref, l_ref, g_ref, acc_ref = refs[2 * n_pages:]
    nt = (((1,), (1,)), ((), ()))
    dh = N_HEADS * HEAD_DIM
    q8 = q_ref[...]
    slots = q8.shape[0]
    lane_head = lax.broadcasted_iota(jnp.int32, (1, dh), 1) // HEAD_DIM
    qbd = jnp.concatenate([jnp.where(lane_head == hp, q8, 0.0) for hp in range(N_HEADS)], axis=0)
    qs = (qbd * (HEAD_DIM ** -0.5)).astype(BF16)
    ones = jnp.ones((8, PAGE_SIZE), BF16)

    for j in range(nbp):
        pages = range(j * pages_per_block, (j + 1) * pages_per_block)
        kts = [k_refs[pg][...].reshape(dh, PAGE_SIZE).astype(BF16) for pg in pages]
        s = jnp.concatenate([jnp.dot(qs, kt, preferred_element_type=F32) for kt in kts], axis=1)
        s = s + bias_ref[0 if j == nbp - 1 else 1]
        m = jnp.max(s, axis=-1, keepdims=True)
        p32 = jnp.exp(s - m)
        p = p32.astype(BF16)
        acc = None
        for idx, pg in enumerate(pages):
            vt = v_refs[pg][...].reshape(dh, PAGE_SIZE).astype(BF16)
            part = lax.dot_general(p[:, idx * PAGE_SIZE:(idx + 1) * PAGE_SIZE], vt, nt,
                                   preferred_element_type=F32)
            acc = part if acc is None else acc + part
        ksum = functools.reduce(jnp.add, [lax.dot_general(ones, kt, nt, preferred_element_type=F32)
                                          for kt in kts])
        m_ref[j] = m
        l_ref[j] = jnp.sum(p32, axis=-1, keepdims=True)
        acc_ref[j] = acc
        g_ref[j] = jnp.sum(qbd * ksum[0:1, :], axis=-1, keepdims=True) * (1.0 / MOBA_BLOCK)

    s_own = [jnp.sum(qbd * kn_ref[i:i + 1, :], axis=-1, keepdims=True) * (HEAD_DIM ** -0.5)
             + bown_ref[:, i:i + 1] for i in range(n_q)]
    m_o = functools.reduce(jnp.maximum, s_own)
    p_own = [jnp.exp(s_i - m_o) for s_i in s_own]
    l_o = functools.reduce(jnp.add, p_own)
    acc_o = functools.reduce(jnp.add, [p_i * vn_ref[i:i + 1, :] for i, p_i in enumerate(p_own)])

    gg = g_ref[...]
    blk = lax.broadcasted_iota(jnp.int32, gg.shape, 0)
    keep = blk < 0
    for _ in range(ksel):
        mx = jnp.max(gg, axis=0, keepdims=True)
        first = jnp.min(jnp.where(gg == mx, blk, nbp), axis=0, keepdims=True)
        hit = (blk == first) & (gg > -jnp.inf)
        keep = keep | hit
        gg = jnp.where(blk == first, -jnp.inf, gg)
    m_all = m_ref[...]
    m_tot = jnp.maximum(jnp.max(jnp.where(keep, m_all, NEG), axis=0), m_o)
    w = jnp.where(keep, jnp.exp(m_all - m_tot[None]), 0.0)
    w_o = jnp.exp(m_o - m_tot)
    l_tot = jnp.sum(w * l_ref[...], axis=0) + w_o * l_o
    acc_tot = (jnp.sum(w * acc_ref[...], axis=0) + w_o * acc_o) * (1.0 / l_tot)
    row_head = lax.broadcasted_iota(jnp.int32, (N_HEADS * slots, 1), 0) // slots
    acc_tot = jnp.where(row_head == lane_head, acc_tot, 0.0)
    o_ref[...] = functools.reduce(
        jnp.add, [acc_tot[:, hp * HEAD_DIM:(hp + 1) * HEAD_DIM] for hp in range(N_HEADS)])


def _moba_sample_bias(rel_table, past_len, n_q):
    del past_len
    slots = SAMPLE_PAD
    bucket = _rel_bucket_table(MOBA_BLOCK + slots)
    rows_h = np.repeat(np.arange(N_HEADS), slots)
    rows_i = np.minimum(np.tile(np.arange(slots), N_HEADS), n_q - 1)
    pos = np.arange(MOBA_BLOCK)[None, :]
    by_head = _bias_lookup(rel_table, bucket[MOBA_BLOCK + rows_i[:, None] - pos])
    pick = jnp.asarray(np.arange(N_HEADS)[:, None, None] == rows_h[None, :, None])
    near = jnp.sum(jnp.where(pick, by_head, 0.0), axis=0)
    far = jnp.broadcast_to(jnp.repeat(rel_table[N_BUCKETS - 1], slots)[:, None], near.shape)
    past = jnp.stack([near, far]).astype(F32)
    new = np.arange(slots)[None, :]
    own_h = _bias_lookup(rel_table, bucket[np.maximum(rows_i[:, None] - new, 0)])
    own = jnp.sum(jnp.where(pick, own_h, 0.0), axis=0)
    ok = jnp.asarray((new <= rows_i[:, None]) & (new < n_q))
    return past, jnp.where(ok, own, NEG).astype(F32)


def _moba_sample(q_rows, k_new, v_new, cache_k, cache_v, page_table, layer, bias_past, bias_own, n_q):
    bsz, slots, dh = q_rows.shape
    hd = HEAD_DIM
    rows = N_HEADS * slots
    pages_per_block = MOBA_BLOCK // PAGE_SIZE
    nbp = page_table.shape[1] // pages_per_block
    ksel = min(MOBA_TOPK, nbp + 1)

    n_pages = nbp * pages_per_block

    def page_spec(page):
        return pl.BlockSpec((None, None, N_HEADS, hd, PAGE_SIZE),
                            lambda b, pt: (layer, pt[b, page], 0, 0, 0))

    per_seq = lambda b, pt: (b, 0, 0)
    stat = pltpu.VMEM((nbp, rows, 1), F32)
    grid_spec = pltpu.PrefetchScalarGridSpec(
        num_scalar_prefetch=1,
        grid=(bsz,),
        in_specs=([pl.BlockSpec((None, slots, dh), per_seq)] * 3
                  + [page_spec(page) for page in range(n_pages)] * 2
                  + [pl.BlockSpec(bias_past.shape, lambda b, pt: (0, 0, 0)),
                     pl.BlockSpec(bias_own.shape, lambda b, pt: (0, 0))]),
        out_specs=pl.BlockSpec((None, rows, hd), per_seq),
        scratch_shapes=[stat, stat, stat, pltpu.VMEM((nbp, rows, dh), F32)])
    return pl.pallas_call(
        functools.partial(_moba_sample_kernel, ksel=ksel, n_q=n_q, nbp=nbp,
                          pages_per_block=pages_per_block),
        grid_spec=grid_spec,
        out_shape=jax.ShapeDtypeStruct((bsz, rows, hd), F32),
        compiler_params=_cparams(1),
        name="moba_sample",
    )(page_table, q_rows, k_new, v_new, *([cache_k] * n_pages), *([cache_v] * n_pages),
      bias_past, bias_own)


def _softplus(x):
    return jnp.maximum(x, 0.0) + jnp.log1p(jnp.exp(-jnp.abs(x)))


def _bmm(a, b):
    return jnp.matmul(a.astype(BF16), b.astype(BF16), preferred_element_type=F32)


def _bmm_nt(a, b):
    return jnp.einsum("nid,njd->nij", a.astype(BF16), b.astype(BF16), preferred_element_type=F32)


def _dot_f32(a, b):
    return jnp.dot(a, b, precision=HIGHEST, preferred_element_type=F32)


def _split_bf16(a):
    hi = a.astype(BF16)
    return hi, (a - hi.astype(F32)).astype(BF16)


def _bmm_3pass(a, b):
    a_hi, a_lo = _split_bf16(a)
    b_hi, b_lo = _split_bf16(b)
    return (jnp.matmul(a_hi, b_hi, preferred_element_type=F32)
            + jnp.matmul(a_hi, b_lo, preferred_element_type=F32)
            + jnp.matmul(a_lo, b_hi, preferred_element_type=F32))


def _unit_lower_inverse(a, row, col):
    c = a.shape[1]
    eye = jnp.where(row == col, 1.0, 0.0)
    in16 = (row // 16) == (col // 16)
    in32 = (row // 32) == (col // 32)
    nil = jnp.where(in16, -a, 0.0)
    x = eye + nil
    p = _bmm_3pass(nil, nil)
    for _ in range(2):
        xp = _bmm_3pass(jnp.concatenate([x, p], axis=1), p)
        x = x + xp[:, :c]
        p = xp[:, c:]
    x = x + _bmm_3pass(x, p)
    for off in (jnp.where(in32 & jnp.logical_not(in16), a, 0.0),
                jnp.where(in32, 0.0, a)):
        x = x - _bmm(x, _bmm(off, x))
    return x


def _gdn_prepare(q, k, v, k_t, g_col, g_row, beta, row, col):
    c = q.shape[1]
    incl = row >= col
    decay = jnp.where(incl, jnp.exp(jnp.where(incl, g_col - g_row, 0.0)), 0.0)
    qk_kk = _bmm_nt(jnp.concatenate([q, k], axis=1), k)
    a_mat = jnp.where(row > col, qk_kk[:, c:] * decay * beta, 0.0)
    e_g = jnp.exp(g_col)
    rhs = jnp.concatenate([v * beta, k * beta * e_g], axis=2)
    sol = _bmm(_unit_lower_inverse(a_mat, row, col), rhs)
    dv = v.shape[2]
    g_last = g_col[:, c - 1:c, :]
    w_qd = jnp.concatenate([sol[:, :, dv:], q * e_g], axis=1).astype(BF16)
    qk = jnp.where(incl, qk_kk[:, :c] * decay, 0.0).astype(BF16)
    k_dec_t = (k_t * jnp.exp(g_last - g_row)).astype(BF16)
    return sol[:, :, :dv], w_qd, qk, k_dec_t, jnp.exp(g_last)


def _gdn_step(u, w_qd, qk, k_dec_t, chunk_decay, state):
    c = u.shape[1]
    ws_qs = jnp.matmul(w_qd, state.astype(BF16), preferred_element_type=F32)
    v_new = (u - ws_qs[:, :c]).astype(BF16)
    o = ws_qs[:, c:] + jnp.matmul(qk, v_new, preferred_element_type=F32)
    new_state = state * chunk_decay + jnp.matmul(k_dec_t, v_new, preferred_element_type=F32)
    return o, new_state


def _gdn_prompt_kernel(x_ref, z_ref, gab_ref, cw_ref, alog_ref, dtb_ref, nw_ref, seg_ref,
                       o_ref, conv_ref, sout_ref,
                       xbuf_ref, halo_ref, s_ref, u_ref, wqd_ref, qk_ref, kdt_ref, cd_ref, oh_ref):
    t = pl.program_id(1)
    tl = x_ref.shape[0]
    cs = GDN_CHUNK
    n_chunks = tl // cs
    hd = HEAD_DIM

    @pl.when(t == 0)
    def _():
        halo_ref[...] = jnp.zeros_like(halo_ref)
        s_ref[...] = jnp.zeros_like(s_ref)

    x = x_ref[...]
    xbuf_ref[0:8, :] = halo_ref[...]
    xbuf_ref[8:8 + tl, :] = x
    y = cw_ref[0:1, :] * xbuf_ref[5:5 + tl, :]
    for i in range(1, GDN_CONV):
        y = y + cw_ref[i:i + 1, :] * xbuf_ref[5 + i:5 + i + tl, :]
    y = _silu(y)
    tail = x[tl - 8:, :]
    halo_ref[...] = tail
    conv_ref[...] = tail[8 - (GDN_CONV - 1):, :]

    dh = N_HEADS * hd
    seg = seg_ref[...]

    def l2n(a):
        sq_hi, sq_lo = _split_bf16(a * a)
        ss = (jnp.dot(sq_hi, seg, preferred_element_type=F32)
              + jnp.dot(sq_lo, seg, preferred_element_type=F32))
        return a * lax.rsqrt(ss + 1e-6)

    qn = l2n(y[:, :dh]) * (hd ** -0.5)
    kn = l2n(y[:, dh:2 * dh])
    vv = y[:, 2 * dh:]
    kn_t = kn.T

    gab = gab_ref[...]
    beta = 1.0 / (1.0 + jnp.exp(-gab))
    glog = -jnp.exp(alog_ref[...]) * _softplus(gab + dtb_ref[...])

    row = lax.broadcasted_iota(jnp.int32, (cs, cs), 0)
    col = lax.broadcasted_iota(jnp.int32, (cs, cs), 1)
    tril = jnp.where(row >= col, 1.0, 0.0)
    pick = jnp.where(lax.broadcasted_iota(jnp.int32, (N_HEADS, LANES), 1)
                     == lax.broadcasted_iota(jnp.int32, (N_HEADS, LANES), 0) + N_HEADS, 1.0, 0.0)
    parts = {name: [] for name in ("q", "k", "v", "kt", "gc", "gr", "bb")}
    for c in range(n_chunks):
        rows = slice(c * cs, (c + 1) * cs)
        gcum = _dot_f32(tril, glog[rows])
        g_rows = lax.dot_general(pick, gcum, (((1,), (1,)), ((), ())), precision=HIGHEST,
                                 preferred_element_type=F32)
        for h in range(N_HEADS):
            lanes = slice(h * hd, (h + 1) * hd)
            parts["q"].append(qn[rows, lanes])
            parts["k"].append(kn[rows, lanes])
            parts["v"].append(vv[rows, lanes])
            parts["kt"].append(kn_t[lanes, rows])
            parts["gc"].append(jnp.broadcast_to(gcum[:, N_HEADS + h:N_HEADS + h + 1], (cs, cs)))
            parts["gr"].append(g_rows[h:h + 1, :])
            parts["bb"].append(jnp.broadcast_to(beta[rows, h:h + 1], (cs, cs)))
    st = {name: jnp.stack(vals) for name, vals in parts.items()}
    u_all, w_qd, qk, k_dec_t, chunk_decay = _gdn_prepare(
        st["q"], st["k"], st["v"], st["kt"], st["gc"], st["gr"], st["bb"], row, col)
    u_ref[...] = u_all.reshape(u_ref.shape)
    wqd_ref[...] = w_qd.reshape(wqd_ref.shape)
    qk_ref[...] = qk.reshape(qk_ref.shape)
    kdt_ref[...] = k_dec_t.reshape(kdt_ref.shape)
    cd_ref[...] = chunk_decay.reshape(cd_ref.shape)

    def chunk_body(c, carry):
        o, s_new = _gdn_step(u_ref[c], wqd_ref[c], qk_ref[c], kdt_ref[c], cd_ref[c], s_ref[...])
        oh_ref[c] = o
        s_ref[...] = s_new
        return carry

    lax.fori_loop(0, n_chunks, chunk_body, 0)

    z = z_ref[...]
    nw = nw_ref[...]
    for h in range(N_HEADS):
        lanes = slice(h * hd, (h + 1) * hd)
        o = jnp.concatenate([oh_ref[c, h] for c in range(n_chunks)], axis=0)
        o = o * lax.rsqrt(jnp.mean(o * o, axis=-1, keepdims=True) + 1e-6) * nw
        o_ref[:, lanes] = (o * _silu(z[:, lanes])).astype(o_ref.dtype)

    @pl.when(t == pl.num_programs(1) - 1)
    def _():
        sout_ref[...] = s_ref[...]


def _gdn_prompt(proj, batch, seq, tl, conv_w, a_log, dt_bias, norm_w):
    hd = HEAD_DIM
    cs = GDN_CHUNK
    n_steps = seq // tl
    n_chunks = tl // cs
    col_x, col_z, col_gab = 1, D_IN_MAIN // D_HEADS - 1, D_IN_MAIN // LANES
    lane = np.arange(D_HEADS)
    seg = jnp.asarray((lane[:, None] // hd == lane[None, :] // hd).astype(np.float32)).astype(BF16)
    alog_row = jnp.zeros((1, LANES), F32).at[0, N_HEADS:2 * N_HEADS].set(a_log)
    dtb_row = jnp.zeros((1, LANES), F32).at[0, N_HEADS:2 * N_HEADS].set(dt_bias)
    const2 = lambda b, t: (0, 0)
    per_inst = pltpu.VMEM((n_chunks, N_HEADS, cs, hd), F32)
    per_inst_bf16 = pltpu.VMEM((n_chunks, N_HEADS, hd, cs), BF16)
    return pl.pallas_call(
        _gdn_prompt_kernel,
        grid=(batch, n_steps),
        in_specs=[pl.BlockSpec((tl, D_CONV), lambda b, t: (b * n_steps + t, col_x)),
                  pl.BlockSpec((tl, D_HEADS), lambda b, t: (b * n_steps + t, col_z)),
                  pl.BlockSpec((tl, LANES), lambda b, t: (b * n_steps + t, col_gab)),
                  pl.BlockSpec((GDN_CONV, D_CONV), const2),
                  pl.BlockSpec((1, LANES), const2),
                  pl.BlockSpec((1, LANES), const2),
                  pl.BlockSpec((1, hd), const2),
                  pl.BlockSpec((D_HEADS, D_HEADS), const2)],
        out_specs=[pl.BlockSpec((tl, D_HEADS), lambda b, t: (b * n_steps + t, 0)),
                   pl.BlockSpec((None, GDN_CONV - 1, D_CONV), lambda b, t: (b, 0, 0)),
                   pl.BlockSpec((None, N_HEADS, hd, hd), lambda b, t: (b, 0, 0, 0))],
        out_shape=[jax.ShapeDtypeStruct((batch * seq, D_HEADS), BF16),
                   jax.ShapeDtypeStruct((batch, GDN_CONV - 1, D_CONV), F32),
                   jax.ShapeDtypeStruct((batch, N_HEADS, hd, hd), F32)],
        scratch_shapes=[pltpu.VMEM((8 + tl + 8, D_CONV), F32),
                        pltpu.VMEM((8, D_CONV), F32),
                        pltpu.VMEM((N_HEADS, hd, hd), F32),
                        per_inst,
                        pltpu.VMEM((n_chunks, N_HEADS, 2 * cs, hd), BF16),
                        per_inst_bf16, per_inst_bf16,
                        pltpu.VMEM((n_chunks, N_HEADS, 1, hd), F32),
                        per_inst],
        compiler_params=_cparams(2),
        name="gdn_prompt",
    )(proj, proj, proj, conv_w, alog_row, dtb_row, norm_w.reshape(1, hd), seg)


def _gdn_sample_kernel(xq_ref, xk_ref, xv_ref, cq_ref, ck_ref, cv_ref, wq_ref, wk_ref, wv_ref,
                       z_ref, gab_ref, alog_ref, dtb_ref, nw_ref, s_ref,
                       o_ref, sout_ref, kq_ref, gt_ref, ot_ref):
    pair = pl.program_id(0)
    n_q = xq_ref.shape[0]
    hd = HEAD_DIM

    def conv_t(x_ref, c_ref, w_ref):
        xp = [c_ref[i] for i in range(GDN_CONV - 1)] + [x_ref[i] for i in range(n_q)]
        out = []
        for i in range(n_q):
            y = w_ref[0:1, :] * xp[i]
            for m in range(1, GDN_CONV):
                y = y + w_ref[m:m + 1, :] * xp[i + m]
            out.append(_silu(y).T)
        return out

    q_t = conv_t(xq_ref, cq_ref, wq_ref)
    k_t = conv_t(xk_ref, ck_ref, wk_ref)
    v_t = conv_t(xv_ref, cv_ref, wv_ref)

    def l2n(a):
        return a * lax.rsqrt(jnp.sum(a * a, axis=0, keepdims=True) + 1e-6)

    for i in range(n_q):
        gab_t = gab_ref[i].T
        gt_ref[0, i] = 1.0 / (1.0 + jnp.exp(-gab_t))
        gt_ref[1, i] = jnp.exp(-jnp.exp(alog_ref[...]) * _softplus(gab_t + dtb_ref[...]))

    for hh in range(2):
        rows = slice(hh * hd, (hh + 1) * hd)
        head = 2 * pair + hh
        for i in range(n_q):
            kq_ref[0] = l2n(k_t[i][rows])
            kq_ref[1] = l2n(q_t[i][rows]) * (hd ** -0.5)
            beta = gt_ref[0, i, pl.ds(head, 1), :]
            decay = gt_ref[1, i, pl.ds(N_HEADS + head, 1), :]
            src = s_ref if i == 0 else sout_ref

            def ks_body(kk, acc):
                return acc + kq_ref[0, pl.ds(kk, 1), :] * src[hh, kk]

            k_s = lax.fori_loop(0, hd, ks_body, jnp.zeros((hd, k_t[i].shape[1]), F32), unroll=8)
            r = beta * (v_t[i][rows] - decay * k_s)

            def upd_body(kk, acc):
                s_new = decay * src[hh, kk] + kq_ref[0, pl.ds(kk, 1), :] * r
                sout_ref[hh, kk] = s_new
                return acc + kq_ref[1, pl.ds(kk, 1), :] * s_new

            o = lax.fori_loop(0, hd, upd_body, jnp.zeros_like(r), unroll=8)
            ot_ref[i, rows, :] = o * lax.rsqrt(jnp.mean(o * o, axis=0, keepdims=True) + 1e-6) * nw_ref[...]

    for i in range(n_q):
        o_ref[i] = (ot_ref[i].T * _silu(z_ref[i])).astype(o_ref.dtype)


def _gdn_sample(proj_s, conv_state, state_t, conv_w, a_log, dt_bias, norm_w):
    n_q, bsz, _ = proj_s.shape
    hd = HEAD_DIM
    pw = 2 * hd
    n_pairs = N_HEADS // 2
    base = D_CONV // pw
    col = jnp.zeros((LANES, 1), F32)
    alog_col = col.at[N_HEADS:2 * N_HEADS, 0].set(a_log)
    dtb_col = col.at[N_HEADS:2 * N_HEADS, 0].set(dt_bias)
    nw_col = norm_w.reshape(hd, 1)

    def xspec(part):
        return pl.BlockSpec((n_q, bsz, pw), lambda p: (0, 0, base + part * n_pairs + p))

    def cspec(rows, part):
        return pl.BlockSpec((rows, bsz, pw) if rows else (GDN_CONV, pw),
                            (lambda p: (0, 0, part * n_pairs + p)) if rows
                            else (lambda p: (0, part * n_pairs + p)))

    const = lambda p: (0, 0)
    state_spec = pl.BlockSpec((2, hd, hd, bsz), lambda p: (p, 0, 0, 0))
    return pl.pallas_call(
        _gdn_sample_kernel,
        grid=(n_pairs,),
        in_specs=[xspec(0), xspec(1), xspec(2),
                  cspec(GDN_CONV - 1, 0), cspec(GDN_CONV - 1, 1), cspec(GDN_CONV - 1, 2),
                  cspec(0, 0), cspec(0, 1), cspec(0, 2),
                  pl.BlockSpec((n_q, bsz, pw), lambda p: (0, 0, D_IN_MAIN // pw - n_pairs + p)),
                  pl.BlockSpec((n_q, bsz, LANES), lambda p: (0, 0, D_IN_MAIN // LANES)),
                  pl.BlockSpec((LANES, 1), const), pl.BlockSpec((LANES, 1), const),
                  pl.BlockSpec((hd, 1), const),
                  state_spec],
        out_specs=[pl.BlockSpec((n_q, bsz, pw), lambda p: (0, 0, p)), state_spec],
        out_shape=[jax.ShapeDtypeStruct((n_q, bsz, D_HEADS), BF16),
                   jax.ShapeDtypeStruct(state_t.shape, F32)],
        scratch_shapes=[pltpu.VMEM((2, hd, bsz), F32),
                        pltpu.VMEM((2, n_q, LANES, bsz), F32),
                        pltpu.VMEM((n_q, pw, bsz), F32)],
        compiler_params=_cparams(1),
        name="gdn_sample",
    )(proj_s, proj_s, proj_s, conv_state, conv_state, conv_state, conv_w, conv_w, conv_w,
      proj_s, proj_s, alog_col, dtb_col, nw_col, state_t)


TM = 256
TM_MOE = 1024
TJ_MOE = 512
GDN_ROWS = 256
SAMPLE_PAD = 8


def kernel(x_prompt, x_sample, cache_k, cache_v, state_conv, state_gdn, page_table, c_prompt, c_sample,
           ln_in_g, ln_in_b, w_mod, b_mod, w_in, conv_w, a_log, dt_bias, gdn_norm_w, w_out, rel_table,
           ln_g, ln_b, ffn_w_gate, ffn_w_up, ffn_w_down, moe_router, moe_w_gate, moe_w_up, moe_w_down):
    bp, seq, d = x_prompt.shape
    bs, n_q, _ = x_sample.shape
    depth = w_in.shape[0]
    tp, ts = bp * seq, bs * n_q
    h, hd = N_HEADS, HEAD_DIM
    assert bs == MOD_ROWS and seq % TM == 0 and ts % TM == 0 and seq % MOBA_BLOCK == 0
    assert depth == 2 and GDN_CONV - 1 <= n_q <= SAMPLE_PAD and GDN_CHUNK == HEAD_DIM
    alpha = (2 * depth) ** 0.25

    def groups(tm):
        return lambda i: jnp.minimum(i // (seq // tm), bp)

    n_c = bp + bs
    c_all = jnp.pad(jnp.concatenate([c_prompt, c_sample]), ((0, (-n_c) % 8), (0, 0)))
    mod = _mod_vectors(c_all, w_mod, b_mod).reshape(depth, -1, 6, d)
    mod_p = jnp.broadcast_to(mod[:, :bp].transpose(0, 2, 1, 3)[:, :, :, None, :],
                             (depth, 6, bp, MOD_ROWS, d))
    mod_s = mod[:, bp:n_c].transpose(0, 2, 1, 3)[:, :, None]
    mod_all = jnp.concatenate([mod_p, mod_s], axis=2)

    x_all = jnp.concatenate([x_prompt.reshape(tp, d), x_sample.transpose(1, 0, 2).reshape(ts, d)])
    x, u = _ln_mod(x_all, ln_in_g, ln_in_b, mod_all, 0, TM, groups(TM))

    bias_tab = _moba_bias_tables(rel_table)
    past_len = page_table.shape[1] * PAGE_SIZE
    bias_past, bias_own = _moba_sample_bias(rel_table, past_len, n_q)
    cache_kt = cache_k.transpose(0, 1, 3, 4, 2)
    cache_vt = cache_v.transpose(0, 1, 3, 4, 2)
    state_t = state_gdn.transpose(0, 2, 3, 4, 1)
    conv_t = state_conv.transpose(0, 2, 1, 3)

    k_p, v_p, conv_p, gdn_p, k_s, v_s, conv_s, gdn_s = [], [], [], [], [], [], [], []
    for layer in range(depth):
        w_l = w_in[layer]
        w_cat = jnp.concatenate(
            [w_l[:, :D_IN_MAIN], jnp.pad(w_l[:, D_IN_MAIN:], ((0, 0), (0, LANES - 2 * h)))],
            axis=1).astype(BF16)
        proj, qkv = _proj_in(u, w_cat, TM)
        proj_s = proj[tp:].reshape(n_q, bs, -1)

        def slots_s(cols):
            return jnp.pad(proj_s[:, :, cols].transpose(1, 0, 2), ((0, 0), (0, SAMPLE_PAD - n_q), (0, 0)))

        k_sr = proj_s[:, :, D_HEADS:2 * D_HEADS].transpose(1, 0, 2).reshape(bs, n_q, h, hd)
        v_sr = proj_s[:, :, 2 * D_HEADS:3 * D_HEADS].transpose(1, 0, 2).reshape(bs, n_q, h, hd)
        att_p = _moba_prompt(qkv, bias_tab, bp, seq)
        att_s = _moba_sample(slots_s(slice(0, D_HEADS)), slots_s(slice(D_HEADS, 2 * D_HEADS)),
                             slots_s(slice(2 * D_HEADS, 3 * D_HEADS)), cache_kt, cache_vt, page_table,
                             layer, bias_past, bias_own, n_q)
        att_s = att_s.reshape(bs, h, SAMPLE_PAD, hd)[:, :, :n_q].transpose(2, 0, 1, 3)
        att = jnp.concatenate([att_p, att_s.reshape(ts, D_HEADS).astype(BF16)])

        o_p, conv_new_p, s_new_p = _gdn_prompt(proj, bp, seq, GDN_ROWS, conv_w[layer], a_log[layer],
                                               dt_bias[layer], gdn_norm_w[layer])
        o_s, s_new_s = _gdn_sample(proj_s, conv_t[layer], state_t[layer], conv_w[layer], a_log[layer],
                                   dt_bias[layer], gdn_norm_w[layer])
        s_new_s = s_new_s.transpose(3, 0, 1, 2)
        conv_new_s = proj_s[n_q - (GDN_CONV - 1):, :, 3 * D_HEADS:3 * D_HEADS + D_CONV].transpose(1, 0, 2)
        gdn_o = jnp.concatenate([o_p, o_s.reshape(ts, D_HEADS)])

        w_o = w_out[layer].astype(BF16)
        i = layer // 2
        if layer % 2 == 0:
            x, u = _res_ln([att, gdn_o], [w_o[:D_HEADS], w_o[D_HEADS:]], x, mod_all, layer, 2, layer,
                           (4, 3), ln_g[layer, 0], ln_b[layer, 0], alpha, TM, groups(TM))
            hdn = _ffn_up(u, ffn_w_gate[i].astype(BF16), ffn_w_up[i].astype(BF16), TM)
            x, u = _res_ln([hdn], [ffn_w_down[i].astype(BF16)], x, mod_all, layer, 5, layer + 1,
                           (1, 0), ln_g[layer, 1], ln_b[layer, 1], alpha, TM, groups(TM))
        else:
            w_r = jnp.pad(moe_router[i], ((0, 0), (0, LANES - N_EXPERTS)))
            x, u, route = _res_ln([att, gdn_o], [w_o[:D_HEADS], w_o[D_HEADS:]], x, mod_all, layer, 2,
                                  layer, (4, 3), ln_g[layer, 0], ln_b[layer, 0], alpha, TM,
                                  groups(TM), w_router=w_r)
            tile_expert, n_valid, token_of_slot, slot = _moe_routing(route, TM_MOE)
            y = _moe_ffn(u, tile_expert, n_valid, token_of_slot, moe_w_gate[i], moe_w_up[i],
                         moe_w_down[i], TM_MOE, TJ_MOE)
            n_t = (tp + ts) // TM
            slots = slot.reshape(n_t, TM, TOP_K).transpose(0, 2, 1).reshape(n_t, 1, TOP_K * TM)
            x = _moe_combine(y, slots, route, x, mod_all, layer, 5, ln_g[layer, 1], ln_b[layer, 1],
                             alpha, TM, groups(TM))

        k_p.append(proj[:tp, D_HEADS:2 * D_HEADS].reshape(bp, seq, h, hd))
        v_p.append(proj[:tp, 2 * D_HEADS:3 * D_HEADS].reshape(bp, seq, h, hd))
        conv_p.append(conv_new_p)
        gdn_p.append(s_new_p)
        k_s.append(k_sr)
        v_s.append(v_sr)
        conv_s.append(conv_new_s)
        gdn_s.append(s_new_s)

    y_prompt = x[:tp].reshape(bp, seq, d)
    y_sample = x[tp:].reshape(n_q, bs, d).transpose(1, 0, 2)
    return (y_prompt, y_sample, jnp.stack(k_p), jnp.stack(v_p), jnp.stack(conv_p), jnp.stack(gdn_p),
            jnp.stack(k_s), jnp.stack(v_s), jnp.stack(conv_s), jnp.stack(gdn_s))
```

```python
import functools
import math

import numpy as np
import jax
import jax.numpy as jnp
from jax import lax
from jax.experimental import pallas as pl
from jax.experimental.pallas import tpu as pltpu

F32 = jnp.float32
BF16 = jnp.bfloat16

D_MODEL = 1024
HEAD_DIM = 64
N_HEADS = 8
D_HEADS = N_HEADS * HEAD_DIM
D_CONV = 3 * D_HEADS
D_IN = 3 * D_HEADS + D_CONV + D_HEADS + 2 * N_HEADS
D_IN_MAIN = D_IN - 2 * N_HEADS
LANES = 128
D_IN_PAD = D_IN_MAIN + LANES
MOBA_BLOCK = 256
MOBA_TOPK = 3
PAGE_SIZE = 128
GDN_CONV = 4
GDN_CHUNK = 64
N_BUCKETS = 32
MAX_DISTANCE = 128
N_EXPERTS = 8
TOP_K = 2
LN_EPS = 1e-5
NEG = -1e30
MOD_ROWS = 128
VMEM_LIMIT = 56 * 1024 * 1024

HIGHEST = lax.Precision.HIGHEST


def _cparams(n_axes):
    return pltpu.CompilerParams(dimension_semantics=("arbitrary",) * n_axes,
                                vmem_limit_bytes=VMEM_LIMIT)


def _silu(x):
    return x * (1.0 / (1.0 + jnp.exp(-x)))


def _layer_norm(x, g, b):
    mu = jnp.mean(x, axis=-1, keepdims=True)
    xc = x - mu
    var = jnp.mean(xc * xc, axis=-1, keepdims=True)
    return xc * lax.rsqrt(var + LN_EPS) * g + b


def _modulate(x, scale, shift):
    tm, d = x.shape
    x3 = x.reshape(tm // MOD_ROWS, MOD_ROWS, d)
    return (x3 * (1.0 + scale[None]) + shift[None]).reshape(tm, d)


def _gated(x, gate, h):
    tm, d = x.shape
    x3 = x.reshape(tm // MOD_ROWS, MOD_ROWS, d)
    h3 = h.reshape(tm // MOD_ROWS, MOD_ROWS, d)
    return (x3 + (1.0 + gate[None]) * h3).reshape(tm, d)


def _mod_kernel(c_ref, w_ref, b_ref, o_ref):
    a = _silu(c_ref[...])
    o_ref[...] = jnp.dot(a, w_ref[...], precision=HIGHEST, preferred_element_type=F32) + b_ref[...]


def _mod_vectors(c_all, w_mod, b_mod, tn=1536):
    depth, d, n = w_mod.shape
    rows = c_all.shape[0]
    return pl.pallas_call(
        _mod_kernel,
        grid=(depth, n // tn),
        in_specs=[pl.BlockSpec((rows, d), lambda l, j: (0, 0)),
                  pl.BlockSpec((None, d, tn), lambda l, j: (l, 0, j)),
                  pl.BlockSpec((None, 1, tn), lambda l, j: (l, 0, j))],
        out_specs=pl.BlockSpec((None, rows, tn), lambda l, j: (l, 0, j)),
        out_shape=jax.ShapeDtypeStruct((depth, rows, n), F32),
        compiler_params=_cparams(2),
        name="mod_vectors",
    )(c_all, w_mod, b_mod.reshape(depth, 1, n))


def _ln_mod_kernel(xa_ref, xb_ref, g_ref, b_ref, mod_ref, xn_ref, u_ref, *, n_first):
    x = jnp.where(pl.program_id(0) < n_first, xa_ref[...], xb_ref[...])
    xn = _layer_norm(x, g_ref[...], b_ref[...])
    xn_ref[...] = xn
    u_ref[...] = _modulate(xn, mod_ref[1], mod_ref[0]).astype(BF16)


def _mod_spec(layer, group_of_tile):
    return pl.BlockSpec((None, 6, None, MOD_ROWS, D_MODEL),
                        lambda i: (layer, 0, group_of_tile(i), 0, 0))


def _ln_mod(x_first, x_rest, g, b, mod_all, layer, tm, group_of_tile):
    d = x_first.shape[1]
    n_first = x_first.shape[0] // tm
    t = x_first.shape[0] + x_rest.shape[0]
    row = pl.BlockSpec((tm, d), lambda i: (i, 0))
    vec = pl.BlockSpec((1, d), lambda i: (0, 0))
    return pl.pallas_call(
        functools.partial(_ln_mod_kernel, n_first=n_first),
        grid=(t // tm,),
        in_specs=[pl.BlockSpec((tm, d), lambda i: (jnp.minimum(i, n_first - 1), 0)),
                  pl.BlockSpec((tm, d), lambda i: (jnp.maximum(i - n_first, 0), 0)),
                  vec, vec, _mod_spec(layer, group_of_tile)],
        out_specs=[row, row],
        out_shape=[jax.ShapeDtypeStruct((t, d), F32), jax.ShapeDtypeStruct((t, d), BF16)],
        compiler_params=_cparams(1),
        name="ln_mod",
    )(x_first, x_rest, g.reshape(1, d), b.reshape(1, d), mod_all)


def _proj_kernel(u_ref, w_ref, proj_ref, qkv_ref, *, chunk):
    u = u_ref[...]
    n = w_ref.shape[1]
    n_qkv = qkv_ref.shape[1]
    for c0 in range(0, n, chunk):
        c1 = min(c0 + chunk, n)
        r = jnp.dot(u, w_ref[:, c0:c1], preferred_element_type=F32)
        proj_ref[:, c0:c1] = r
        if c1 <= n_qkv:
            qkv_ref[:, c0:c1] = r.astype(BF16)


def _proj_in(u, w, tm, chunk=512):
    t, d = u.shape
    n = w.shape[1]
    return pl.pallas_call(
        functools.partial(_proj_kernel, chunk=chunk),
        grid=(t // tm,),
        in_specs=[pl.BlockSpec((tm, d), lambda i: (i, 0)),
                  pl.BlockSpec((d, n), lambda i: (0, 0))],
        out_specs=[pl.BlockSpec((tm, n), lambda i: (i, 0)),
                   pl.BlockSpec((tm, 3 * D_HEADS), lambda i: (i, 0))],
        out_shape=[jax.ShapeDtypeStruct((t, n), F32),
                   jax.ShapeDtypeStruct((t, 3 * D_HEADS), BF16)],
        compiler_params=_cparams(1),
        name="proj_in",
    )(u, w)


def _top2_gates(logits):
    lane = lax.broadcasted_iota(jnp.int32, logits.shape, 1)
    v1 = jnp.max(logits, axis=-1, keepdims=True)
    i1 = jnp.min(jnp.where(logits == v1, lane, LANES), axis=-1, keepdims=True)
    rest = jnp.where(lane == i1, -jnp.inf, logits)
    v2 = jnp.max(rest, axis=-1, keepdims=True)
    i2 = jnp.min(jnp.where(rest == v2, lane, LANES), axis=-1, keepdims=True)
    e2 = jnp.exp(v2 - v1)
    inv = 1.0 / (1.0 + e2)
    route = jnp.where(lane == 0, i1.astype(F32), 0.0) + jnp.where(lane == 1, i2.astype(F32), 0.0)
    return route + jnp.where(lane == 2, inv, 0.0) + jnp.where(lane == 3, e2 * inv, 0.0)


def _res_ln_kernel(*refs, split, n_first, alpha, next_rows, with_router):
    n_in = len(split)
    a_vals, pos = [], 0
    for two in split:
        if two:
            a_vals.append(jnp.where(pl.program_id(0) < n_first, refs[pos][...], refs[pos + 1][...]))
        else:
            a_vals.append(refs[pos][...])
        pos += 2 if two else 1
    w_refs = refs[pos:pos + n_in]
    x_ref, modg_ref, modn_ref, g_ref, b_ref = refs[pos + n_in:pos + n_in + 5]
    pos += n_in + 5
    if with_router:
        wr_ref = refs[pos]
        pos += 1
    outs = refs[pos:]
    h = jnp.dot(a_vals[0], w_refs[0][...], preferred_element_type=F32)
    for a_val, w_ref in zip(a_vals[1:], w_refs[1:]):
        h = h + jnp.dot(a_val, w_ref[...], preferred_element_type=F32)
    xn = _layer_norm(_gated(alpha * x_ref[...], modg_ref[...], h), g_ref[...], b_ref[...])
    outs[0][...] = xn
    if next_rows is not None:
        u = _modulate(xn, modn_ref[next_rows[0]], modn_ref[next_rows[1]])
        outs[1][...] = u.astype(outs[1].dtype)
        if with_router:
            logits = jnp.dot(u, wr_ref[...], precision=HIGHEST, preferred_element_type=F32)
            lane = lax.broadcasted_iota(jnp.int32, logits.shape, 1)
            outs[2][...] = _top2_gates(jnp.where(lane < N_EXPERTS, logits, -jnp.inf))


def _res_ln(a_list, w_list, x, mod_all, gate_layer, gate_row, next_layer, next_rows, ln_g, ln_b,
            alpha, tm, group_of_tile, w_router=None):
    t, d = x.shape
    row = pl.BlockSpec((tm, d), lambda i: (i, 0))
    vec = pl.BlockSpec((1, d), lambda i: (0, 0))
    split = tuple(isinstance(a, tuple) for a in a_list)
    n_first = a_list[split.index(True)][0].shape[0] // tm if any(split) else 0
    in_specs, flat_a = [], []
    for a in a_list:
        if isinstance(a, tuple):
            in_specs += [pl.BlockSpec((tm, a[0].shape[1]), lambda i: (jnp.minimum(i, n_first - 1), 0)),
                         pl.BlockSpec((tm, a[1].shape[1]), lambda i: (jnp.maximum(i - n_first, 0), 0))]
            flat_a += list(a)
        else:
            in_specs.append(pl.BlockSpec((tm, a.shape[1]), lambda i: (i, 0)))
            flat_a.append(a)
    in_specs += [pl.BlockSpec(w.shape, lambda i: (0, 0)) for w in w_list]
    in_specs += [row,
                 pl.BlockSpec((None, None, None, MOD_ROWS, d),
                              lambda i: (gate_layer, gate_row, group_of_tile(i), 0, 0)),
                 _mod_spec(next_layer if next_rows is not None else gate_layer, group_of_tile),
                 vec, vec]
    args = flat_a + list(w_list) + [x, mod_all, mod_all, ln_g.reshape(1, d), ln_b.reshape(1, d)]
    out_specs = [row]
    out_shape = [jax.ShapeDtypeStruct((t, d), F32)]
    with_router = w_router is not None
    if next_rows is not None:
        out_specs.append(row)
        out_shape.append(jax.ShapeDtypeStruct((t, d), F32 if with_router else BF16))
    if with_router:
        in_specs.append(pl.BlockSpec(w_router.shape, lambda i: (0, 0)))
        args.append(w_router)
        out_specs.append(pl.BlockSpec((tm, LANES), lambda i: (i, 0)))
        out_shape.append(jax.ShapeDtypeStruct((t, LANES), F32))
    return pl.pallas_call(
        functools.partial(_res_ln_kernel, split=split, n_first=n_first, alpha=alpha,
                          next_rows=next_rows, with_router=with_router),
        grid=(t // tm,),
        in_specs=in_specs,
        out_specs=out_specs,
        out_shape=out_shape,
        compiler_params=_cparams(1),
        name="res_ln",
    )(*args)


def _ffn_up_kernel(u_ref, wg_ref, wu_ref, o_ref, *, chunk):
    u = u_ref[...]
    n = o_ref.shape[1]
    for c0 in range(0, n, chunk):
        c1 = min(c0 + chunk, n)
        hg = jnp.dot(u, wg_ref[:, c0:c1], preferred_element_type=F32)
        hu = jnp.dot(u, wu_ref[:, c0:c1], preferred_element_type=F32)
        o_ref[:, c0:c1] = (_silu(hg) * hu).astype(BF16)


def _ffn_up(u, wg, wu, tm, chunk=256):
    t, d = u.shape
    n = wg.shape[1]
    wspec = pl.BlockSpec((d, n), lambda i: (0, 0))
    return pl.pallas_call(
        functools.partial(_ffn_up_kernel, chunk=chunk),
        grid=(t // tm,),
        in_specs=[pl.BlockSpec((tm, d), lambda i: (i, 0)), wspec, wspec],
        out_specs=pl.BlockSpec((tm, n), lambda i: (i, 0)),
        out_shape=jax.ShapeDtypeStruct((t, n), BF16),
        compiler_params=_cparams(1),
        name="ffn_up",
    )(u, wg, wu)


def _row_copy(src_hbm, src_row, dst_ref, r, sem):
    return pltpu.make_async_copy(src_hbm.at[pl.ds(src_row, 1)], dst_ref.at[pl.ds(r, 1)], sem)


def _gather_start(idx_ref, idx_base, src_hbm, dst_ref, sem, n_rows):
    def body(r, carry):
        _row_copy(src_hbm, idx_ref[0, idx_base + r], dst_ref, r, sem).start()
        return carry

    lax.fori_loop(0, n_rows, body, 0, unroll=8)


def _gather_wait(src_hbm, dst_ref, sem):
    pltpu.make_async_copy(src_hbm.at[pl.ds(0, dst_ref.shape[0])], dst_ref, sem).wait()


def _moe_ffn_kernel(te_ref, nv_ref, tos_ref, u_hbm, wg_ref, wu_ref, wd_ref, y_ref,
                    xg_ref, xb_ref, acc_ref, sem):
    del te_ref
    i = pl.program_id(0)
    j = pl.program_id(1)
    valid = i < nv_ref[0]

    @pl.when(valid & (j == 0))
    def _():
        _gather_start(tos_ref, 0, u_hbm, xg_ref, sem, xg_ref.shape[0])
        _gather_wait(u_hbm, xg_ref, sem)
        xb_ref[...] = xg_ref[...].astype(BF16)
        acc_ref[...] = jnp.zeros_like(acc_ref)

    @pl.when(valid)
    def _():
        x = xb_ref[...]
        hg = jnp.dot(x, wg_ref[...].astype(BF16), preferred_element_type=F32)
        hu = jnp.dot(x, wu_ref[...].astype(BF16), preferred_element_type=F32)
        acc_ref[...] += jnp.dot((_silu(hg) * hu).astype(BF16), wd_ref[...].astype(BF16),
                                preferred_element_type=F32)

    last = j == pl.num_programs(1) - 1

    @pl.when(valid & last)
    def _():
        y_ref[...] = acc_ref[...]

    @pl.when(jnp.logical_not(valid) & last)
    def _():
        y_ref[...] = jnp.zeros_like(y_ref)


def _moe_ffn(u, tile_expert, n_valid, token_of_slot, wg, wu, wd, tm, tj):
    n_tiles = tile_expert.shape[0]
    d = u.shape[1]
    f = wg.shape[2]
    nj = f // tj

    def jj(i, j, nv):
        return jnp.where(i < nv[0], j, nj - 1)

    grid_spec = pltpu.PrefetchScalarGridSpec(
        num_scalar_prefetch=2,
        grid=(n_tiles, nj),
        in_specs=[pl.BlockSpec((None, 1, tm), lambda i, j, te, nv: (i, 0, 0), memory_space=pltpu.SMEM),
                  pl.BlockSpec(memory_space=pl.ANY),
                  pl.BlockSpec((None, d, tj), lambda i, j, te, nv: (te[i], 0, jj(i, j, nv))),
                  pl.BlockSpec((None, d, tj), lambda i, j, te, nv: (te[i], 0, jj(i, j, nv))),
                  pl.BlockSpec((None, tj, d), lambda i, j, te, nv: (te[i], jj(i, j, nv), 0))],
        out_specs=pl.BlockSpec((tm, d), lambda i, j, te, nv: (i, 0)),
        scratch_shapes=[pltpu.VMEM((tm, d), F32), pltpu.VMEM((tm, d), BF16),
                        pltpu.VMEM((tm, d), F32), pltpu.SemaphoreType.DMA(())])
    return pl.pallas_call(
        _moe_ffn_kernel,
        grid_spec=grid_spec,
        out_shape=jax.ShapeDtypeStruct((n_tiles * tm, d), F32),
        compiler_params=_cparams(2),
        name="moe_ffn",
    )(tile_expert, n_valid, token_of_slot, u, wg, wu, wd)


def _moe_combine_kernel(slots_ref, y_hbm, route_ref, x_ref, modg_ref, g_ref, b_ref, o_ref,
                        buf_ref, sem, *, alpha):
    rows = x_ref.shape[0]
    for k in range(TOP_K):
        _gather_start(slots_ref, k * rows, y_hbm, buf_ref.at[k], sem, rows)
    for k in range(TOP_K):
        _gather_wait(y_hbm, buf_ref.at[k], sem)
    route = route_ref[...]
    f = route[:, 2:3] * buf_ref[0] + route[:, 3:4] * buf_ref[1]
    o_ref[...] = _layer_norm(_gated(alpha * x_ref[...], modg_ref[...], f), g_ref[...], b_ref[...])


def _moe_combine(y, slots, route, x, mod_all, gate_layer, gate_row, ln_g, ln_b, alpha, tm,
                 group_of_tile):
    t, d = x.shape
    row = pl.BlockSpec((tm, d), lambda i: (i, 0))
    vec = pl.BlockSpec((1, d), lambda i: (0, 0))
    return pl.pallas_call(
        functools.partial(_moe_combine_kernel, alpha=alpha),
        grid=(t // tm,),
        in_specs=[pl.BlockSpec((None, 1, 2 * tm), lambda i: (i, 0, 0), memory_space=pltpu.SMEM),
                  pl.BlockSpec(memory_space=pl.ANY),
                  pl.BlockSpec((tm, LANES), lambda i: (i, 0)),
                  row,
                  pl.BlockSpec((None, None, None, MOD_ROWS, d),
                               lambda i: (gate_layer, gate_row, group_of_tile(i), 0, 0)),
                  vec, vec],
        out_specs=row,
        out_shape=jax.ShapeDtypeStruct((t, d), F32),
        scratch_shapes=[pltpu.VMEM((2, tm, d), F32), pltpu.SemaphoreType.DMA(())],
        compiler_params=_cparams(1),
        name="moe_combine",
    )(slots, y, route, x, mod_all, ln_g.reshape(1, d), ln_b.reshape(1, d))


def _moe_routing(route, tm):
    t = route.shape[0]
    n_assign = TOP_K * t
    n_tiles = -(-n_assign // tm) + N_EXPERTS
    experts = route[:, :TOP_K].astype(jnp.int32).reshape(n_assign)
    onehot = (experts[:, None] == jnp.arange(N_EXPERTS, dtype=jnp.int32)[None, :]).astype(jnp.int32)
    rank = jnp.sum((jnp.cumsum(onehot, axis=0) - onehot) * onehot, axis=1)
    count = jnp.sum(onehot, axis=0)
    padded = (count + tm - 1) // tm * tm
    ends = jnp.cumsum(padded)
    slot = jnp.sum(onehot * (ends - padded)[None, :], axis=1) + rank
    n_valid = ends[-1] // tm
    starts = jnp.arange(n_tiles, dtype=jnp.int32) * tm
    tile_expert = jnp.minimum(jnp.sum((starts[:, None] >= ends[None, :]).astype(jnp.int32), axis=1),
                              N_EXPERTS - 1)
    tile_expert = jnp.where(starts < ends[-1], tile_expert, tile_expert[jnp.maximum(n_valid - 1, 0)])
    token_of_slot = jnp.zeros((n_tiles * tm,), jnp.int32).at[slot].set(
        jnp.arange(n_assign, dtype=jnp.int32) // TOP_K)
    return (tile_expert.astype(jnp.int32), n_valid.reshape(1).astype(jnp.int32),
            token_of_slot.reshape(n_tiles, 1, tm), slot.reshape(t, TOP_K).astype(jnp.int32))


def _rel_bucket_table(max_dist):
    n = np.arange(max_dist + 1)
    max_exact = N_BUCKETS // 2
    ratio = np.log(np.maximum(n, 1).astype(np.float32) / np.float32(max_exact))
    large = max_exact + (ratio / np.float32(math.log(MAX_DISTANCE / max_exact))
                         * np.float32(N_BUCKETS - max_exact)).astype(np.int32)
    large = np.minimum(large, N_BUCKETS - 1)
    return np.where(n < max_exact, n, large).astype(np.int32)


def _bias_lookup(rel_table, bucket_idx):
    onehot = jnp.asarray(np.eye(N_BUCKETS, dtype=np.float32)[bucket_idx.reshape(-1)])
    vals = jnp.dot(onehot, rel_table, precision=HIGHEST)
    return vals.T.reshape((rel_table.shape[1],) + bucket_idx.shape)


def _select_top_blocks(gate, n_valid, ksel):
    blk = lax.broadcasted_iota(jnp.int32, gate.shape, 0)
    nb = gate.shape[0]
    g = jnp.where(blk < n_valid, gate, -jnp.inf)
    keep = blk == n_valid
    for _ in range(ksel):
        mx = jnp.max(g, axis=0, keepdims=True)
        first = jnp.min(jnp.where(g == mx, blk, nb), axis=0, keepdims=True)
        hit = blk == first
        keep = keep | (hit & (blk < n_valid))
        g = jnp.where(hit, -jnp.inf, g)
    return jnp.where(keep, 0.0, NEG)


def _moba_prompt_kernel(q_ref, k_ref, v_ref, bias_ref, o_ref,
                        kmean_ref, vt_ref, qm_ref, sel_ref, m_ref, l_ref, acc_ref, *, ksel):
    qt = pl.program_id(1)
    seq = k_ref.shape[0]
    nb = seq // MOBA_BLOCK
    gw = 4 * HEAD_DIM
    n_groups = N_HEADS // 4
    nt = (((1,), (1,)), ((), ()))

    @pl.when(qt == 0)
    def _():
        row = lax.broadcasted_iota(jnp.int32, (nb, seq), 0)
        col = lax.broadcasted_iota(jnp.int32, (nb, seq), 1)
        pool = jnp.where(col // MOBA_BLOCK == row, 1.0, 0.0).astype(BF16)
        kmean_ref[...] = jnp.dot(pool, k_ref[...], preferred_element_type=F32) * (1.0 / MOBA_BLOCK)
        for n in range(nb):
            vt_ref[n] = v_ref[n * MOBA_BLOCK:(n + 1) * MOBA_BLOCK, :].astype(F32).T.astype(BF16)

    head_of_lane = lax.broadcasted_iota(jnp.int32, (1, gw), 1) // HEAD_DIM
    q = q_ref[...] * (HEAD_DIM ** -0.5)
    kmean = kmean_ref[...].astype(BF16)
    for h in range(N_HEADS):
        g, hh = divmod(h, 4)
        qm = jnp.where(head_of_lane == hh, q[:, g * gw:(g + 1) * gw], 0.0).astype(BF16)
        qm_ref[h] = qm
        gate = lax.dot_general(kmean[:, g * gw:(g + 1) * gw], qm, nt, preferred_element_type=F32)
        sel_ref[h] = _select_top_blocks(gate, qt, ksel)
        m_ref[h] = jnp.full(m_ref.shape[1:], NEG, F32)
        l_ref[h] = jnp.zeros(l_ref.shape[1:], F32)
    acc_ref[...] = jnp.zeros_like(acc_ref)

    def body(step, carry, near):
        n = qt - step
        start = pl.multiple_of(n * MOBA_BLOCK, MOBA_BLOCK)
        kb = k_ref[pl.ds(start, MOBA_BLOCK), :]
        vt = vt_ref[n]
        scores = [lax.dot_general(kb[:, (h // 4) * gw:(h // 4 + 1) * gw], qm_ref[h], nt,
                                  preferred_element_type=F32) for h in range(N_HEADS)]
        probs, alphas = [], []
        for h in range(N_HEADS):
            shift = sel_ref[h, pl.ds(n, 1), :]
            if near:
                s = scores[h] + bias_ref[h, step]
            else:
                s = scores[h]
                shift = shift + bias_ref[h, 2, 0:1, :]
            m_old = m_ref[h]
            m_new = jnp.maximum(m_old, jnp.max(s, axis=0, keepdims=True) + shift)
            alpha = jnp.exp(m_old - m_new)
            p = jnp.exp(s - (m_new - shift))
            l_ref[h] = alpha * l_ref[h] + jnp.sum(p, axis=0, keepdims=True)
            m_ref[h] = m_new
            probs.append(p.astype(BF16))
            alphas.append(alpha)
        outs = [jnp.dot(vt[h * HEAD_DIM:(h + 1) * HEAD_DIM, :], probs[h],
                        preferred_element_type=F32) for h in range(N_HEADS)]
        for h in range(N_HEADS):
            g, hh = divmod(h, 4)
            rows = slice(hh * HEAD_DIM, (hh + 1) * HEAD_DIM)
            acc_ref[g, rows, :] = alphas[h] * acc_ref[g, rows, :] + outs[h]
        return carry

    lax.fori_loop(0, jnp.minimum(qt + 1, 2), functools.partial(body, near=True), 0)
    lax.fori_loop(2, qt + 1, functools.partial(body, near=False), 0)

    for g in range(n_groups):
        o_t = jnp.concatenate(
            [acc_ref[g, hh * HEAD_DIM:(hh + 1) * HEAD_DIM, :] * (1.0 / l_ref[4 * g + hh])
             for hh in range(4)], axis=0)
        o_ref[:, g * gw:(g + 1) * gw] = o_t.T.astype(o_ref.dtype)


def _moba_bias_tables(rel_table):
    bucket = _rel_bucket_table(2 * MOBA_BLOCK)
    i = np.arange(MOBA_BLOCK)[None, :]
    j = np.arange(MOBA_BLOCK)[:, None]
    own = jnp.where(jnp.asarray(i >= j), _bias_lookup(rel_table, bucket[np.maximum(i - j, 0)]), NEG)
    prev = _bias_lookup(rel_table, bucket[MOBA_BLOCK + i - j])
    far = jnp.broadcast_to(rel_table[N_BUCKETS - 1][:, None, None], prev.shape)
    return jnp.stack([own, prev, far], axis=1).astype(F32)


def _moba_prompt(qkv, bias_tab, batch, seq):
    nb = seq // MOBA_BLOCK
    ksel = min(MOBA_TOPK, nb)
    stat = pltpu.VMEM((N_HEADS, 1, MOBA_BLOCK), F32)
    return pl.pallas_call(
        functools.partial(_moba_prompt_kernel, ksel=ksel),
        grid=(batch, nb),
        in_specs=[pl.BlockSpec((MOBA_BLOCK, D_HEADS), lambda b, t: (b * nb + t, 0)),
                  pl.BlockSpec((seq, D_HEADS), lambda b, t: (b, 1)),
                  pl.BlockSpec((seq, D_HEADS), lambda b, t: (b, 2)),
                  pl.BlockSpec(bias_tab.shape, lambda b, t: (0, 0, 0, 0))],
        out_specs=pl.BlockSpec((MOBA_BLOCK, D_HEADS), lambda b, t: (b * nb + t, 0)),
        out_shape=jax.ShapeDtypeStruct((batch * seq, D_HEADS), BF16),
        scratch_shapes=[pltpu.VMEM((nb, D_HEADS), F32),
                        pltpu.VMEM((nb, D_HEADS, MOBA_BLOCK), BF16),
                        pltpu.VMEM((N_HEADS, MOBA_BLOCK, 4 * HEAD_DIM), BF16),
                        pltpu.VMEM((N_HEADS, nb, MOBA_BLOCK), F32),
                        stat, stat,
                        pltpu.VMEM((N_HEADS // 4, 4 * HEAD_DIM, MOBA_BLOCK), F32)],
        compiler_params=_cparams(2),
        name="moba_prompt",
    )(qkv, qkv, qkv, bias_tab)


def _moba_sample_kernel(pt_ref, q_ref, kn_ref, vn_ref, *refs, ksel, n_q, nbp, pages_per_block):
    del pt_ref
    n_pages = nbp * pages_per_block
    k_refs, v_refs = refs[:n_pages], refs[n_pages:2 * n_pages]
    bias_ref, bown_ref, o_ref, m_ref, l_ref, g_ref, acc_ref = refs[2 * n_pages:]
    nt = (((1,), (1,)), ((), ()))
    dh = N_HEADS * HEAD_DIM
    q8 = q_ref[...]
    slots = q8.shape[0]
    lane_head = lax.broadcasted_iota(jnp.int32, (1, dh), 1) // HEAD_DIM
    qbd = jnp.concatenate([jnp.where(lane_head == hp, q8, 0.0) for hp in range(N_HEADS)], axis=0)
    qs = (qbd * (HEAD_DIM ** -0.5)).astype(BF16)
    ones = jnp.ones((8, PAGE_SIZE), BF16)

    kts = [k_ref[...].reshape(dh, PAGE_SIZE).astype(BF16) for k_ref in k_refs]
    s_pages = [jnp.dot(qs, kt, preferred_element_type=F32) for kt in kts]
    ksums = [lax.dot_general(ones, kt, nt, preferred_element_type=F32) for kt in kts]
    probs = []
    for j in range(nbp):
        pages = range(j * pages_per_block, (j + 1) * pages_per_block)
        s = jnp.concatenate([s_pages[pg] for pg in pages], axis=1) + bias_ref[0 if j == nbp - 1 else 1]
        m = jnp.max(s, axis=-1, keepdims=True)
        p32 = jnp.exp(s - m)
        probs.append(p32.astype(BF16))
        m_ref[j] = m
        l_ref[j] = jnp.sum(p32, axis=-1, keepdims=True)
        ksum = functools.reduce(jnp.add, [ksums[pg] for pg in pages])
        g_ref[j] = jnp.sum(qbd * ksum[0:1, :], axis=-1, keepdims=True) * (1.0 / MOBA_BLOCK)
    pv = [lax.dot_general(probs[pg // pages_per_block][:, (pg % pages_per_block) * PAGE_SIZE:
                                                       (pg % pages_per_block + 1) * PAGE_SIZE],
                          v_refs[pg][...].reshape(dh, PAGE_SIZE).astype(BF16), nt,
                          preferred_element_type=F32) for pg in range(n_pages)]
    for j in range(nbp):
        acc_ref[j] = functools.reduce(jnp.add, pv[j * pages_per_block:(j + 1) * pages_per_block])

    s_own = [jnp.sum(qbd * kn_ref[i:i + 1, :], axis=-1, keepdims=True) * (HEAD_DIM ** -0.5)
             + bown_ref[:, i:i + 1] for i in range(n_q)]
    m_o = functools.reduce(jnp.maximum, s_own)
    p_own = [jnp.exp(s_i - m_o) for s_i in s_own]
    l_o = functools.reduce(jnp.add, p_own)
    acc_o = functools.reduce(jnp.add, [p_i * vn_ref[i:i + 1, :] for i, p_i in enumerate(p_own)])

    gg = g_ref[...]
    blk = lax.broadcasted_iota(jnp.int32, gg.shape, 0)
    keep = blk < 0
    for _ in range(ksel):
        mx = jnp.max(gg, axis=0, keepdims=True)
        first = jnp.min(jnp.where(gg == mx, blk, nbp), axis=0, keepdims=True)
        hit = (blk == first) & (gg > -jnp.inf)
        keep = keep | hit
        gg = jnp.where(blk == first, -jnp.inf, gg)
    m_all = m_ref[...]
    m_tot = jnp.maximum(jnp.max(jnp.where(keep, m_all, NEG), axis=0), m_o)
    w = jnp.where(keep, jnp.exp(m_all - m_tot[None]), 0.0)
    w_o = jnp.exp(m_o - m_tot)
    l_tot = jnp.sum(w * l_ref[...], axis=0) + w_o * l_o
    acc_tot = (jnp.sum(w * acc_ref[...], axis=0) + w_o * acc_o) * (1.0 / l_tot)
    row_head = lax.broadcasted_iota(jnp.int32, (N_HEADS * slots, 1), 0) // slots
    acc_tot = jnp.where(row_head == lane_head, acc_tot, 0.0)
    o_ref[...] = functools.reduce(
        jnp.add, [acc_tot[:, hp * HEAD_DIM:(hp + 1) * HEAD_DIM] for hp in range(N_HEADS)])


def _moba_sample_bias(rel_table, past_len, n_q):
    del past_len
    slots = SAMPLE_PAD
    bucket = _rel_bucket_table(MOBA_BLOCK + slots)
    rows_h = np.repeat(np.arange(N_HEADS), slots)
    rows_i = np.minimum(np.tile(np.arange(slots), N_HEADS), n_q - 1)
    pos = np.arange(MOBA_BLOCK)[None, :]
    by_head = _bias_lookup(rel_table, bucket[MOBA_BLOCK + rows_i[:, None] - pos])
    pick = jnp.asarray(np.arange(N_HEADS)[:, None, None] == rows_h[None, :, None])
    near = jnp.sum(jnp.where(pick, by_head, 0.0), axis=0)
    far = jnp.broadcast_to(jnp.repeat(rel_table[N_BUCKETS - 1], slots)[:, None], near.shape)
    past = jnp.stack([near, far]).astype(F32)
    new = np.arange(slots)[None, :]
    own_h = _bias_lookup(rel_table, bucket[np.maximum(rows_i[:, None] - new, 0)])
    own = jnp.sum(jnp.where(pick, own_h, 0.0), axis=0)
    ok = jnp.asarray((new <= rows_i[:, None]) & (new < n_q))
    return past, jnp.where(ok, own, NEG).astype(F32)


def _moba_sample(q_rows, k_new, v_new, cache_k, cache_v, page_table, layer, bias_past, bias_own, n_q):
    bsz, slots, dh = q_rows.shape
    hd = HEAD_DIM
    rows = N_HEADS * slots
    pages_per_block = MOBA_BLOCK // PAGE_SIZE
    nbp = page_table.shape[1] // pages_per_block
    ksel = min(MOBA_TOPK, nbp + 1)

    n_pages = nbp * pages_per_block

    def page_spec(page):
        return pl.BlockSpec((None, None, N_HEADS, hd, PAGE_SIZE),
                            lambda b, pt: (layer, pt[b, page], 0, 0, 0))

    per_seq = lambda b, pt: (b, 0, 0)
    stat = pltpu.VMEM((nbp, rows, 1), F32)
    grid_spec = pltpu.PrefetchScalarGridSpec(
        num_scalar_prefetch=1,
        grid=(bsz,),
        in_specs=([pl.BlockSpec((None, slots, dh), per_seq)] * 3
                  + [page_spec(page) for page in range(n_pages)] * 2
                  + [pl.BlockSpec(bias_past.shape, lambda b, pt: (0, 0, 0)),
                     pl.BlockSpec(bias_own.shape, lambda b, pt: (0, 0))]),
        out_specs=pl.BlockSpec((None, rows, hd), per_seq),
        scratch_shapes=[stat, stat, stat, pltpu.VMEM((nbp, rows, dh), F32)])
    return pl.pallas_call(
        functools.partial(_moba_sample_kernel, ksel=ksel, n_q=n_q, nbp=nbp,
                          pages_per_block=pages_per_block),
        grid_spec=grid_spec,
        out_shape=jax.ShapeDtypeStruct((bsz, rows, hd), F32),
        compiler_params=_cparams(1),
        name="moba_sample",
    )(page_table, q_rows, k_new, v_new, *([cache_k] * n_pages), *([cache_v] * n_pages),
      bias_past, bias_own)


def _softplus(x):
    return jnp.maximum(x, 0.0) + jnp.log1p(jnp.exp(-jnp.abs(x)))


def _bmm(a, b):
    return jnp.matmul(a.astype(BF16), b.astype(BF16), preferred_element_type=F32)


def _bmm_nt(a, b):
    return jnp.einsum("nid,njd->nij", a.astype(BF16), b.astype(BF16), preferred_element_type=F32)


def _dot_f32(a, b):
    return jnp.dot(a, b, precision=HIGHEST, preferred_element_type=F32)


def _split_bf16(a):
    hi = a.astype(BF16)
    return hi, (a - hi.astype(F32)).astype(BF16)


def _bmm_3pass(a, b):
    a_hi, a_lo = _split_bf16(a)
    b_hi, b_lo = _split_bf16(b)
    return (jnp.matmul(a_hi, b_hi, preferred_element_type=F32)
            + jnp.matmul(a_hi, b_lo, preferred_element_type=F32)
            + jnp.matmul(a_lo, b_hi, preferred_element_type=F32))


def _unit_lower_inverse(a, row, col):
    c = a.shape[1]
    eye = jnp.where(row == col, 1.0, 0.0)
    in16 = (row // 16) == (col // 16)
    in32 = (row // 32) == (col // 32)
    nil = jnp.where(in16, -a, 0.0)
    x = eye + nil
    p = _bmm_3pass(nil, nil)
    for _ in range(2):
        xp = _bmm_3pass(jnp.concatenate([x, p], axis=1), p)
        x = x + xp[:, :c]
        p = xp[:, c:]
    x = x + _bmm_3pass(x, p)
    for off in (jnp.where(in32 & jnp.logical_not(in16), a, 0.0),
                jnp.where(in32, 0.0, a)):
        x = x - _bmm(x, _bmm(off, x))
    return x


def _gdn_prepare(q, k, v, k_t, g_col, g_row, beta, row, col):
    c = q.shape[1]
    incl = row >= col
    decay = jnp.where(incl, jnp.exp(jnp.where(incl, g_col - g_row, 0.0)), 0.0)
    qk_kk = _bmm_nt(jnp.concatenate([q, k], axis=1), k)
    a_mat = jnp.where(row > col, qk_kk[:, c:] * decay * beta, 0.0)
    e_g = jnp.exp(g_col)
    rhs = jnp.concatenate([v * beta, k * beta * e_g], axis=2)
    sol = _bmm(_unit_lower_inverse(a_mat, row, col), rhs)
    dv = v.shape[2]
    g_last = g_col[:, c - 1:c, :]
    w_qd = jnp.concatenate([sol[:, :, dv:], q * e_g], axis=1).astype(BF16)
    qk = jnp.where(incl, qk_kk[:, :c] * decay, 0.0).astype(BF16)
    k_dec_t = (k_t * jnp.exp(g_last - g_row)).astype(BF16)
    return sol[:, :, :dv], w_qd, qk, k_dec_t, jnp.exp(g_last)


def _gdn_step(u, w_qd, qk, k_dec_t, chunk_decay, state):
    c = u.shape[1]
    ws_qs = jnp.matmul(w_qd, state.astype(BF16), preferred_element_type=F32)
    v_new = (u - ws_qs[:, :c]).astype(BF16)
    o = ws_qs[:, c:] + jnp.matmul(qk, v_new, preferred_element_type=F32)
    new_state = state * chunk_decay + jnp.matmul(k_dec_t, v_new, preferred_element_type=F32)
    return o, new_state


def _gdn_prompt_kernel(x_ref, z_ref, gab_ref, cw_ref, alog_ref, dtb_ref, nw_ref, seg_ref,
                       o_ref, conv_ref, sout_ref,
                       xbuf_ref, halo_ref, s_ref, u_ref, wqd_ref, qk_ref, kdt_ref, cd_ref, oh_ref):
    t = pl.program_id(1)
    tl = x_ref.shape[0]
    cs = GDN_CHUNK
    n_chunks = tl // cs
    hd = HEAD_DIM

    @pl.when(t == 0)
    def _():
        halo_ref[...] = jnp.zeros_like(halo_ref)
        s_ref[...] = jnp.zeros_like(s_ref)

    x = x_ref[...]
    xbuf_ref[0:8, :] = halo_ref[...]
    xbuf_ref[8:8 + tl, :] = x
    y = cw_ref[0:1, :] * xbuf_ref[5:5 + tl, :]
    for i in range(1, GDN_CONV):
        y = y + cw_ref[i:i + 1, :] * xbuf_ref[5 + i:5 + i + tl, :]
    y = _silu(y)
    tail = x[tl - 8:, :]
    halo_ref[...] = tail
    conv_ref[...] = tail[8 - (GDN_CONV - 1):, :]

    dh = N_HEADS * hd
    seg = seg_ref[...]

    def l2n(a):
        sq_hi, sq_lo = _split_bf16(a * a)
        ss = (jnp.dot(sq_hi, seg, preferred_element_type=F32)
              + jnp.dot(sq_lo, seg, preferred_element_type=F32))
        return a * lax.rsqrt(ss + 1e-6)

    qn = l2n(y[:, :dh]) * (hd ** -0.5)
    kn = l2n(y[:, dh:2 * dh])
    vv = y[:, 2 * dh:]
    kn_t = kn.T

    gab = gab_ref[...]
    beta = 1.0 / (1.0 + jnp.exp(-gab))
    glog = -jnp.exp(alog_ref[...]) * _softplus(gab + dtb_ref[...])

    row = lax.broadcasted_iota(jnp.int32, (cs, cs), 0)
    col = lax.broadcasted_iota(jnp.int32, (cs, cs), 1)
    tril = jnp.where(row >= col, 1.0, 0.0)
    pick = jnp.where(lax.broadcasted_iota(jnp.int32, (N_HEADS, LANES), 1)
                     == lax.broadcasted_iota(jnp.int32, (N_HEADS, LANES), 0) + N_HEADS, 1.0, 0.0)
    parts = {name: [] for name in ("q", "k", "v", "kt", "gc", "gr", "bb")}
    for c in range(n_chunks):
        rows = slice(c * cs, (c + 1) * cs)
        gcum = _dot_f32(tril, glog[rows])
        g_rows = lax.dot_general(pick, gcum, (((1,), (1,)), ((), ())), precision=HIGHEST,
                                 preferred_element_type=F32)
        for h in range(N_HEADS):
            lanes = slice(h * hd, (h + 1) * hd)
            parts["q"].append(qn[rows, lanes])
            parts["k"].append(kn[rows, lanes])
            parts["v"].append(vv[rows, lanes])
            parts["kt"].append(kn_t[lanes, rows])
            parts["gc"].append(jnp.broadcast_to(gcum[:, N_HEADS + h:N_HEADS + h + 1], (cs, cs)))
            parts["gr"].append(g_rows[h:h + 1, :])
            parts["bb"].append(jnp.broadcast_to(beta[rows, h:h + 1], (cs, cs)))
    st = {name: jnp.stack(vals) for name, vals in parts.items()}
    u_all, w_qd, qk, k_dec_t, chunk_decay = _gdn_prepare(
        st["q"], st["k"], st["v"], st["kt"], st["gc"], st["gr"], st["bb"], row, col)
    u_ref[...] = u_all.reshape(u_ref.shape)
    wqd_ref[...] = w_qd.reshape(wqd_ref.shape)
    qk_ref[...] = qk.reshape(qk_ref.shape)
    kdt_ref[...] = k_dec_t.reshape(kdt_ref.shape)
    cd_ref[...] = chunk_decay.reshape(cd_ref.shape)

    def chunk_body(c, carry):
        o, s_new = _gdn_step(u_ref[c], wqd_ref[c], qk_ref[c], kdt_ref[c], cd_ref[c], s_ref[...])
        oh_ref[c] = o
        s_ref[...] = s_new
        return carry

    lax.fori_loop(0, n_chunks, chunk_body, 0)

    z = z_ref[...]
    nw = nw_ref[...]
    for h in range(N_HEADS):
        lanes = slice(h * hd, (h + 1) * hd)
        o = jnp.concatenate([oh_ref[c, h] for c in range(n_chunks)], axis=0)
        o = o * lax.rsqrt(jnp.mean(o * o, axis=-1, keepdims=True) + 1e-6) * nw
        o_ref[:, lanes] = (o * _silu(z[:, lanes])).astype(o_ref.dtype)

    @pl.when(t == pl.num_programs(1) - 1)
    def _():
        sout_ref[...] = s_ref[...]


def _gdn_prompt(proj, batch, seq, tl, conv_w, a_log, dt_bias, norm_w):
    hd = HEAD_DIM
    cs = GDN_CHUNK
    n_steps = seq // tl
    n_chunks = tl // cs
    col_x, col_z, col_gab = 1, D_IN_MAIN // D_HEADS - 1, D_IN_MAIN // LANES
    lane = np.arange(D_HEADS)
    seg = jnp.asarray((lane[:, None] // hd == lane[None, :] // hd).astype(np.float32)).astype(BF16)
    alog_row = jnp.zeros((1, LANES), F32).at[0, N_HEADS:2 * N_HEADS].set(a_log)
    dtb_row = jnp.zeros((1, LANES), F32).at[0, N_HEADS:2 * N_HEADS].set(dt_bias)
    const2 = lambda b, t: (0, 0)
    per_inst = pltpu.VMEM((n_chunks, N_HEADS, cs, hd), F32)
    per_inst_bf16 = pltpu.VMEM((n_chunks, N_HEADS, hd, cs), BF16)
    return pl.pallas_call(
        _gdn_prompt_kernel,
        grid=(batch, n_steps),
        in_specs=[pl.BlockSpec((tl, D_CONV), lambda b, t: (b * n_steps + t, col_x)),
                  pl.BlockSpec((tl, D_HEADS), lambda b, t: (b * n_steps + t, col_z)),
                  pl.BlockSpec((tl, LANES), lambda b, t: (b * n_steps + t, col_gab)),
                  pl.BlockSpec((GDN_CONV, D_CONV), const2),
                  pl.BlockSpec((1, LANES), const2),
                  pl.BlockSpec((1, LANES), const2),
                  pl.BlockSpec((1, hd), const2),
                  pl.BlockSpec((D_HEADS, D_HEADS), const2)],
        out_specs=[pl.BlockSpec((tl, D_HEADS), lambda b, t: (b * n_steps + t, 0)),
                   pl.BlockSpec((None, GDN_CONV - 1, D_CONV), lambda b, t: (b, 0, 0)),
                   pl.BlockSpec((None, N_HEADS, hd, hd), lambda b, t: (b, 0, 0, 0))],
        out_shape=[jax.ShapeDtypeStruct((batch * seq, D_HEADS), BF16),
                   jax.ShapeDtypeStruct((batch, GDN_CONV - 1, D_CONV), F32),
                   jax.ShapeDtypeStruct((batch, N_HEADS, hd, hd), F32)],
        scratch_shapes=[pltpu.VMEM((8 + tl + 8, D_CONV), F32),
                        pltpu.VMEM((8, D_CONV), F32),
                        pltpu.VMEM((N_HEADS, hd, hd), F32),
                        per_inst,
                        pltpu.VMEM((n_chunks, N_HEADS, 2 * cs, hd), BF16),
                        per_inst_bf16, per_inst_bf16,
                        pltpu.VMEM((n_chunks, N_HEADS, 1, hd), F32),
                        per_inst],
        compiler_params=_cparams(2),
        name="gdn_prompt",
    )(proj, proj, proj, conv_w, alog_row, dtb_row, norm_w.reshape(1, hd), seg)


def _gdn_sample_kernel(xq_ref, xk_ref, xv_ref, cq_ref, ck_ref, cv_ref, wq_ref, wk_ref, wv_ref,
                       z_ref, gab_ref, alog_ref, dtb_ref, nw_ref, s_ref,
                       o_ref, sout_ref, kq_ref, gt_ref, ot_ref):
    pair = pl.program_id(0)
    n_q = xq_ref.shape[0]
    hd = HEAD_DIM

    def conv_t(x_ref, c_ref, w_ref):
        xp = [c_ref[i] for i in range(GDN_CONV - 1)] + [x_ref[i] for i in range(n_q)]
        out = []
        for i in range(n_q):
            y = w_ref[0:1, :] * xp[i]
            for m in range(1, GDN_CONV):
                y = y + w_ref[m:m + 1, :] * xp[i + m]
            out.append(_silu(y).T)
        return out

    q_t = conv_t(xq_ref, cq_ref, wq_ref)
    k_t = conv_t(xk_ref, ck_ref, wk_ref)
    v_t = conv_t(xv_ref, cv_ref, wv_ref)

    def l2n(a):
        return a * lax.rsqrt(jnp.sum(a * a, axis=0, keepdims=True) + 1e-6)

    for i in range(n_q):
        gab_t = gab_ref[i].T
        gt_ref[0, i] = 1.0 / (1.0 + jnp.exp(-gab_t))
        gt_ref[1, i] = jnp.exp(-jnp.exp(alog_ref[...]) * _softplus(gab_t + dtb_ref[...]))

    for hh in range(2):
        rows = slice(hh * hd, (hh + 1) * hd)
        head = 2 * pair + hh
        for i in range(n_q):
            kq_ref[0] = l2n(k_t[i][rows])
            kq_ref[1] = l2n(q_t[i][rows]) * (hd ** -0.5)
            beta = gt_ref[0, i, pl.ds(head, 1), :]
            decay = gt_ref[1, i, pl.ds(N_HEADS + head, 1), :]
            src = s_ref if i == 0 else sout_ref

            def ks_body(kk, acc):
                return acc + kq_ref[0, pl.ds(kk, 1), :] * src[hh, kk]

            k_s = lax.fori_loop(0, hd, ks_body, jnp.zeros((hd, k_t[i].shape[1]), F32), unroll=8)
            r = beta * (v_t[i][rows] - decay * k_s)

            def upd_body(kk, acc):
                s_new = decay * src[hh, kk] + kq_ref[0, pl.ds(kk, 1), :] * r
                sout_ref[hh, kk] = s_new
                return acc + kq_ref[1, pl.ds(kk, 1), :] * s_new

            o = lax.fori_loop(0, hd, upd_body, jnp.zeros_like(r), unroll=8)
            ot_ref[i, rows, :] = o * lax.rsqrt(jnp.mean(o * o, axis=0, keepdims=True) + 1e-6) * nw_ref[...]

    for i in range(n_q):
        o_ref[i] = (ot_ref[i].T * _silu(z_ref[i])).astype(o_ref.dtype)


def _gdn_sample(proj_s, conv_state, state_t, conv_w, a_log, dt_bias, norm_w):
    n_q, bsz, _ = proj_s.shape
    hd = HEAD_DIM
    pw = 2 * hd
    n_pairs = N_HEADS // 2
    base = D_CONV // pw
    col = jnp.zeros((LANES, 1), F32)
    alog_col = col.at[N_HEADS:2 * N_HEADS, 0].set(a_log)
    dtb_col = col.at[N_HEADS:2 * N_HEADS, 0].set(dt_bias)
    nw_col = norm_w.reshape(hd, 1)

    def xspec(part):
        return pl.BlockSpec((n_q, bsz, pw), lambda p: (0, 0, base + part * n_pairs + p))

    def cspec(rows, part):
        return pl.BlockSpec((rows, bsz, pw) if rows else (GDN_CONV, pw),
                            (lambda p: (0, 0, part * n_pairs + p)) if rows
                            else (lambda p: (0, part * n_pairs + p)))

    const = lambda p: (0, 0)
    state_spec = pl.BlockSpec((2, hd, hd, bsz), lambda p: (p, 0, 0, 0))
    return pl.pallas_call(
        _gdn_sample_kernel,
        grid=(n_pairs,),
        in_specs=[xspec(0), xspec(1), xspec(2),
                  cspec(GDN_CONV - 1, 0), cspec(GDN_CONV - 1, 1), cspec(GDN_CONV - 1, 2),
                  cspec(0, 0), cspec(0, 1), cspec(0, 2),
                  pl.BlockSpec((n_q, bsz, pw), lambda p: (0, 0, D_IN_MAIN // pw - n_pairs + p)),
                  pl.BlockSpec((n_q, bsz, LANES), lambda p: (0, 0, D_IN_MAIN // LANES)),
                  pl.BlockSpec((LANES, 1), const), pl.BlockSpec((LANES, 1), const),
                  pl.BlockSpec((hd, 1), const),
                  state_spec],
        out_specs=[pl.BlockSpec((n_q, bsz, pw), lambda p: (0, 0, p)), state_spec],
        out_shape=[jax.ShapeDtypeStruct((n_q, bsz, D_HEADS), BF16),
                   jax.ShapeDtypeStruct(state_t.shape, F32)],
        scratch_shapes=[pltpu.VMEM((2, hd, bsz), F32),
                        pltpu.VMEM((2, n_q, LANES, bsz), F32),
                        pltpu.VMEM((n_q, pw, bsz), F32)],
        compiler_params=_cparams(1),
        name="gdn_sample",
    )(proj_s, proj_s, proj_s, conv_state, conv_state, conv_state, conv_w, conv_w, conv_w,
      proj_s, proj_s, alog_col, dtb_col, nw_col, state_t)


TM = 256
TM_MOE = 1024
TJ_MOE = 512
GDN_ROWS = 256
SAMPLE_PAD = 8


def kernel(x_prompt, x_sample, cache_k, cache_v, state_conv, state_gdn, page_table, c_prompt, c_sample,
           ln_in_g, ln_in_b, w_mod, b_mod, w_in, conv_w, a_log, dt_bias, gdn_norm_w, w_out, rel_table,
           ln_g, ln_b, ffn_w_gate, ffn_w_up, ffn_w_down, moe_router, moe_w_gate, moe_w_up, moe_w_down):
    bp, seq, d = x_prompt.shape
    bs, n_q, _ = x_sample.shape
    depth = w_in.shape[0]
    tp, ts = bp * seq, bs * n_q
    h, hd = N_HEADS, HEAD_DIM
    assert bs == MOD_ROWS and seq % TM == 0 and ts % TM == 0 and seq % MOBA_BLOCK == 0
    assert depth == 2 and GDN_CONV - 1 <= n_q <= SAMPLE_PAD and GDN_CHUNK == HEAD_DIM
    alpha = (2 * depth) ** 0.25

    def groups(tm):
        return lambda i: jnp.minimum(i // (seq // tm), bp)

    n_c = bp + bs
    c_all = jnp.pad(jnp.concatenate([c_prompt, c_sample]), ((0, (-n_c) % 8), (0, 0)))
    mod = _mod_vectors(c_all, w_mod, b_mod).reshape(depth, -1, 6, d)
    mod_p = jnp.broadcast_to(mod[:, :bp].transpose(0, 2, 1, 3)[:, :, :, None, :],
                             (depth, 6, bp, MOD_ROWS, d))
    mod_s = mod[:, bp:n_c].transpose(0, 2, 1, 3)[:, :, None]
    mod_all = jnp.concatenate([mod_p, mod_s], axis=2)

    x, u = _ln_mod(x_prompt.reshape(tp, d), x_sample.transpose(1, 0, 2).reshape(ts, d),
                   ln_in_g, ln_in_b, mod_all, 0, TM, groups(TM))

    bias_tab = _moba_bias_tables(rel_table)
    past_len = page_table.shape[1] * PAGE_SIZE
    bias_past, bias_own = _moba_sample_bias(rel_table, past_len, n_q)
    cache_kt = cache_k.transpose(0, 1, 3, 4, 2)
    cache_vt = cache_v.transpose(0, 1, 3, 4, 2)
    state_t = state_gdn.transpose(0, 2, 3, 4, 1)
    conv_t = state_conv.transpose(0, 2, 1, 3)

    k_p, v_p, conv_p, gdn_p, k_s, v_s, conv_s, gdn_s = [], [], [], [], [], [], [], []
    for layer in range(depth):
        w_l = w_in[layer]
        w_cat = jnp.concatenate(
            [w_l[:, :D_IN_MAIN], jnp.pad(w_l[:, D_IN_MAIN:], ((0, 0), (0, LANES - 2 * h)))],
            axis=1).astype(BF16)
        proj, qkv = _proj_in(u, w_cat, TM)
        proj_s = proj[tp:].reshape(n_q, bs, -1)

        def slots_s(cols):
            return jnp.pad(proj_s[:, :, cols].transpose(1, 0, 2), ((0, 0), (0, SAMPLE_PAD - n_q), (0, 0)))

        k_sr = proj_s[:, :, D_HEADS:2 * D_HEADS].transpose(1, 0, 2).reshape(bs, n_q, h, hd)
        v_sr = proj_s[:, :, 2 * D_HEADS:3 * D_HEADS].transpose(1, 0, 2).reshape(bs, n_q, h, hd)
        att_p = _moba_prompt(qkv, bias_tab, bp, seq)
        att_s = _moba_sample(slots_s(slice(0, D_HEADS)), slots_s(slice(D_HEADS, 2 * D_HEADS)),
                             slots_s(slice(2 * D_HEADS, 3 * D_HEADS)), cache_kt, cache_vt, page_table,
                             layer, bias_past, bias_own, n_q)
        att_s = att_s.reshape(bs, h, SAMPLE_PAD, hd)[:, :, :n_q].transpose(2, 0, 1, 3)
        att = (att_p, att_s.reshape(ts, D_HEADS).astype(BF16))

        o_p, conv_new_p, s_new_p = _gdn_prompt(proj, bp, seq, GDN_ROWS, conv_w[layer], a_log[layer],
                                               dt_bias[layer], gdn_norm_w[layer])
        o_s, s_new_s = _gdn_sample(proj_s, conv_t[layer], state_t[layer], conv_w[layer], a_log[layer],
                                   dt_bias[layer], gdn_norm_w[layer])
        s_new_s = s_new_s.transpose(3, 0, 1, 2)
        conv_new_s = proj_s[n_q - (GDN_CONV - 1):, :, 3 * D_HEADS:3 * D_HEADS + D_CONV].transpose(1, 0, 2)
        gdn_o = (o_p, o_s.reshape(ts, D_HEADS))

        w_o = w_out[layer].astype(BF16)
        i = layer // 2
        if layer % 2 == 0:
            x, u = _res_ln([att, gdn_o], [w_o[:D_HEADS], w_o[D_HEADS:]], x, mod_all, layer, 2, layer,
                           (4, 3), ln_g[layer, 0], ln_b[layer, 0], alpha, TM, groups(TM))
            hdn = _ffn_up(u, ffn_w_gate[i].astype(BF16), ffn_w_up[i].astype(BF16), TM)
            x, u = _res_ln([hdn], [ffn_w_down[i].astype(BF16)], x, mod_all, layer, 5, layer + 1,
                           (1, 0), ln_g[layer, 1], ln_b[layer, 1], alpha, TM, groups(TM))
        else:
            w_r = jnp.pad(moe_router[i], ((0, 0), (0, LANES - N_EXPERTS)))
            x, u, route = _res_ln([att, gdn_o], [w_o[:D_HEADS], w_o[D_HEADS:]], x, mod_all, layer, 2,
                                  layer, (4, 3), ln_g[layer, 0], ln_b[layer, 0], alpha, TM,
                                  groups(TM), w_router=w_r)
            tile_expert, n_valid, token_of_slot, slot = _moe_routing(route, TM_MOE)
            y = _moe_ffn(u, tile_expert, n_valid, token_of_slot, moe_w_gate[i], moe_w_up[i],
                         moe_w_down[i], TM_MOE, TJ_MOE)
            n_t = (tp + ts) // TM
            slots = slot.reshape(n_t, TM, TOP_K).transpose(0, 2, 1).reshape(n_t, 1, TOP_K * TM)
            x = _moe_combine(y, slots, route, x, mod_all, layer, 5, ln_g[layer, 1], ln_b[layer, 1],
                             alpha, TM, groups(TM))

        k_p.append(proj[:tp, D_HEADS:2 * D_HEADS].reshape(bp, seq, h, hd))
        v_p.append(proj[:tp, 2 * D_HEADS:3 * D_HEADS].reshape(bp, seq, h, hd))
        conv_p.append(conv_new_p)
        gdn_p.append(s_new_p)
        k_s.append(k_sr)
        v_s.append(v_sr)
        conv_s.append(conv_new_s)
        gdn_s.append(s_new_s)

    y_prompt = x[:tp].reshape(bp, seq, d)
    y_sample = x[tp:].reshape(n_q, bs, d).transpose(1, 0, 2)
    return (y_prompt, y_sample, jnp.stack(k_p), jnp.stack(v_p), jnp.stack(conv_p), jnp.stack(gdn_p),
            jnp.stack(k_s), jnp.stack(v_s), jnp.stack(conv_s), jnp.stack(gdn_s))
```

```python
import functools
import math

import numpy as np
import jax
import jax.numpy as jnp
from jax import lax
from jax.experimental import pallas as pl
from jax.experimental.pallas import tpu as pltpu

F32 = jnp.float32
BF16 = jnp.bfloat16

D_MODEL = 1024
HEAD_DIM = 64
N_HEADS = 8
D_HEADS = N_HEADS * HEAD_DIM
D_CONV = 3 * D_HEADS
D_IN = 3 * D_HEADS + D_CONV + D_HEADS + 2 * N_HEADS
D_IN_MAIN = D_IN - 2 * N_HEADS
LANES = 128
D_IN_PAD = D_IN_MAIN + LANES
MOBA_BLOCK = 256
MOBA_TOPK = 3
PAGE_SIZE = 128
GDN_CONV = 4
GDN_CHUNK = 64
N_BUCKETS = 32
MAX_DISTANCE = 128
N_EXPERTS = 8
TOP_K = 2
LN_EPS = 1e-5
NEG = -1e30
MOD_ROWS = 128
VMEM_LIMIT = 56 * 1024 * 1024

HIGHEST = lax.Precision.HIGHEST


def _cparams(n_axes):
    return pltpu.CompilerParams(dimension_semantics=("arbitrary",) * n_axes,
                                vmem_limit_bytes=VMEM_LIMIT)


def _silu(x):
    return x * (1.0 / (1.0 + jnp.exp(-x)))


def _layer_norm(x, g, b):
    mu = jnp.mean(x, axis=-1, keepdims=True)
    xc = x - mu
    var = jnp.mean(xc * xc, axis=-1, keepdims=True)
    return xc * lax.rsqrt(var + LN_EPS) * g + b


def _modulate(x, scale, shift):
    tm, d = x.shape
    x3 = x.reshape(tm // MOD_ROWS, MOD_ROWS, d)
    return (x3 * (1.0 + scale[None]) + shift[None]).reshape(tm, d)


def _gated(x, gate, h):
    tm, d = x.shape
    x3 = x.reshape(tm // MOD_ROWS, MOD_ROWS, d)
    h3 = h.reshape(tm // MOD_ROWS, MOD_ROWS, d)
    return (x3 + (1.0 + gate[None]) * h3).reshape(tm, d)


def _mod_kernel(c_ref, w_ref, b_ref, o_ref):
    a = _silu(c_ref[...])
    o_ref[...] = jnp.dot(a, w_ref[...], precision=HIGHEST, preferred_element_type=F32) + b_ref[...]


def _mod_vectors(c_all, w_mod, b_mod, tn=1536):
    depth, d, n = w_mod.shape
    rows = c_all.shape[0]
    return pl.pallas_call(
        _mod_kernel,
        grid=(depth, n // tn),
        in_specs=[pl.BlockSpec((rows, d), lambda l, j: (0, 0)),
                  pl.BlockSpec((None, d, tn), lambda l, j: (l, 0, j)),
                  pl.BlockSpec((None, 1, tn), lambda l, j: (l, 0, j))],
        out_specs=pl.BlockSpec((None, rows, tn), lambda l, j: (l, 0, j)),
        out_shape=jax.ShapeDtypeStruct((depth, rows, n), F32),
        compiler_params=_cparams(2),
        name="mod_vectors",
    )(c_all, w_mod, b_mod.reshape(depth, 1, n))


def _ln_mod_kernel(xa_ref, xb_ref, g_ref, b_ref, mod_ref, xn_ref, u_ref, *, n_first):
    x = jnp.where(pl.program_id(0) < n_first, xa_ref[...], xb_ref[...])
    xn = _layer_norm(x, g_ref[...], b_ref[...])
    xn_ref[...] = xn
    u_ref[...] = _modulate(xn, mod_ref[1], mod_ref[0]).astype(BF16)


def _mod_spec(layer, group_of_tile):
    return pl.BlockSpec((None, 6, None, MOD_ROWS, D_MODEL),
                        lambda i: (layer, 0, group_of_tile(i), 0, 0))


def _ln_mod(x_first, x_rest, g, b, mod_all, layer, tm, group_of_tile):
    d = x_first.shape[1]
    n_first = x_first.shape[0] // tm
    t = x_first.shape[0] + x_rest.shape[0]
    row = pl.BlockSpec((tm, d), lambda i: (i, 0))
    vec = pl.BlockSpec((1, d), lambda i: (0, 0))
    return pl.pallas_call(
        functools.partial(_ln_mod_kernel, n_first=n_first),
        grid=(t // tm,),
        in_specs=[pl.BlockSpec((tm, d), lambda i: (jnp.minimum(i, n_first - 1), 0)),
                  pl.BlockSpec((tm, d), lambda i: (jnp.maximum(i - n_first, 0), 0)),
                  vec, vec, _mod_spec(layer, group_of_tile)],
        out_specs=[row, row],
        out_shape=[jax.ShapeDtypeStruct((t, d), F32), jax.ShapeDtypeStruct((t, d), BF16)],
        compiler_params=_cparams(1),
        name="ln_mod",
    )(x_first, x_rest, g.reshape(1, d), b.reshape(1, d), mod_all)


def _proj_kernel(u_ref, w_ref, proj_ref, qkv_ref, *, chunk):
    u = u_ref[...]
    n = w_ref.shape[1]
    n_qkv = qkv_ref.shape[1]
    for c0 in range(0, n, chunk):
        c1 = min(c0 + chunk, n)
        r = jnp.dot(u, w_ref[:, c0:c1], preferred_element_type=F32)
        proj_ref[:, c0:c1] = r
        if c1 <= n_qkv:
            qkv_ref[:, c0:c1] = r.astype(BF16)


def _proj_in(u, w, tm, chunk=512):
    t, d = u.shape
    n = w.shape[1]
    return pl.pallas_call(
        functools.partial(_proj_kernel, chunk=chunk),
        grid=(t // tm,),
        in_specs=[pl.BlockSpec((tm, d), lambda i: (i, 0)),
                  pl.BlockSpec((d, n), lambda i: (0, 0))],
        out_specs=[pl.BlockSpec((tm, n), lambda i: (i, 0)),
                   pl.BlockSpec((tm, 3 * D_HEADS), lambda i: (i, 0))],
        out_shape=[jax.ShapeDtypeStruct((t, n), F32),
                   jax.ShapeDtypeStruct((t, 3 * D_HEADS), BF16)],
        compiler_params=_cparams(1),
        name="proj_in",
    )(u, w)


def _top2_gates(logits):
    lane = lax.broadcasted_iota(jnp.int32, logits.shape, 1)
    v1 = jnp.max(logits, axis=-1, keepdims=True)
    i1 = jnp.min(jnp.where(logits == v1, lane, LANES), axis=-1, keepdims=True)
    rest = jnp.where(lane == i1, -jnp.inf, logits)
    v2 = jnp.max(rest, axis=-1, keepdims=True)
    i2 = jnp.min(jnp.where(rest == v2, lane, LANES), axis=-1, keepdims=True)
    e2 = jnp.exp(v2 - v1)
    inv = 1.0 / (1.0 + e2)
    route = jnp.where(lane == 0, i1.astype(F32), 0.0) + jnp.where(lane == 1, i2.astype(F32), 0.0)
    return route + jnp.where(lane == 2, inv, 0.0) + jnp.where(lane == 3, e2 * inv, 0.0)


def _res_ln_kernel(*refs, split, n_first, alpha, next_rows, with_router):
    n_in = len(split)
    a_vals, pos = [], 0
    for two in split:
        if two:
            a_vals.append(jnp.where(pl.program_id(0) < n_first, refs[pos][...], refs[pos + 1][...]))
        else:
            a_vals.append(refs[pos][...])
        pos += 2 if two else 1
    w_refs = refs[pos:pos + n_in]
    x_ref, modg_ref, modn_ref, g_ref, b_ref = refs[pos + n_in:pos + n_in + 5]
    pos += n_in + 5
    if with_router:
        wr_ref = refs[pos]
        pos += 1
    outs = refs[pos:]
    h = jnp.dot(a_vals[0], w_refs[0][...], preferred_element_type=F32)
    for a_val, w_ref in zip(a_vals[1:], w_refs[1:]):
        h = h + jnp.dot(a_val, w_ref[...], preferred_element_type=F32)
    xn = _layer_norm(_gated(alpha * x_ref[...], modg_ref[...], h), g_ref[...], b_ref[...])
    outs[0][...] = xn
    if next_rows is not None:
        u = _modulate(xn, modn_ref[next_rows[0]], modn_ref[next_rows[1]])
        outs[1][...] = u.astype(outs[1].dtype)
        if with_router:
            logits = jnp.dot(u, wr_ref[...], precision=HIGHEST, preferred_element_type=F32)
            lane = lax.broadcasted_iota(jnp.int32, logits.shape, 1)
            outs[2][...] = _top2_gates(jnp.where(lane < N_EXPERTS, logits, -jnp.inf))


def _res_ln(a_list, w_list, x, mod_all, gate_layer, gate_row, next_layer, next_rows, ln_g, ln_b,
            alpha, tm, group_of_tile, w_router=None):
    t, d = x.shape
    row = pl.BlockSpec((tm, d), lambda i: (i, 0))
    vec = pl.BlockSpec((1, d), lambda i: (0, 0))
    split = tuple(isinstance(a, tuple) for a in a_list)
    n_first = a_list[split.index(True)][0].shape[0] // tm if any(split) else 0
    in_specs, flat_a = [], []
    for a in a_list:
        if isinstance(a, tuple):
            in_specs += [pl.BlockSpec((tm, a[0].shape[1]), lambda i: (jnp.minimum(i, n_first - 1), 0)),
                         pl.BlockSpec((tm, a[1].shape[1]), lambda i: (jnp.maximum(i - n_first, 0), 0))]
            flat_a += list(a)
        else:
            in_specs.append(pl.BlockSpec((tm, a.shape[1]), lambda i: (i, 0)))
            flat_a.append(a)
    in_specs += [pl.BlockSpec(w.shape, lambda i: (0, 0)) for w in w_list]
    in_specs += [row,
                 pl.BlockSpec((None, None, None, MOD_ROWS, d),
                              lambda i: (gate_layer, gate_row, group_of_tile(i), 0, 0)),
                 _mod_spec(next_layer if next_rows is not None else gate_layer, group_of_tile),
                 vec, vec]
    args = flat_a + list(w_list) + [x, mod_all, mod_all, ln_g.reshape(1, d), ln_b.reshape(1, d)]
    out_specs = [row]
    out_shape = [jax.ShapeDtypeStruct((t, d), F32)]
    with_router = w_router is not None
    if next_rows is not None:
        out_specs.append(row)
        out_shape.append(jax.ShapeDtypeStruct((t, d), F32 if with_router else BF16))
    if with_router:
        in_specs.append(pl.BlockSpec(w_router.shape, lambda i: (0, 0)))
        args.append(w_router)
        out_specs.append(pl.BlockSpec((tm, LANES), lambda i: (i, 0)))
        out_shape.append(jax.ShapeDtypeStruct((t, LANES), F32))
    return pl.pallas_call(
        functools.partial(_res_ln_kernel, split=split, n_first=n_first, alpha=alpha,
                          next_rows=next_rows, with_router=with_router),
        grid=(t // tm,),
        in_specs=in_specs,
        out_specs=out_specs,
        out_shape=out_shape,
        compiler_params=_cparams(1),
        name="res_ln",
    )(*args)


def _ffn_up_kernel(u_ref, wg_ref, wu_ref, o_ref, *, chunk):
    u = u_ref[...]
    n = o_ref.shape[1]
    for c0 in range(0, n, chunk):
        c1 = min(c0 + chunk, n)
        hg = jnp.dot(u, wg_ref[:, c0:c1], preferred_element_type=F32)
        hu = jnp.dot(u, wu_ref[:, c0:c1], preferred_element_type=F32)
        o_ref[:, c0:c1] = (_silu(hg) * hu).astype(BF16)


def _ffn_up(u, wg, wu, tm, chunk=256):
    t, d = u.shape
    n = wg.shape[1]
    wspec = pl.BlockSpec((d, n), lambda i: (0, 0))
    return pl.pallas_call(
        functools.partial(_ffn_up_kernel, chunk=chunk),
        grid=(t // tm,),
        in_specs=[pl.BlockSpec((tm, d), lambda i: (i, 0)), wspec, wspec],
        out_specs=pl.BlockSpec((tm, n), lambda i: (i, 0)),
        out_shape=jax.ShapeDtypeStruct((t, n), BF16),
        compiler_params=_cparams(1),
        name="ffn_up",
    )(u, wg, wu)


def _row_copy(src_hbm, src_row, dst_ref, r, sem):
    return pltpu.make_async_copy(src_hbm.at[pl.ds(src_row, 1)], dst_ref.at[pl.ds(r, 1)], sem)


def _gather_start(idx_ref, idx_base, src_hbm, dst_ref, sem, n_rows):
    def body(r, carry):
        _row_copy(src_hbm, idx_ref[0, idx_base + r], dst_ref, r, sem).start()
        return carry

    lax.fori_loop(0, n_rows, body, 0, unroll=8)


def _gather_wait(src_hbm, dst_ref, sem):
    pltpu.make_async_copy(src_hbm.at[pl.ds(0, dst_ref.shape[0])], dst_ref, sem).wait()


def _moe_ffn_kernel(te_ref, nv_ref, tos_ref, u_hbm, wg_ref, wu_ref, wd_ref, y_ref,
                    xg_ref, xb_ref, acc_ref, sem):
    del te_ref
    i = pl.program_id(0)
    j = pl.program_id(1)
    valid = i < nv_ref[0]

    @pl.when(valid & (j == 0))
    def _():
        _gather_start(tos_ref, 0, u_hbm, xg_ref, sem, xg_ref.shape[0])
        _gather_wait(u_hbm, xg_ref, sem)
        xb_ref[...] = xg_ref[...].astype(BF16)
        acc_ref[...] = jnp.zeros_like(acc_ref)

    @pl.when(valid)
    def _():
        x = xb_ref[...]
        hg = jnp.dot(x, wg_ref[...].astype(BF16), preferred_element_type=F32)
        hu = jnp.dot(x, wu_ref[...].astype(BF16), preferred_element_type=F32)
        acc_ref[...] += jnp.dot((_silu(hg) * hu).astype(BF16), wd_ref[...].astype(BF16),
                                preferred_element_type=F32)

    last = j == pl.num_programs(1) - 1

    @pl.when(valid & last)
    def _():
        y_ref[...] = acc_ref[...]

    @pl.when(jnp.logical_not(valid) & last)
    def _():
        y_ref[...] = jnp.zeros_like(y_ref)


def _moe_ffn(u, tile_expert, n_valid, token_of_slot, wg, wu, wd, tm, tj):
    n_tiles = tile_expert.shape[0]
    d = u.shape[1]
    f = wg.shape[2]
    assert f % tj == 0
    nj = f // tj

    def jj(i, j, nv):
        return jnp.where(i < nv[0], j, nj - 1)

    grid_spec = pltpu.PrefetchScalarGridSpec(
        num_scalar_prefetch=2,
        grid=(n_tiles, nj),
        in_specs=[pl.BlockSpec((None, 1, tm), lambda i, j, te, nv: (i, 0, 0), memory_space=pltpu.SMEM),
                  pl.BlockSpec(memory_space=pl.ANY),
                  pl.BlockSpec((None, d, tj), lambda i, j, te, nv: (te[i], 0, jj(i, j, nv))),
                  pl.BlockSpec((None, d, tj), lambda i, j, te, nv: (te[i], 0, jj(i, j, nv))),
                  pl.BlockSpec((None, tj, d), lambda i, j, te, nv: (te[i], jj(i, j, nv), 0))],
        out_specs=pl.BlockSpec((tm, d), lambda i, j, te, nv: (i, 0)),
        scratch_shapes=[pltpu.VMEM((tm, d), F32), pltpu.VMEM((tm, d), BF16),
                        pltpu.VMEM((tm, d), F32), pltpu.SemaphoreType.DMA(())])
    return pl.pallas_call(
        _moe_ffn_kernel,
        grid_spec=grid_spec,
        out_shape=jax.ShapeDtypeStruct((n_tiles * tm, d), F32),
        compiler_params=_cparams(2),
        name="moe_ffn",
    )(tile_expert, n_valid, token_of_slot, u, wg, wu, wd)


def _moe_combine_kernel(slots_ref, y_hbm, route_ref, x_ref, modg_ref, g_ref, b_ref, o_ref,
                        buf_ref, sem, *, alpha):
    rows = x_ref.shape[0]
    for k in range(TOP_K):
        _gather_start(slots_ref, k * rows, y_hbm, buf_ref.at[k], sem, rows)
    for k in range(TOP_K):
        _gather_wait(y_hbm, buf_ref.at[k], sem)
    route = route_ref[...]
    f = route[:, 2:3] * buf_ref[0] + route[:, 3:4] * buf_ref[1]
    o_ref[...] = _layer_norm(_gated(alpha * x_ref[...], modg_ref[...], f), g_ref[...], b_ref[...])


def _moe_combine(y, slots, route, x, mod_all, gate_layer, gate_row, ln_g, ln_b, alpha, tm,
                 group_of_tile):
    t, d = x.shape
    row = pl.BlockSpec((tm, d), lambda i: (i, 0))
    vec = pl.BlockSpec((1, d), lambda i: (0, 0))
    return pl.pallas_call(
        functools.partial(_moe_combine_kernel, alpha=alpha),
        grid=(t // tm,),
        in_specs=[pl.BlockSpec((None, 1, 2 * tm), lambda i: (i, 0, 0), memory_space=pltpu.SMEM),
                  pl.BlockSpec(memory_space=pl.ANY),
                  pl.BlockSpec((tm, LANES), lambda i: (i, 0)),
                  row,
                  pl.BlockSpec((None, None, None, MOD_ROWS, d),
                               lambda i: (gate_layer, gate_row, group_of_tile(i), 0, 0)),
                  vec, vec],
        out_specs=row,
        out_shape=jax.ShapeDtypeStruct((t, d), F32),
        scratch_shapes=[pltpu.VMEM((2, tm, d), F32), pltpu.SemaphoreType.DMA(())],
        compiler_params=_cparams(1),
        name="moe_combine",
    )(slots, y, route, x, mod_all, ln_g.reshape(1, d), ln_b.reshape(1, d))


def _moe_routing(route, tm):
    t = route.shape[0]
    n_assign = TOP_K * t
    n_tiles = -(-n_assign // tm) + N_EXPERTS
    experts = route[:, :TOP_K].astype(jnp.int32).reshape(n_assign)
    onehot = (experts[:, None] == jnp.arange(N_EXPERTS, dtype=jnp.int32)[None, :]).astype(jnp.int32)
    rank = jnp.sum((jnp.cumsum(onehot, axis=0) - onehot) * onehot, axis=1)
    count = jnp.sum(onehot, axis=0)
    padded = (count + tm - 1) // tm * tm
    ends = jnp.cumsum(padded)
    slot = jnp.sum(onehot * (ends - padded)[None, :], axis=1) + rank
    n_valid = ends[-1] // tm
    starts = jnp.arange(n_tiles, dtype=jnp.int32) * tm
    tile_expert = jnp.minimum(jnp.sum((starts[:, None] >= ends[None, :]).astype(jnp.int32), axis=1),
                              N_EXPERTS - 1)
    tile_expert = jnp.where(starts < ends[-1], tile_expert, tile_expert[jnp.maximum(n_valid - 1, 0)])
    token_of_slot = jnp.zeros((n_tiles * tm,), jnp.int32).at[slot].set(
        jnp.arange(n_assign, dtype=jnp.int32) // TOP_K)
    return (tile_expert.astype(jnp.int32), n_valid.reshape(1).astype(jnp.int32),
            token_of_slot.reshape(n_tiles, 1, tm), slot.reshape(t, TOP_K).astype(jnp.int32))


def _rel_bucket_table(max_dist):
    n = np.arange(max_dist + 1)
    max_exact = N_BUCKETS // 2
    ratio = np.log(np.maximum(n, 1).astype(np.float32) / np.float32(max_exact))
    large = max_exact + (ratio / np.float32(math.log(MAX_DISTANCE / max_exact))
                         * np.float32(N_BUCKETS - max_exact)).astype(np.int32)
    large = np.minimum(large, N_BUCKETS - 1)
    return np.where(n < max_exact, n, large).astype(np.int32)


def _bias_lookup(rel_table, bucket_idx):
    onehot = jnp.asarray(np.eye(N_BUCKETS, dtype=np.float32)[bucket_idx.reshape(-1)])
    vals = jnp.dot(onehot, rel_table, precision=HIGHEST)
    return vals.T.reshape((rel_table.shape[1],) + bucket_idx.shape)


def _select_top_blocks(gate, n_valid, ksel):
    blk = lax.broadcasted_iota(jnp.int32, gate.shape, 0)
    nb = gate.shape[0]
    g = jnp.where(blk < n_valid, gate, -jnp.inf)
    keep = blk == n_valid
    for _ in range(ksel):
        mx = jnp.max(g, axis=0, keepdims=True)
        first = jnp.min(jnp.where(g == mx, blk, nb), axis=0, keepdims=True)
        hit = blk == first
        keep = keep | (hit & (blk < n_valid))
        g = jnp.where(hit, -jnp.inf, g)
    return jnp.where(keep, 0.0, NEG)


def _moba_prompt_kernel(q_ref, k_ref, v_ref, bias_ref, o_ref,
                        kmean_ref, vt_ref, qm_ref, sel_ref, m_ref, l_ref, acc_ref, *, ksel):
    qt = pl.program_id(1)
    seq = k_ref.shape[0]
    nb = seq // MOBA_BLOCK
    gw = 4 * HEAD_DIM
    n_groups = N_HEADS // 4
    nt = (((1,), (1,)), ((), ()))

    @pl.when(qt == 0)
    def _():
        row = lax.broadcasted_iota(jnp.int32, (nb, seq), 0)
        col = lax.broadcasted_iota(jnp.int32, (nb, seq), 1)
        pool = jnp.where(col // MOBA_BLOCK == row, 1.0, 0.0).astype(BF16)
        kmean_ref[...] = jnp.dot(pool, k_ref[...], preferred_element_type=F32) * (1.0 / MOBA_BLOCK)
        for n in range(nb):
            vt_ref[n] = v_ref[n * MOBA_BLOCK:(n + 1) * MOBA_BLOCK, :].astype(F32).T.astype(BF16)

    head_of_lane = lax.broadcasted_iota(jnp.int32, (1, gw), 1) // HEAD_DIM
    q = q_ref[...] * (HEAD_DIM ** -0.5)
    kmean = kmean_ref[...].astype(BF16)
    for h in range(N_HEADS):
        g, hh = divmod(h, 4)
        qm = jnp.where(head_of_lane == hh, q[:, g * gw:(g + 1) * gw], 0.0).astype(BF16)
        qm_ref[h] = qm
        gate = lax.dot_general(kmean[:, g * gw:(g + 1) * gw], qm, nt, preferred_element_type=F32)
        sel_ref[h] = _select_top_blocks(gate, qt, ksel)
        m_ref[h] = jnp.full(m_ref.shape[1:], NEG, F32)
        l_ref[h] = jnp.zeros(l_ref.shape[1:], F32)
    acc_ref[...] = jnp.zeros_like(acc_ref)

    def body(step, carry, near):
        n = qt - step
        start = pl.multiple_of(n * MOBA_BLOCK, MOBA_BLOCK)
        kb = k_ref[pl.ds(start, MOBA_BLOCK), :]
        vt = vt_ref[n]
        scores = [lax.dot_general(kb[:, (h // 4) * gw:(h // 4 + 1) * gw], qm_ref[h], nt,
                                  preferred_element_type=F32) for h in range(N_HEADS)]
        probs, alphas = [], []
        for h in range(N_HEADS):
            shift = sel_ref[h, pl.ds(n, 1), :]
            if near:
                s = scores[h] + bias_ref[h, step]
            else:
                s = scores[h]
                shift = shift + bias_ref[h, 2, 0:1, :]
            m_old = m_ref[h]
            m_new = jnp.maximum(m_old, jnp.max(s, axis=0, keepdims=True) + shift)
            alpha = jnp.exp(m_old - m_new)
            p = jnp.exp(s - (m_new - shift))
            l_ref[h] = alpha * l_ref[h] + jnp.sum(p, axis=0, keepdims=True)
            m_ref[h] = m_new
            probs.append(p.astype(BF16))
            alphas.append(alpha)
        outs = [jnp.dot(vt[h * HEAD_DIM:(h + 1) * HEAD_DIM, :], probs[h],
                        preferred_element_type=F32) for h in range(N_HEADS)]
        for h in range(N_HEADS):
            g, hh = divmod(h, 4)
            rows = slice(hh * HEAD_DIM, (hh + 1) * HEAD_DIM)
            acc_ref[g, rows, :] = alphas[h] * acc_ref[g, rows, :] + outs[h]
        return carry

    lax.fori_loop(0, jnp.minimum(qt + 1, 2), functools.partial(body, near=True), 0)
    lax.fori_loop(2, qt + 1, functools.partial(body, near=False), 0)

    for g in range(n_groups):
        o_t = jnp.concatenate(
            [acc_ref[g, hh * HEAD_DIM:(hh + 1) * HEAD_DIM, :] * (1.0 / l_ref[4 * g + hh])
             for hh in range(4)], axis=0)
        o_ref[:, g * gw:(g + 1) * gw] = o_t.T.astype(o_ref.dtype)


def _moba_bias_tables(rel_table):
    bucket = _rel_bucket_table(2 * MOBA_BLOCK)
    i = np.arange(MOBA_BLOCK)[None, :]
    j = np.arange(MOBA_BLOCK)[:, None]
    own = jnp.where(jnp.asarray(i >= j), _bias_lookup(rel_table, bucket[np.maximum(i - j, 0)]), NEG)
    prev = _bias_lookup(rel_table, bucket[MOBA_BLOCK + i - j])
    far = jnp.broadcast_to(rel_table[N_BUCKETS - 1][:, None, None], prev.shape)
    return jnp.stack([own, prev, far], axis=1).astype(F32)


def _moba_prompt(qkv, bias_tab, batch, seq):
    nb = seq // MOBA_BLOCK
    ksel = min(MOBA_TOPK, nb)
    stat = pltpu.VMEM((N_HEADS, 1, MOBA_BLOCK), F32)
    return pl.pallas_call(
        functools.partial(_moba_prompt_kernel, ksel=ksel),
        grid=(batch, nb),
        in_specs=[pl.BlockSpec((MOBA_BLOCK, D_HEADS), lambda b, t: (b * nb + t, 0)),
                  pl.BlockSpec((seq, D_HEADS), lambda b, t: (b, 1)),
                  pl.BlockSpec((seq, D_HEADS), lambda b, t: (b, 2)),
                  pl.BlockSpec(bias_tab.shape, lambda b, t: (0, 0, 0, 0))],
        out_specs=pl.BlockSpec((MOBA_BLOCK, D_HEADS), lambda b, t: (b * nb + t, 0)),
        out_shape=jax.ShapeDtypeStruct((batch * seq, D_HEADS), BF16),
        scratch_shapes=[pltpu.VMEM((nb, D_HEADS), F32),
                        pltpu.VMEM((nb, D_HEADS, MOBA_BLOCK), BF16),
                        pltpu.VMEM((N_HEADS, MOBA_BLOCK, 4 * HEAD_DIM), BF16),
                        pltpu.VMEM((N_HEADS, nb, MOBA_BLOCK), F32),
                        stat, stat,
                        pltpu.VMEM((N_HEADS // 4, 4 * HEAD_DIM, MOBA_BLOCK), F32)],
        compiler_params=_cparams(2),
        name="moba_prompt",
    )(qkv, qkv, qkv, bias_tab)


def _moba_sample_kernel(pt_ref, q_ref, kn_ref, vn_ref, *refs, ksel, n_q, nbp, pages_per_block):
    del pt_ref
    n_pages = nbp * pages_per_block
    k_refs, v_refs = refs[:n_pages], refs[n_pages:2 * n_pages]
    bias_ref, bown_ref, o_ref, m_ref, l_ref, g_ref, acc_ref = refs[2 * n_pages:]
    nt = (((1,), (1,)), ((), ()))
    dh = N_HEADS * HEAD_DIM
    q8 = q_ref[...]
    slots = q8.shape[0]
    lane_head = lax.broadcasted_iota(jnp.int32, (1, dh), 1) // HEAD_DIM
    qbd = jnp.concatenate([jnp.where(lane_head == hp, q8, 0.0) for hp in range(N_HEADS)], axis=0)
    qs = (qbd * (HEAD_DIM ** -0.5)).astype(BF16)
    ones = jnp.ones((8, PAGE_SIZE), BF16)

    kts = [k_ref[...].reshape(dh, PAGE_SIZE).astype(BF16) for k_ref in k_refs]
    s_pages = [jnp.dot(qs, kt, preferred_element_type=F32) for kt in kts]
    ksums = [lax.dot_general(ones, kt, nt, preferred_element_type=F32) for kt in kts]
    probs = []
    for j in range(nbp):
        pages = range(j * pages_per_block, (j + 1) * pages_per_block)
        s = jnp.concatenate([s_pages[pg] for pg in pages], axis=1) + bias_ref[0 if j == nbp - 1 else 1]
        m = jnp.max(s, axis=-1, keepdims=True)
        p32 = jnp.exp(s - m)
        probs.append(p32.astype(BF16))
        m_ref[j] = m
        l_ref[j] = jnp.sum(p32, axis=-1, keepdims=True)
        ksum = functools.reduce(jnp.add, [ksums[pg] for pg in pages])
        g_ref[j] = jnp.sum(qbd * ksum[0:1, :], axis=-1, keepdims=True) * (1.0 / MOBA_BLOCK)
    pv = [lax.dot_general(probs[pg // pages_per_block][:, (pg % pages_per_block) * PAGE_SIZE:
                                                       (pg % pages_per_block + 1) * PAGE_SIZE],
                          v_refs[pg][...].reshape(dh, PAGE_SIZE).astype(BF16), nt,
                          preferred_element_type=F32) for pg in range(n_pages)]
    for j in range(nbp):
        acc_ref[j] = functools.reduce(jnp.add, pv[j * pages_per_block:(j + 1) * pages_per_block])

    s_own = [jnp.sum(qbd * kn_ref[i:i + 1, :], axis=-1, keepdims=True) * (HEAD_DIM ** -0.5)
             + bown_ref[:, i:i + 1] for i in range(n_q)]
    m_o = functools.reduce(jnp.maximum, s_own)
    p_own = [jnp.exp(s_i - m_o) for s_i in s_own]
    l_o = functools.reduce(jnp.add, p_own)
    acc_o = functools.reduce(jnp.add, [p_i * vn_ref[i:i + 1, :] for i, p_i in enumerate(p_own)])

    gg = g_ref[...]
    blk = lax.broadcasted_iota(jnp.int32, gg.shape, 0)
    keep = blk < 0
    for _ in range(ksel):
        mx = jnp.max(gg, axis=0, keepdims=True)
        first = jnp.min(jnp.where(gg == mx, blk, nbp), axis=0, keepdims=True)
        hit = (blk == first) & (gg > -jnp.inf)
        keep = keep | hit
        gg = jnp.where(blk == first, -jnp.inf, gg)
    m_all = m_ref[...]
    m_tot = jnp.maximum(jnp.max(jnp.where(keep, m_all, NEG), axis=0), m_o)
    w = jnp.where(keep, jnp.exp(m_all - m_tot[None]), 0.0)
    w_o = jnp.exp(m_o - m_tot)
    l_tot = jnp.sum(w * l_ref[...], axis=0) + w_o * l_o
    acc_tot = (jnp.sum(w * acc_ref[...], axis=0) + w_o * acc_o) * (1.0 / l_tot)
    row_head = lax.broadcasted_iota(jnp.int32, (N_HEADS * slots, 1), 0) // slots
    acc_tot = jnp.where(row_head == lane_head, acc_tot, 0.0)
    o_ref[...] = functools.reduce(
        jnp.add, [acc_tot[:, hp * HEAD_DIM:(hp + 1) * HEAD_DIM] for hp in range(N_HEADS)])


def _moba_sample_bias(rel_table, past_len, n_q):
    del past_len
    slots = SAMPLE_PAD
    bucket = _rel_bucket_table(MOBA_BLOCK + slots)
    rows_h = np.repeat(np.arange(N_HEADS), slots)
    rows_i = np.minimum(np.tile(np.arange(slots), N_HEADS), n_q - 1)
    pos = np.arange(MOBA_BLOCK)[None, :]
    by_head = _bias_lookup(rel_table, bucket[MOBA_BLOCK + rows_i[:, None] - pos])
    pick = jnp.asarray(np.arange(N_HEADS)[:, None, None] == rows_h[None, :, None])
    near = jnp.sum(jnp.where(pick, by_head, 0.0), axis=0)
    far = jnp.broadcast_to(jnp.repeat(rel_table[N_BUCKETS - 1], slots)[:, None], near.shape)
    past = jnp.stack([near, far]).astype(F32)
    new = np.arange(slots)[None, :]
    own_h = _bias_lookup(rel_table, bucket[np.maximum(rows_i[:, None] - new, 0)])
    own = jnp.sum(jnp.where(pick, own_h, 0.0), axis=0)
    ok = jnp.asarray((new <= rows_i[:, None]) & (new < n_q))
    return past, jnp.where(ok, own, NEG).astype(F32)


def _moba_sample(q_rows, k_new, v_new, cache_k, cache_v, page_table, layer, bias_past, bias_own, n_q):
    bsz, slots, dh = q_rows.shape
    hd = HEAD_DIM
    rows = N_HEADS * slots
    pages_per_block = MOBA_BLOCK // PAGE_SIZE
    nbp = page_table.shape[1] // pages_per_block
    ksel = min(MOBA_TOPK, nbp + 1)

    n_pages = nbp * pages_per_block

    def page_spec(page):
        return pl.BlockSpec((None, None, N_HEADS, hd, PAGE_SIZE),
                            lambda b, pt: (layer, pt[b, page], 0, 0, 0))

    per_seq = lambda b, pt: (b, 0, 0)
    stat = pltpu.VMEM((nbp, rows, 1), F32)
    grid_spec = pltpu.PrefetchScalarGridSpec(
        num_scalar_prefetch=1,
        grid=(bsz,),
        in_specs=([pl.BlockSpec((None, slots, dh), per_seq)] * 3
                  + [page_spec(page) for page in range(n_pages)] * 2
                  + [pl.BlockSpec(bias_past.shape, lambda b, pt: (0, 0, 0)),
                     pl.BlockSpec(bias_own.shape, lambda b, pt: (0, 0))]),
        out_specs=pl.BlockSpec((None, rows, hd), per_seq),
        scratch_shapes=[stat, stat, stat, pltpu.VMEM((nbp, rows, dh), F32)])
    return pl.pallas_call(
        functools.partial(_moba_sample_kernel, ksel=ksel, n_q=n_q, nbp=nbp,
                          pages_per_block=pages_per_block),
        grid_spec=grid_spec,
        out_shape=jax.ShapeDtypeStruct((bsz, rows, hd), F32),
        compiler_params=_cparams(1),
        name="moba_sample",
    )(page_table, q_rows, k_new, v_new, *([cache_k] * n_pages), *([cache_v] * n_pages),
      bias_past, bias_own)


def _softplus(x):
    return jnp.maximum(x, 0.0) + jnp.log1p(jnp.exp(-jnp.abs(x)))


def _bmm(a, b):
    return jnp.matmul(a.astype(BF16), b.astype(BF16), preferred_element_type=F32)


def _dot_f32(a, b):
    return jnp.dot(a, b, precision=HIGHEST, preferred_element_type=F32)


def _split_bf16(a):
    hi = a.astype(BF16)
    return hi, (a - hi.astype(F32)).astype(BF16)


def _bmm_3pass(a, b):
    a_hi, a_lo = _split_bf16(a)
    b_hi, b_lo = _split_bf16(b)
    return (jnp.matmul(a_hi, b_hi, preferred_element_type=F32)
            + jnp.matmul(a_hi, b_lo, preferred_element_type=F32)
            + jnp.matmul(a_lo, b_hi, preferred_element_type=F32))


def _unit_lower_inverse(a, row, col):
    c = a.shape[1]
    eye = jnp.where(row == col, 1.0, 0.0)
    in16 = (row // 16) == (col // 16)
    in32 = (row // 32) == (col // 32)
    nil = jnp.where(in16, -a, 0.0)
    x = eye + nil
    p = _bmm_3pass(nil, nil)
    for _ in range(2):
        xp = _bmm_3pass(jnp.concatenate([x, p], axis=1), p)
        x = x + xp[:, :c]
        p = xp[:, c:]
    x = x + _bmm_3pass(x, p)
    for off in (jnp.where(in32 & jnp.logical_not(in16), a, 0.0),
                jnp.where(in32, 0.0, a)):
        x = x - _bmm(x, _bmm(off, x))
    return x


def _gdn_prepare(q, k, v, k_t, g_col, g_row, beta, row, col):
    c = q.shape[1]
    incl = row >= col
    decay = jnp.where(incl, jnp.exp(jnp.where(incl, g_col - g_row, 0.0)), 0.0)
    qk_kk = _bmm(jnp.concatenate([q, k], axis=1), k_t)
    a_mat = jnp.where(row > col, qk_kk[:, c:] * decay * beta, 0.0)
    e_g = jnp.exp(g_col)
    rhs = jnp.concatenate([v * beta, k * beta * e_g], axis=2)
    sol = _bmm(_unit_lower_inverse(a_mat, row, col), rhs)
    dv = v.shape[2]
    g_last = g_col[:, c - 1:c, :]
    w_qd = jnp.concatenate([sol[:, :, dv:], q * e_g], axis=1).astype(BF16)
    qk = jnp.where(incl, qk_kk[:, :c] * decay, 0.0).astype(BF16)
    k_dec_t = (k_t * jnp.exp(g_last - g_row)).astype(BF16)
    return sol[:, :, :dv], w_qd, qk, k_dec_t, jnp.exp(g_last)


def _gdn_step(u, w_qd, qk, k_dec_t, chunk_decay, state):
    c = u.shape[1]
    ws_qs = jnp.matmul(w_qd, state.astype(BF16), preferred_element_type=F32)
    v_new = (u - ws_qs[:, :c]).astype(BF16)
    o = ws_qs[:, c:] + jnp.matmul(qk, v_new, preferred_element_type=F32)
    new_state = state * chunk_decay + jnp.matmul(k_dec_t, v_new, preferred_element_type=F32)
    return o, new_state


def _gdn_prompt_kernel(x_ref, z_ref, gab_ref, cw_ref, alog_ref, dtb_ref, nw_ref, seg_ref,
                       o_ref, conv_ref, sout_ref,
                       xbuf_ref, halo_ref, s_ref, u_ref, wqd_ref, qk_ref, kdt_ref, cd_ref, oh_ref):
    t = pl.program_id(1)
    tl = x_ref.shape[0]
    cs = GDN_CHUNK
    n_chunks = tl // cs
    hd = HEAD_DIM

    @pl.when(t == 0)
    def _():
        halo_ref[...] = jnp.zeros_like(halo_ref)
        s_ref[...] = jnp.zeros_like(s_ref)

    x = x_ref[...]
    xbuf_ref[0:8, :] = halo_ref[...]
    xbuf_ref[8:8 + tl, :] = x
    y = cw_ref[0:1, :] * xbuf_ref[5:5 + tl, :]
    for i in range(1, GDN_CONV):
        y = y + cw_ref[i:i + 1, :] * xbuf_ref[5 + i:5 + i + tl, :]
    y = _silu(y)
    tail = x[tl - 8:, :]
    halo_ref[...] = tail
    conv_ref[...] = tail[8 - (GDN_CONV - 1):, :]

    dh = N_HEADS * hd
    seg = seg_ref[...]

    def l2n(a):
        sq_hi, sq_lo = _split_bf16(a * a)
        ss = (jnp.dot(sq_hi, seg, preferred_element_type=F32)
              + jnp.dot(sq_lo, seg, preferred_element_type=F32))
        return a * lax.rsqrt(ss + 1e-6)

    qn = l2n(y[:, :dh]) * (hd ** -0.5)
    kn = l2n(y[:, dh:2 * dh])
    vv = y[:, 2 * dh:]
    kn_t = kn.T

    gab = gab_ref[...]
    beta = 1.0 / (1.0 + jnp.exp(-gab))
    glog = -jnp.exp(alog_ref[...]) * _softplus(gab + dtb_ref[...])

    row = lax.broadcasted_iota(jnp.int32, (cs, cs), 0)
    col = lax.broadcasted_iota(jnp.int32, (cs, cs), 1)
    tril = jnp.where(row >= col, 1.0, 0.0)
    pick = jnp.where(lax.broadcasted_iota(jnp.int32, (N_HEADS, LANES), 1)
                     == lax.broadcasted_iota(jnp.int32, (N_HEADS, LANES), 0) + N_HEADS, 1.0, 0.0)
    parts = {name: [] for name in ("q", "k", "v", "kt", "gc", "gr", "bb")}
    for c in range(n_chunks):
        rows = slice(c * cs, (c + 1) * cs)
        gcum = _dot_f32(tril, glog[rows])
        g_rows = lax.dot_general(pick, gcum, (((1,), (1,)), ((), ())), precision=HIGHEST,
                                 preferred_element_type=F32)
        for h in range(N_HEADS):
            lanes = slice(h * hd, (h + 1) * hd)
            parts["q"].append(qn[rows, lanes])
            parts["k"].append(kn[rows, lanes])
            parts["v"].append(vv[rows, lanes])
            parts["kt"].append(kn_t[lanes, rows])
            parts["gc"].append(jnp.broadcast_to(gcum[:, N_HEADS + h:N_HEADS + h + 1], (cs, cs)))
            parts["gr"].append(g_rows[h:h + 1, :])
            parts["bb"].append(jnp.broadcast_to(beta[rows, h:h + 1], (cs, cs)))
    st = {name: jnp.stack(vals) for name, vals in parts.items()}
    u_all, w_qd, qk, k_dec_t, chunk_decay = _gdn_prepare(
        st["q"], st["k"], st["v"], st["kt"], st["gc"], st["gr"], st["bb"], row, col)
    u_ref[...] = u_all.reshape(u_ref.shape)
    wqd_ref[...] = w_qd.reshape(wqd_ref.shape)
    qk_ref[...] = qk.reshape(qk_ref.shape)
    kdt_ref[...] = k_dec_t.reshape(kdt_ref.shape)
    cd_ref[...] = chunk_decay.reshape(cd_ref.shape)

    def chunk_body(c, carry):
        o, s_new = _gdn_step(u_ref[c], wqd_ref[c], qk_ref[c], kdt_ref[c], cd_ref[c], s_ref[...])
        oh_ref[c] = o
        s_ref[...] = s_new
        return carry

    lax.fori_loop(0, n_chunks, chunk_body, 0)

    z = z_ref[...]
    nw = nw_ref[...]
    for h in range(N_HEADS):
        lanes = slice(h * hd, (h + 1) * hd)
        o = jnp.concatenate([oh_ref[c, h] for c in range(n_chunks)], axis=0)
        o = o * lax.rsqrt(jnp.mean(o * o, axis=-1, keepdims=True) + 1e-6) * nw
        o_ref[:, lanes] = (o * _silu(z[:, lanes])).astype(o_ref.dtype)

    @pl.when(t == pl.num_programs(1) - 1)
    def _():
        sout_ref[...] = s_ref[...]


def _gdn_prompt(proj, batch, seq, tl, conv_w, a_log, dt_bias, norm_w):
    hd = HEAD_DIM
    cs = GDN_CHUNK
    n_steps = seq // tl
    n_chunks = tl // cs
    col_x, col_z, col_gab = 1, D_IN_MAIN // D_HEADS - 1, D_IN_MAIN // LANES
    lane = np.arange(D_HEADS)
    seg = jnp.asarray((lane[:, None] // hd == lane[None, :] // hd).astype(np.float32)).astype(BF16)
    alog_row = jnp.zeros((1, LANES), F32).at[0, N_HEADS:2 * N_HEADS].set(a_log)
    dtb_row = jnp.zeros((1, LANES), F32).at[0, N_HEADS:2 * N_HEADS].set(dt_bias)
    const2 = lambda b, t: (0, 0)
    per_inst = pltpu.VMEM((n_chunks, N_HEADS, cs, hd), F32)
    per_inst_bf16 = pltpu.VMEM((n_chunks, N_HEADS, hd, cs), BF16)
    return pl.pallas_call(
        _gdn_prompt_kernel,
        grid=(batch, n_steps),
        in_specs=[pl.BlockSpec((tl, D_CONV), lambda b, t: (b * n_steps + t, col_x)),
                  pl.BlockSpec((tl, D_HEADS), lambda b, t: (b * n_steps + t, col_z)),
                  pl.BlockSpec((tl, LANES), lambda b, t: (b * n_steps + t, col_gab)),
                  pl.BlockSpec((GDN_CONV, D_CONV), const2),
                  pl.BlockSpec((1, LANES), const2),
                  pl.BlockSpec((1, LANES), const2),
                  pl.BlockSpec((1, hd), const2),
                  pl.BlockSpec((D_HEADS, D_HEADS), const2)],
        out_specs=[pl.BlockSpec((tl, D_HEADS), lambda b, t: (b * n_steps + t, 0)),
                   pl.BlockSpec((None, GDN_CONV - 1, D_CONV), lambda b, t: (b, 0, 0)),
                   pl.BlockSpec((None, N_HEADS, hd, hd), lambda b, t: (b, 0, 0, 0))],
        out_shape=[jax.ShapeDtypeStruct((batch * seq, D_HEADS), BF16),
                   jax.ShapeDtypeStruct((batch, GDN_CONV - 1, D_CONV), F32),
                   jax.ShapeDtypeStruct((batch, N_HEADS, hd, hd), F32)],
        scratch_shapes=[pltpu.VMEM((8 + tl + 8, D_CONV), F32),
                        pltpu.VMEM((8, D_CONV), F32),
                        pltpu.VMEM((N_HEADS, hd, hd), F32),
                        per_inst,
                        pltpu.VMEM((n_chunks, N_HEADS, 2 * cs, hd), BF16),
                        per_inst_bf16, per_inst_bf16,
                        pltpu.VMEM((n_chunks, N_HEADS, 1, hd), F32),
                        per_inst],
        compiler_params=_cparams(2),
        name="gdn_prompt",
    )(proj, proj, proj, conv_w, alog_row, dtb_row, norm_w.reshape(1, hd), seg)


def _gdn_sample_kernel(xq_ref, xk_ref, xv_ref, cq_ref, ck_ref, cv_ref, wq_ref, wk_ref, wv_ref,
                       z_ref, gab_ref, alog_ref, dtb_ref, nw_ref, s_ref,
                       o_ref, sout_ref, kq_ref, gt_ref, ot_ref):
    pair = pl.program_id(0)
    n_q = xq_ref.shape[0]
    hd = HEAD_DIM

    def conv_t(x_ref, c_ref, w_ref):
        xp = [c_ref[i] for i in range(GDN_CONV - 1)] + [x_ref[i] for i in range(n_q)]
        out = []
        for i in range(n_q):
            y = w_ref[0:1, :] * xp[i]
            for m in range(1, GDN_CONV):
                y = y + w_ref[m:m + 1, :] * xp[i + m]
            out.append(_silu(y).T)
        return out

    q_t = conv_t(xq_ref, cq_ref, wq_ref)
    k_t = conv_t(xk_ref, ck_ref, wk_ref)
    v_t = conv_t(xv_ref, cv_ref, wv_ref)

    def l2n(a):
        return a * lax.rsqrt(jnp.sum(a * a, axis=0, keepdims=True) + 1e-6)

    for i in range(n_q):
        gab_t = gab_ref[i].T
        gt_ref[0, i] = 1.0 / (1.0 + jnp.exp(-gab_t))
        gt_ref[1, i] = jnp.exp(-jnp.exp(alog_ref[...]) * _softplus(gab_t + dtb_ref[...]))

    for hh in range(2):
        rows = slice(hh * hd, (hh + 1) * hd)
        head = 2 * pair + hh
        for i in range(n_q):
            kq_ref[0] = l2n(k_t[i][rows])
            kq_ref[1] = l2n(q_t[i][rows]) * (hd ** -0.5)
            beta = gt_ref[0, i, pl.ds(head, 1), :]
            decay = gt_ref[1, i, pl.ds(N_HEADS + head, 1), :]
            src = s_ref if i == 0 else sout_ref

            def ks_body(kk, acc):
                return acc + kq_ref[0, pl.ds(kk, 1), :] * src[hh, kk]

            k_s = lax.fori_loop(0, hd, ks_body, jnp.zeros((hd, k_t[i].shape[1]), F32), unroll=8)
            r = beta * (v_t[i][rows] - decay * k_s)

            def upd_body(kk, acc):
                s_new = decay * src[hh, kk] + kq_ref[0, pl.ds(kk, 1), :] * r
                sout_ref[hh, kk] = s_new
                return acc + kq_ref[1, pl.ds(kk, 1), :] * s_new

            o = lax.fori_loop(0, hd, upd_body, jnp.zeros_like(r), unroll=8)
            ot_ref[i, rows, :] = o * lax.rsqrt(jnp.mean(o * o, axis=0, keepdims=True) + 1e-6) * nw_ref[...]

    for i in range(n_q):
        o_ref[i] = (ot_ref[i].T * _silu(z_ref[i])).astype(o_ref.dtype)


def _gdn_sample(proj_s, conv_state, state_t, conv_w, a_log, dt_bias, norm_w):
    n_q, bsz, _ = proj_s.shape
    hd = HEAD_DIM
    pw = 2 * hd
    n_pairs = N_HEADS // 2
    base = D_CONV // pw
    col = jnp.zeros((LANES, 1), F32)
    alog_col = col.at[N_HEADS:2 * N_HEADS, 0].set(a_log)
    dtb_col = col.at[N_HEADS:2 * N_HEADS, 0].set(dt_bias)
    nw_col = norm_w.reshape(hd, 1)

    def xspec(part):
        return pl.BlockSpec((n_q, bsz, pw), lambda p: (0, 0, base + part * n_pairs + p))

    def cspec(rows, part):
        return pl.BlockSpec((rows, bsz, pw) if rows else (GDN_CONV, pw),
                            (lambda p: (0, 0, part * n_pairs + p)) if rows
                            else (lambda p: (0, part * n_pairs + p)))

    const = lambda p: (0, 0)
    state_spec = pl.BlockSpec((2, hd, hd, bsz), lambda p: (p, 0, 0, 0))
    return pl.pallas_call(
        _gdn_sample_kernel,
        grid=(n_pairs,),
        in_specs=[xspec(0), xspec(1), xspec(2),
                  cspec(GDN_CONV - 1, 0), cspec(GDN_CONV - 1, 1), cspec(GDN_CONV - 1, 2),
                  cspec(0, 0), cspec(0, 1), cspec(0, 2),
                  pl.BlockSpec((n_q, bsz, pw), lambda p: (0, 0, D_IN_MAIN // pw - n_pairs + p)),
                  pl.BlockSpec((n_q, bsz, LANES), lambda p: (0, 0, D_IN_MAIN // LANES)),
                  pl.BlockSpec((LANES, 1), const), pl.BlockSpec((LANES, 1), const),
                  pl.BlockSpec((hd, 1), const),
                  state_spec],
        out_specs=[pl.BlockSpec((n_q, bsz, pw), lambda p: (0, 0, p)), state_spec],
        out_shape=[jax.ShapeDtypeStruct((n_q, bsz, D_HEADS), BF16),
                   jax.ShapeDtypeStruct(state_t.shape, F32)],
        scratch_shapes=[pltpu.VMEM((2, hd, bsz), F32),
                        pltpu.VMEM((2, n_q, LANES, bsz), F32),
                        pltpu.VMEM((n_q, pw, bsz), F32)],
        compiler_params=_cparams(1),
        name="gdn_sample",
    )(proj_s, proj_s, proj_s, conv_state, conv_state, conv_state, conv_w, conv_w, conv_w,
      proj_s, proj_s, alog_col, dtb_col, nw_col, state_t)


TM = 512
TM_MOE = 1024
TJ_MOE = 896
GDN_ROWS = 256
SAMPLE_PAD = 8


def kernel(x_prompt, x_sample, cache_k, cache_v, state_conv, state_gdn, page_table, c_prompt, c_sample,
           ln_in_g, ln_in_b, w_mod, b_mod, w_in, conv_w, a_log, dt_bias, gdn_norm_w, w_out, rel_table,
           ln_g, ln_b, ffn_w_gate, ffn_w_up, ffn_w_down, moe_router, moe_w_gate, moe_w_up, moe_w_down):
    bp, seq, d = x_prompt.shape
    bs, n_q, _ = x_sample.shape
    depth = w_in.shape[0]
    tp, ts = bp * seq, bs * n_q
    h, hd = N_HEADS, HEAD_DIM
    assert bs == MOD_ROWS and seq % TM == 0 and ts % TM == 0 and seq % MOBA_BLOCK == 0
    assert depth == 2 and GDN_CONV - 1 <= n_q <= SAMPLE_PAD and GDN_CHUNK == HEAD_DIM
    alpha = (2 * depth) ** 0.25

    def groups(tm):
        return lambda i: jnp.minimum(i // (seq // tm), bp)

    n_c = bp + bs
    c_all = jnp.pad(jnp.concatenate([c_prompt, c_sample]), ((0, (-n_c) % 8), (0, 0)))
    mod = _mod_vectors(c_all, w_mod, b_mod).reshape(depth, -1, 6, d)
    mod_p = jnp.broadcast_to(mod[:, :bp].transpose(0, 2, 1, 3)[:, :, :, None, :],
                             (depth, 6, bp, MOD_ROWS, d))
    mod_s = mod[:, bp:n_c].transpose(0, 2, 1, 3)[:, :, None]
    mod_all = jnp.concatenate([mod_p, mod_s], axis=2)

    x, u = _ln_mod(x_prompt.reshape(tp, d), x_sample.transpose(1, 0, 2).reshape(ts, d),
                   ln_in_g, ln_in_b, mod_all, 0, TM, groups(TM))

    bias_tab = _moba_bias_tables(rel_table)
    past_len = page_table.shape[1] * PAGE_SIZE
    bias_past, bias_own = _moba_sample_bias(rel_table, past_len, n_q)
    cache_kt = cache_k.transpose(0, 1, 3, 4, 2)
    cache_vt = cache_v.transpose(0, 1, 3, 4, 2)
    state_t = state_gdn.transpose(0, 2, 3, 4, 1)
    conv_t = state_conv.transpose(0, 2, 1, 3)

    k_p, v_p, conv_p, gdn_p, k_s, v_s, conv_s, gdn_s = [], [], [], [], [], [], [], []
    for layer in range(depth):
        w_l = w_in[layer]
        w_cat = jnp.concatenate(
            [w_l[:, :D_IN_MAIN], jnp.pad(w_l[:, D_IN_MAIN:], ((0, 0), (0, LANES - 2 * h)))],
            axis=1).astype(BF16)
        proj, qkv = _proj_in(u, w_cat, TM)
        proj_s = proj[tp:].reshape(n_q, bs, -1)

        def slots_s(cols):
            return jnp.pad(proj_s[:, :, cols].transpose(1, 0, 2), ((0, 0), (0, SAMPLE_PAD - n_q), (0, 0)))

        k_sr = proj_s[:, :, D_HEADS:2 * D_HEADS].transpose(1, 0, 2).reshape(bs, n_q, h, hd)
        v_sr = proj_s[:, :, 2 * D_HEADS:3 * D_HEADS].transpose(1, 0, 2).reshape(bs, n_q, h, hd)
        att_p = _moba_prompt(qkv, bias_tab, bp, seq)
        att_s = _moba_sample(slots_s(slice(0, D_HEADS)), slots_s(slice(D_HEADS, 2 * D_HEADS)),
                             slots_s(slice(2 * D_HEADS, 3 * D_HEADS)), cache_kt, cache_vt, page_table,
                             layer, bias_past, bias_own, n_q)
        att_s = att_s.reshape(bs, h, SAMPLE_PAD, hd)[:, :, :n_q].transpose(2, 0, 1, 3)
        att = (att_p, att_s.reshape(ts, D_HEADS).astype(BF16))

        o_p, conv_new_p, s_new_p = _gdn_prompt(proj, bp, seq, GDN_ROWS, conv_w[layer], a_log[layer],
                                               dt_bias[layer], gdn_norm_w[layer])
        o_s, s_new_s = _gdn_sample(proj_s, conv_t[layer], state_t[layer], conv_w[layer], a_log[layer],
                                   dt_bias[layer], gdn_norm_w[layer])
        s_new_s = s_new_s.transpose(3, 0, 1, 2)
        conv_new_s = proj_s[n_q - (GDN_CONV - 1):, :, 3 * D_HEADS:3 * D_HEADS + D_CONV].transpose(1, 0, 2)
        gdn_o = (o_p, o_s.reshape(ts, D_HEADS))

        w_o = w_out[layer].astype(BF16)
        i = layer // 2
        if layer % 2 == 0:
            x, u = _res_ln([att, gdn_o], [w_o[:D_HEADS], w_o[D_HEADS:]], x, mod_all, layer, 2, layer,
                           (4, 3), ln_g[layer, 0], ln_b[layer, 0], alpha, TM, groups(TM))
            hdn = _ffn_up(u, ffn_w_gate[i].astype(BF16), ffn_w_up[i].astype(BF16), TM)
            x, u = _res_ln([hdn], [ffn_w_down[i].astype(BF16)], x, mod_all, layer, 5, layer + 1,
                           (1, 0), ln_g[layer, 1], ln_b[layer, 1], alpha, TM, groups(TM))
        else:
            w_r = jnp.pad(moe_router[i], ((0, 0), (0, LANES - N_EXPERTS)))
            x, u, route = _res_ln([att, gdn_o], [w_o[:D_HEADS], w_o[D_HEADS:]], x, mod_all, layer, 2,
                                  layer, (4, 3), ln_g[layer, 0], ln_b[layer, 0], alpha, TM,
                                  groups(TM), w_router=w_r)
            tile_expert, n_valid, token_of_slot, slot = _moe_routing(route, TM_MOE)
            y = _moe_ffn(u, tile_expert, n_valid, token_of_slot, moe_w_gate[i], moe_w_up[i],
                         moe_w_down[i], TM_MOE, TJ_MOE)
            n_t = (tp + ts) // TM
            slots = slot.reshape(n_t, TM, TOP_K).transpose(0, 2, 1).reshape(n_t, 1, TOP_K * TM)
            x = _moe_combine(y, slots, route, x, mod_all, layer, 5, ln_g[layer, 1], ln_b[layer, 1],
                             alpha, TM, groups(TM))

        k_p.append(proj[:tp, D_HEADS:2 * D_HEADS].reshape(bp, seq, h, hd))
        v_p.append(proj[:tp, 2 * D_HEADS:3 * D_HEADS].reshape(bp, seq, h, hd))
        conv_p.append(conv_new_p)
        gdn_p.append(s_new_p)
        k_s.append(k_sr)
        v_s.append(v_sr)
        conv_s.append(conv_new_s)
        gdn_s.append(s_new_s)

    y_prompt = x[:tp].reshape(bp, seq, d)
    y_sample = x[tp:].reshape(n_q, bs, d).transpose(1, 0, 2)
    return (y_prompt, y_sample, jnp.stack(k_p), jnp.stack(v_p), jnp.stack(conv_p), jnp.stack(gdn_p),
            jnp.stack(k_s), jnp.stack(v_s), jnp.stack(conv_s), jnp.stack(gdn_s))
```

```python
import functools
import math

import numpy as np
import jax
import jax.numpy as jnp
from jax import lax
from jax.experimental import pallas as pl
from jax.experimental.pallas import tpu as pltpu

F32 = jnp.float32
BF16 = jnp.bfloat16

D_MODEL = 1024
HEAD_DIM = 64
N_HEADS = 8
D_HEADS = N_HEADS * HEAD_DIM
D_CONV = 3 * D_HEADS
D_IN = 3 * D_HEADS + D_CONV + D_HEADS + 2 * N_HEADS
D_IN_MAIN = D_IN - 2 * N_HEADS
LANES = 128
D_IN_PAD = D_IN_MAIN + LANES
MOBA_BLOCK = 256
MOBA_TOPK = 3
PAGE_SIZE = 128
GDN_CONV = 4
GDN_CHUNK = 64
N_BUCKETS = 32
MAX_DISTANCE = 128
N_EXPERTS = 8
TOP_K = 2
LN_EPS = 1e-5
NEG = -1e30
MOD_ROWS = 128
VMEM_LIMIT = 56 * 1024 * 1024

HIGHEST = lax.Precision.HIGHEST


def _cparams(n_axes):
    return pltpu.CompilerParams(dimension_semantics=("arbitrary",) * n_axes,
                                vmem_limit_bytes=VMEM_LIMIT)


def _silu(x):
    return x * (1.0 / (1.0 + jnp.exp(-x)))


def _layer_norm(x, g, b):
    mu = jnp.mean(x, axis=-1, keepdims=True)
    xc = x - mu
    var = jnp.mean(xc * xc, axis=-1, keepdims=True)
    return xc * lax.rsqrt(var + LN_EPS) * g + b


def _modulate(x, scale, shift):
    tm, d = x.shape
    x3 = x.reshape(tm // MOD_ROWS, MOD_ROWS, d)
    return (x3 * (1.0 + scale[None]) + shift[None]).reshape(tm, d)


def _gated(x, gate, h):
    tm, d = x.shape
    x3 = x.reshape(tm // MOD_ROWS, MOD_ROWS, d)
    h3 = h.reshape(tm // MOD_ROWS, MOD_ROWS, d)
    return (x3 + (1.0 + gate[None]) * h3).reshape(tm, d)


def _mod_kernel(c_ref, w_ref, b_ref, o_ref):
    a = _silu(c_ref[...])
    o_ref[...] = jnp.dot(a, w_ref[...], precision=HIGHEST, preferred_element_type=F32) + b_ref[...]


def _mod_vectors(c_all, w_mod, b_mod, tn=1536):
    depth, d, n = w_mod.shape
    rows = c_all.shape[0]
    return pl.pallas_call(
        _mod_kernel,
        grid=(depth, n // tn),
        in_specs=[pl.BlockSpec((rows, d), lambda l, j: (0, 0)),
                  pl.BlockSpec((None, d, tn), lambda l, j: (l, 0, j)),
                  pl.BlockSpec((None, 1, tn), lambda l, j: (l, 0, j))],
        out_specs=pl.BlockSpec((None, rows, tn), lambda l, j: (l, 0, j)),
        out_shape=jax.ShapeDtypeStruct((depth, rows, n), F32),
        compiler_params=_cparams(2),
        name="mod_vectors",
    )(c_all, w_mod, b_mod.reshape(depth, 1, n))


def _ln_mod_kernel(xa_ref, xb_ref, g_ref, b_ref, mod_ref, xn_ref, u_ref, *, n_first):
    x = jnp.where(pl.program_id(0) < n_first, xa_ref[...], xb_ref[...])
    xn = _layer_norm(x, g_ref[...], b_ref[...])
    xn_ref[...] = xn
    u_ref[...] = _modulate(xn, mod_ref[1], mod_ref[0]).astype(BF16)


def _mod_spec(layer, group_of_tile):
    return pl.BlockSpec((None, 6, None, MOD_ROWS, D_MODEL),
                        lambda i: (layer, 0, group_of_tile(i), 0, 0))


def _ln_mod(x_first, x_rest, g, b, mod_all, layer, tm, group_of_tile):
    d = x_first.shape[1]
    n_first = x_first.shape[0] // tm
    t = x_first.shape[0] + x_rest.shape[0]
    row = pl.BlockSpec((tm, d), lambda i: (i, 0))
    vec = pl.BlockSpec((1, d), lambda i: (0, 0))
    return pl.pallas_call(
        functools.partial(_ln_mod_kernel, n_first=n_first),
        grid=(t // tm,),
        in_specs=[pl.BlockSpec((tm, d), lambda i: (jnp.minimum(i, n_first - 1), 0)),
                  pl.BlockSpec((tm, d), lambda i: (jnp.maximum(i - n_first, 0), 0)),
                  vec, vec, _mod_spec(layer, group_of_tile)],
        out_specs=[row, row],
        out_shape=[jax.ShapeDtypeStruct((t, d), F32), jax.ShapeDtypeStruct((t, d), BF16)],
        compiler_params=_cparams(1),
        name="ln_mod",
    )(x_first, x_rest, g.reshape(1, d), b.reshape(1, d), mod_all)


def _proj_kernel(u_ref, w_ref, proj_ref, qkv_ref, kt_ref, vt_ref, *, n_first):
    u = u_ref[...]
    n = w_ref.shape[1]
    chunk = D_HEADS
    n_qkv = qkv_ref.shape[1]
    is_prompt = pl.program_id(0) < n_first
    for c0 in range(0, n, chunk):
        c1 = min(c0 + chunk, n)
        r = jnp.dot(u, w_ref[:, c0:c1], preferred_element_type=F32)
        proj_ref[:, c0:c1] = r
        if c1 <= n_qkv:
            qkv_ref[:, c0:c1] = r.astype(BF16)
        for col, t_ref in ((D_HEADS, kt_ref), (2 * D_HEADS, vt_ref)):
            if c0 == col:
                @pl.when(is_prompt)
                def _(r=r, t_ref=t_ref):
                    t_ref[...] = r.T.reshape(t_ref.shape)


def _proj_in(u, w, tm, batch, seq):
    t, d = u.shape
    n = w.shape[1]
    per_seq = seq // tm
    n_first = batch * per_seq

    def t_spec():
        return pl.BlockSpec((None, N_HEADS, HEAD_DIM, tm),
                            lambda i: (jnp.minimum(i, n_first - 1) // per_seq, 0, 0,
                                       jnp.minimum(i, n_first - 1) % per_seq))

    t_shape = jax.ShapeDtypeStruct((batch, N_HEADS, HEAD_DIM, seq), F32)
    return pl.pallas_call(
        functools.partial(_proj_kernel, n_first=n_first),
        grid=(t // tm,),
        in_specs=[pl.BlockSpec((tm, d), lambda i: (i, 0)),
                  pl.BlockSpec((d, n), lambda i: (0, 0))],
        out_specs=[pl.BlockSpec((tm, n), lambda i: (i, 0)),
                   pl.BlockSpec((tm, 3 * D_HEADS), lambda i: (i, 0)),
                   t_spec(), t_spec()],
        out_shape=[jax.ShapeDtypeStruct((t, n), F32),
                   jax.ShapeDtypeStruct((t, 3 * D_HEADS), BF16),
                   t_shape, t_shape],
        compiler_params=_cparams(1),
        name="proj_in",
    )(u, w)


def _top2_gates(logits):
    lane = lax.broadcasted_iota(jnp.int32, logits.shape, 1)
    v1 = jnp.max(logits, axis=-1, keepdims=True)
    i1 = jnp.min(jnp.where(logits == v1, lane, LANES), axis=-1, keepdims=True)
    rest = jnp.where(lane == i1, -jnp.inf, logits)
    v2 = jnp.max(rest, axis=-1, keepdims=True)
    i2 = jnp.min(jnp.where(rest == v2, lane, LANES), axis=-1, keepdims=True)
    e2 = jnp.exp(v2 - v1)
    inv = 1.0 / (1.0 + e2)
    route = jnp.where(lane == 0, i1.astype(F32), 0.0) + jnp.where(lane == 1, i2.astype(F32), 0.0)
    return route + jnp.where(lane == 2, inv, 0.0) + jnp.where(lane == 3, e2 * inv, 0.0)


def _res_ln_kernel(*refs, split, n_first, alpha, next_rows, with_router):
    n_in = len(split)
    a_vals, pos = [], 0
    for two in split:
        if two:
            a_vals.append(jnp.where(pl.program_id(0) < n_first, refs[pos][...], refs[pos + 1][...]))
        else:
            a_vals.append(refs[pos][...])
        pos += 2 if two else 1
    w_refs = refs[pos:pos + n_in]
    x_ref, modg_ref, modn_ref, g_ref, b_ref = refs[pos + n_in:pos + n_in + 5]
    pos += n_in + 5
    if with_router:
        wr_ref = refs[pos]
        pos += 1
    outs = refs[pos:]
    h = jnp.dot(a_vals[0], w_refs[0][...], preferred_element_type=F32)
    for a_val, w_ref in zip(a_vals[1:], w_refs[1:]):
        h = h + jnp.dot(a_val, w_ref[...], preferred_element_type=F32)
    xn = _layer_norm(_gated(alpha * x_ref[...], modg_ref[...], h), g_ref[...], b_ref[...])
    outs[0][...] = xn
    if next_rows is not None:
        u = _modulate(xn, modn_ref[next_rows[0]], modn_ref[next_rows[1]])
        outs[1][...] = u.astype(outs[1].dtype)
        if with_router:
            logits = jnp.dot(u, wr_ref[...], precision=HIGHEST, preferred_element_type=F32)
            lane = lax.broadcasted_iota(jnp.int32, logits.shape, 1)
            outs[2][...] = _top2_gates(jnp.where(lane < N_EXPERTS, logits, -jnp.inf))


def _res_ln(a_list, w_list, x, mod_all, gate_layer, gate_row, next_layer, next_rows, ln_g, ln_b,
            alpha, tm, group_of_tile, w_router=None):
    t, d = x.shape
    row = pl.BlockSpec((tm, d), lambda i: (i, 0))
    vec = pl.BlockSpec((1, d), lambda i: (0, 0))
    split = tuple(isinstance(a, tuple) for a in a_list)
    n_first = a_list[split.index(True)][0].shape[0] // tm if any(split) else 0
    in_specs, flat_a = [], []
    for a in a_list:
        if isinstance(a, tuple):
            in_specs += [pl.BlockSpec((tm, a[0].shape[1]), lambda i: (jnp.minimum(i, n_first - 1), 0)),
                         pl.BlockSpec((tm, a[1].shape[1]), lambda i: (jnp.maximum(i - n_first, 0), 0))]
            flat_a += list(a)
        else:
            in_specs.append(pl.BlockSpec((tm, a.shape[1]), lambda i: (i, 0)))
            flat_a.append(a)
    in_specs += [pl.BlockSpec(w.shape, lambda i: (0, 0)) for w in w_list]
    in_specs += [row,
                 pl.BlockSpec((None, None, None, MOD_ROWS, d),
                              lambda i: (gate_layer, gate_row, group_of_tile(i), 0, 0)),
                 _mod_spec(next_layer if next_rows is not None else gate_layer, group_of_tile),
                 vec, vec]
    args = flat_a + list(w_list) + [x, mod_all, mod_all, ln_g.reshape(1, d), ln_b.reshape(1, d)]
    out_specs = [row]
    out_shape = [jax.ShapeDtypeStruct((t, d), F32)]
    with_router = w_router is not None
    if next_rows is not None:
        out_specs.append(row)
        out_shape.append(jax.ShapeDtypeStruct((t, d), F32 if with_router else BF16))
    if with_router:
        in_specs.append(pl.BlockSpec(w_router.shape, lambda i: (0, 0)))
        args.append(w_router)
        out_specs.append(pl.BlockSpec((tm, LANES), lambda i: (i, 0)))
        out_shape.append(jax.ShapeDtypeStruct((t, LANES), F32))
    return pl.pallas_call(
        functools.partial(_res_ln_kernel, split=split, n_first=n_first, alpha=alpha,
                          next_rows=next_rows, with_router=with_router),
        grid=(t // tm,),
        in_specs=in_specs,
        out_specs=out_specs,
        out_shape=out_shape,
        compiler_params=_cparams(1),
        name="res_ln",
    )(*args)


def _ffn_up_kernel(u_ref, wg_ref, wu_ref, o_ref, *, chunk):
    u = u_ref[...]
    n = o_ref.shape[1]
    for c0 in range(0, n, chunk):
        c1 = min(c0 + chunk, n)
        hg = jnp.dot(u, wg_ref[:, c0:c1], preferred_element_type=F32)
        hu = jnp.dot(u, wu_ref[:, c0:c1], preferred_element_type=F32)
        o_ref[:, c0:c1] = (_silu(hg) * hu).astype(BF16)


def _ffn_up(u, wg, wu, tm, chunk=256):
    t, d = u.shape
    n = wg.shape[1]
    wspec = pl.BlockSpec((d, n), lambda i: (0, 0))
    return pl.pallas_call(
        functools.partial(_ffn_up_kernel, chunk=chunk),
        grid=(t // tm,),
        in_specs=[pl.BlockSpec((tm, d), lambda i: (i, 0)), wspec, wspec],
        out_specs=pl.BlockSpec((tm, n), lambda i: (i, 0)),
        out_shape=jax.ShapeDtypeStruct((t, n), BF16),
        compiler_params=_cparams(1),
        name="ffn_up",
    )(u, wg, wu)


def _row_copy(src_hbm, src_row, dst_ref, r, sem):
    return pltpu.make_async_copy(src_hbm.at[pl.ds(src_row, 1)], dst_ref.at[pl.ds(r, 1)], sem)


def _gather_start(idx_ref, idx_base, src_hbm, dst_ref, sem, n_rows):
    def body(r, carry):
        _row_copy(src_hbm, idx_ref[0, idx_base + r], dst_ref, r, sem).start()
        return carry

    lax.fori_loop(0, n_rows, body, 0, unroll=8)


def _gather_wait(src_hbm, dst_ref, sem):
    pltpu.make_async_copy(src_hbm.at[pl.ds(0, dst_ref.shape[0])], dst_ref, sem).wait()


def _moe_ffn_kernel(te_ref, nv_ref, tos_ref, u_hbm, wg_ref, wu_ref, wd_ref, y_ref,
                    xg_ref, xb_ref, acc_ref, sem):
    del te_ref
    i = pl.program_id(0)
    j = pl.program_id(1)
    valid = i < nv_ref[0]

    @pl.when(valid & (j == 0))
    def _():
        _gather_start(tos_ref, 0, u_hbm, xg_ref, sem, xg_ref.shape[0])
        _gather_wait(u_hbm, xg_ref, sem)
        xb_ref[...] = xg_ref[...].astype(BF16)
        acc_ref[...] = jnp.zeros_like(acc_ref)

    @pl.when(valid)
    def _():
        x = xb_ref[...]
        hg = jnp.dot(x, wg_ref[...].astype(BF16), preferred_element_type=F32)
        hu = jnp.dot(x, wu_ref[...].astype(BF16), preferred_element_type=F32)
        acc_ref[...] += jnp.dot((_silu(hg) * hu).astype(BF16), wd_ref[...].astype(BF16),
                                preferred_element_type=F32)

    last = j == pl.num_programs(1) - 1

    @pl.when(valid & last)
    def _():
        y_ref[...] = acc_ref[...]

    @pl.when(jnp.logical_not(valid) & last)
    def _():
        y_ref[...] = jnp.zeros_like(y_ref)


def _moe_ffn(u, tile_expert, n_valid, token_of_slot, wg, wu, wd, tm, tj):
    n_tiles = tile_expert.shape[0]
    d = u.shape[1]
    f = wg.shape[2]
    assert f % tj == 0
    nj = f // tj

    def jj(i, j, nv):
        return jnp.where(i < nv[0], j, nj - 1)

    grid_spec = pltpu.PrefetchScalarGridSpec(
        num_scalar_prefetch=2,
        grid=(n_tiles, nj),
        in_specs=[pl.BlockSpec((None, 1, tm), lambda i, j, te, nv: (i, 0, 0), memory_space=pltpu.SMEM),
                  pl.BlockSpec(memory_space=pl.ANY),
                  pl.BlockSpec((None, d, tj), lambda i, j, te, nv: (te[i], 0, jj(i, j, nv))),
                  pl.BlockSpec((None, d, tj), lambda i, j, te, nv: (te[i], 0, jj(i, j, nv))),
                  pl.BlockSpec((None, tj, d), lambda i, j, te, nv: (te[i], jj(i, j, nv), 0))],
        out_specs=pl.BlockSpec((tm, d), lambda i, j, te, nv: (i, 0)),
        scratch_shapes=[pltpu.VMEM((tm, d), F32), pltpu.VMEM((tm, d), BF16),
                        pltpu.VMEM((tm, d), F32), pltpu.SemaphoreType.DMA(())])
    return pl.pallas_call(
        _moe_ffn_kernel,
        grid_spec=grid_spec,
        out_shape=jax.ShapeDtypeStruct((n_tiles * tm, d), F32),
        compiler_params=_cparams(2),
        name="moe_ffn",
    )(tile_expert, n_valid, token_of_slot, u, wg, wu, wd)


def _moe_combine_kernel(slots_ref, y_hbm, route_ref, x_ref, modg_ref, g_ref, b_ref, oa_ref, ob_ref,
                        buf_ref, sem, *, alpha, n_first):
    rows = x_ref.shape[0]
    for k in range(TOP_K):
        _gather_start(slots_ref, k * rows, y_hbm, buf_ref.at[k], sem, rows)
    for k in range(TOP_K):
        _gather_wait(y_hbm, buf_ref.at[k], sem)
    route = route_ref[...]
    f = route[:, 2:3] * buf_ref[0] + route[:, 3:4] * buf_ref[1]
    o = _layer_norm(_gated(alpha * x_ref[...], modg_ref[...], f), g_ref[...], b_ref[...])

    @pl.when(pl.program_id(0) < n_first)
    def _():
        oa_ref[...] = o

    @pl.when(pl.program_id(0) >= n_first)
    def _():
        ob_ref[...] = o


def _moe_combine(y, slots, route, x, mod_all, gate_layer, gate_row, ln_g, ln_b, alpha, tm,
                 group_of_tile, n_first):
    t, d = x.shape
    row = pl.BlockSpec((tm, d), lambda i: (i, 0))
    vec = pl.BlockSpec((1, d), lambda i: (0, 0))
    return pl.pallas_call(
        functools.partial(_moe_combine_kernel, alpha=alpha, n_first=n_first),
        grid=(t // tm,),
        in_specs=[pl.BlockSpec((None, 1, 2 * tm), lambda i: (i, 0, 0), memory_space=pltpu.SMEM),
                  pl.BlockSpec(memory_space=pl.ANY),
                  pl.BlockSpec((tm, LANES), lambda i: (i, 0)),
                  row,
                  pl.BlockSpec((None, None, None, MOD_ROWS, d),
                               lambda i: (gate_layer, gate_row, group_of_tile(i), 0, 0)),
                  vec, vec],
        out_specs=[pl.BlockSpec((tm, d), lambda i: (jnp.minimum(i, n_first - 1), 0)),
                   pl.BlockSpec((tm, d), lambda i: (jnp.maximum(i - n_first, 0), 0))],
        out_shape=[jax.ShapeDtypeStruct((n_first * tm, d), F32),
                   jax.ShapeDtypeStruct((t - n_first * tm, d), F32)],
        scratch_shapes=[pltpu.VMEM((2, tm, d), F32), pltpu.SemaphoreType.DMA(())],
        compiler_params=_cparams(1),
        name="moe_combine",
    )(slots, y, route, x, mod_all, ln_g.reshape(1, d), ln_b.reshape(1, d))


def _moe_routing(route, tm):
    t = route.shape[0]
    n_assign = TOP_K * t
    n_tiles = -(-n_assign // tm) + N_EXPERTS
    experts = route[:, :TOP_K].astype(jnp.int32).reshape(n_assign)
    onehot = (experts[:, None] == jnp.arange(N_EXPERTS, dtype=jnp.int32)[None, :]).astype(jnp.int32)
    rank = jnp.sum((jnp.cumsum(onehot, axis=0) - onehot) * onehot, axis=1)
    count = jnp.sum(onehot, axis=0)
    padded = (count + tm - 1) // tm * tm
    ends = jnp.cumsum(padded)
    slot = jnp.sum(onehot * (ends - padded)[None, :], axis=1) + rank
    n_valid = ends[-1] // tm
    starts = jnp.arange(n_tiles, dtype=jnp.int32) * tm
    tile_expert = jnp.minimum(jnp.sum((starts[:, None] >= ends[None, :]).astype(jnp.int32), axis=1),
                              N_EXPERTS - 1)
    tile_expert = jnp.where(starts < ends[-1], tile_expert, tile_expert[jnp.maximum(n_valid - 1, 0)])
    token_of_slot = jnp.zeros((n_tiles * tm,), jnp.int32).at[slot].set(
        jnp.arange(n_assign, dtype=jnp.int32) // TOP_K)
    return (tile_expert.astype(jnp.int32), n_valid.reshape(1).astype(jnp.int32),
            token_of_slot.reshape(n_tiles, 1, tm), slot.reshape(t, TOP_K).astype(jnp.int32))


def _rel_bucket_table(max_dist):
    n = np.arange(max_dist + 1)
    max_exact = N_BUCKETS // 2
    ratio = np.log(np.maximum(n, 1).astype(np.float32) / np.float32(max_exact))
    large = max_exact + (ratio / np.float32(math.log(MAX_DISTANCE / max_exact))
                         * np.float32(N_BUCKETS - max_exact)).astype(np.int32)
    large = np.minimum(large, N_BUCKETS - 1)
    return np.where(n < max_exact, n, large).astype(np.int32)


def _bias_lookup(rel_table, bucket_idx):
    onehot = jnp.asarray(np.eye(N_BUCKETS, dtype=np.float32)[bucket_idx.reshape(-1)])
    vals = jnp.dot(onehot, rel_table, precision=HIGHEST)
    return vals.T.reshape((rel_table.shape[1],) + bucket_idx.shape)


def _select_top_blocks(gate, n_valid, ksel):
    blk = lax.broadcasted_iota(jnp.int32, gate.shape, 0)
    nb = gate.shape[0]
    g = jnp.where(blk < n_valid, gate, -jnp.inf)
    keep = blk == n_valid
    for _ in range(ksel):
        mx = jnp.max(g, axis=0, keepdims=True)
        first = jnp.min(jnp.where(g == mx, blk, nb), axis=0, keepdims=True)
        hit = blk == first
        keep = keep | (hit & (blk < n_valid))
        g = jnp.where(hit, -jnp.inf, g)
    return jnp.where(keep, 0.0, NEG)


def _moba_prompt_kernel(q_ref, k_ref, v_ref, bias_ref, o_ref,
                        kmean_ref, vt_ref, qm_ref, sel_ref, m_ref, l_ref, acc_ref, *, ksel):
    qt = pl.program_id(1)
    seq = k_ref.shape[0]
    nb = seq // MOBA_BLOCK
    gw = 4 * HEAD_DIM
    n_groups = N_HEADS // 4
    nt = (((1,), (1,)), ((), ()))

    @pl.when(qt == 0)
    def _():
        row = lax.broadcasted_iota(jnp.int32, (nb, seq), 0)
        col = lax.broadcasted_iota(jnp.int32, (nb, seq), 1)
        pool = jnp.where(col // MOBA_BLOCK == row, 1.0, 0.0).astype(BF16)
        kmean_ref[...] = jnp.dot(pool, k_ref[...], preferred_element_type=F32) * (1.0 / MOBA_BLOCK)
        for n in range(nb):
            vt_ref[n] = v_ref[n * MOBA_BLOCK:(n + 1) * MOBA_BLOCK, :].astype(F32).T.astype(BF16)

    head_of_lane = lax.broadcasted_iota(jnp.int32, (1, gw), 1) // HEAD_DIM
    q = q_ref[...] * (HEAD_DIM ** -0.5)
    kmean = kmean_ref[...].astype(BF16)
    for h in range(N_HEADS):
        g, hh = divmod(h, 4)
        qm = jnp.where(head_of_lane == hh, q[:, g * gw:(g + 1) * gw], 0.0).astype(BF16)
        qm_ref[h] = qm
        gate = lax.dot_general(kmean[:, g * gw:(g + 1) * gw], qm, nt, preferred_element_type=F32)
        sel_ref[h] = _select_top_blocks(gate, qt, ksel)
        m_ref[h] = jnp.full(m_ref.shape[1:], NEG, F32)
        l_ref[h] = jnp.zeros(l_ref.shape[1:], F32)
    acc_ref[...] = jnp.zeros_like(acc_ref)

    def body(step, carry, near):
        n = qt - step
        start = pl.multiple_of(n * MOBA_BLOCK, MOBA_BLOCK)
        kb = k_ref[pl.ds(start, MOBA_BLOCK), :]
        vt = vt_ref[n]
        scores = [lax.dot_general(kb[:, (h // 4) * gw:(h // 4 + 1) * gw], qm_ref[h], nt,
                                  preferred_element_type=F32) for h in range(N_HEADS)]
        probs, alphas = [], []
        for h in range(N_HEADS):
            shift = sel_ref[h, pl.ds(n, 1), :]
            if near:
                s = scores[h] + bias_ref[h, step]
            else:
                s = scores[h]
                shift = shift + bias_ref[h, 2, 0:1, :]
            m_old = m_ref[h]
            m_new = jnp.maximum(m_old, jnp.max(s, axis=0, keepdims=True) + shift)
            alpha = jnp.exp(m_old - m_new)
            p = jnp.exp(s - (m_new - shift))
            l_ref[h] = alpha * l_ref[h] + jnp.sum(p, axis=0, keepdims=True)
            m_ref[h] = m_new
            probs.append(p.astype(BF16))
            alphas.append(alpha)
        outs = [jnp.dot(vt[h * HEAD_DIM:(h + 1) * HEAD_DIM, :], probs[h],
                        preferred_element_type=F32) for h in range(N_HEADS)]
        for h in range(N_HEADS):
            g, hh = divmod(h, 4)
            rows = slice(hh * HEAD_DIM, (hh + 1) * HEAD_DIM)
            acc_ref[g, rows, :] = alphas[h] * acc_ref[g, rows, :] + outs[h]
        return carry

    lax.fori_loop(0, jnp.minimum(qt + 1, 2), functools.partial(body, near=True), 0)
    lax.fori_loop(2, qt + 1, functools.partial(body, near=False), 0)

    for g in range(n_groups):
        o_t = jnp.concatenate(
            [acc_ref[g, hh * HEAD_DIM:(hh + 1) * HEAD_DIM, :] * (1.0 / l_ref[4 * g + hh])
             for hh in range(4)], axis=0)
        o_ref[:, g * gw:(g + 1) * gw] = o_t.T.astype(o_ref.dtype)


def _moba_bias_tables(rel_table):
    bucket = _rel_bucket_table(2 * MOBA_BLOCK)
    i = np.arange(MOBA_BLOCK)[None, :]
    j = np.arange(MOBA_BLOCK)[:, None]
    own = jnp.where(jnp.asarray(i >= j), _bias_lookup(rel_table, bucket[np.maximum(i - j, 0)]), NEG)
    prev = _bias_lookup(rel_table, bucket[MOBA_BLOCK + i - j])
    far = jnp.broadcast_to(rel_table[N_BUCKETS - 1][:, None, None], prev.shape)
    return jnp.stack([own, prev, far], axis=1).astype(F32)


def _moba_prompt(qkv, bias_tab, batch, seq):
    nb = seq // MOBA_BLOCK
    ksel = min(MOBA_TOPK, nb)
    stat = pltpu.VMEM((N_HEADS, 1, MOBA_BLOCK), F32)
    return pl.pallas_call(
        functools.partial(_moba_prompt_kernel, ksel=ksel),
        grid=(batch, nb),
        in_specs=[pl.BlockSpec((MOBA_BLOCK, D_HEADS), lambda b, t: (b * nb + t, 0)),
                  pl.BlockSpec((seq, D_HEADS), lambda b, t: (b, 1)),
                  pl.BlockSpec((seq, D_HEADS), lambda b, t: (b, 2)),
                  pl.BlockSpec(bias_tab.shape, lambda b, t: (0, 0, 0, 0))],
        out_specs=pl.BlockSpec((MOBA_BLOCK, D_HEADS), lambda b, t: (b * nb + t, 0)),
        out_shape=jax.ShapeDtypeStruct((batch * seq, D_HEADS), BF16),
        scratch_shapes=[pltpu.VMEM((nb, D_HEADS), F32),
                        pltpu.VMEM((nb, D_HEADS, MOBA_BLOCK), BF16),
                        pltpu.VMEM((N_HEADS, MOBA_BLOCK, 4 * HEAD_DIM), BF16),
                        pltpu.VMEM((N_HEADS, nb, MOBA_BLOCK), F32),
                        stat, stat,
                        pltpu.VMEM((N_HEADS // 4, 4 * HEAD_DIM, MOBA_BLOCK), F32)],
        compiler_params=_cparams(2),
        name="moba_prompt",
    )(qkv, qkv, qkv, bias_tab)


def _moba_sample_kernel(pt_ref, q_ref, kn_ref, vn_ref, *refs, ksel, n_q, nbp, pages_per_block):
    del pt_ref
    n_pages = nbp * pages_per_block
    k_refs, v_refs = refs[:n_pages], refs[n_pages:2 * n_pages]
    bias_ref, bown_ref, o_ref, m_ref, l_ref, g_ref, acc_ref = refs[2 * n_pages:]
    nt = (((1,), (1,)), ((), ()))
    dh = N_HEADS * HEAD_DIM
    q8 = q_ref[...]
    slots = q8.shape[0]
    lane_head = lax.broadcasted_iota(jnp.int32, (1, dh), 1) // HEAD_DIM
    qbd = jnp.concatenate([jnp.where(lane_head == hp, q8, 0.0) for hp in range(N_HEADS)], axis=0)
    qs = (qbd * (HEAD_DIM ** -0.5)).astype(BF16)
    ones = jnp.ones((8, PAGE_SIZE), BF16)

    kts = [k_ref[...].reshape(dh, PAGE_SIZE).astype(BF16) for k_ref in k_refs]
    s_pages = [jnp.dot(qs, kt, preferred_element_type=F32) for kt in kts]
    ksums = [lax.dot_general(ones, kt, nt, preferred_element_type=F32) for kt in kts]
    probs = []
    for j in range(nbp):
        pages = range(j * pages_per_block, (j + 1) * pages_per_block)
        s = jnp.concatenate([s_pages[pg] for pg in pages], axis=1) + bias_ref[0 if j == nbp - 1 else 1]
        m = jnp.max(s, axis=-1, keepdims=True)
        p32 = jnp.exp(s - m)
        probs.append(p32.astype(BF16))
        m_ref[j] = m
        l_ref[j] = jnp.sum(p32, axis=-1, keepdims=True)
        ksum = functools.reduce(jnp.add, [ksums[pg] for pg in pages])
        g_ref[j] = jnp.sum(qbd * ksum[0:1, :], axis=-1, keepdims=True) * (1.0 / MOBA_BLOCK)
    pv = [lax.dot_general(probs[pg // pages_per_block][:, (pg % pages_per_block) * PAGE_SIZE:
                                                       (pg % pages_per_block + 1) * PAGE_SIZE],
                          v_refs[pg][...].reshape(dh, PAGE_SIZE).astype(BF16), nt,
                          preferred_element_type=F32) for pg in range(n_pages)]
    for j in range(nbp):
        acc_ref[j] = functools.reduce(jnp.add, pv[j * pages_per_block:(j + 1) * pages_per_block])

    s_own = [jnp.sum(qbd * kn_ref[i:i + 1, :], axis=-1, keepdims=True) * (HEAD_DIM ** -0.5)
             + bown_ref[:, i:i + 1] for i in range(n_q)]
    m_o = functools.reduce(jnp.maximum, s_own)
    p_own = [jnp.exp(s_i - m_o) for s_i in s_own]
    l_o = functools.reduce(jnp.add, p_own)
    acc_o = functools.reduce(jnp.add, [p_i * vn_ref[i:i + 1, :] for i, p_i in enumerate(p_own)])

    gg = g_ref[...]
    blk = lax.broadcasted_iota(jnp.int32, gg.shape, 0)
    keep = blk < 0
    for _ in range(ksel):
        mx = jnp.max(gg, axis=0, keepdims=True)
        first = jnp.min(jnp.where(gg == mx, blk, nbp), axis=0, keepdims=True)
        hit = (blk == first) & (gg > -jnp.inf)
        keep = keep | hit
        gg = jnp.where(blk == first, -jnp.inf, gg)
    m_all = m_ref[...]
    m_tot = jnp.maximum(jnp.max(jnp.where(keep, m_all, NEG), axis=0), m_o)
    w = jnp.where(keep, jnp.exp(m_all - m_tot[None]), 0.0)
    w_o = jnp.exp(m_o - m_tot)
    l_tot = jnp.sum(w * l_ref[...], axis=0) + w_o * l_o
    acc_tot = (jnp.sum(w * acc_ref[...], axis=0) + w_o * acc_o) * (1.0 / l_tot)
    row_head = lax.broadcasted_iota(jnp.int32, (N_HEADS * slots, 1), 0) // slots
    acc_tot = jnp.where(row_head == lane_head, acc_tot, 0.0)
    o_ref[...] = functools.reduce(
        jnp.add, [acc_tot[:, hp * HEAD_DIM:(hp + 1) * HEAD_DIM] for hp in range(N_HEADS)])


def _moba_sample_bias(rel_table, past_len, n_q):
    del past_len
    slots = SAMPLE_PAD
    bucket = _rel_bucket_table(MOBA_BLOCK + slots)
    rows_h = np.repeat(np.arange(N_HEADS), slots)
    rows_i = np.minimum(np.tile(np.arange(slots), N_HEADS), n_q - 1)
    pos = np.arange(MOBA_BLOCK)[None, :]
    by_head = _bias_lookup(rel_table, bucket[MOBA_BLOCK + rows_i[:, None] - pos])
    pick = jnp.asarray(np.arange(N_HEADS)[:, None, None] == rows_h[None, :, None])
    near = jnp.sum(jnp.where(pick, by_head, 0.0), axis=0)
    far = jnp.broadcast_to(jnp.repeat(rel_table[N_BUCKETS - 1], slots)[:, None], near.shape)
    past = jnp.stack([near, far]).astype(F32)
    new = np.arange(slots)[None, :]
    own_h = _bias_lookup(rel_table, bucket[np.maximum(rows_i[:, None] - new, 0)])
    own = jnp.sum(jnp.where(pick, own_h, 0.0), axis=0)
    ok = jnp.asarray((new <= rows_i[:, None]) & (new < n_q))
    return past, jnp.where(ok, own, NEG).astype(F32)


def _moba_sample(q_rows, k_new, v_new, cache_k, cache_v, page_table, layer, bias_past, bias_own, n_q):
    bsz, slots, dh = q_rows.shape
    hd = HEAD_DIM
    rows = N_HEADS * slots
    pages_per_block = MOBA_BLOCK // PAGE_SIZE
    nbp = page_table.shape[1] // pages_per_block
    ksel = min(MOBA_TOPK, nbp + 1)

    n_pages = nbp * pages_per_block

    def page_spec(page):
        return pl.BlockSpec((None, None, N_HEADS, hd, PAGE_SIZE),
                            lambda b, pt: (layer, pt[b, page], 0, 0, 0))

    per_seq = lambda b, pt: (b, 0, 0)
    stat = pltpu.VMEM((nbp, rows, 1), F32)
    grid_spec = pltpu.PrefetchScalarGridSpec(
        num_scalar_prefetch=1,
        grid=(bsz,),
        in_specs=([pl.BlockSpec((None, slots, dh), per_seq)] * 3
                  + [page_spec(page) for page in range(n_pages)] * 2
                  + [pl.BlockSpec(bias_past.shape, lambda b, pt: (0, 0, 0)),
                     pl.BlockSpec(bias_own.shape, lambda b, pt: (0, 0))]),
        out_specs=pl.BlockSpec((None, rows, hd), per_seq),
        scratch_shapes=[stat, stat, stat, pltpu.VMEM((nbp, rows, dh), F32)])
    return pl.pallas_call(
        functools.partial(_moba_sample_kernel, ksel=ksel, n_q=n_q, nbp=nbp,
                          pages_per_block=pages_per_block),
        grid_spec=grid_spec,
        out_shape=jax.ShapeDtypeStruct((bsz, rows, hd), F32),
        compiler_params=_cparams(1),
        name="moba_sample",
    )(page_table, q_rows, k_new, v_new, *([cache_k] * n_pages), *([cache_v] * n_pages),
      bias_past, bias_own)


def _softplus(x):
    return jnp.maximum(x, 0.0) + jnp.log1p(jnp.exp(-jnp.abs(x)))


def _bmm(a, b):
    return jnp.matmul(a.astype(BF16), b.astype(BF16), preferred_element_type=F32)


def _dot_f32(a, b):
    return jnp.dot(a, b, precision=HIGHEST, preferred_element_type=F32)


def _split_bf16(a):
    hi = a.astype(BF16)
    return hi, (a - hi.astype(F32)).astype(BF16)


def _bmm_3pass(a, b):
    a_hi, a_lo = _split_bf16(a)
    b_hi, b_lo = _split_bf16(b)
    return (jnp.matmul(a_hi, b_hi, preferred_element_type=F32)
            + jnp.matmul(a_hi, b_lo, preferred_element_type=F32)
            + jnp.matmul(a_lo, b_hi, preferred_element_type=F32))


def _unit_lower_inverse(a, row, col):
    c = a.shape[1]
    eye = jnp.where(row == col, 1.0, 0.0)
    in16 = (row // 16) == (col // 16)
    in32 = (row // 32) == (col // 32)
    nil = jnp.where(in16, -a, 0.0)
    x = eye + nil
    p = _bmm_3pass(nil, nil)
    for _ in range(2):
        xp = _bmm_3pass(jnp.concatenate([x, p], axis=1), p)
        x = x + xp[:, :c]
        p = xp[:, c:]
    x = x + _bmm_3pass(x, p)
    for off in (jnp.where(in32 & jnp.logical_not(in16), a, 0.0),
                jnp.where(in32, 0.0, a)):
        x = x - _bmm(x, _bmm(off, x))
    return x


def _gdn_prepare(q, k, v, k_t, g_col, g_row, beta, row, col):
    c = q.shape[1]
    incl = row >= col
    decay = jnp.where(incl, jnp.exp(jnp.where(incl, g_col - g_row, 0.0)), 0.0)
    qk_kk = _bmm(jnp.concatenate([q, k], axis=1), k_t)
    a_mat = jnp.where(row > col, qk_kk[:, c:] * decay * beta, 0.0)
    e_g = jnp.exp(g_col)
    rhs = jnp.concatenate([v * beta, k * beta * e_g], axis=2)
    sol = _bmm(_unit_lower_inverse(a_mat, row, col), rhs)
    dv = v.shape[2]
    g_last = g_col[:, c - 1:c, :]
    w_qd = jnp.concatenate([sol[:, :, dv:], q * e_g], axis=1).astype(BF16)
    qk = jnp.where(incl, qk_kk[:, :c] * decay, 0.0).astype(BF16)
    k_dec_t = (k_t * jnp.exp(g_last - g_row)).astype(BF16)
    return sol[:, :, :dv], w_qd, qk, k_dec_t, jnp.exp(g_last)


def _gdn_step(u, w_qd, qk, k_dec_t, chunk_decay, state):
    c = u.shape[1]
    ws_qs = jnp.matmul(w_qd, state.astype(BF16), preferred_element_type=F32)
    v_new = (u - ws_qs[:, :c]).astype(BF16)
    o = ws_qs[:, c:] + jnp.matmul(qk, v_new, preferred_element_type=F32)
    new_state = state * chunk_decay + jnp.matmul(k_dec_t, v_new, preferred_element_type=F32)
    return o, new_state


def _gdn_prompt_kernel(x_ref, z_ref, gab_ref, cw_ref, alog_ref, dtb_ref, nw_ref, seg_ref,
                       o_ref, conv_ref, sout_ref,
                       xbuf_ref, halo_ref, s_ref, u_ref, wqd_ref, qk_ref, kdt_ref, cd_ref, oh_ref):
    t = pl.program_id(1)
    tl = x_ref.shape[0]
    cs = GDN_CHUNK
    n_chunks = tl // cs
    hd = HEAD_DIM

    @pl.when(t == 0)
    def _():
        halo_ref[...] = jnp.zeros_like(halo_ref)
        s_ref[...] = jnp.zeros_like(s_ref)

    x = x_ref[...]
    xbuf_ref[0:8, :] = halo_ref[...]
    xbuf_ref[8:8 + tl, :] = x
    y = cw_ref[0:1, :] * xbuf_ref[5:5 + tl, :]
    for i in range(1, GDN_CONV):
        y = y + cw_ref[i:i + 1, :] * xbuf_ref[5 + i:5 + i + tl, :]
    y = _silu(y)
    tail = x[tl - 8:, :]
    halo_ref[...] = tail
    conv_ref[...] = tail[8 - (GDN_CONV - 1):, :]

    dh = N_HEADS * hd
    seg = seg_ref[...]

    def l2n(a):
        sq_hi, sq_lo = _split_bf16(a * a)
        ss = (jnp.dot(sq_hi, seg, preferred_element_type=F32)
              + jnp.dot(sq_lo, seg, preferred_element_type=F32))
        return a * lax.rsqrt(ss + 1e-6)

    qn = l2n(y[:, :dh]) * (hd ** -0.5)
    kn = l2n(y[:, dh:2 * dh])
    vv = y[:, 2 * dh:]
    kn_t = kn.T

    gab = gab_ref[...]
    beta = 1.0 / (1.0 + jnp.exp(-gab))
    glog = -jnp.exp(alog_ref[...]) * _softplus(gab + dtb_ref[...])

    row = lax.broadcasted_iota(jnp.int32, (cs, cs), 0)
    col = lax.broadcasted_iota(jnp.int32, (cs, cs), 1)
    tril = jnp.where(row >= col, 1.0, 0.0)
    pick = jnp.where(lax.broadcasted_iota(jnp.int32, (N_HEADS, LANES), 1)
                     == lax.broadcasted_iota(jnp.int32, (N_HEADS, LANES), 0) + N_HEADS, 1.0, 0.0)
    parts = {name: [] for name in ("q", "k", "v", "kt", "gc", "gr", "bb")}
    for c in range(n_chunks):
        rows = slice(c * cs, (c + 1) * cs)
        gcum = _dot_f32(tril, glog[rows])
        g_rows = lax.dot_general(pick, gcum, (((1,), (1,)), ((), ())), precision=HIGHEST,
                                 preferred_element_type=F32)
        for h in range(N_HEADS):
            lanes = slice(h * hd, (h + 1) * hd)
            parts["q"].append(qn[rows, lanes])
            parts["k"].append(kn[rows, lanes])
            parts["v"].append(vv[rows, lanes])
            parts["kt"].append(kn_t[lanes, rows])
            parts["gc"].append(jnp.broadcast_to(gcum[:, N_HEADS + h:N_HEADS + h + 1], (cs, cs)))
            parts["gr"].append(g_rows[h:h + 1, :])
            parts["bb"].append(jnp.broadcast_to(beta[rows, h:h + 1], (cs, cs)))
    st = {name: jnp.stack(vals) for name, vals in parts.items()}
    u_all, w_qd, qk, k_dec_t, chunk_decay = _gdn_prepare(
        st["q"], st["k"], st["v"], st["kt"], st["gc"], st["gr"], st["bb"], row, col)
    u_ref[...] = u_all.reshape(u_ref.shape)
    wqd_ref[...] = w_qd.reshape(wqd_ref.shape)
    qk_ref[...] = qk.reshape(qk_ref.shape)
    kdt_ref[...] = k_dec_t.reshape(kdt_ref.shape)
    cd_ref[...] = chunk_decay.reshape(cd_ref.shape)

    def chunk_body(c, carry):
        o, s_new = _gdn_step(u_ref[c], wqd_ref[c], qk_ref[c], kdt_ref[c], cd_ref[c], s_ref[...])
        oh_ref[c] = o
        s_ref[...] = s_new
        return carry

    lax.fori_loop(0, n_chunks, chunk_body, 0)

    z = z_ref[...]
    nw = nw_ref[...]
    for h in range(N_HEADS):
        lanes = slice(h * hd, (h + 1) * hd)
        o = jnp.concatenate([oh_ref[c, h] for c in range(n_chunks)], axis=0)
        o = o * lax.rsqrt(jnp.mean(o * o, axis=-1, keepdims=True) + 1e-6) * nw
        o_ref[:, lanes] = (o * _silu(z[:, lanes])).astype(o_ref.dtype)

    @pl.when(t == pl.num_programs(1) - 1)
    def _():
        sout_ref[...] = s_ref[...]


def _gdn_prompt(proj, batch, seq, tl, conv_w, a_log, dt_bias, norm_w):
    hd = HEAD_DIM
    cs = GDN_CHUNK
    n_steps = seq // tl
    n_chunks = tl // cs
    col_x, col_z, col_gab = 1, D_IN_MAIN // D_HEADS - 1, D_IN_MAIN // LANES
    lane = np.arange(D_HEADS)
    seg = jnp.asarray((lane[:, None] // hd == lane[None, :] // hd).astype(np.float32)).astype(BF16)
    alog_row = jnp.zeros((1, LANES), F32).at[0, N_HEADS:2 * N_HEADS].set(a_log)
    dtb_row = jnp.zeros((1, LANES), F32).at[0, N_HEADS:2 * N_HEADS].set(dt_bias)
    const2 = lambda b, t: (0, 0)
    per_inst = pltpu.VMEM((n_chunks, N_HEADS, cs, hd), F32)
    per_inst_bf16 = pltpu.VMEM((n_chunks, N_HEADS, hd, cs), BF16)
    return pl.pallas_call(
        _gdn_prompt_kernel,
        grid=(batch, n_steps),
        in_specs=[pl.BlockSpec((tl, D_CONV), lambda b, t: (b * n_steps + t, col_x)),
                  pl.BlockSpec((tl, D_HEADS), lambda b, t: (b * n_steps + t, col_z)),
                  pl.BlockSpec((tl, LANES), lambda b, t: (b * n_steps + t, col_gab)),
                  pl.BlockSpec((GDN_CONV, D_CONV), const2),
                  pl.BlockSpec((1, LANES), const2),
                  pl.BlockSpec((1, LANES), const2),
                  pl.BlockSpec((1, hd), const2),
                  pl.BlockSpec((D_HEADS, D_HEADS), const2)],
        out_specs=[pl.BlockSpec((tl, D_HEADS), lambda b, t: (b * n_steps + t, 0)),
                   pl.BlockSpec((None, GDN_CONV - 1, D_CONV), lambda b, t: (b, 0, 0)),
                   pl.BlockSpec((None, N_HEADS, hd, hd), lambda b, t: (b, 0, 0, 0))],
        out_shape=[jax.ShapeDtypeStruct((batch * seq, D_HEADS), BF16),
                   jax.ShapeDtypeStruct((batch, GDN_CONV - 1, D_CONV), F32),
                   jax.ShapeDtypeStruct((batch, N_HEADS, hd, hd), F32)],
        scratch_shapes=[pltpu.VMEM((8 + tl + 8, D_CONV), F32),
                        pltpu.VMEM((8, D_CONV), F32),
                        pltpu.VMEM((N_HEADS, hd, hd), F32),
                        per_inst,
                        pltpu.VMEM((n_chunks, N_HEADS, 2 * cs, hd), BF16),
                        per_inst_bf16, per_inst_bf16,
                        pltpu.VMEM((n_chunks, N_HEADS, 1, hd), F32),
                        per_inst],
        compiler_params=_cparams(2),
        name="gdn_prompt",
    )(proj, proj, proj, conv_w, alog_row, dtb_row, norm_w.reshape(1, hd), seg)


def _gdn_sample_kernel(xq_ref, xk_ref, xv_ref, cq_ref, ck_ref, cv_ref, wq_ref, wk_ref, wv_ref,
                       z_ref, gab_ref, alog_ref, dtb_ref, nw_ref, s_ref,
                       o_ref, sout_ref, kq_ref, gt_ref, ot_ref):
    pair = pl.program_id(0)
    n_q = xq_ref.shape[0]
    hd = HEAD_DIM

    def conv_t(x_ref, c_ref, w_ref):
        xp = [c_ref[i] for i in range(GDN_CONV - 1)] + [x_ref[i] for i in range(n_q)]
        out = []
        for i in range(n_q):
            y = w_ref[0:1, :] * xp[i]
            for m in range(1, GDN_CONV):
                y = y + w_ref[m:m + 1, :] * xp[i + m]
            out.append(_silu(y).T)
        return out

    q_t = conv_t(xq_ref, cq_ref, wq_ref)
    k_t = conv_t(xk_ref, ck_ref, wk_ref)
    v_t = conv_t(xv_ref, cv_ref, wv_ref)

    def l2n(a):
        return a * lax.rsqrt(jnp.sum(a * a, axis=0, keepdims=True) + 1e-6)

    for i in range(n_q):
        gab_t = gab_ref[i].T
        gt_ref[0, i] = 1.0 / (1.0 + jnp.exp(-gab_t))
        gt_ref[1, i] = jnp.exp(-jnp.exp(alog_ref[...]) * _softplus(gab_t + dtb_ref[...]))

    for hh in range(2):
        rows = slice(hh * hd, (hh + 1) * hd)
        head = 2 * pair + hh
        for i in range(n_q):
            kq_ref[0] = l2n(k_t[i][rows])
            kq_ref[1] = l2n(q_t[i][rows]) * (hd ** -0.5)
            beta = gt_ref[0, i, pl.ds(head, 1), :]
            decay = gt_ref[1, i, pl.ds(N_HEADS + head, 1), :]
            src = s_ref if i == 0 else sout_ref

            def ks_body(kk, acc):
                return acc + kq_ref[0, pl.ds(kk, 1), :] * src[hh, kk]

            k_s = lax.fori_loop(0, hd, ks_body, jnp.zeros((hd, k_t[i].shape[1]), F32), unroll=8)
            r = beta * (v_t[i][rows] - decay * k_s)

            def upd_body(kk, acc):
                s_new = decay * src[hh, kk] + kq_ref[0, pl.ds(kk, 1), :] * r
                sout_ref[hh, kk] = s_new
                return acc + kq_ref[1, pl.ds(kk, 1), :] * s_new

            o = lax.fori_loop(0, hd, upd_body, jnp.zeros_like(r), unroll=8)
            ot_ref[i, rows, :] = o * lax.rsqrt(jnp.mean(o * o, axis=0, keepdims=True) + 1e-6) * nw_ref[...]

    for i in range(n_q):
        o_ref[i] = (ot_ref[i].T * _silu(z_ref[i])).astype(o_ref.dtype)


def _gdn_sample(proj_s, conv_state, state_t, conv_w, a_log, dt_bias, norm_w):
    n_q, bsz, _ = proj_s.shape
    hd = HEAD_DIM
    pw = 2 * hd
    n_pairs = N_HEADS // 2
    base = D_CONV // pw
    col = jnp.zeros((LANES, 1), F32)
    alog_col = col.at[N_HEADS:2 * N_HEADS, 0].set(a_log)
    dtb_col = col.at[N_HEADS:2 * N_HEADS, 0].set(dt_bias)
    nw_col = norm_w.reshape(hd, 1)

    def xspec(part):
        return pl.BlockSpec((n_q, bsz, pw), lambda p: (0, 0, base + part * n_pairs + p))

    def cspec(rows, part):
        return pl.BlockSpec((rows, bsz, pw) if rows else (GDN_CONV, pw),
                            (lambda p: (0, 0, part * n_pairs + p)) if rows
                            else (lambda p: (0, part * n_pairs + p)))

    const = lambda p: (0, 0)
    state_spec = pl.BlockSpec((2, hd, hd, bsz), lambda p: (p, 0, 0, 0))
    return pl.pallas_call(
        _gdn_sample_kernel,
        grid=(n_pairs,),
        in_specs=[xspec(0), xspec(1), xspec(2),
                  cspec(GDN_CONV - 1, 0), cspec(GDN_CONV - 1, 1), cspec(GDN_CONV - 1, 2),
                  cspec(0, 0), cspec(0, 1), cspec(0, 2),
                  pl.BlockSpec((n_q, bsz, pw), lambda p: (0, 0, D_IN_MAIN // pw - n_pairs + p)),
                  pl.BlockSpec((n_q, bsz, LANES), lambda p: (0, 0, D_IN_MAIN // LANES)),
                  pl.BlockSpec((LANES, 1), const), pl.BlockSpec((LANES, 1), const),
                  pl.BlockSpec((hd, 1), const),
                  state_spec],
        out_specs=[pl.BlockSpec((n_q, bsz, pw), lambda p: (0, 0, p)), state_spec],
        out_shape=[jax.ShapeDtypeStruct((n_q, bsz, D_HEADS), BF16),
                   jax.ShapeDtypeStruct(state_t.shape, F32)],
        scratch_shapes=[pltpu.VMEM((2, hd, bsz), F32),
                        pltpu.VMEM((2, n_q, LANES, bsz), F32),
                        pltpu.VMEM((n_q, pw, bsz), F32)],
        compiler_params=_cparams(1),
        name="gdn_sample",
    )(proj_s, proj_s, proj_s, conv_state, conv_state, conv_state, conv_w, conv_w, conv_w,
      proj_s, proj_s, alog_col, dtb_col, nw_col, state_t)


TM = 512
TM_MOE = 1024
TJ_MOE = 512
GDN_ROWS = 256
SAMPLE_PAD = 8


def kernel(x_prompt, x_sample, cache_k, cache_v, state_conv, state_gdn, page_table, c_prompt, c_sample,
           ln_in_g, ln_in_b, w_mod, b_mod, w_in, conv_w, a_log, dt_bias, gdn_norm_w, w_out, rel_table,
           ln_g, ln_b, ffn_w_gate, ffn_w_up, ffn_w_down, moe_router, moe_w_gate, moe_w_up, moe_w_down):
    bp, seq, d = x_prompt.shape
    bs, n_q, _ = x_sample.shape
    depth = w_in.shape[0]
    tp, ts = bp * seq, bs * n_q
    h, hd = N_HEADS, HEAD_DIM
    assert bs == MOD_ROWS and seq % TM == 0 and ts % TM == 0 and seq % MOBA_BLOCK == 0
    assert depth == 2 and GDN_CONV - 1 <= n_q <= SAMPLE_PAD and GDN_CHUNK == HEAD_DIM
    alpha = (2 * depth) ** 0.25

    def groups(tm):
        return lambda i: jnp.minimum(i // (seq // tm), bp)

    n_c = bp + bs
    c_all = jnp.pad(jnp.concatenate([c_prompt, c_sample]), ((0, (-n_c) % 8), (0, 0)))
    mod = _mod_vectors(c_all, w_mod, b_mod).reshape(depth, -1, 6, d)
    mod_p = jnp.broadcast_to(mod[:, :bp].transpose(0, 2, 1, 3)[:, :, :, None, :],
                             (depth, 6, bp, MOD_ROWS, d))
    mod_s = mod[:, bp:n_c].transpose(0, 2, 1, 3)[:, :, None]
    mod_all = jnp.concatenate([mod_p, mod_s], axis=2)

    x, u = _ln_mod(x_prompt.reshape(tp, d), x_sample.transpose(1, 0, 2).reshape(ts, d),
                   ln_in_g, ln_in_b, mod_all, 0, TM, groups(TM))

    bias_tab = _moba_bias_tables(rel_table)
    past_len = page_table.shape[1] * PAGE_SIZE
    bias_past, bias_own = _moba_sample_bias(rel_table, past_len, n_q)
    cache_kt = cache_k.transpose(0, 1, 3, 4, 2)
    cache_vt = cache_v.transpose(0, 1, 3, 4, 2)
    state_t = state_gdn.transpose(0, 2, 3, 4, 1)
    conv_t = state_conv.transpose(0, 2, 1, 3)

    k_p, v_p, conv_p, gdn_p, k_s, v_s, conv_s, gdn_s = [], [], [], [], [], [], [], []
    for layer in range(depth):
        w_l = w_in[layer]
        w_cat = jnp.concatenate(
            [w_l[:, :D_IN_MAIN], jnp.pad(w_l[:, D_IN_MAIN:], ((0, 0), (0, LANES - 2 * h)))],
            axis=1).astype(BF16)
        proj, qkv, k_t, v_t = _proj_in(u, w_cat, TM, bp, seq)
        proj_s = proj[tp:].reshape(n_q, bs, -1)

        def slots_s(cols):
            return jnp.pad(proj_s[:, :, cols].transpose(1, 0, 2), ((0, 0), (0, SAMPLE_PAD - n_q), (0, 0)))

        k_sr = proj_s[:, :, D_HEADS:2 * D_HEADS].transpose(1, 0, 2).reshape(bs, n_q, h, hd)
        v_sr = proj_s[:, :, 2 * D_HEADS:3 * D_HEADS].transpose(1, 0, 2).reshape(bs, n_q, h, hd)
        att_p = _moba_prompt(qkv, bias_tab, bp, seq)
        att_s = _moba_sample(slots_s(slice(0, D_HEADS)), slots_s(slice(D_HEADS, 2 * D_HEADS)),
                             slots_s(slice(2 * D_HEADS, 3 * D_HEADS)), cache_kt, cache_vt, page_table,
                             layer, bias_past, bias_own, n_q)
        att_s = att_s.reshape(bs, h, SAMPLE_PAD, hd)[:, :, :n_q].transpose(2, 0, 1, 3)
        att = (att_p, att_s.reshape(ts, D_HEADS).astype(BF16))

        o_p, conv_new_p, s_new_p = _gdn_prompt(proj, bp, seq, GDN_ROWS, conv_w[layer], a_log[layer],
                                               dt_bias[layer], gdn_norm_w[layer])
        o_s, s_new_s = _gdn_sample(proj_s, conv_t[layer], state_t[layer], conv_w[layer], a_log[layer],
                                   dt_bias[layer], gdn_norm_w[layer])
        s_new_s = s_new_s.transpose(3, 0, 1, 2)
        conv_new_s = proj_s[n_q - (GDN_CONV - 1):, :, 3 * D_HEADS:3 * D_HEADS + D_CONV].transpose(1, 0, 2)
        gdn_o = (o_p, o_s.reshape(ts, D_HEADS))

        w_o = w_out[layer].astype(BF16)
        i = layer // 2
        if layer % 2 == 0:
            x, u = _res_ln([att, gdn_o], [w_o[:D_HEADS], w_o[D_HEADS:]], x, mod_all, layer, 2, layer,
                           (4, 3), ln_g[layer, 0], ln_b[layer, 0], alpha, TM, groups(TM))
            hdn = _ffn_up(u, ffn_w_gate[i].astype(BF16), ffn_w_up[i].astype(BF16), TM)
            x, u = _res_ln([hdn], [ffn_w_down[i].astype(BF16)], x, mod_all, layer, 5, layer + 1,
                           (1, 0), ln_g[layer, 1], ln_b[layer, 1], alpha, TM, groups(TM))
        else:
            w_r = jnp.pad(moe_router[i], ((0, 0), (0, LANES - N_EXPERTS)))
            x, u, route = _res_ln([att, gdn_o], [w_o[:D_HEADS], w_o[D_HEADS:]], x, mod_all, layer, 2,
                                  layer, (4, 3), ln_g[layer, 0], ln_b[layer, 0], alpha, TM,
                                  groups(TM), w_router=w_r)
            tile_expert, n_valid, token_of_slot, slot = _moe_routing(route, TM_MOE)
            y = _moe_ffn(u, tile_expert, n_valid, token_of_slot, moe_w_gate[i], moe_w_up[i],
                         moe_w_down[i], TM_MOE, TJ_MOE)
            n_t = (tp + ts) // TM
            slots = slot.reshape(n_t, TM, TOP_K).transpose(0, 2, 1).reshape(n_t, 1, TOP_K * TM)
            x_p, x_s = _moe_combine(y, slots, route, x, mod_all, layer, 5, ln_g[layer, 1],
                                    ln_b[layer, 1], alpha, TM, groups(TM), tp // TM)

        k_p.append(k_t)
        v_p.append(v_t)
        conv_p.append(conv_new_p)
        gdn_p.append(s_new_p)
        k_s.append(k_sr)
        v_s.append(v_sr)
        conv_s.append(conv_new_s)
        gdn_s.append(s_new_s)

    y_prompt = x_p.reshape(bp, seq, d)
    y_sample = x_s.reshape(n_q, bs, d).transpose(1, 0, 2)
    k_prompt = jnp.stack(k_p).transpose(0, 1, 4, 2, 3)
    v_prompt = jnp.stack(v_p).transpose(0, 1, 4, 2, 3)
    return (y_prompt, y_sample, k_prompt, v_prompt, jnp.stack(conv_p), jnp.stack(gdn_p),
            jnp.stack(k_s), jnp.stack(v_s), jnp.stack(conv_s), jnp.stack(gdn_s))
```

```python
import functools
import math

import numpy as np
import jax
import jax.numpy as jnp
from jax import lax
from jax.experimental import pallas as pl
from jax.experimental.pallas import tpu as pltpu

F32 = jnp.float32
BF16 = jnp.bfloat16

D_MODEL = 1024
HEAD_DIM = 64
N_HEADS = 8
D_HEADS = N_HEADS * HEAD_DIM
D_CONV = 3 * D_HEADS
D_IN = 3 * D_HEADS + D_CONV + D_HEADS + 2 * N_HEADS
D_IN_MAIN = D_IN - 2 * N_HEADS
LANES = 128
D_IN_PAD = D_IN_MAIN + LANES
MOBA_BLOCK = 256
MOBA_TOPK = 3
PAGE_SIZE = 128
GDN_CONV = 4
GDN_CHUNK = 64
N_BUCKETS = 32
MAX_DISTANCE = 128
N_EXPERTS = 8
TOP_K = 2
LN_EPS = 1e-5
NEG = -1e30
MOD_ROWS = 128
VMEM_LIMIT = 56 * 1024 * 1024

HIGHEST = lax.Precision.HIGHEST


def _cparams(n_axes):
    return pltpu.CompilerParams(dimension_semantics=("arbitrary",) * n_axes,
                                vmem_limit_bytes=VMEM_LIMIT)


def _silu(x):
    return x * (1.0 / (1.0 + jnp.exp(-x)))


def _layer_norm(x, g, b):
    mu = jnp.mean(x, axis=-1, keepdims=True)
    xc = x - mu
    var = jnp.mean(xc * xc, axis=-1, keepdims=True)
    return xc * lax.rsqrt(var + LN_EPS) * g + b


def _modulate(x, scale, shift):
    tm, d = x.shape
    x3 = x.reshape(tm // MOD_ROWS, MOD_ROWS, d)
    return (x3 * (1.0 + scale[None]) + shift[None]).reshape(tm, d)


def _gated(x, gate, h):
    tm, d = x.shape
    x3 = x.reshape(tm // MOD_ROWS, MOD_ROWS, d)
    h3 = h.reshape(tm // MOD_ROWS, MOD_ROWS, d)
    return (x3 + (1.0 + gate[None]) * h3).reshape(tm, d)


def _mod_kernel(c_ref, w_ref, b_ref, o_ref):
    a = _silu(c_ref[...])
    o_ref[...] = jnp.dot(a, w_ref[...], precision=HIGHEST, preferred_element_type=F32) + b_ref[...]


def _mod_vectors(c_all, w_mod, b_mod, tn=1536):
    depth, d, n = w_mod.shape
    rows = c_all.shape[0]
    return pl.pallas_call(
        _mod_kernel,
        grid=(depth, n // tn),
        in_specs=[pl.BlockSpec((rows, d), lambda l, j: (0, 0)),
                  pl.BlockSpec((None, d, tn), lambda l, j: (l, 0, j)),
                  pl.BlockSpec((None, 1, tn), lambda l, j: (l, 0, j))],
        out_specs=pl.BlockSpec((None, rows, tn), lambda l, j: (l, 0, j)),
        out_shape=jax.ShapeDtypeStruct((depth, rows, n), F32),
        compiler_params=_cparams(2),
        name="mod_vectors",
    )(c_all, w_mod, b_mod.reshape(depth, 1, n))


def _ln_mod_kernel(xa_ref, xb_ref, g_ref, b_ref, mod_ref, xn_ref, u_ref, *, n_first):
    x = jnp.where(pl.program_id(0) < n_first, xa_ref[...], xb_ref[...])
    xn = _layer_norm(x, g_ref[...], b_ref[...])
    xn_ref[...] = xn
    u_ref[...] = _modulate(xn, mod_ref[1], mod_ref[0]).astype(BF16)


def _mod_spec(layer, group_of_tile):
    return pl.BlockSpec((None, 6, None, MOD_ROWS, D_MODEL),
                        lambda i: (layer, 0, group_of_tile(i), 0, 0))


def _ln_mod(x_first, x_rest, g, b, mod_all, layer, tm, group_of_tile):
    d = x_first.shape[1]
    n_first = x_first.shape[0] // tm
    t = x_first.shape[0] + x_rest.shape[0]
    row = pl.BlockSpec((tm, d), lambda i: (i, 0))
    vec = pl.BlockSpec((1, d), lambda i: (0, 0))
    return pl.pallas_call(
        functools.partial(_ln_mod_kernel, n_first=n_first),
        grid=(t // tm,),
        in_specs=[pl.BlockSpec((tm, d), lambda i: (jnp.minimum(i, n_first - 1), 0)),
                  pl.BlockSpec((tm, d), lambda i: (jnp.maximum(i - n_first, 0), 0)),
                  vec, vec, _mod_spec(layer, group_of_tile)],
        out_specs=[row, row],
        out_shape=[jax.ShapeDtypeStruct((t, d), F32), jax.ShapeDtypeStruct((t, d), BF16)],
        compiler_params=_cparams(1),
        name="ln_mod",
    )(x_first, x_rest, g.reshape(1, d), b.reshape(1, d), mod_all)


def _proj_kernel(u_ref, w_ref, proj_ref, qkv_ref, kt_ref, vt_ref, *, n_first):
    u = u_ref[...]
    n = w_ref.shape[1]
    chunk = D_HEADS
    n_qkv = qkv_ref.shape[1]
    is_prompt = pl.program_id(0) < n_first
    for c0 in range(0, n, chunk):
        c1 = min(c0 + chunk, n)
        r = jnp.dot(u, w_ref[:, c0:c1], preferred_element_type=F32)
        proj_ref[:, c0:c1] = r
        if c1 <= n_qkv:
            qkv_ref[:, c0:c1] = r.astype(BF16)
        for col, t_ref in ((D_HEADS, kt_ref), (2 * D_HEADS, vt_ref)):
            if c0 == col:
                @pl.when(is_prompt)
                def _(r=r, t_ref=t_ref):
                    t_ref[...] = r.T.reshape(t_ref.shape)


def _proj_in(u, w, tm, batch, seq):
    t, d = u.shape
    n = w.shape[1]
    per_seq = seq // tm
    n_first = batch * per_seq

    def t_spec():
        return pl.BlockSpec((None, N_HEADS, HEAD_DIM, tm),
                            lambda i: (jnp.minimum(i, n_first - 1) // per_seq, 0, 0,
                                       jnp.minimum(i, n_first - 1) % per_seq))

    t_shape = jax.ShapeDtypeStruct((batch, N_HEADS, HEAD_DIM, seq), F32)
    return pl.pallas_call(
        functools.partial(_proj_kernel, n_first=n_first),
        grid=(t // tm,),
        in_specs=[pl.BlockSpec((tm, d), lambda i: (i, 0)),
                  pl.BlockSpec((d, n), lambda i: (0, 0))],
        out_specs=[pl.BlockSpec((tm, n), lambda i: (i, 0)),
                   pl.BlockSpec((tm, 3 * D_HEADS), lambda i: (i, 0)),
                   t_spec(), t_spec()],
        out_shape=[jax.ShapeDtypeStruct((t, n), F32),
                   jax.ShapeDtypeStruct((t, 3 * D_HEADS), BF16),
                   t_shape, t_shape],
        compiler_params=_cparams(1),
        name="proj_in",
    )(u, w)


def _top2_gates(logits):
    lane = lax.broadcasted_iota(jnp.int32, logits.shape, 1)
    v1 = jnp.max(logits, axis=-1, keepdims=True)
    i1 = jnp.min(jnp.where(logits == v1, lane, LANES), axis=-1, keepdims=True)
    rest = jnp.where(lane == i1, -jnp.inf, logits)
    v2 = jnp.max(rest, axis=-1, keepdims=True)
    i2 = jnp.min(jnp.where(rest == v2, lane, LANES), axis=-1, keepdims=True)
    e2 = jnp.exp(v2 - v1)
    inv = 1.0 / (1.0 + e2)
    route = jnp.where(lane == 0, i1.astype(F32), 0.0) + jnp.where(lane == 1, i2.astype(F32), 0.0)
    return route + jnp.where(lane == 2, inv, 0.0) + jnp.where(lane == 3, e2 * inv, 0.0)


def _res_ln_kernel(*refs, split, n_first, alpha, next_rows, with_router):
    n_in = len(split)
    a_vals, pos = [], 0
    for two in split:
        if two:
            a_vals.append(jnp.where(pl.program_id(0) < n_first, refs[pos][...], refs[pos + 1][...]))
        else:
            a_vals.append(refs[pos][...])
        pos += 2 if two else 1
    w_refs = refs[pos:pos + n_in]
    x_ref, modg_ref, modn_ref, g_ref, b_ref = refs[pos + n_in:pos + n_in + 5]
    pos += n_in + 5
    if with_router:
        wr_ref = refs[pos]
        pos += 1
    outs = refs[pos:]
    h = jnp.dot(a_vals[0], w_refs[0][...], preferred_element_type=F32)
    for a_val, w_ref in zip(a_vals[1:], w_refs[1:]):
        h = h + jnp.dot(a_val, w_ref[...], preferred_element_type=F32)
    xn = _layer_norm(_gated(alpha * x_ref[...], modg_ref[...], h), g_ref[...], b_ref[...])
    outs[0][...] = xn
    if next_rows is not None:
        u = _modulate(xn, modn_ref[next_rows[0]], modn_ref[next_rows[1]])
        outs[1][...] = u.astype(outs[1].dtype)
        if with_router:
            logits = jnp.dot(u, wr_ref[...], precision=HIGHEST, preferred_element_type=F32)
            lane = lax.broadcasted_iota(jnp.int32, logits.shape, 1)
            outs[2][...] = _top2_gates(jnp.where(lane < N_EXPERTS, logits, -jnp.inf))


def _res_ln(a_list, w_list, x, mod_all, gate_layer, gate_row, next_layer, next_rows, ln_g, ln_b,
            alpha, tm, group_of_tile, w_router=None):
    t, d = x.shape
    row = pl.BlockSpec((tm, d), lambda i: (i, 0))
    vec = pl.BlockSpec((1, d), lambda i: (0, 0))
    split = tuple(isinstance(a, tuple) for a in a_list)
    n_first = a_list[split.index(True)][0].shape[0] // tm if any(split) else 0
    in_specs, flat_a = [], []
    for a in a_list:
        if isinstance(a, tuple):
            in_specs += [pl.BlockSpec((tm, a[0].shape[1]), lambda i: (jnp.minimum(i, n_first - 1), 0)),
                         pl.BlockSpec((tm, a[1].shape[1]), lambda i: (jnp.maximum(i - n_first, 0), 0))]
            flat_a += list(a)
        else:
            in_specs.append(pl.BlockSpec((tm, a.shape[1]), lambda i: (i, 0)))
            flat_a.append(a)
    in_specs += [pl.BlockSpec(w.shape, lambda i: (0, 0)) for w in w_list]
    in_specs += [row,
                 pl.BlockSpec((None, None, None, MOD_ROWS, d),
                              lambda i: (gate_layer, gate_row, group_of_tile(i), 0, 0)),
                 _mod_spec(next_layer if next_rows is not None else gate_layer, group_of_tile),
                 vec, vec]
    args = flat_a + list(w_list) + [x, mod_all, mod_all, ln_g.reshape(1, d), ln_b.reshape(1, d)]
    out_specs = [row]
    out_shape = [jax.ShapeDtypeStruct((t, d), F32)]
    with_router = w_router is not None
    if next_rows is not None:
        out_specs.append(row)
        out_shape.append(jax.ShapeDtypeStruct((t, d), F32 if with_router else BF16))
    if with_router:
        in_specs.append(pl.BlockSpec(w_router.shape, lambda i: (0, 0)))
        args.append(w_router)
        out_specs.append(pl.BlockSpec((tm, LANES), lambda i: (i, 0)))
        out_shape.append(jax.ShapeDtypeStruct((t, LANES), F32))
    return pl.pallas_call(
        functools.partial(_res_ln_kernel, split=split, n_first=n_first, alpha=alpha,
                          next_rows=next_rows, with_router=with_router),
        grid=(t // tm,),
        in_specs=in_specs,
        out_specs=out_specs,
        out_shape=out_shape,
        compiler_params=_cparams(1),
        name="res_ln",
    )(*args)


def _ffn_up_kernel(u_ref, wg_ref, wu_ref, o_ref, *, chunk):
    u = u_ref[...]
    n = o_ref.shape[1]
    for c0 in range(0, n, chunk):
        c1 = min(c0 + chunk, n)
        hg = jnp.dot(u, wg_ref[:, c0:c1], preferred_element_type=F32)
        hu = jnp.dot(u, wu_ref[:, c0:c1], preferred_element_type=F32)
        o_ref[:, c0:c1] = (_silu(hg) * hu).astype(BF16)


def _ffn_up(u, wg, wu, tm, chunk=256):
    t, d = u.shape
    n = wg.shape[1]
    wspec = pl.BlockSpec((d, n), lambda i: (0, 0))
    return pl.pallas_call(
        functools.partial(_ffn_up_kernel, chunk=chunk),
        grid=(t // tm,),
        in_specs=[pl.BlockSpec((tm, d), lambda i: (i, 0)), wspec, wspec],
        out_specs=pl.BlockSpec((tm, n), lambda i: (i, 0)),
        out_shape=jax.ShapeDtypeStruct((t, n), BF16),
        compiler_params=_cparams(1),
        name="ffn_up",
    )(u, wg, wu)


def _row_copy(src_hbm, src_row, dst_ref, r, sem):
    return pltpu.make_async_copy(src_hbm.at[pl.ds(src_row, 1)], dst_ref.at[pl.ds(r, 1)], sem)


def _gather_start(idx_ref, idx_base, src_hbm, dst_ref, sem, n_rows):
    def body(r, carry):
        _row_copy(src_hbm, idx_ref[0, idx_base + r], dst_ref, r, sem).start()
        return carry

    lax.fori_loop(0, n_rows, body, 0, unroll=8)


def _gather_wait(src_hbm, dst_ref, sem):
    pltpu.make_async_copy(src_hbm.at[pl.ds(0, dst_ref.shape[0])], dst_ref, sem).wait()


def _moe_ffn_kernel(te_ref, nv_ref, tos_ref, u_hbm, wg_ref, wu_ref, wd_ref, y_ref,
                    xg_ref, xb_ref, acc_ref, sem):
    del te_ref
    i = pl.program_id(0)
    j = pl.program_id(1)
    valid = i < nv_ref[0]

    @pl.when(valid & (j == 0))
    def _():
        _gather_start(tos_ref, 0, u_hbm, xg_ref, sem, xg_ref.shape[0])
        _gather_wait(u_hbm, xg_ref, sem)
        xb_ref[...] = xg_ref[...].astype(BF16)
        acc_ref[...] = jnp.zeros_like(acc_ref)

    @pl.when(valid)
    def _():
        x = xb_ref[...]
        hg = jnp.dot(x, wg_ref[...], preferred_element_type=F32)
        hu = jnp.dot(x, wu_ref[...], preferred_element_type=F32)
        acc_ref[...] += jnp.dot((_silu(hg) * hu).astype(BF16), wd_ref[...].astype(BF16),
                                preferred_element_type=F32)

    last = j == pl.num_programs(1) - 1

    @pl.when(valid & last)
    def _():
        y_ref[...] = acc_ref[...]

    @pl.when(jnp.logical_not(valid) & last)
    def _():
        y_ref[...] = jnp.zeros_like(y_ref)


def _column_tiles(w, tj):
    e, d, f = w.shape
    assert f % tj == 0
    return w.reshape(e, d, f // tj, tj).transpose(0, 2, 1, 3).astype(BF16)


def _moe_ffn(u, tile_expert, n_valid, token_of_slot, wg, wu, wd, tm):
    n_tiles = tile_expert.shape[0]
    d = u.shape[1]
    nj, tj = wg.shape[1], wg.shape[3]

    def jj(i, j, nv):
        return jnp.where(i < nv[0], j, nj - 1)

    grid_spec = pltpu.PrefetchScalarGridSpec(
        num_scalar_prefetch=2,
        grid=(n_tiles, nj),
        in_specs=[pl.BlockSpec((None, 1, tm), lambda i, j, te, nv: (i, 0, 0), memory_space=pltpu.SMEM),
                  pl.BlockSpec(memory_space=pl.ANY),
                  pl.BlockSpec((None, None, d, tj), lambda i, j, te, nv: (te[i], jj(i, j, nv), 0, 0)),
                  pl.BlockSpec((None, None, d, tj), lambda i, j, te, nv: (te[i], jj(i, j, nv), 0, 0)),
                  pl.BlockSpec((None, tj, d), lambda i, j, te, nv: (te[i], jj(i, j, nv), 0))],
        out_specs=pl.BlockSpec((tm, d), lambda i, j, te, nv: (i, 0)),
        scratch_shapes=[pltpu.VMEM((tm, d), F32), pltpu.VMEM((tm, d), BF16),
                        pltpu.VMEM((tm, d), F32), pltpu.SemaphoreType.DMA(())])
    return pl.pallas_call(
        _moe_ffn_kernel,
        grid_spec=grid_spec,
        out_shape=jax.ShapeDtypeStruct((n_tiles * tm, d), F32),
        compiler_params=_cparams(2),
        name="moe_ffn",
    )(tile_expert, n_valid, token_of_slot, u, wg, wu, wd)


def _moe_combine_kernel(slots_ref, y_hbm, route_ref, x_ref, modg_ref, g_ref, b_ref, oa_ref, ob_ref,
                        buf_ref, sem, *, alpha, n_first):
    rows = x_ref.shape[0]
    for k in range(TOP_K):
        _gather_start(slots_ref, k * rows, y_hbm, buf_ref.at[k], sem, rows)
    for k in range(TOP_K):
        _gather_wait(y_hbm, buf_ref.at[k], sem)
    route = route_ref[...]
    f = route[:, 2:3] * buf_ref[0] + route[:, 3:4] * buf_ref[1]
    o = _layer_norm(_gated(alpha * x_ref[...], modg_ref[...], f), g_ref[...], b_ref[...])

    @pl.when(pl.program_id(0) < n_first)
    def _():
        oa_ref[...] = o

    @pl.when(pl.program_id(0) >= n_first)
    def _():
        ob_ref[...] = o


def _moe_combine(y, slots, route, x, mod_all, gate_layer, gate_row, ln_g, ln_b, alpha, tm,
                 group_of_tile, n_first):
    t, d = x.shape
    row = pl.BlockSpec((tm, d), lambda i: (i, 0))
    vec = pl.BlockSpec((1, d), lambda i: (0, 0))
    return pl.pallas_call(
        functools.partial(_moe_combine_kernel, alpha=alpha, n_first=n_first),
        grid=(t // tm,),
        in_specs=[pl.BlockSpec((None, 1, 2 * tm), lambda i: (i, 0, 0), memory_space=pltpu.SMEM),
                  pl.BlockSpec(memory_space=pl.ANY),
                  pl.BlockSpec((tm, LANES), lambda i: (i, 0)),
                  row,
                  pl.BlockSpec((None, None, None, MOD_ROWS, d),
                               lambda i: (gate_layer, gate_row, group_of_tile(i), 0, 0)),
                  vec, vec],
        out_specs=[pl.BlockSpec((tm, d), lambda i: (jnp.minimum(i, n_first - 1), 0)),
                   pl.BlockSpec((tm, d), lambda i: (jnp.maximum(i - n_first, 0), 0))],
        out_shape=[jax.ShapeDtypeStruct((n_first * tm, d), F32),
                   jax.ShapeDtypeStruct((t - n_first * tm, d), F32)],
        scratch_shapes=[pltpu.VMEM((2, tm, d), F32), pltpu.SemaphoreType.DMA(())],
        compiler_params=_cparams(1),
        name="moe_combine",
    )(slots, y, route, x, mod_all, ln_g.reshape(1, d), ln_b.reshape(1, d))


def _moe_routing(route, tm):
    t = route.shape[0]
    n_assign = TOP_K * t
    n_tiles = -(-n_assign // tm) + N_EXPERTS
    experts = route[:, :TOP_K].astype(jnp.int32).reshape(n_assign)
    onehot = (experts[:, None] == jnp.arange(N_EXPERTS, dtype=jnp.int32)[None, :]).astype(jnp.int32)
    rank = jnp.sum((jnp.cumsum(onehot, axis=0) - onehot) * onehot, axis=1)
    count = jnp.sum(onehot, axis=0)
    padded = (count + tm - 1) // tm * tm
    ends = jnp.cumsum(padded)
    slot = jnp.sum(onehot * (ends - padded)[None, :], axis=1) + rank
    n_valid = ends[-1] // tm
    starts = jnp.arange(n_tiles, dtype=jnp.int32) * tm
    tile_expert = jnp.minimum(jnp.sum((starts[:, None] >= ends[None, :]).astype(jnp.int32), axis=1),
                              N_EXPERTS - 1)
    tile_expert = jnp.where(starts < ends[-1], tile_expert, tile_expert[jnp.maximum(n_valid - 1, 0)])
    token_of_slot = jnp.zeros((n_tiles * tm,), jnp.int32).at[slot].set(
        jnp.arange(n_assign, dtype=jnp.int32) // TOP_K)
    return (tile_expert.astype(jnp.int32), n_valid.reshape(1).astype(jnp.int32),
            token_of_slot.reshape(n_tiles, 1, tm), slot.reshape(t, TOP_K).astype(jnp.int32))


def _rel_bucket_table(max_dist):
    n = np.arange(max_dist + 1)
    max_exact = N_BUCKETS // 2
    ratio = np.log(np.maximum(n, 1).astype(np.float32) / np.float32(max_exact))
    large = max_exact + (ratio / np.float32(math.log(MAX_DISTANCE / max_exact))
                         * np.float32(N_BUCKETS - max_exact)).astype(np.int32)
    large = np.minimum(large, N_BUCKETS - 1)
    return np.where(n < max_exact, n, large).astype(np.int32)


def _bias_lookup(rel_table, bucket_idx):
    onehot = jnp.asarray(np.eye(N_BUCKETS, dtype=np.float32)[bucket_idx.reshape(-1)])
    vals = jnp.dot(onehot, rel_table, precision=HIGHEST)
    return vals.T.reshape((rel_table.shape[1],) + bucket_idx.shape)


def _select_top_blocks(gate, n_valid, ksel):
    blk = lax.broadcasted_iota(jnp.int32, gate.shape, 0)
    nb = gate.shape[0]
    g = jnp.where(blk < n_valid, gate, -jnp.inf)
    keep = blk == n_valid
    for _ in range(ksel):
        mx = jnp.max(g, axis=0, keepdims=True)
        first = jnp.min(jnp.where(g == mx, blk, nb), axis=0, keepdims=True)
        hit = blk == first
        keep = keep | (hit & (blk < n_valid))
        g = jnp.where(hit, -jnp.inf, g)
    return jnp.where(keep, 0.0, NEG)


def _moba_prompt_kernel(q_ref, k_ref, v_ref, bias_ref, o_ref,
                        kmean_ref, vt_ref, qm_ref, sel_ref, m_ref, l_ref, acc_ref, *, ksel):
    qt = pl.program_id(1)
    seq = k_ref.shape[0]
    nb = seq // MOBA_BLOCK
    gw = 4 * HEAD_DIM
    n_groups = N_HEADS // 4
    nt = (((1,), (1,)), ((), ()))

    @pl.when(qt == 0)
    def _():
        row = lax.broadcasted_iota(jnp.int32, (nb, seq), 0)
        col = lax.broadcasted_iota(jnp.int32, (nb, seq), 1)
        pool = jnp.where(col // MOBA_BLOCK == row, 1.0, 0.0).astype(BF16)
        kmean_ref[...] = jnp.dot(pool, k_ref[...], preferred_element_type=F32) * (1.0 / MOBA_BLOCK)
        for n in range(nb):
            vt_ref[n] = v_ref[n * MOBA_BLOCK:(n + 1) * MOBA_BLOCK, :].astype(F32).T.astype(BF16)

    head_of_lane = lax.broadcasted_iota(jnp.int32, (1, gw), 1) // HEAD_DIM
    q = q_ref[...] * (HEAD_DIM ** -0.5)
    kmean = kmean_ref[...].astype(BF16)
    for h in range(N_HEADS):
        g, hh = divmod(h, 4)
        qm = jnp.where(head_of_lane == hh, q[:, g * gw:(g + 1) * gw], 0.0).astype(BF16)
        qm_ref[h] = qm
        gate = lax.dot_general(kmean[:, g * gw:(g + 1) * gw], qm, nt, preferred_element_type=F32)
        sel_ref[h] = _select_top_blocks(gate, qt, ksel)
        m_ref[h] = jnp.full(m_ref.shape[1:], NEG, F32)
        l_ref[h] = jnp.zeros(l_ref.shape[1:], F32)
    acc_ref[...] = jnp.zeros_like(acc_ref)

    def body(step, carry, near):
        n = qt - step
        start = pl.multiple_of(n * MOBA_BLOCK, MOBA_BLOCK)
        kb = k_ref[pl.ds(start, MOBA_BLOCK), :]
        vt = vt_ref[n]
        scores = [lax.dot_general(kb[:, (h // 4) * gw:(h // 4 + 1) * gw], qm_ref[h], nt,
                                  preferred_element_type=F32) for h in range(N_HEADS)]
        probs, alphas = [], []
        for h in range(N_HEADS):
            shift = sel_ref[h, pl.ds(n, 1), :]
            if near:
                s = scores[h] + bias_ref[h, step]
            else:
                s = scores[h]
                shift = shift + bias_ref[h, 2, 0:1, :]
            m_old = m_ref[h]
            m_new = jnp.maximum(m_old, jnp.max(s, axis=0, keepdims=True) + shift)
            alpha = jnp.exp(m_old - m_new)
            p = jnp.exp(s - (m_new - shift))
            l_ref[h] = alpha * l_ref[h] + jnp.sum(p, axis=0, keepdims=True)
            m_ref[h] = m_new
            probs.append(p.astype(BF16))
            alphas.append(alpha)
        outs = [jnp.dot(vt[h * HEAD_DIM:(h + 1) * HEAD_DIM, :], probs[h],
                        preferred_element_type=F32) for h in range(N_HEADS)]
        for h in range(N_HEADS):
            g, hh = divmod(h, 4)
            rows = slice(hh * HEAD_DIM, (hh + 1) * HEAD_DIM)
            acc_ref[g, rows, :] = alphas[h] * acc_ref[g, rows, :] + outs[h]
        return carry

    lax.fori_loop(0, jnp.minimum(qt + 1, 2), functools.partial(body, near=True), 0)
    lax.fori_loop(2, qt + 1, functools.partial(body, near=False), 0)

    for g in range(n_groups):
        o_t = jnp.concatenate(
            [acc_ref[g, hh * HEAD_DIM:(hh + 1) * HEAD_DIM, :] * (1.0 / l_ref[4 * g + hh])
             for hh in range(4)], axis=0)
        o_ref[:, g * gw:(g + 1) * gw] = o_t.T.astype(o_ref.dtype)


def _moba_bias_tables(rel_table):
    bucket = _rel_bucket_table(2 * MOBA_BLOCK)
    i = np.arange(MOBA_BLOCK)[None, :]
    j = np.arange(MOBA_BLOCK)[:, None]
    own = jnp.where(jnp.asarray(i >= j), _bias_lookup(rel_table, bucket[np.maximum(i - j, 0)]), NEG)
    prev = _bias_lookup(rel_table, bucket[MOBA_BLOCK + i - j])
    far = jnp.broadcast_to(rel_table[N_BUCKETS - 1][:, None, None], prev.shape)
    return jnp.stack([own, prev, far], axis=1).astype(F32)


def _moba_prompt(qkv, bias_tab, batch, seq):
    nb = seq // MOBA_BLOCK
    ksel = min(MOBA_TOPK, nb)
    stat = pltpu.VMEM((N_HEADS, 1, MOBA_BLOCK), F32)
    return pl.pallas_call(
        functools.partial(_moba_prompt_kernel, ksel=ksel),
        grid=(batch, nb),
        in_specs=[pl.BlockSpec((MOBA_BLOCK, D_HEADS), lambda b, t: (b * nb + t, 0)),
                  pl.BlockSpec((seq, D_HEADS), lambda b, t: (b, 1)),
                  pl.BlockSpec((seq, D_HEADS), lambda b, t: (b, 2)),
                  pl.BlockSpec(bias_tab.shape, lambda b, t: (0, 0, 0, 0))],
        out_specs=pl.BlockSpec((MOBA_BLOCK, D_HEADS), lambda b, t: (b * nb + t, 0)),
        out_shape=jax.ShapeDtypeStruct((batch * seq, D_HEADS), BF16),
        scratch_shapes=[pltpu.VMEM((nb, D_HEADS), F32),
                        pltpu.VMEM((nb, D_HEADS, MOBA_BLOCK), BF16),
                        pltpu.VMEM((N_HEADS, MOBA_BLOCK, 4 * HEAD_DIM), BF16),
                        pltpu.VMEM((N_HEADS, nb, MOBA_BLOCK), F32),
                        stat, stat,
                        pltpu.VMEM((N_HEADS // 4, 4 * HEAD_DIM, MOBA_BLOCK), F32)],
        compiler_params=_cparams(2),
        name="moba_prompt",
    )(qkv, qkv, qkv, bias_tab)


def _moba_sample_kernel(pt_ref, q_ref, kn_ref, vn_ref, *refs, ksel, n_q, nbp, pages_per_block):
    del pt_ref
    n_pages = nbp * pages_per_block
    k_refs, v_refs = refs[:n_pages], refs[n_pages:2 * n_pages]
    bias_ref, bown_ref, o_ref, m_ref, l_ref, g_ref, acc_ref = refs[2 * n_pages:]
    nt = (((1,), (1,)), ((), ()))
    dh = N_HEADS * HEAD_DIM
    q8 = q_ref[...]
    slots = q8.shape[0]
    lane_head = lax.broadcasted_iota(jnp.int32, (1, dh), 1) // HEAD_DIM
    qbd = jnp.concatenate([jnp.where(lane_head == hp, q8, 0.0) for hp in range(N_HEADS)], axis=0)
    qs = (qbd * (HEAD_DIM ** -0.5)).astype(BF16)
    ones = jnp.ones((8, PAGE_SIZE), BF16)

    kts = [k_ref[...].reshape(dh, PAGE_SIZE).astype(BF16) for k_ref in k_refs]
    s_pages = [jnp.dot(qs, kt, preferred_element_type=F32) for kt in kts]
    ksums = [lax.dot_general(ones, kt, nt, preferred_element_type=F32) for kt in kts]
    probs = []
    for j in range(nbp):
        pages = range(j * pages_per_block, (j + 1) * pages_per_block)
        s = jnp.concatenate([s_pages[pg] for pg in pages], axis=1) + bias_ref[0 if j == nbp - 1 else 1]
        m = jnp.max(s, axis=-1, keepdims=True)
        p32 = jnp.exp(s - m)
        probs.append(p32.astype(BF16))
        m_ref[j] = m
        l_ref[j] = jnp.sum(p32, axis=-1, keepdims=True)
        ksum = functools.reduce(jnp.add, [ksums[pg] for pg in pages])
        g_ref[j] = jnp.sum(qbd * ksum[0:1, :], axis=-1, keepdims=True) * (1.0 / MOBA_BLOCK)
    pv = [lax.dot_general(probs[pg // pages_per_block][:, (pg % pages_per_block) * PAGE_SIZE:
                                                       (pg % pages_per_block + 1) * PAGE_SIZE],
                          v_refs[pg][...].reshape(dh, PAGE_SIZE).astype(BF16), nt,
                          preferred_element_type=F32) for pg in range(n_pages)]
    for j in range(nbp):
        acc_ref[j] = functools.reduce(jnp.add, pv[j * pages_per_block:(j + 1) * pages_per_block])

    s_own = [jnp.sum(qbd * kn_ref[i:i + 1, :], axis=-1, keepdims=True) * (HEAD_DIM ** -0.5)
             + bown_ref[:, i:i + 1] for i in range(n_q)]
    m_o = functools.reduce(jnp.maximum, s_own)
    p_own = [jnp.exp(s_i - m_o) for s_i in s_own]
    l_o = functools.reduce(jnp.add, p_own)
    acc_o = functools.reduce(jnp.add, [p_i * vn_ref[i:i + 1, :] for i, p_i in enumerate(p_own)])

    gg = g_ref[...]
    blk = lax.broadcasted_iota(jnp.int32, gg.shape, 0)
    keep = blk < 0
    for _ in range(ksel):
        mx = jnp.max(gg, axis=0, keepdims=True)
        first = jnp.min(jnp.where(gg == mx, blk, nbp), axis=0, keepdims=True)
        hit = (blk == first) & (gg > -jnp.inf)
        keep = keep | hit
        gg = jnp.where(blk == first, -jnp.inf, gg)
    m_all = m_ref[...]
    m_tot = jnp.maximum(jnp.max(jnp.where(keep, m_all, NEG), axis=0), m_o)
    w = jnp.where(keep, jnp.exp(m_all - m_tot[None]), 0.0)
    w_o = jnp.exp(m_o - m_tot)
    l_tot = jnp.sum(w * l_ref[...], axis=0) + w_o * l_o
    acc_tot = (jnp.sum(w * acc_ref[...], axis=0) + w_o * acc_o) * (1.0 / l_tot)
    row_head = lax.broadcasted_iota(jnp.int32, (N_HEADS * slots, 1), 0) // slots
    acc_tot = jnp.where(row_head == lane_head, acc_tot, 0.0)
    o_ref[...] = functools.reduce(
        jnp.add, [acc_tot[:, hp * HEAD_DIM:(hp + 1) * HEAD_DIM] for hp in range(N_HEADS)])


def _moba_sample_bias(rel_table, past_len, n_q):
    del past_len
    slots = SAMPLE_PAD
    bucket = _rel_bucket_table(MOBA_BLOCK + slots)
    rows_h = np.repeat(np.arange(N_HEADS), slots)
    rows_i = np.minimum(np.tile(np.arange(slots), N_HEADS), n_q - 1)
    pos = np.arange(MOBA_BLOCK)[None, :]
    by_head = _bias_lookup(rel_table, bucket[MOBA_BLOCK + rows_i[:, None] - pos])
    pick = jnp.asarray(np.arange(N_HEADS)[:, None, None] == rows_h[None, :, None])
    near = jnp.sum(jnp.where(pick, by_head, 0.0), axis=0)
    far = jnp.broadcast_to(jnp.repeat(rel_table[N_BUCKETS - 1], slots)[:, None], near.shape)
    past = jnp.stack([near, far]).astype(F32)
    new = np.arange(slots)[None, :]
    own_h = _bias_lookup(rel_table, bucket[np.maximum(rows_i[:, None] - new, 0)])
    own = jnp.sum(jnp.where(pick, own_h, 0.0), axis=0)
    ok = jnp.asarray((new <= rows_i[:, None]) & (new < n_q))
    return past, jnp.where(ok, own, NEG).astype(F32)


def _moba_sample(q_rows, k_new, v_new, cache_k, cache_v, page_table, layer, bias_past, bias_own, n_q):
    bsz, slots, dh = q_rows.shape
    hd = HEAD_DIM
    rows = N_HEADS * slots
    pages_per_block = MOBA_BLOCK // PAGE_SIZE
    nbp = page_table.shape[1] // pages_per_block
    ksel = min(MOBA_TOPK, nbp + 1)

    n_pages = nbp * pages_per_block

    def page_spec(page):
        return pl.BlockSpec((None, None, N_HEADS, hd, PAGE_SIZE),
                            lambda b, pt: (layer, pt[b, page], 0, 0, 0))

    per_seq = lambda b, pt: (b, 0, 0)
    stat = pltpu.VMEM((nbp, rows, 1), F32)
    grid_spec = pltpu.PrefetchScalarGridSpec(
        num_scalar_prefetch=1,
        grid=(bsz,),
        in_specs=([pl.BlockSpec((None, slots, dh), per_seq)] * 3
                  + [page_spec(page) for page in range(n_pages)] * 2
                  + [pl.BlockSpec(bias_past.shape, lambda b, pt: (0, 0, 0)),
                     pl.BlockSpec(bias_own.shape, lambda b, pt: (0, 0))]),
        out_specs=pl.BlockSpec((None, rows, hd), per_seq),
        scratch_shapes=[stat, stat, stat, pltpu.VMEM((nbp, rows, dh), F32)])
    return pl.pallas_call(
        functools.partial(_moba_sample_kernel, ksel=ksel, n_q=n_q, nbp=nbp,
                          pages_per_block=pages_per_block),
        grid_spec=grid_spec,
        out_shape=jax.ShapeDtypeStruct((bsz, rows, hd), F32),
        compiler_params=_cparams(1),
        name="moba_sample",
    )(page_table, q_rows, k_new, v_new, *([cache_k] * n_pages), *([cache_v] * n_pages),
      bias_past, bias_own)


def _softplus(x):
    return jnp.maximum(x, 0.0) + jnp.log1p(jnp.exp(-jnp.abs(x)))


def _bmm(a, b):
    return jnp.matmul(a.astype(BF16), b.astype(BF16), preferred_element_type=F32)


def _dot_f32(a, b):
    return jnp.dot(a, b, precision=HIGHEST, preferred_element_type=F32)


def _split_bf16(a):
    hi = a.astype(BF16)
    return hi, (a - hi.astype(F32)).astype(BF16)


def _bmm_3pass(a, b):
    a_hi, a_lo = _split_bf16(a)
    b_hi, b_lo = _split_bf16(b)
    return (jnp.matmul(a_hi, b_hi, preferred_element_type=F32)
            + jnp.matmul(a_hi, b_lo, preferred_element_type=F32)
            + jnp.matmul(a_lo, b_hi, preferred_element_type=F32))


def _unit_lower_inverse(a, row, col):
    c = a.shape[1]
    eye = jnp.where(row == col, 1.0, 0.0)
    in16 = (row // 16) == (col // 16)
    in32 = (row // 32) == (col // 32)
    nil = jnp.where(in16, -a, 0.0)
    x = eye + nil
    p = _bmm_3pass(nil, nil)
    for _ in range(2):
        xp = _bmm_3pass(jnp.concatenate([x, p], axis=1), p)
        x = x + xp[:, :c]
        p = xp[:, c:]
    x = x + _bmm_3pass(x, p)
    for off in (jnp.where(in32 & jnp.logical_not(in16), a, 0.0),
                jnp.where(in32, 0.0, a)):
        x = x - _bmm(x, _bmm(off, x))
    return x


def _gdn_prepare(q, k, v, k_t, g_col, g_row, beta, row, col):
    c = q.shape[1]
    incl = row >= col
    decay = jnp.where(incl, jnp.exp(jnp.where(incl, g_col - g_row, 0.0)), 0.0)
    qk_kk = _bmm(jnp.concatenate([q, k], axis=1), k_t)
    a_mat = jnp.where(row > col, qk_kk[:, c:] * decay * beta, 0.0)
    e_g = jnp.exp(g_col)
    rhs = jnp.concatenate([v * beta, k * beta * e_g], axis=2)
    sol = _bmm(_unit_lower_inverse(a_mat, row, col), rhs)
    dv = v.shape[2]
    g_last = g_col[:, c - 1:c, :]
    w_qd = jnp.concatenate([sol[:, :, dv:], q * e_g], axis=1).astype(BF16)
    qk = jnp.where(incl, qk_kk[:, :c] * decay, 0.0).astype(BF16)
    k_dec_t = (k_t * jnp.exp(g_last - g_row)).astype(BF16)
    return sol[:, :, :dv], w_qd, qk, k_dec_t, jnp.exp(g_last)


def _gdn_step(u, w_qd, qk, k_dec_t, chunk_decay, state):
    c = u.shape[1]
    ws_qs = jnp.matmul(w_qd, state.astype(BF16), preferred_element_type=F32)
    v_new = (u - ws_qs[:, :c]).astype(BF16)
    o = ws_qs[:, c:] + jnp.matmul(qk, v_new, preferred_element_type=F32)
    new_state = state * chunk_decay + jnp.matmul(k_dec_t, v_new, preferred_element_type=F32)
    return o, new_state


def _gdn_prompt_kernel(x_ref, z_ref, gab_ref, cw_ref, alog_ref, dtb_ref, nw_ref, seg_ref,
                       o_ref, conv_ref, sout_ref,
                       xbuf_ref, halo_ref, s_ref, u_ref, wqd_ref, qk_ref, kdt_ref, cd_ref, oh_ref):
    t = pl.program_id(1)
    tl = x_ref.shape[0]
    cs = GDN_CHUNK
    n_chunks = tl // cs
    hd = HEAD_DIM

    @pl.when(t == 0)
    def _():
        halo_ref[...] = jnp.zeros_like(halo_ref)
        s_ref[...] = jnp.zeros_like(s_ref)

    x = x_ref[...]
    xbuf_ref[0:8, :] = halo_ref[...]
    xbuf_ref[8:8 + tl, :] = x
    y = cw_ref[0:1, :] * xbuf_ref[5:5 + tl, :]
    for i in range(1, GDN_CONV):
        y = y + cw_ref[i:i + 1, :] * xbuf_ref[5 + i:5 + i + tl, :]
    y = _silu(y)
    tail = x[tl - 8:, :]
    halo_ref[...] = tail
    conv_ref[...] = tail[8 - (GDN_CONV - 1):, :]

    dh = N_HEADS * hd
    seg = seg_ref[...]

    def l2n(a):
        sq_hi, sq_lo = _split_bf16(a * a)
        ss = (jnp.dot(sq_hi, seg, preferred_element_type=F32)
              + jnp.dot(sq_lo, seg, preferred_element_type=F32))
        return a * lax.rsqrt(ss + 1e-6)

    qn = l2n(y[:, :dh]) * (hd ** -0.5)
    kn = l2n(y[:, dh:2 * dh])
    vv = y[:, 2 * dh:]
    kn_t = kn.T

    gab = gab_ref[...]
    beta = 1.0 / (1.0 + jnp.exp(-gab))
    glog = -jnp.exp(alog_ref[...]) * _softplus(gab + dtb_ref[...])

    row = lax.broadcasted_iota(jnp.int32, (cs, cs), 0)
    col = lax.broadcasted_iota(jnp.int32, (cs, cs), 1)
    tril = jnp.where(row >= col, 1.0, 0.0)
    pick = jnp.where(lax.broadcasted_iota(jnp.int32, (N_HEADS, LANES), 1)
                     == lax.broadcasted_iota(jnp.int32, (N_HEADS, LANES), 0) + N_HEADS, 1.0, 0.0)
    parts = {name: [] for name in ("q", "k", "v", "kt", "gc", "gr", "bb")}
    for c in range(n_chunks):
        rows = slice(c * cs, (c + 1) * cs)
        gcum = _dot_f32(tril, glog[rows])
        g_rows = lax.dot_general(pick, gcum, (((1,), (1,)), ((), ())), precision=HIGHEST,
                                 preferred_element_type=F32)
        for h in range(N_HEADS):
            lanes = slice(h * hd, (h + 1) * hd)
            parts["q"].append(qn[rows, lanes])
            parts["k"].append(kn[rows, lanes])
            parts["v"].append(vv[rows, lanes])
            parts["kt"].append(kn_t[lanes, rows])
            parts["gc"].append(jnp.broadcast_to(gcum[:, N_HEADS + h:N_HEADS + h + 1], (cs, cs)))
            parts["gr"].append(g_rows[h:h + 1, :])
            parts["bb"].append(jnp.broadcast_to(beta[rows, h:h + 1], (cs, cs)))
    st = {name: jnp.stack(vals) for name, vals in parts.items()}
    u_all, w_qd, qk, k_dec_t, chunk_decay = _gdn_prepare(
        st["q"], st["k"], st["v"], st["kt"], st["gc"], st["gr"], st["bb"], row, col)
    u_ref[...] = u_all.reshape(u_ref.shape)
    wqd_ref[...] = w_qd.reshape(wqd_ref.shape)
    qk_ref[...] = qk.reshape(qk_ref.shape)
    kdt_ref[...] = k_dec_t.reshape(kdt_ref.shape)
    cd_ref[...] = chunk_decay.reshape(cd_ref.shape)

    def chunk_body(c, carry):
        o, s_new = _gdn_step(u_ref[c], wqd_ref[c], qk_ref[c], kdt_ref[c], cd_ref[c], s_ref[...])
        oh_ref[c] = o
        s_ref[...] = s_new
        return carry

    lax.fori_loop(0, n_chunks, chunk_body, 0)

    z = z_ref[...]
    nw = nw_ref[...]
    for h in range(N_HEADS):
        lanes = slice(h * hd, (h + 1) * hd)
        o = jnp.concatenate([oh_ref[c, h] for c in range(n_chunks)], axis=0)
        o = o * lax.rsqrt(jnp.mean(o * o, axis=-1, keepdims=True) + 1e-6) * nw
        o_ref[:, lanes] = (o * _silu(z[:, lanes])).astype(o_ref.dtype)

    @pl.when(t == pl.num_programs(1) - 1)
    def _():
        sout_ref[...] = s_ref[...]


def _gdn_prompt(proj, batch, seq, tl, conv_w, a_log, dt_bias, norm_w):
    hd = HEAD_DIM
    cs = GDN_CHUNK
    n_steps = seq // tl
    n_chunks = tl // cs
    col_x, col_z, col_gab = 1, D_IN_MAIN // D_HEADS - 1, D_IN_MAIN // LANES
    lane = np.arange(D_HEADS)
    seg = jnp.asarray((lane[:, None] // hd == lane[None, :] // hd).astype(np.float32)).astype(BF16)
    alog_row = jnp.zeros((1, LANES), F32).at[0, N_HEADS:2 * N_HEADS].set(a_log)
    dtb_row = jnp.zeros((1, LANES), F32).at[0, N_HEADS:2 * N_HEADS].set(dt_bias)
    const2 = lambda b, t: (0, 0)
    per_inst = pltpu.VMEM((n_chunks, N_HEADS, cs, hd), F32)
    per_inst_bf16 = pltpu.VMEM((n_chunks, N_HEADS, hd, cs), BF16)
    return pl.pallas_call(
        _gdn_prompt_kernel,
        grid=(batch, n_steps),
        in_specs=[pl.BlockSpec((tl, D_CONV), lambda b, t: (b * n_steps + t, col_x)),
                  pl.BlockSpec((tl, D_HEADS), lambda b, t: (b * n_steps + t, col_z)),
                  pl.BlockSpec((tl, LANES), lambda b, t: (b * n_steps + t, col_gab)),
                  pl.BlockSpec((GDN_CONV, D_CONV), const2),
                  pl.BlockSpec((1, LANES), const2),
                  pl.BlockSpec((1, LANES), const2),
                  pl.BlockSpec((1, hd), const2),
                  pl.BlockSpec((D_HEADS, D_HEADS), const2)],
        out_specs=[pl.BlockSpec((tl, D_HEADS), lambda b, t: (b * n_steps + t, 0)),
                   pl.BlockSpec((None, GDN_CONV - 1, D_CONV), lambda b, t: (b, 0, 0)),
                   pl.BlockSpec((None, N_HEADS, hd, hd), lambda b, t: (b, 0, 0, 0))],
        out_shape=[jax.ShapeDtypeStruct((batch * seq, D_HEADS), BF16),
                   jax.ShapeDtypeStruct((batch, GDN_CONV - 1, D_CONV), F32),
                   jax.ShapeDtypeStruct((batch, N_HEADS, hd, hd), F32)],
        scratch_shapes=[pltpu.VMEM((8 + tl + 8, D_CONV), F32),
                        pltpu.VMEM((8, D_CONV), F32),
                        pltpu.VMEM((N_HEADS, hd, hd), F32),
                        per_inst,
                        pltpu.VMEM((n_chunks, N_HEADS, 2 * cs, hd), BF16),
                        per_inst_bf16, per_inst_bf16,
                        pltpu.VMEM((n_chunks, N_HEADS, 1, hd), F32),
                        per_inst],
        compiler_params=_cparams(2),
        name="gdn_prompt",
    )(proj, proj, proj, conv_w, alog_row, dtb_row, norm_w.reshape(1, hd), seg)


def _gdn_sample_kernel(xq_ref, xk_ref, xv_ref, cq_ref, ck_ref, cv_ref, wq_ref, wk_ref, wv_ref,
                       z_ref, gab_ref, alog_ref, dtb_ref, nw_ref, s_ref,
                       o_ref, sout_ref, kq_ref, gt_ref, ot_ref):
    pair = pl.program_id(0)
    n_q = xq_ref.shape[0]
    hd = HEAD_DIM

    def conv_t(x_ref, c_ref, w_ref):
        xp = [c_ref[i] for i in range(GDN_CONV - 1)] + [x_ref[i] for i in range(n_q)]
        out = []
        for i in range(n_q):
            y = w_ref[0:1, :] * xp[i]
            for m in range(1, GDN_CONV):
                y = y + w_ref[m:m + 1, :] * xp[i + m]
            out.append(_silu(y).T)
        return out

    q_t = conv_t(xq_ref, cq_ref, wq_ref)
    k_t = conv_t(xk_ref, ck_ref, wk_ref)
    v_t = conv_t(xv_ref, cv_ref, wv_ref)

    def l2n(a):
        return a * lax.rsqrt(jnp.sum(a * a, axis=0, keepdims=True) + 1e-6)

    for i in range(n_q):
        gab_t = gab_ref[i].T
        gt_ref[0, i] = 1.0 / (1.0 + jnp.exp(-gab_t))
        gt_ref[1, i] = jnp.exp(-jnp.exp(alog_ref[...]) * _softplus(gab_t + dtb_ref[...]))

    for hh in range(2):
        rows = slice(hh * hd, (hh + 1) * hd)
        head = 2 * pair + hh
        for i in range(n_q):
            kq_ref[0] = l2n(k_t[i][rows])
            kq_ref[1] = l2n(q_t[i][rows]) * (hd ** -0.5)
            beta = gt_ref[0, i, pl.ds(head, 1), :]
            decay = gt_ref[1, i, pl.ds(N_HEADS + head, 1), :]
            src = s_ref if i == 0 else sout_ref

            def ks_body(kk, acc):
                return acc + kq_ref[0, pl.ds(kk, 1), :] * src[hh, kk]

            k_s = lax.fori_loop(0, hd, ks_body, jnp.zeros((hd, k_t[i].shape[1]), F32), unroll=8)
            r = beta * (v_t[i][rows] - decay * k_s)

            def upd_body(kk, acc):
                s_new = decay * src[hh, kk] + kq_ref[0, pl.ds(kk, 1), :] * r
                sout_ref[hh, kk] = s_new
                return acc + kq_ref[1, pl.ds(kk, 1), :] * s_new

            o = lax.fori_loop(0, hd, upd_body, jnp.zeros_like(r), unroll=8)
            ot_ref[i, rows, :] = o * lax.rsqrt(jnp.mean(o * o, axis=0, keepdims=True) + 1e-6) * nw_ref[...]

    for i in range(n_q):
        o_ref[i] = (ot_ref[i].T * _silu(z_ref[i])).astype(o_ref.dtype)


def _gdn_sample(proj_s, conv_state, state_t, conv_w, a_log, dt_bias, norm_w):
    n_q, bsz, _ = proj_s.shape
    hd = HEAD_DIM
    pw = 2 * hd
    n_pairs = N_HEADS // 2
    base = D_CONV // pw
    col = jnp.zeros((LANES, 1), F32)
    alog_col = col.at[N_HEADS:2 * N_HEADS, 0].set(a_log)
    dtb_col = col.at[N_HEADS:2 * N_HEADS, 0].set(dt_bias)
    nw_col = norm_w.reshape(hd, 1)

    def xspec(part):
        return pl.BlockSpec((n_q, bsz, pw), lambda p: (0, 0, base + part * n_pairs + p))

    def cspec(rows, part):
        return pl.BlockSpec((rows, bsz, pw) if rows else (GDN_CONV, pw),
                            (lambda p: (0, 0, part * n_pairs + p)) if rows
                            else (lambda p: (0, part * n_pairs + p)))

    const = lambda p: (0, 0)
    state_spec = pl.BlockSpec((2, hd, hd, bsz), lambda p: (p, 0, 0, 0))
    return pl.pallas_call(
        _gdn_sample_kernel,
        grid=(n_pairs,),
        in_specs=[xspec(0), xspec(1), xspec(2),
                  cspec(GDN_CONV - 1, 0), cspec(GDN_CONV - 1, 1), cspec(GDN_CONV - 1, 2),
                  cspec(0, 0), cspec(0, 1), cspec(0, 2),
                  pl.BlockSpec((n_q, bsz, pw), lambda p: (0, 0, D_IN_MAIN // pw - n_pairs + p)),
                  pl.BlockSpec((n_q, bsz, LANES), lambda p: (0, 0, D_IN_MAIN // LANES)),
                  pl.BlockSpec((LANES, 1), const), pl.BlockSpec((LANES, 1), const),
                  pl.BlockSpec((hd, 1), const),
                  state_spec],
        out_specs=[pl.BlockSpec((n_q, bsz, pw), lambda p: (0, 0, p)), state_spec],
        out_shape=[jax.ShapeDtypeStruct((n_q, bsz, D_HEADS), BF16),
                   jax.ShapeDtypeStruct(state_t.shape, F32)],
        scratch_shapes=[pltpu.VMEM((2, hd, bsz), F32),
                        pltpu.VMEM((2, n_q, LANES, bsz), F32),
                        pltpu.VMEM((n_q, pw, bsz), F32)],
        compiler_params=_cparams(1),
        name="gdn_sample",
    )(proj_s, proj_s, proj_s, conv_state, conv_state, conv_state, conv_w, conv_w, conv_w,
      proj_s, proj_s, alog_col, dtb_col, nw_col, state_t)


TM = 512
TM_MOE = 1024
TJ_MOE = 512
GDN_ROWS = 256
SAMPLE_PAD = 8


def kernel(x_prompt, x_sample, cache_k, cache_v, state_conv, state_gdn, page_table, c_prompt, c_sample,
           ln_in_g, ln_in_b, w_mod, b_mod, w_in, conv_w, a_log, dt_bias, gdn_norm_w, w_out, rel_table,
           ln_g, ln_b, ffn_w_gate, ffn_w_up, ffn_w_down, moe_router, moe_w_gate, moe_w_up, moe_w_down):
    bp, seq, d = x_prompt.shape
    bs, n_q, _ = x_sample.shape
    depth = w_in.shape[0]
    tp, ts = bp * seq, bs * n_q
    h, hd = N_HEADS, HEAD_DIM
    assert bs == MOD_ROWS and seq % TM == 0 and ts % TM == 0 and seq % MOBA_BLOCK == 0
    assert depth == 2 and GDN_CONV - 1 <= n_q <= SAMPLE_PAD and GDN_CHUNK == HEAD_DIM
    alpha = (2 * depth) ** 0.25

    def groups(tm):
        return lambda i: jnp.minimum(i // (seq // tm), bp)

    n_c = bp + bs
    c_all = jnp.pad(jnp.concatenate([c_prompt, c_sample]), ((0, (-n_c) % 8), (0, 0)))
    mod = _mod_vectors(c_all, w_mod, b_mod).reshape(depth, -1, 6, d)
    mod_p = jnp.broadcast_to(mod[:, :bp].transpose(0, 2, 1, 3)[:, :, :, None, :],
                             (depth, 6, bp, MOD_ROWS, d))
    mod_s = mod[:, bp:n_c].transpose(0, 2, 1, 3)[:, :, None]
    mod_all = jnp.concatenate([mod_p, mod_s], axis=2)

    x, u = _ln_mod(x_prompt.reshape(tp, d), x_sample.transpose(1, 0, 2).reshape(ts, d),
                   ln_in_g, ln_in_b, mod_all, 0, TM, groups(TM))

    bias_tab = _moba_bias_tables(rel_table)
    past_len = page_table.shape[1] * PAGE_SIZE
    bias_past, bias_own = _moba_sample_bias(rel_table, past_len, n_q)
    cache_kt = cache_k.transpose(0, 1, 3, 4, 2)
    cache_vt = cache_v.transpose(0, 1, 3, 4, 2)
    state_t = state_gdn.transpose(0, 2, 3, 4, 1)
    conv_t = state_conv.transpose(0, 2, 1, 3)

    k_p, v_p, conv_p, gdn_p, k_s, v_s, conv_s, gdn_s = [], [], [], [], [], [], [], []
    for layer in range(depth):
        w_l = w_in[layer]
        w_cat = jnp.concatenate(
            [w_l[:, :D_IN_MAIN], jnp.pad(w_l[:, D_IN_MAIN:], ((0, 0), (0, LANES - 2 * h)))],
            axis=1).astype(BF16)
        proj, qkv, k_t, v_t = _proj_in(u, w_cat, TM, bp, seq)
        proj_s = proj[tp:].reshape(n_q, bs, -1)

        def slots_s(cols):
            return jnp.pad(proj_s[:, :, cols].transpose(1, 0, 2), ((0, 0), (0, SAMPLE_PAD - n_q), (0, 0)))

        k_sr = proj_s[:, :, D_HEADS:2 * D_HEADS].transpose(1, 0, 2).reshape(bs, n_q, h, hd)
        v_sr = proj_s[:, :, 2 * D_HEADS:3 * D_HEADS].transpose(1, 0, 2).reshape(bs, n_q, h, hd)
        att_p = _moba_prompt(qkv, bias_tab, bp, seq)
        att_s = _moba_sample(slots_s(slice(0, D_HEADS)), slots_s(slice(D_HEADS, 2 * D_HEADS)),
                             slots_s(slice(2 * D_HEADS, 3 * D_HEADS)), cache_kt, cache_vt, page_table,
                             layer, bias_past, bias_own, n_q)
        att_s = att_s.reshape(bs, h, SAMPLE_PAD, hd)[:, :, :n_q].transpose(2, 0, 1, 3)
        att = (att_p, att_s.reshape(ts, D_HEADS).astype(BF16))

        o_p, conv_new_p, s_new_p = _gdn_prompt(proj, bp, seq, GDN_ROWS, conv_w[layer], a_log[layer],
                                               dt_bias[layer], gdn_norm_w[layer])
        o_s, s_new_s = _gdn_sample(proj_s, conv_t[layer], state_t[layer], conv_w[layer], a_log[layer],
                                   dt_bias[layer], gdn_norm_w[layer])
        s_new_s = s_new_s.transpose(3, 0, 1, 2)
        conv_new_s = proj_s[n_q - (GDN_CONV - 1):, :, 3 * D_HEADS:3 * D_HEADS + D_CONV].transpose(1, 0, 2)
        gdn_o = (o_p, o_s.reshape(ts, D_HEADS))

        w_o = w_out[layer].astype(BF16)
        i = layer // 2
        if layer % 2 == 0:
            x, u = _res_ln([att, gdn_o], [w_o[:D_HEADS], w_o[D_HEADS:]], x, mod_all, layer, 2, layer,
                           (4, 3), ln_g[layer, 0], ln_b[layer, 0], alpha, TM, groups(TM))
            hdn = _ffn_up(u, ffn_w_gate[i].astype(BF16), ffn_w_up[i].astype(BF16), TM)
            x, u = _res_ln([hdn], [ffn_w_down[i].astype(BF16)], x, mod_all, layer, 5, layer + 1,
                           (1, 0), ln_g[layer, 1], ln_b[layer, 1], alpha, TM, groups(TM))
        else:
            w_r = jnp.pad(moe_router[i], ((0, 0), (0, LANES - N_EXPERTS)))
            x, u, route = _res_ln([att, gdn_o], [w_o[:D_HEADS], w_o[D_HEADS:]], x, mod_all, layer, 2,
                                  layer, (4, 3), ln_g[layer, 0], ln_b[layer, 0], alpha, TM,
                                  groups(TM), w_router=w_r)
            tile_expert, n_valid, token_of_slot, slot = _moe_routing(route, TM_MOE)
            y = _moe_ffn(u, tile_expert, n_valid, token_of_slot, _column_tiles(moe_w_gate[i], TJ_MOE),
                         _column_tiles(moe_w_up[i], TJ_MOE), moe_w_down[i], TM_MOE)
            n_t = (tp + ts) // TM
            slots = slot.reshape(n_t, TM, TOP_K).transpose(0, 2, 1).reshape(n_t, 1, TOP_K * TM)
            x_p, x_s = _moe_combine(y, slots, route, x, mod_all, layer, 5, ln_g[layer, 1],
                                    ln_b[layer, 1], alpha, TM, groups(TM), tp // TM)

        k_p.append(k_t)
        v_p.append(v_t)
        conv_p.append(conv_new_p)
        gdn_p.append(s_new_p)
        k_s.append(k_sr)
        v_s.append(v_sr)
        conv_s.append(conv_new_s)
        gdn_s.append(s_new_s)

    y_prompt = x_p.reshape(bp, seq, d)
    y_sample = x_s.reshape(n_q, bs, d).transpose(1, 0, 2)
    k_prompt = jnp.stack(k_p).transpose(0, 1, 4, 2, 3)
    v_prompt = jnp.stack(v_p).transpose(0, 1, 4, 2, 3)
    return (y_prompt, y_sample, k_prompt, v_prompt, jnp.stack(conv_p), jnp.stack(gdn_p),
            jnp.stack(k_s), jnp.stack(v_s), jnp.stack(conv_s), jnp.stack(gdn_s))
```

```python
import functools
import math

import numpy as np
import jax
import jax.numpy as jnp
from jax import lax
from jax.experimental import pallas as pl
from jax.experimental.pallas import tpu as pltpu

F32 = jnp.float32
BF16 = jnp.bfloat16

D_MODEL = 1024
HEAD_DIM = 64
N_HEADS = 8
D_HEADS = N_HEADS * HEAD_DIM
D_CONV = 3 * D_HEADS
D_IN = 3 * D_HEADS + D_CONV + D_HEADS + 2 * N_HEADS
D_IN_MAIN = D_IN - 2 * N_HEADS
LANES = 128
D_IN_PAD = D_IN_MAIN + LANES
MOBA_BLOCK = 256
MOBA_TOPK = 3
PAGE_SIZE = 128
GDN_CONV = 4
GDN_CHUNK = 64
N_BUCKETS = 32
MAX_DISTANCE = 128
N_EXPERTS = 8
TOP_K = 2
LN_EPS = 1e-5
NEG = -1e30
MOD_ROWS = 128
VMEM_LIMIT = 56 * 1024 * 1024

HIGHEST = lax.Precision.HIGHEST


def _cparams(n_axes):
    return pltpu.CompilerParams(dimension_semantics=("arbitrary",) * n_axes,
                                vmem_limit_bytes=VMEM_LIMIT)


def _silu(x):
    return x * (1.0 / (1.0 + jnp.exp(-x)))


def _layer_norm(x, g, b):
    mu = jnp.mean(x, axis=-1, keepdims=True)
    xc = x - mu
    var = jnp.mean(xc * xc, axis=-1, keepdims=True)
    return xc * lax.rsqrt(var + LN_EPS) * g + b


def _modulate(x, scale, shift):
    tm, d = x.shape
    x3 = x.reshape(tm // MOD_ROWS, MOD_ROWS, d)
    return (x3 * (1.0 + scale[None]) + shift[None]).reshape(tm, d)


def _gated(x, gate, h):
    tm, d = x.shape
    x3 = x.reshape(tm // MOD_ROWS, MOD_ROWS, d)
    h3 = h.reshape(tm // MOD_ROWS, MOD_ROWS, d)
    return (x3 + (1.0 + gate[None]) * h3).reshape(tm, d)


def _mod_kernel(c_ref, w_ref, b_ref, o_ref):
    a = _silu(c_ref[...])
    o_ref[...] = jnp.dot(a, w_ref[...], precision=HIGHEST, preferred_element_type=F32) + b_ref[...]


def _mod_vectors(c_all, w_mod, b_mod, tn=1536):
    depth, d, n = w_mod.shape
    rows = c_all.shape[0]
    return pl.pallas_call(
        _mod_kernel,
        grid=(depth, n // tn),
        in_specs=[pl.BlockSpec((rows, d), lambda l, j: (0, 0)),
                  pl.BlockSpec((None, d, tn), lambda l, j: (l, 0, j)),
                  pl.BlockSpec((None, 1, tn), lambda l, j: (l, 0, j))],
        out_specs=pl.BlockSpec((None, rows, tn), lambda l, j: (l, 0, j)),
        out_shape=jax.ShapeDtypeStruct((depth, rows, n), F32),
        compiler_params=_cparams(2),
        name="mod_vectors",
    )(c_all, w_mod, b_mod.reshape(depth, 1, n))


def _ln_mod_kernel(xa_ref, xb_ref, g_ref, b_ref, mod_ref, xn_ref, u_ref, *, n_first):
    x = jnp.where(pl.program_id(0) < n_first, xa_ref[...], xb_ref[...])
    xn = _layer_norm(x, g_ref[...], b_ref[...])
    xn_ref[...] = xn
    u_ref[...] = _modulate(xn, mod_ref[1], mod_ref[0]).astype(BF16)


def _mod_spec(layer, group_of_tile):
    return pl.BlockSpec((None, 6, None, MOD_ROWS, D_MODEL),
                        lambda i: (layer, 0, group_of_tile(i), 0, 0))


def _ln_mod(x_first, x_rest, g, b, mod_all, layer, tm, group_of_tile):
    d = x_first.shape[1]
    n_first = x_first.shape[0] // tm
    t = x_first.shape[0] + x_rest.shape[0]
    row = pl.BlockSpec((tm, d), lambda i: (i, 0))
    vec = pl.BlockSpec((1, d), lambda i: (0, 0))
    return pl.pallas_call(
        functools.partial(_ln_mod_kernel, n_first=n_first),
        grid=(t // tm,),
        in_specs=[pl.BlockSpec((tm, d), lambda i: (jnp.minimum(i, n_first - 1), 0)),
                  pl.BlockSpec((tm, d), lambda i: (jnp.maximum(i - n_first, 0), 0)),
                  vec, vec, _mod_spec(layer, group_of_tile)],
        out_specs=[row, row],
        out_shape=[jax.ShapeDtypeStruct((t, d), F32), jax.ShapeDtypeStruct((t, d), BF16)],
        compiler_params=_cparams(1),
        name="ln_mod",
    )(x_first, x_rest, g.reshape(1, d), b.reshape(1, d), mod_all)


def _proj_kernel(u_ref, w_ref, proj_ref, qkv_ref, kt_ref, vt_ref, *, n_first):
    u = u_ref[...]
    n = w_ref.shape[1]
    chunk = D_HEADS
    n_qkv = qkv_ref.shape[1]
    is_prompt = pl.program_id(0) < n_first
    for c0 in range(0, n, chunk):
        c1 = min(c0 + chunk, n)
        r = jnp.dot(u, w_ref[:, c0:c1], preferred_element_type=F32)
        proj_ref[:, c0:c1] = r
        if c1 <= n_qkv:
            qkv_ref[:, c0:c1] = r.astype(BF16)
        for col, t_ref in ((D_HEADS, kt_ref), (2 * D_HEADS, vt_ref)):
            if c0 == col:
                @pl.when(is_prompt)
                def _(r=r, t_ref=t_ref):
                    t_ref[...] = r.T.reshape(t_ref.shape)


def _proj_in(u, w, tm, batch, seq):
    t, d = u.shape
    n = w.shape[1]
    per_seq = seq // tm
    n_first = batch * per_seq

    def t_spec():
        return pl.BlockSpec((None, N_HEADS, HEAD_DIM, tm),
                            lambda i: (jnp.minimum(i, n_first - 1) // per_seq, 0, 0,
                                       jnp.minimum(i, n_first - 1) % per_seq))

    t_shape = jax.ShapeDtypeStruct((batch, N_HEADS, HEAD_DIM, seq), F32)
    return pl.pallas_call(
        functools.partial(_proj_kernel, n_first=n_first),
        grid=(t // tm,),
        in_specs=[pl.BlockSpec((tm, d), lambda i: (i, 0)),
                  pl.BlockSpec((d, n), lambda i: (0, 0))],
        out_specs=[pl.BlockSpec((tm, n), lambda i: (i, 0)),
                   pl.BlockSpec((tm, 3 * D_HEADS), lambda i: (i, 0)),
                   t_spec(), t_spec()],
        out_shape=[jax.ShapeDtypeStruct((t, n), F32),
                   jax.ShapeDtypeStruct((t, 3 * D_HEADS), BF16),
                   t_shape, t_shape],
        compiler_params=_cparams(1),
        name="proj_in",
    )(u, w)


def _top2_gates(logits):
    lane = lax.broadcasted_iota(jnp.int32, logits.shape, 1)
    v1 = jnp.max(logits, axis=-1, keepdims=True)
    i1 = jnp.min(jnp.where(logits == v1, lane, LANES), axis=-1, keepdims=True)
    rest = jnp.where(lane == i1, -jnp.inf, logits)
    v2 = jnp.max(rest, axis=-1, keepdims=True)
    i2 = jnp.min(jnp.where(rest == v2, lane, LANES), axis=-1, keepdims=True)
    e2 = jnp.exp(v2 - v1)
    inv = 1.0 / (1.0 + e2)
    route = jnp.where(lane == 0, i1.astype(F32), 0.0) + jnp.where(lane == 1, i2.astype(F32), 0.0)
    return route + jnp.where(lane == 2, inv, 0.0) + jnp.where(lane == 3, e2 * inv, 0.0)


def _res_ln_kernel(*refs, split, n_first, alpha, next_rows, with_router):
    n_in = len(split)
    a_vals, pos = [], 0
    for two in split:
        if two:
            a_vals.append(jnp.where(pl.program_id(0) < n_first, refs[pos][...], refs[pos + 1][...]))
        else:
            a_vals.append(refs[pos][...])
        pos += 2 if two else 1
    w_refs = refs[pos:pos + n_in]
    x_ref, modg_ref, modn_ref, g_ref, b_ref = refs[pos + n_in:pos + n_in + 5]
    pos += n_in + 5
    if with_router:
        wr_ref = refs[pos]
        pos += 1
    outs = refs[pos:]
    h = jnp.dot(a_vals[0], w_refs[0][...], preferred_element_type=F32)
    for a_val, w_ref in zip(a_vals[1:], w_refs[1:]):
        h = h + jnp.dot(a_val, w_ref[...], preferred_element_type=F32)
    xn = _layer_norm(_gated(alpha * x_ref[...], modg_ref[...], h), g_ref[...], b_ref[...])
    outs[0][...] = xn
    if next_rows is not None:
        u = _modulate(xn, modn_ref[next_rows[0]], modn_ref[next_rows[1]])
        outs[1][...] = u.astype(outs[1].dtype)
        if with_router:
            logits = jnp.dot(u, wr_ref[...], precision=HIGHEST, preferred_element_type=F32)
            lane = lax.broadcasted_iota(jnp.int32, logits.shape, 1)
            outs[2][...] = _top2_gates(jnp.where(lane < N_EXPERTS, logits, -jnp.inf))


def _res_ln(a_list, w_list, x, mod_all, gate_layer, gate_row, next_layer, next_rows, ln_g, ln_b,
            alpha, tm, group_of_tile, w_router=None):
    t, d = x.shape
    row = pl.BlockSpec((tm, d), lambda i: (i, 0))
    vec = pl.BlockSpec((1, d), lambda i: (0, 0))
    split = tuple(isinstance(a, tuple) for a in a_list)
    n_first = a_list[split.index(True)][0].shape[0] // tm if any(split) else 0
    in_specs, flat_a = [], []
    for a in a_list:
        if isinstance(a, tuple):
            in_specs += [pl.BlockSpec((tm, a[0].shape[1]), lambda i: (jnp.minimum(i, n_first - 1), 0)),
                         pl.BlockSpec((tm, a[1].shape[1]), lambda i: (jnp.maximum(i - n_first, 0), 0))]
            flat_a += list(a)
        else:
            in_specs.append(pl.BlockSpec((tm, a.shape[1]), lambda i: (i, 0)))
            flat_a.append(a)
    in_specs += [pl.BlockSpec(w.shape, lambda i: (0, 0)) for w in w_list]
    in_specs += [row,
                 pl.BlockSpec((None, None, None, MOD_ROWS, d),
                              lambda i: (gate_layer, gate_row, group_of_tile(i), 0, 0)),
                 _mod_spec(next_layer if next_rows is not None else gate_layer, group_of_tile),
                 vec, vec]
    args = flat_a + list(w_list) + [x, mod_all, mod_all, ln_g.reshape(1, d), ln_b.reshape(1, d)]
    out_specs = [row]
    out_shape = [jax.ShapeDtypeStruct((t, d), F32)]
    with_router = w_router is not None
    if next_rows is not None:
        out_specs.append(row)
        out_shape.append(jax.ShapeDtypeStruct((t, d), F32 if with_router else BF16))
    if with_router:
        in_specs.append(pl.BlockSpec(w_router.shape, lambda i: (0, 0)))
        args.append(w_router)
        out_specs.append(pl.BlockSpec((tm, LANES), lambda i: (i, 0)))
        out_shape.append(jax.ShapeDtypeStruct((t, LANES), F32))
    return pl.pallas_call(
        functools.partial(_res_ln_kernel, split=split, n_first=n_first, alpha=alpha,
                          next_rows=next_rows, with_router=with_router),
        grid=(t // tm,),
        in_specs=in_specs,
        out_specs=out_specs,
        out_shape=out_shape,
        compiler_params=_cparams(1),
        name="res_ln",
    )(*args)


def _ffn_up_kernel(u_ref, wg_ref, wu_ref, o_ref, *, chunk):
    u = u_ref[...]
    n = o_ref.shape[1]
    for c0 in range(0, n, chunk):
        c1 = min(c0 + chunk, n)
        hg = jnp.dot(u, wg_ref[:, c0:c1], preferred_element_type=F32)
        hu = jnp.dot(u, wu_ref[:, c0:c1], preferred_element_type=F32)
        o_ref[:, c0:c1] = (_silu(hg) * hu).astype(BF16)


def _ffn_up(u, wg, wu, tm, chunk=256):
    t, d = u.shape
    n = wg.shape[1]
    wspec = pl.BlockSpec((d, n), lambda i: (0, 0))
    return pl.pallas_call(
        functools.partial(_ffn_up_kernel, chunk=chunk),
        grid=(t // tm,),
        in_specs=[pl.BlockSpec((tm, d), lambda i: (i, 0)), wspec, wspec],
        out_specs=pl.BlockSpec((tm, n), lambda i: (i, 0)),
        out_shape=jax.ShapeDtypeStruct((t, n), BF16),
        compiler_params=_cparams(1),
        name="ffn_up",
    )(u, wg, wu)


def _row_copy(src_hbm, src_row, dst_ref, r, sem):
    return pltpu.make_async_copy(src_hbm.at[pl.ds(src_row, 1)], dst_ref.at[pl.ds(r, 1)], sem)


def _gather_start(idx_ref, idx_base, src_hbm, dst_ref, sem, n_rows):
    def body(r, carry):
        _row_copy(src_hbm, idx_ref[0, idx_base + r], dst_ref, r, sem).start()
        return carry

    lax.fori_loop(0, n_rows, body, 0, unroll=8)


def _gather_wait(src_hbm, dst_ref, sem):
    pltpu.make_async_copy(src_hbm.at[pl.ds(0, dst_ref.shape[0])], dst_ref, sem).wait()


def _moe_ffn_kernel(te_ref, nv_ref, tos_ref, tos_next_ref, u_hbm, wg_ref, wu_ref, wd_ref, y_ref,
                    xg_ref, xb_ref, acc_ref, sem):
    del te_ref
    i = pl.program_id(0)
    j = pl.program_id(1)
    n_valid = nv_ref[0]
    valid = i < n_valid
    cur = i % 2
    rows = xb_ref.shape[0]

    @pl.when(valid & (i == 0) & (j == 0))
    def _():
        _gather_start(tos_ref, 0, u_hbm, xg_ref.at[0], sem.at[0], rows)

    @pl.when(valid & (j == 0))
    def _():
        _gather_wait(u_hbm, xg_ref.at[cur], sem.at[cur])
        xb_ref[...] = xg_ref[cur].astype(BF16)
        acc_ref[...] = jnp.zeros_like(acc_ref)

    @pl.when((i + 1 < n_valid) & (j == 0))
    def _():
        _gather_start(tos_next_ref, 0, u_hbm, xg_ref.at[1 - cur], sem.at[1 - cur], rows)

    @pl.when(valid)
    def _():
        x = xb_ref[...]
        hg = jnp.dot(x, wg_ref[...].astype(BF16), preferred_element_type=F32)
        hu = jnp.dot(x, wu_ref[...].astype(BF16), preferred_element_type=F32)
        acc_ref[...] += jnp.dot((_silu(hg) * hu).astype(BF16), wd_ref[...].astype(BF16),
                                preferred_element_type=F32)

    last = j == pl.num_programs(1) - 1

    @pl.when(valid & last)
    def _():
        y_ref[...] = acc_ref[...]

    @pl.when(jnp.logical_not(valid) & last)
    def _():
        y_ref[...] = jnp.zeros_like(y_ref)


def _moe_ffn(u, tile_expert, n_valid, token_of_slot, wg, wu, wd, tm, tj):
    n_tiles = tile_expert.shape[0]
    d = u.shape[1]
    f = wg.shape[2]
    assert f % tj == 0
    nj = f // tj

    def jj(i, j, nv):
        return jnp.where(i < nv[0], j, nj - 1)

    grid_spec = pltpu.PrefetchScalarGridSpec(
        num_scalar_prefetch=2,
        grid=(n_tiles, nj),
        in_specs=[pl.BlockSpec((None, 1, tm), lambda i, j, te, nv: (i, 0, 0), memory_space=pltpu.SMEM),
                  pl.BlockSpec((None, 1, tm), lambda i, j, te, nv: (jnp.minimum(i + 1, n_tiles - 1), 0, 0),
                               memory_space=pltpu.SMEM),
                  pl.BlockSpec(memory_space=pl.ANY),
                  pl.BlockSpec((None, d, tj), lambda i, j, te, nv: (te[i], 0, jj(i, j, nv))),
                  pl.BlockSpec((None, d, tj), lambda i, j, te, nv: (te[i], 0, jj(i, j, nv))),
                  pl.BlockSpec((None, tj, d), lambda i, j, te, nv: (te[i], jj(i, j, nv), 0))],
        out_specs=pl.BlockSpec((tm, d), lambda i, j, te, nv: (i, 0)),
        scratch_shapes=[pltpu.VMEM((2, tm, d), F32), pltpu.VMEM((tm, d), BF16),
                        pltpu.VMEM((tm, d), F32), pltpu.SemaphoreType.DMA((2,))])
    return pl.pallas_call(
        _moe_ffn_kernel,
        grid_spec=grid_spec,
        out_shape=jax.ShapeDtypeStruct((n_tiles * tm, d), F32),
        compiler_params=_cparams(2),
        name="moe_ffn",
    )(tile_expert, n_valid, token_of_slot, token_of_slot, u, wg, wu, wd)


def _moe_combine_kernel(slots_ref, slots_next_ref, y_hbm, route_ref, x_ref, modg_ref, g_ref, b_ref,
                        oa_ref, ob_ref, buf_ref, sem, *, alpha, n_first):
    rows = x_ref.shape[0]
    i = pl.program_id(0)
    cur = i % 2

    @pl.when(i == 0)
    def _():
        for k in range(TOP_K):
            _gather_start(slots_ref, k * rows, y_hbm, buf_ref.at[0, k], sem.at[0], rows)

    for k in range(TOP_K):
        _gather_wait(y_hbm, buf_ref.at[cur, k], sem.at[cur])

    @pl.when(i + 1 < pl.num_programs(0))
    def _():
        for k in range(TOP_K):
            _gather_start(slots_next_ref, k * rows, y_hbm, buf_ref.at[1 - cur, k], sem.at[1 - cur], rows)

    route = route_ref[...]
    f = route[:, 2:3] * buf_ref[cur, 0] + route[:, 3:4] * buf_ref[cur, 1]
    o = _layer_norm(_gated(alpha * x_ref[...], modg_ref[...], f), g_ref[...], b_ref[...])

    @pl.when(pl.program_id(0) < n_first)
    def _():
        oa_ref[...] = o

    @pl.when(pl.program_id(0) >= n_first)
    def _():
        ob_ref[...] = o


def _moe_combine(y, slots, route, x, mod_all, gate_layer, gate_row, ln_g, ln_b, alpha, tm,
                 group_of_tile, n_first):
    t, d = x.shape
    row = pl.BlockSpec((tm, d), lambda i: (i, 0))
    vec = pl.BlockSpec((1, d), lambda i: (0, 0))
    return pl.pallas_call(
        functools.partial(_moe_combine_kernel, alpha=alpha, n_first=n_first),
        grid=(t // tm,),
        in_specs=[pl.BlockSpec((None, 1, 2 * tm), lambda i: (i, 0, 0), memory_space=pltpu.SMEM),
                  pl.BlockSpec((None, 1, 2 * tm), lambda i: (jnp.minimum(i + 1, t // tm - 1), 0, 0),
                               memory_space=pltpu.SMEM),
                  pl.BlockSpec(memory_space=pl.ANY),
                  pl.BlockSpec((tm, LANES), lambda i: (i, 0)),
                  row,
                  pl.BlockSpec((None, None, None, MOD_ROWS, d),
                               lambda i: (gate_layer, gate_row, group_of_tile(i), 0, 0)),
                  vec, vec],
        out_specs=[pl.BlockSpec((tm, d), lambda i: (jnp.minimum(i, n_first - 1), 0)),
                   pl.BlockSpec((tm, d), lambda i: (jnp.maximum(i - n_first, 0), 0))],
        out_shape=[jax.ShapeDtypeStruct((n_first * tm, d), F32),
                   jax.ShapeDtypeStruct((t - n_first * tm, d), F32)],
        scratch_shapes=[pltpu.VMEM((2, TOP_K, tm, d), F32), pltpu.SemaphoreType.DMA((2,))],
        compiler_params=_cparams(1),
        name="moe_combine",
    )(slots, slots, y, route, x, mod_all, ln_g.reshape(1, d), ln_b.reshape(1, d))


def _moe_routing(route, tm):
    t = route.shape[0]
    n_assign = TOP_K * t
    n_tiles = -(-n_assign // tm) + N_EXPERTS
    experts = route[:, :TOP_K].astype(jnp.int32).reshape(n_assign)
    onehot = (experts[:, None] == jnp.arange(N_EXPERTS, dtype=jnp.int32)[None, :]).astype(jnp.int32)
    rank = jnp.sum((jnp.cumsum(onehot, axis=0) - onehot) * onehot, axis=1)
    count = jnp.sum(onehot, axis=0)
    padded = (count + tm - 1) // tm * tm
    ends = jnp.cumsum(padded)
    slot = jnp.sum(onehot * (ends - padded)[None, :], axis=1) + rank
    n_valid = ends[-1] // tm
    starts = jnp.arange(n_tiles, dtype=jnp.int32) * tm
    tile_expert = jnp.minimum(jnp.sum((starts[:, None] >= ends[None, :]).astype(jnp.int32), axis=1),
                              N_EXPERTS - 1)
    tile_expert = jnp.where(starts < ends[-1], tile_expert, tile_expert[jnp.maximum(n_valid - 1, 0)])
    token_of_slot = jnp.zeros((n_tiles * tm,), jnp.int32).at[slot].set(
        jnp.arange(n_assign, dtype=jnp.int32) // TOP_K)
    return (tile_expert.astype(jnp.int32), n_valid.reshape(1).astype(jnp.int32),
            token_of_slot.reshape(n_tiles, 1, tm), slot.reshape(t, TOP_K).astype(jnp.int32))


def _rel_bucket_table(max_dist):
    n = np.arange(max_dist + 1)
    max_exact = N_BUCKETS // 2
    ratio = np.log(np.maximum(n, 1).astype(np.float32) / np.float32(max_exact))
    large = max_exact + (ratio / np.float32(math.log(MAX_DISTANCE / max_exact))
                         * np.float32(N_BUCKETS - max_exact)).astype(np.int32)
    large = np.minimum(large, N_BUCKETS - 1)
    return np.where(n < max_exact, n, large).astype(np.int32)


def _bias_lookup(rel_table, bucket_idx):
    onehot = jnp.asarray(np.eye(N_BUCKETS, dtype=np.float32)[bucket_idx.reshape(-1)])
    vals = jnp.dot(onehot, rel_table, precision=HIGHEST)
    return vals.T.reshape((rel_table.shape[1],) + bucket_idx.shape)


def _select_top_blocks(gate, n_valid, ksel):
    blk = lax.broadcasted_iota(jnp.int32, gate.shape, 0)
    nb = gate.shape[0]
    g = jnp.where(blk < n_valid, gate, -jnp.inf)
    keep = blk == n_valid
    for _ in range(ksel):
        mx = jnp.max(g, axis=0, keepdims=True)
        first = jnp.min(jnp.where(g == mx, blk, nb), axis=0, keepdims=True)
        hit = blk == first
        keep = keep | (hit & (blk < n_valid))
        g = jnp.where(hit, -jnp.inf, g)
    return jnp.where(keep, 0.0, NEG)


def _moba_prompt_kernel(q_ref, k_ref, v_ref, bias_ref, o_ref,
                        kmean_ref, vt_ref, qm_ref, sel_ref, m_ref, l_ref, acc_ref, *, ksel):
    qt = pl.program_id(1)
    seq = k_ref.shape[0]
    nb = seq // MOBA_BLOCK
    gw = 4 * HEAD_DIM
    n_groups = N_HEADS // 4
    nt = (((1,), (1,)), ((), ()))

    @pl.when(qt == 0)
    def _():
        row = lax.broadcasted_iota(jnp.int32, (nb, seq), 0)
        col = lax.broadcasted_iota(jnp.int32, (nb, seq), 1)
        pool = jnp.where(col // MOBA_BLOCK == row, 1.0, 0.0).astype(BF16)
        kmean_ref[...] = jnp.dot(pool, k_ref[...], preferred_element_type=F32) * (1.0 / MOBA_BLOCK)
        for n in range(nb):
            vt_ref[n] = v_ref[n * MOBA_BLOCK:(n + 1) * MOBA_BLOCK, :].astype(F32).T.astype(BF16)

    head_of_lane = lax.broadcasted_iota(jnp.int32, (1, gw), 1) // HEAD_DIM
    q = q_ref[...] * (HEAD_DIM ** -0.5)
    kmean = kmean_ref[...].astype(BF16)
    for h in range(N_HEADS):
        g, hh = divmod(h, 4)
        qm = jnp.where(head_of_lane == hh, q[:, g * gw:(g + 1) * gw], 0.0).astype(BF16)
        qm_ref[h] = qm
        gate = lax.dot_general(kmean[:, g * gw:(g + 1) * gw], qm, nt, preferred_element_type=F32)
        sel_ref[h] = _select_top_blocks(gate, qt, ksel)
        m_ref[h] = jnp.full(m_ref.shape[1:], NEG, F32)
        l_ref[h] = jnp.zeros(l_ref.shape[1:], F32)
    acc_ref[...] = jnp.zeros_like(acc_ref)

    def body(step, carry, near):
        n = qt - step
        start = pl.multiple_of(n * MOBA_BLOCK, MOBA_BLOCK)
        kb = k_ref[pl.ds(start, MOBA_BLOCK), :]
        vt = vt_ref[n]
        scores = [lax.dot_general(kb[:, (h // 4) * gw:(h // 4 + 1) * gw], qm_ref[h], nt,
                                  preferred_element_type=F32) for h in range(N_HEADS)]
        probs, alphas = [], []
        for h in range(N_HEADS):
            shift = sel_ref[h, pl.ds(n, 1), :]
            if near:
                s = scores[h] + bias_ref[h, step]
            else:
                s = scores[h]
                shift = shift + bias_ref[h, 2, 0:1, :]
            m_old = m_ref[h]
            m_new = jnp.maximum(m_old, jnp.max(s, axis=0, keepdims=True) + shift)
            alpha = jnp.exp(m_old - m_new)
            p = jnp.exp(s - (m_new - shift))
            l_ref[h] = alpha * l_ref[h] + jnp.sum(p, axis=0, keepdims=True)
            m_ref[h] = m_new
            probs.append(p.astype(BF16))
            alphas.append(alpha)
        outs = [jnp.dot(vt[h * HEAD_DIM:(h + 1) * HEAD_DIM, :], probs[h],
                        preferred_element_type=F32) for h in range(N_HEADS)]
        for h in range(N_HEADS):
            g, hh = divmod(h, 4)
            rows = slice(hh * HEAD_DIM, (hh + 1) * HEAD_DIM)
            acc_ref[g, rows, :] = alphas[h] * acc_ref[g, rows, :] + outs[h]
        return carry

    lax.fori_loop(0, jnp.minimum(qt + 1, 2), functools.partial(body, near=True), 0)
    lax.fori_loop(2, qt + 1, functools.partial(body, near=False), 0)

    for g in range(n_groups):
        o_t = jnp.concatenate(
            [acc_ref[g, hh * HEAD_DIM:(hh + 1) * HEAD_DIM, :] * (1.0 / l_ref[4 * g + hh])
             for hh in range(4)], axis=0)
        o_ref[:, g * gw:(g + 1) * gw] = o_t.T.astype(o_ref.dtype)


def _moba_bias_tables(rel_table):
    bucket = _rel_bucket_table(2 * MOBA_BLOCK)
    i = np.arange(MOBA_BLOCK)[None, :]
    j = np.arange(MOBA_BLOCK)[:, None]
    own = jnp.where(jnp.asarray(i >= j), _bias_lookup(rel_table, bucket[np.maximum(i - j, 0)]), NEG)
    prev = _bias_lookup(rel_table, bucket[MOBA_BLOCK + i - j])
    far = jnp.broadcast_to(rel_table[N_BUCKETS - 1][:, None, None], prev.shape)
    return jnp.stack([own, prev, far], axis=1).astype(F32)


def _moba_prompt(qkv, bias_tab, batch, seq):
    nb = seq // MOBA_BLOCK
    ksel = min(MOBA_TOPK, nb)
    stat = pltpu.VMEM((N_HEADS, 1, MOBA_BLOCK), F32)
    return pl.pallas_call(
        functools.partial(_moba_prompt_kernel, ksel=ksel),
        grid=(batch, nb),
        in_specs=[pl.BlockSpec((MOBA_BLOCK, D_HEADS), lambda b, t: (b * nb + t, 0)),
                  pl.BlockSpec((seq, D_HEADS), lambda b, t: (b, 1)),
                  pl.BlockSpec((seq, D_HEADS), lambda b, t: (b, 2)),
                  pl.BlockSpec(bias_tab.shape, lambda b, t: (0, 0, 0, 0))],
        out_specs=pl.BlockSpec((MOBA_BLOCK, D_HEADS), lambda b, t: (b * nb + t, 0)),
        out_shape=jax.ShapeDtypeStruct((batch * seq, D_HEADS), BF16),
        scratch_shapes=[pltpu.VMEM((nb, D_HEADS), F32),
                        pltpu.VMEM((nb, D_HEADS, MOBA_BLOCK), BF16),
                        pltpu.VMEM((N_HEADS, MOBA_BLOCK, 4 * HEAD_DIM), BF16),
                        pltpu.VMEM((N_HEADS, nb, MOBA_BLOCK), F32),
                        stat, stat,
                        pltpu.VMEM((N_HEADS // 4, 4 * HEAD_DIM, MOBA_BLOCK), F32)],
        compiler_params=_cparams(2),
        name="moba_prompt",
    )(qkv, qkv, qkv, bias_tab)


def _moba_sample_kernel(pt_ref, q_ref, kn_ref, vn_ref, *refs, ksel, n_q, nbp, pages_per_block):
    del pt_ref
    n_pages = nbp * pages_per_block
    k_refs, v_refs = refs[:n_pages], refs[n_pages:2 * n_pages]
    bias_ref, bown_ref, o_ref, m_ref, l_ref, g_ref, acc_ref = refs[2 * n_pages:]
    nt = (((1,), (1,)), ((), ()))
    dh = N_HEADS * HEAD_DIM
    q8 = q_ref[...]
    slots = q8.shape[0]
    lane_head = lax.broadcasted_iota(jnp.int32, (1, dh), 1) // HEAD_DIM
    qbd = jnp.concatenate([jnp.where(lane_head == hp, q8, 0.0) for hp in range(N_HEADS)], axis=0)
    qs = (qbd * (HEAD_DIM ** -0.5)).astype(BF16)
    ones = jnp.ones((8, PAGE_SIZE), BF16)

    kts = [k_ref[...].reshape(dh, PAGE_SIZE).astype(BF16) for k_ref in k_refs]
    s_pages = [jnp.dot(qs, kt, preferred_element_type=F32) for kt in kts]
    ksums = [lax.dot_general(ones, kt, nt, preferred_element_type=F32) for kt in kts]
    probs = []
    for j in range(nbp):
        pages = range(j * pages_per_block, (j + 1) * pages_per_block)
        s = jnp.concatenate([s_pages[pg] for pg in pages], axis=1) + bias_ref[0 if j == nbp - 1 else 1]
        m = jnp.max(s, axis=-1, keepdims=True)
        p32 = jnp.exp(s - m)
        probs.append(p32.astype(BF16))
        m_ref[j] = m
        l_ref[j] = jnp.sum(p32, axis=-1, keepdims=True)
        ksum = functools.reduce(jnp.add, [ksums[pg] for pg in pages])
        g_ref[j] = jnp.sum(qbd * ksum[0:1, :], axis=-1, keepdims=True) * (1.0 / MOBA_BLOCK)
    pv = [lax.dot_general(probs[pg // pages_per_block][:, (pg % pages_per_block) * PAGE_SIZE:
                                                       (pg % pages_per_block + 1) * PAGE_SIZE],
                          v_refs[pg][...].reshape(dh, PAGE_SIZE).astype(BF16), nt,
                          preferred_element_type=F32) for pg in range(n_pages)]
    for j in range(nbp):
        acc_ref[j] = functools.reduce(jnp.add, pv[j * pages_per_block:(j + 1) * pages_per_block])

    s_own = [jnp.sum(qbd * kn_ref[i:i + 1, :], axis=-1, keepdims=True) * (HEAD_DIM ** -0.5)
             + bown_ref[:, i:i + 1] for i in range(n_q)]
    m_o = functools.reduce(jnp.maximum, s_own)
    p_own = [jnp.exp(s_i - m_o) for s_i in s_own]
    l_o = functools.reduce(jnp.add, p_own)
    acc_o = functools.reduce(jnp.add, [p_i * vn_ref[i:i + 1, :] for i, p_i in enumerate(p_own)])

    gg = g_ref[...]
    blk = lax.broadcasted_iota(jnp.int32, gg.shape, 0)
    keep = blk < 0
    for _ in range(ksel):
        mx = jnp.max(gg, axis=0, keepdims=True)
        first = jnp.min(jnp.where(gg == mx, blk, nbp), axis=0, keepdims=True)
        hit = (blk == first) & (gg > -jnp.inf)
        keep = keep | hit
        gg = jnp.where(blk == first, -jnp.inf, gg)
    m_all = m_ref[...]
    m_tot = jnp.maximum(jnp.max(jnp.where(keep, m_all, NEG), axis=0), m_o)
    w = jnp.where(keep, jnp.exp(m_all - m_tot[None]), 0.0)
    w_o = jnp.exp(m_o - m_tot)
    l_tot = jnp.sum(w * l_ref[...], axis=0) + w_o * l_o
    acc_tot = (jnp.sum(w * acc_ref[...], axis=0) + w_o * acc_o) * (1.0 / l_tot)
    row_head = lax.broadcasted_iota(jnp.int32, (N_HEADS * slots, 1), 0) // slots
    acc_tot = jnp.where(row_head == lane_head, acc_tot, 0.0)
    o_ref[...] = functools.reduce(
        jnp.add, [acc_tot[:, hp * HEAD_DIM:(hp + 1) * HEAD_DIM] for hp in range(N_HEADS)])


def _moba_sample_bias(rel_table, past_len, n_q):
    del past_len
    slots = SAMPLE_PAD
    bucket = _rel_bucket_table(MOBA_BLOCK + slots)
    rows_h = np.repeat(np.arange(N_HEADS), slots)
    rows_i = np.minimum(np.tile(np.arange(slots), N_HEADS), n_q - 1)
    pos = np.arange(MOBA_BLOCK)[None, :]
    by_head = _bias_lookup(rel_table, bucket[MOBA_BLOCK + rows_i[:, None] - pos])
    pick = jnp.asarray(np.arange(N_HEADS)[:, None, None] == rows_h[None, :, None])
    near = jnp.sum(jnp.where(pick, by_head, 0.0), axis=0)
    far = jnp.broadcast_to(jnp.repeat(rel_table[N_BUCKETS - 1], slots)[:, None], near.shape)
    past = jnp.stack([near, far]).astype(F32)
    new = np.arange(slots)[None, :]
    own_h = _bias_lookup(rel_table, bucket[np.maximum(rows_i[:, None] - new, 0)])
    own = jnp.sum(jnp.where(pick, own_h, 0.0), axis=0)
    ok = jnp.asarray((new <= rows_i[:, None]) & (new < n_q))
    return past, jnp.where(ok, own, NEG).astype(F32)


def _moba_sample(q_rows, k_new, v_new, cache_k, cache_v, page_table, layer, bias_past, bias_own, n_q):
    bsz, slots, dh = q_rows.shape
    hd = HEAD_DIM
    rows = N_HEADS * slots
    pages_per_block = MOBA_BLOCK // PAGE_SIZE
    nbp = page_table.shape[1] // pages_per_block
    ksel = min(MOBA_TOPK, nbp + 1)

    n_pages = nbp * pages_per_block

    def page_spec(page):
        return pl.BlockSpec((None, None, N_HEADS, hd, PAGE_SIZE),
                            lambda b, pt: (layer, pt[b, page], 0, 0, 0))

    per_seq = lambda b, pt: (b, 0, 0)
    stat = pltpu.VMEM((nbp, rows, 1), F32)
    grid_spec = pltpu.PrefetchScalarGridSpec(
        num_scalar_prefetch=1,
        grid=(bsz,),
        in_specs=([pl.BlockSpec((None, slots, dh), per_seq)] * 3
                  + [page_spec(page) for page in range(n_pages)] * 2
                  + [pl.BlockSpec(bias_past.shape, lambda b, pt: (0, 0, 0)),
                     pl.BlockSpec(bias_own.shape, lambda b, pt: (0, 0))]),
        out_specs=pl.BlockSpec((None, rows, hd), per_seq),
        scratch_shapes=[stat, stat, stat, pltpu.VMEM((nbp, rows, dh), F32)])
    return pl.pallas_call(
        functools.partial(_moba_sample_kernel, ksel=ksel, n_q=n_q, nbp=nbp,
                          pages_per_block=pages_per_block),
        grid_spec=grid_spec,
        out_shape=jax.ShapeDtypeStruct((bsz, rows, hd), F32),
        compiler_params=_cparams(1),
        name="moba_sample",
    )(page_table, q_rows, k_new, v_new, *([cache_k] * n_pages), *([cache_v] * n_pages),
      bias_past, bias_own)


def _softplus(x):
    return jnp.maximum(x, 0.0) + jnp.log1p(jnp.exp(-jnp.abs(x)))


def _bmm(a, b):
    return jnp.matmul(a.astype(BF16), b.astype(BF16), preferred_element_type=F32)


def _dot_f32(a, b):
    return jnp.dot(a, b, precision=HIGHEST, preferred_element_type=F32)


def _split_bf16(a):
    hi = a.astype(BF16)
    return hi, (a - hi.astype(F32)).astype(BF16)


def _bmm_3pass(a, b):
    a_hi, a_lo = _split_bf16(a)
    b_hi, b_lo = _split_bf16(b)
    return (jnp.matmul(a_hi, b_hi, preferred_element_type=F32)
            + jnp.matmul(a_hi, b_lo, preferred_element_type=F32)
            + jnp.matmul(a_lo, b_hi, preferred_element_type=F32))


def _unit_lower_inverse(a, row, col):
    c = a.shape[1]
    eye = jnp.where(row == col, 1.0, 0.0)
    in16 = (row // 16) == (col // 16)
    in32 = (row // 32) == (col // 32)
    nil = jnp.where(in16, -a, 0.0)
    x = eye + nil
    p = _bmm_3pass(nil, nil)
    for _ in range(2):
        xp = _bmm_3pass(jnp.concatenate([x, p], axis=1), p)
        x = x + xp[:, :c]
        p = xp[:, c:]
    x = x + _bmm_3pass(x, p)
    for off in (jnp.where(in32 & jnp.logical_not(in16), a, 0.0),
                jnp.where(in32, 0.0, a)):
        x = x - _bmm(x, _bmm(off, x))
    return x


def _gdn_prepare(q, k, v, k_t, g_col, g_row, beta, row, col):
    c = q.shape[1]
    incl = row >= col
    decay = jnp.where(incl, jnp.exp(jnp.where(incl, g_col - g_row, 0.0)), 0.0)
    qk_kk = _bmm(jnp.concatenate([q, k], axis=1), k_t)
    a_mat = jnp.where(row > col, qk_kk[:, c:] * decay * beta, 0.0)
    e_g = jnp.exp(g_col)
    rhs = jnp.concatenate([v * beta, k * beta * e_g], axis=2)
    sol = _bmm(_unit_lower_inverse(a_mat, row, col), rhs)
    dv = v.shape[2]
    g_last = g_col[:, c - 1:c, :]
    w_qd = jnp.concatenate([sol[:, :, dv:], q * e_g], axis=1).astype(BF16)
    qk = jnp.where(incl, qk_kk[:, :c] * decay, 0.0).astype(BF16)
    k_dec_t = (k_t * jnp.exp(g_last - g_row)).astype(BF16)
    return sol[:, :, :dv], w_qd, qk, k_dec_t, jnp.exp(g_last)


def _gdn_step(u, w_qd, qk, k_dec_t, chunk_decay, state):
    c = u.shape[1]
    ws_qs = jnp.matmul(w_qd, state.astype(BF16), preferred_element_type=F32)
    v_new = (u - ws_qs[:, :c]).astype(BF16)
    o = ws_qs[:, c:] + jnp.matmul(qk, v_new, preferred_element_type=F32)
    new_state = state * chunk_decay + jnp.matmul(k_dec_t, v_new, preferred_element_type=F32)
    return o, new_state


def _gdn_prompt_kernel(x_ref, z_ref, gab_ref, cw_ref, alog_ref, dtb_ref, nw_ref, seg_ref,
                       o_ref, conv_ref, sout_ref,
                       xbuf_ref, halo_ref, s_ref, u_ref, wqd_ref, qk_ref, kdt_ref, cd_ref, oh_ref):
    t = pl.program_id(1)
    tl = x_ref.shape[0]
    cs = GDN_CHUNK
    n_chunks = tl // cs
    hd = HEAD_DIM

    @pl.when(t == 0)
    def _():
        halo_ref[...] = jnp.zeros_like(halo_ref)
        s_ref[...] = jnp.zeros_like(s_ref)

    x = x_ref[...]
    xbuf_ref[0:8, :] = halo_ref[...]
    xbuf_ref[8:8 + tl, :] = x
    y = cw_ref[0:1, :] * xbuf_ref[5:5 + tl, :]
    for i in range(1, GDN_CONV):
        y = y + cw_ref[i:i + 1, :] * xbuf_ref[5 + i:5 + i + tl, :]
    y = _silu(y)
    tail = x[tl - 8:, :]
    halo_ref[...] = tail
    conv_ref[...] = tail[8 - (GDN_CONV - 1):, :]

    dh = N_HEADS * hd
    seg = seg_ref[...]

    def l2n(a):
        sq_hi, sq_lo = _split_bf16(a * a)
        ss = (jnp.dot(sq_hi, seg, preferred_element_type=F32)
              + jnp.dot(sq_lo, seg, preferred_element_type=F32))
        return a * lax.rsqrt(ss + 1e-6)

    qn = l2n(y[:, :dh]) * (hd ** -0.5)
    kn = l2n(y[:, dh:2 * dh])
    vv = y[:, 2 * dh:]
    kn_t = kn.T

    gab = gab_ref[...]
    beta = 1.0 / (1.0 + jnp.exp(-gab))
    glog = -jnp.exp(alog_ref[...]) * _softplus(gab + dtb_ref[...])

    row = lax.broadcasted_iota(jnp.int32, (cs, cs), 0)
    col = lax.broadcasted_iota(jnp.int32, (cs, cs), 1)
    tril = jnp.where(row >= col, 1.0, 0.0)
    pick = jnp.where(lax.broadcasted_iota(jnp.int32, (N_HEADS, LANES), 1)
                     == lax.broadcasted_iota(jnp.int32, (N_HEADS, LANES), 0) + N_HEADS, 1.0, 0.0)
    parts = {name: [] for name in ("q", "k", "v", "kt", "gc", "gr", "bb")}
    for c in range(n_chunks):
        rows = slice(c * cs, (c + 1) * cs)
        gcum = _dot_f32(tril, glog[rows])
        g_rows = lax.dot_general(pick, gcum, (((1,), (1,)), ((), ())), precision=HIGHEST,
                                 preferred_element_type=F32)
        for h in range(N_HEADS):
            lanes = slice(h * hd, (h + 1) * hd)
            parts["q"].append(qn[rows, lanes])
            parts["k"].append(kn[rows, lanes])
            parts["v"].append(vv[rows, lanes])
            parts["kt"].append(kn_t[lanes, rows])
            parts["gc"].append(jnp.broadcast_to(gcum[:, N_HEADS + h:N_HEADS + h + 1], (cs, cs)))
            parts["gr"].append(g_rows[h:h + 1, :])
            parts["bb"].append(jnp.broadcast_to(beta[rows, h:h + 1], (cs, cs)))
    st = {name: jnp.stack(vals) for name, vals in parts.items()}
    u_all, w_qd, qk, k_dec_t, chunk_decay = _gdn_prepare(
        st["q"], st["k"], st["v"], st["kt"], st["gc"], st["gr"], st["bb"], row, col)
    u_ref[...] = u_all.reshape(u_ref.shape)
    wqd_ref[...] = w_qd.reshape(wqd_ref.shape)
    qk_ref[...] = qk.reshape(qk_ref.shape)
    kdt_ref[...] = k_dec_t.reshape(kdt_ref.shape)
    cd_ref[...] = chunk_decay.reshape(cd_ref.shape)

    def chunk_body(c, carry):
        o, s_new = _gdn_step(u_ref[c], wqd_ref[c], qk_ref[c], kdt_ref[c], cd_ref[c], s_ref[...])
        oh_ref[c] = o
        s_ref[...] = s_new
        return carry

    lax.fori_loop(0, n_chunks, chunk_body, 0)

    z = z_ref[...]
    nw = nw_ref[...]
    for h in range(N_HEADS):
        lanes = slice(h * hd, (h + 1) * hd)
        o = jnp.concatenate([oh_ref[c, h] for c in range(n_chunks)], axis=0)
        o = o * lax.rsqrt(jnp.mean(o * o, axis=-1, keepdims=True) + 1e-6) * nw
        o_ref[:, lanes] = (o * _silu(z[:, lanes])).astype(o_ref.dtype)

    @pl.when(t == pl.num_programs(1) - 1)
    def _():
        sout_ref[...] = s_ref[...]


def _gdn_prompt(proj, batch, seq, tl, conv_w, a_log, dt_bias, norm_w):
    hd = HEAD_DIM
    cs = GDN_CHUNK
    n_steps = seq // tl
    n_chunks = tl // cs
    col_x, col_z, col_gab = 1, D_IN_MAIN // D_HEADS - 1, D_IN_MAIN // LANES
    lane = np.arange(D_HEADS)
    seg = jnp.asarray((lane[:, None] // hd == lane[None, :] // hd).astype(np.float32)).astype(BF16)
    alog_row = jnp.zeros((1, LANES), F32).at[0, N_HEADS:2 * N_HEADS].set(a_log)
    dtb_row = jnp.zeros((1, LANES), F32).at[0, N_HEADS:2 * N_HEADS].set(dt_bias)
    const2 = lambda b, t: (0, 0)
    per_inst = pltpu.VMEM((n_chunks, N_HEADS, cs, hd), F32)
    per_inst_bf16 = pltpu.VMEM((n_chunks, N_HEADS, hd, cs), BF16)
    return pl.pallas_call(
        _gdn_prompt_kernel,
        grid=(batch, n_steps),
        in_specs=[pl.BlockSpec((tl, D_CONV), lambda b, t: (b * n_steps + t, col_x)),
                  pl.BlockSpec((tl, D_HEADS), lambda b, t: (b * n_steps + t, col_z)),
                  pl.BlockSpec((tl, LANES), lambda b, t: (b * n_steps + t, col_gab)),
                  pl.BlockSpec((GDN_CONV, D_CONV), const2),
                  pl.BlockSpec((1, LANES), const2),
                  pl.BlockSpec((1, LANES), const2),
                  pl.BlockSpec((1, hd), const2),
                  pl.BlockSpec((D_HEADS, D_HEADS), const2)],
        out_specs=[pl.BlockSpec((tl, D_HEADS), lambda b, t: (b * n_steps + t, 0)),
                   pl.BlockSpec((None, GDN_CONV - 1, D_CONV), lambda b, t: (b, 0, 0)),
                   pl.BlockSpec((None, N_HEADS, hd, hd), lambda b, t: (b, 0, 0, 0))],
        out_shape=[jax.ShapeDtypeStruct((batch * seq, D_HEADS), BF16),
                   jax.ShapeDtypeStruct((batch, GDN_CONV - 1, D_CONV), F32),
                   jax.ShapeDtypeStruct((batch, N_HEADS, hd, hd), F32)],
        scratch_shapes=[pltpu.VMEM((8 + tl + 8, D_CONV), F32),
                        pltpu.VMEM((8, D_CONV), F32),
                        pltpu.VMEM((N_HEADS, hd, hd), F32),
                        per_inst,
                        pltpu.VMEM((n_chunks, N_HEADS, 2 * cs, hd), BF16),
                        per_inst_bf16, per_inst_bf16,
                        pltpu.VMEM((n_chunks, N_HEADS, 1, hd), F32),
                        per_inst],
        compiler_params=_cparams(2),
        name="gdn_prompt",
    )(proj, proj, proj, conv_w, alog_row, dtb_row, norm_w.reshape(1, hd), seg)


def _gdn_sample_kernel(xq_ref, xk_ref, xv_ref, cq_ref, ck_ref, cv_ref, wq_ref, wk_ref, wv_ref,
                       z_ref, gab_ref, alog_ref, dtb_ref, nw_ref, s_ref,
                       o_ref, sout_ref, kq_ref, gt_ref, ot_ref):
    pair = pl.program_id(0)
    n_q = xq_ref.shape[0]
    hd = HEAD_DIM

    def conv_t(x_ref, c_ref, w_ref):
        xp = [c_ref[i] for i in range(GDN_CONV - 1)] + [x_ref[i] for i in range(n_q)]
        out = []
        for i in range(n_q):
            y = w_ref[0:1, :] * xp[i]
            for m in range(1, GDN_CONV):
                y = y + w_ref[m:m + 1, :] * xp[i + m]
            out.append(_silu(y).T)
        return out

    q_t = conv_t(xq_ref, cq_ref, wq_ref)
    k_t = conv_t(xk_ref, ck_ref, wk_ref)
    v_t = conv_t(xv_ref, cv_ref, wv_ref)

    def l2n(a):
        return a * lax.rsqrt(jnp.sum(a * a, axis=0, keepdims=True) + 1e-6)

    for i in range(n_q):
        gab_t = gab_ref[i].T
        gt_ref[0, i] = 1.0 / (1.0 + jnp.exp(-gab_t))
        gt_ref[1, i] = jnp.exp(-jnp.exp(alog_ref[...]) * _softplus(gab_t + dtb_ref[...]))

    for hh in range(2):
        rows = slice(hh * hd, (hh + 1) * hd)
        head = 2 * pair + hh
        for i in range(n_q):
            kq_ref[0] = l2n(k_t[i][rows])
            kq_ref[1] = l2n(q_t[i][rows]) * (hd ** -0.5)
            beta = gt_ref[0, i, pl.ds(head, 1), :]
            decay = gt_ref[1, i, pl.ds(N_HEADS + head, 1), :]
            src = s_ref if i == 0 else sout_ref

            def ks_body(kk, acc):
                return acc + kq_ref[0, pl.ds(kk, 1), :] * src[hh, kk]

            k_s = lax.fori_loop(0, hd, ks_body, jnp.zeros((hd, k_t[i].shape[1]), F32), unroll=8)
            r = beta * (v_t[i][rows] - decay * k_s)

            def upd_body(kk, acc):
                s_new = decay * src[hh, kk] + kq_ref[0, pl.ds(kk, 1), :] * r
                sout_ref[hh, kk] = s_new
                return acc + kq_ref[1, pl.ds(kk, 1), :] * s_new

            o = lax.fori_loop(0, hd, upd_body, jnp.zeros_like(r), unroll=8)
            ot_ref[i, rows, :] = o * lax.rsqrt(jnp.mean(o * o, axis=0, keepdims=True) + 1e-6) * nw_ref[...]

    for i in range(n_q):
        o_ref[i] = (ot_ref[i].T * _silu(z_ref[i])).astype(o_ref.dtype)


def _gdn_sample(proj_s, conv_state, state_t, conv_w, a_log, dt_bias, norm_w):
    n_q, bsz, _ = proj_s.shape
    hd = HEAD_DIM
    pw = 2 * hd
    n_pairs = N_HEADS // 2
    base = D_CONV // pw
    col = jnp.zeros((LANES, 1), F32)
    alog_col = col.at[N_HEADS:2 * N_HEADS, 0].set(a_log)
    dtb_col = col.at[N_HEADS:2 * N_HEADS, 0].set(dt_bias)
    nw_col = norm_w.reshape(hd, 1)

    def xspec(part):
        return pl.BlockSpec((n_q, bsz, pw), lambda p: (0, 0, base + part * n_pairs + p))

    def cspec(rows, part):
        return pl.BlockSpec((rows, bsz, pw) if rows else (GDN_CONV, pw),
                            (lambda p: (0, 0, part * n_pairs + p)) if rows
                            else (lambda p: (0, part * n_pairs + p)))

    const = lambda p: (0, 0)
    state_spec = pl.BlockSpec((2, hd, hd, bsz), lambda p: (p, 0, 0, 0))
    return pl.pallas_call(
        _gdn_sample_kernel,
        grid=(n_pairs,),
        in_specs=[xspec(0), xspec(1), xspec(2),
                  cspec(GDN_CONV - 1, 0), cspec(GDN_CONV - 1, 1), cspec(GDN_CONV - 1, 2),
                  cspec(0, 0), cspec(0, 1), cspec(0, 2),
                  pl.BlockSpec((n_q, bsz, pw), lambda p: (0, 0, D_IN_MAIN // pw - n_pairs + p)),
                  pl.BlockSpec((n_q, bsz, LANES), lambda p: (0, 0, D_IN_MAIN // LANES)),
                  pl.BlockSpec((LANES, 1), const), pl.BlockSpec((LANES, 1), const),
                  pl.BlockSpec((hd, 1), const),
                  state_spec],
        out_specs=[pl.BlockSpec((n_q, bsz, pw), lambda p: (0, 0, p)), state_spec],
        out_shape=[jax.ShapeDtypeStruct((n_q, bsz, D_HEADS), BF16),
                   jax.ShapeDtypeStruct(state_t.shape, F32)],
        scratch_shapes=[pltpu.VMEM((2, hd, bsz), F32),
                        pltpu.VMEM((2, n_q, LANES, bsz), F32),
                        pltpu.VMEM((n_q, pw, bsz), F32)],
        compiler_params=_cparams(1),
        name="gdn_sample",
    )(proj_s, proj_s, proj_s, conv_state, conv_state, conv_state, conv_w, conv_w, conv_w,
      proj_s, proj_s, alog_col, dtb_col, nw_col, state_t)


TM = 512
TM_MOE = 1024
TJ_MOE = 512
GDN_ROWS = 256
SAMPLE_PAD = 8


def kernel(x_prompt, x_sample, cache_k, cache_v, state_conv, state_gdn, page_table, c_prompt, c_sample,
           ln_in_g, ln_in_b, w_mod, b_mod, w_in, conv_w, a_log, dt_bias, gdn_norm_w, w_out, rel_table,
           ln_g, ln_b, ffn_w_gate, ffn_w_up, ffn_w_down, moe_router, moe_w_gate, moe_w_up, moe_w_down):
    bp, seq, d = x_prompt.shape
    bs, n_q, _ = x_sample.shape
    depth = w_in.shape[0]
    tp, ts = bp * seq, bs * n_q
    h, hd = N_HEADS, HEAD_DIM
    assert bs == MOD_ROWS and seq % TM == 0 and ts % TM == 0 and seq % MOBA_BLOCK == 0
    assert depth == 2 and GDN_CONV - 1 <= n_q <= SAMPLE_PAD and GDN_CHUNK == HEAD_DIM
    alpha = (2 * depth) ** 0.25

    def groups(tm):
        return lambda i: jnp.minimum(i // (seq // tm), bp)

    n_c = bp + bs
    c_all = jnp.pad(jnp.concatenate([c_prompt, c_sample]), ((0, (-n_c) % 8), (0, 0)))
    mod = _mod_vectors(c_all, w_mod, b_mod).reshape(depth, -1, 6, d)
    mod_p = jnp.broadcast_to(mod[:, :bp].transpose(0, 2, 1, 3)[:, :, :, None, :],
                             (depth, 6, bp, MOD_ROWS, d))
    mod_s = mod[:, bp:n_c].transpose(0, 2, 1, 3)[:, :, None]
    mod_all = jnp.concatenate([mod_p, mod_s], axis=2)

    x, u = _ln_mod(x_prompt.reshape(tp, d), x_sample.transpose(1, 0, 2).reshape(ts, d),
                   ln_in_g, ln_in_b, mod_all, 0, TM, groups(TM))

    bias_tab = _moba_bias_tables(rel_table)
    past_len = page_table.shape[1] * PAGE_SIZE
    bias_past, bias_own = _moba_sample_bias(rel_table, past_len, n_q)
    cache_kt = cache_k.transpose(0, 1, 3, 4, 2)
    cache_vt = cache_v.transpose(0, 1, 3, 4, 2)
    state_t = state_gdn.transpose(0, 2, 3, 4, 1)
    conv_t = state_conv.transpose(0, 2, 1, 3)

    k_p, v_p, conv_p, gdn_p, k_s, v_s, conv_s, gdn_s = [], [], [], [], [], [], [], []
    for layer in range(depth):
        w_l = w_in[layer]
        w_cat = jnp.concatenate(
            [w_l[:, :D_IN_MAIN], jnp.pad(w_l[:, D_IN_MAIN:], ((0, 0), (0, LANES - 2 * h)))],
            axis=1).astype(BF16)
        proj, qkv, k_t, v_t = _proj_in(u, w_cat, TM, bp, seq)
        proj_s = proj[tp:].reshape(n_q, bs, -1)

        def slots_s(cols):
            return jnp.pad(proj_s[:, :, cols].transpose(1, 0, 2), ((0, 0), (0, SAMPLE_PAD - n_q), (0, 0)))

        k_sr = proj_s[:, :, D_HEADS:2 * D_HEADS].transpose(1, 0, 2).reshape(bs, n_q, h, hd)
        v_sr = proj_s[:, :, 2 * D_HEADS:3 * D_HEADS].transpose(1, 0, 2).reshape(bs, n_q, h, hd)
        att_p = _moba_prompt(qkv, bias_tab, bp, seq)
        att_s = _moba_sample(slots_s(slice(0, D_HEADS)), slots_s(slice(D_HEADS, 2 * D_HEADS)),
                             slots_s(slice(2 * D_HEADS, 3 * D_HEADS)), cache_kt, cache_vt, page_table,
                             layer, bias_past, bias_own, n_q)
        att_s = att_s.reshape(bs, h, SAMPLE_PAD, hd)[:, :, :n_q].transpose(2, 0, 1, 3)
        att = (att_p, att_s.reshape(ts, D_HEADS).astype(BF16))

        o_p, conv_new_p, s_new_p = _gdn_prompt(proj, bp, seq, GDN_ROWS, conv_w[layer], a_log[layer],
                                               dt_bias[layer], gdn_norm_w[layer])
        o_s, s_new_s = _gdn_sample(proj_s, conv_t[layer], state_t[layer], conv_w[layer], a_log[layer],
                                   dt_bias[layer], gdn_norm_w[layer])
        s_new_s = s_new_s.transpose(3, 0, 1, 2)
        conv_new_s = proj_s[n_q - (GDN_CONV - 1):, :, 3 * D_HEADS:3 * D_HEADS + D_CONV].transpose(1, 0, 2)
        gdn_o = (o_p, o_s.reshape(ts, D_HEADS))

        w_o = w_out[layer].astype(BF16)
        i = layer // 2
        if layer % 2 == 0:
            x, u = _res_ln([att, gdn_o], [w_o[:D_HEADS], w_o[D_HEADS:]], x, mod_all, layer, 2, layer,
                           (4, 3), ln_g[layer, 0], ln_b[layer, 0], alpha, TM, groups(TM))
            hdn = _ffn_up(u, ffn_w_gate[i].astype(BF16), ffn_w_up[i].astype(BF16), TM)
            x, u = _res_ln([hdn], [ffn_w_down[i].astype(BF16)], x, mod_all, layer, 5, layer + 1,
                           (1, 0), ln_g[layer, 1], ln_b[layer, 1], alpha, TM, groups(TM))
        else:
            w_r = jnp.pad(moe_router[i], ((0, 0), (0, LANES - N_EXPERTS)))
            x, u, route = _res_ln([att, gdn_o], [w_o[:D_HEADS], w_o[D_HEADS:]], x, mod_all, layer, 2,
                                  layer, (4, 3), ln_g[layer, 0], ln_b[layer, 0], alpha, TM,
                                  groups(TM), w_router=w_r)
            tile_expert, n_valid, token_of_slot, slot = _moe_routing(route, TM_MOE)
            y = _moe_ffn(u, tile_expert, n_valid, token_of_slot, moe_w_gate[i], moe_w_up[i],
                         moe_w_down[i], TM_MOE, TJ_MOE)
            n_t = (tp + ts) // TM
            slots = slot.reshape(n_t, TM, TOP_K).transpose(0, 2, 1).reshape(n_t, 1, TOP_K * TM)
            x_p, x_s = _moe_combine(y, slots, route, x, mod_all, layer, 5, ln_g[layer, 1],
                                    ln_b[layer, 1], alpha, TM, groups(TM), tp // TM)

        k_p.append(k_t)
        v_p.append(v_t)
        conv_p.append(conv_new_p)
        gdn_p.append(s_new_p)
        k_s.append(k_sr)
        v_s.append(v_sr)
        conv_s.append(conv_new_s)
        gdn_s.append(s_new_s)

    y_prompt = x_p.reshape(bp, seq, d)
    y_sample = x_s.reshape(n_q, bs, d).transpose(1, 0, 2)
    k_prompt = jnp.stack(k_p).transpose(0, 1, 4, 2, 3)
    v_prompt = jnp.stack(v_p).transpose(0, 1, 4, 2, 3)
    return (y_prompt, y_sample, k_prompt, v_prompt, jnp.stack(conv_p), jnp.stack(gdn_p),
            jnp.stack(k_s), jnp.stack(v_s), jnp.stack(conv_s), jnp.stack(gdn_s))
```

```python
import functools
import math

import numpy as np
import jax
import jax.numpy as jnp
from jax import lax
from jax.experimental import pallas as pl
from jax.experimental.pallas import tpu as pltpu

F32 = jnp.float32
BF16 = jnp.bfloat16

D_MODEL = 1024
HEAD_DIM = 64
N_HEADS = 8
D_HEADS = N_HEADS * HEAD_DIM
D_CONV = 3 * D_HEADS
D_IN = 3 * D_HEADS + D_CONV + D_HEADS + 2 * N_HEADS
D_IN_MAIN = D_IN - 2 * N_HEADS
LANES = 128
MOBA_BLOCK = 256
MOBA_TOPK = 3
PAGE_SIZE = 128
GDN_CONV = 4
GDN_CHUNK = 64
N_BUCKETS = 32
MAX_DISTANCE = 128
N_EXPERTS = 8
TOP_K = 2
LN_EPS = 1e-5
NEG = -1e30
MOD_ROWS = 128
VMEM_LIMIT = 56 * 1024 * 1024

HIGHEST = lax.Precision.HIGHEST


def _cparams(n_axes):
    return pltpu.CompilerParams(dimension_semantics=("arbitrary",) * n_axes,
                                vmem_limit_bytes=VMEM_LIMIT)


def _silu(x):
    return x * (1.0 / (1.0 + jnp.exp(-x)))


def _layer_norm(x, g, b):
    mu = jnp.mean(x, axis=-1, keepdims=True)
    xc = x - mu
    var = jnp.mean(xc * xc, axis=-1, keepdims=True)
    return xc * lax.rsqrt(var + LN_EPS) * g + b


def _modulate(x, scale, shift):
    tm, d = x.shape
    x3 = x.reshape(tm // MOD_ROWS, MOD_ROWS, d)
    return (x3 * (1.0 + scale[None]) + shift[None]).reshape(tm, d)


def _gated(x, gate, h):
    tm, d = x.shape
    x3 = x.reshape(tm // MOD_ROWS, MOD_ROWS, d)
    h3 = h.reshape(tm // MOD_ROWS, MOD_ROWS, d)
    return (x3 + (1.0 + gate[None]) * h3).reshape(tm, d)


def _mod_kernel(c_ref, w_ref, b_ref, o_ref):
    a = _silu(c_ref[...])
    o_ref[...] = jnp.dot(a, w_ref[...], precision=HIGHEST, preferred_element_type=F32) + b_ref[...]


def _mod_vectors(c_all, w_mod, b_mod, tn=1536):
    depth, d, n = w_mod.shape
    rows = c_all.shape[0]
    return pl.pallas_call(
        _mod_kernel,
        grid=(depth, n // tn),
        in_specs=[pl.BlockSpec((rows, d), lambda l, j: (0, 0)),
                  pl.BlockSpec((None, d, tn), lambda l, j: (l, 0, j)),
                  pl.BlockSpec((None, 1, tn), lambda l, j: (l, 0, j))],
        out_specs=pl.BlockSpec((None, rows, tn), lambda l, j: (l, 0, j)),
        out_shape=jax.ShapeDtypeStruct((depth, rows, n), F32),
        compiler_params=_cparams(2),
        name="mod_vectors",
    )(c_all, w_mod, b_mod.reshape(depth, 1, n))


def _ln_mod_kernel(xa_ref, xb_ref, g_ref, b_ref, mod_ref, xn_ref, u_ref, *, n_first):
    x = jnp.where(pl.program_id(0) < n_first, xa_ref[...], xb_ref[...])
    xn = _layer_norm(x, g_ref[...], b_ref[...])
    xn_ref[...] = xn
    u_ref[...] = _modulate(xn, mod_ref[1], mod_ref[0]).astype(BF16)


def _mod_spec(layer, group_of_tile):
    return pl.BlockSpec((None, 6, None, MOD_ROWS, D_MODEL),
                        lambda i: (layer, 0, group_of_tile(i), 0, 0))


def _ln_mod(x_first, x_rest, g, b, mod_all, layer, tm, group_of_tile):
    d = x_first.shape[1]
    n_first = x_first.shape[0] // tm
    t = x_first.shape[0] + x_rest.shape[0]
    row = pl.BlockSpec((tm, d), lambda i: (i, 0))
    vec = pl.BlockSpec((1, d), lambda i: (0, 0))
    return pl.pallas_call(
        functools.partial(_ln_mod_kernel, n_first=n_first),
        grid=(t // tm,),
        in_specs=[pl.BlockSpec((tm, d), lambda i: (jnp.minimum(i, n_first - 1), 0)),
                  pl.BlockSpec((tm, d), lambda i: (jnp.maximum(i - n_first, 0), 0)),
                  vec, vec, _mod_spec(layer, group_of_tile)],
        out_specs=[row, row],
        out_shape=[jax.ShapeDtypeStruct((t, d), F32), jax.ShapeDtypeStruct((t, d), BF16)],
        compiler_params=_cparams(1),
        name="ln_mod",
    )(x_first, x_rest, g.reshape(1, d), b.reshape(1, d), mod_all)


def _proj_kernel(u_ref, w_ref, proj_ref, qkv_ref):
    u = u_ref[...]
    n = w_ref.shape[1]
    chunk = D_HEADS
    n_qkv = qkv_ref.shape[1]
    for c0 in range(0, n, chunk):
        c1 = min(c0 + chunk, n)
        r = jnp.dot(u, w_ref[:, c0:c1], preferred_element_type=F32)
        proj_ref[:, c0:c1] = r
        if c1 <= n_qkv:
            qkv_ref[:, c0:c1] = r.astype(BF16)


def _proj_in(u, w, tm):
    t, d = u.shape
    n = w.shape[1]
    return pl.pallas_call(
        _proj_kernel,
        grid=(t // tm,),
        in_specs=[pl.BlockSpec((tm, d), lambda i: (i, 0)),
                  pl.BlockSpec((d, n), lambda i: (0, 0))],
        out_specs=[pl.BlockSpec((tm, n), lambda i: (i, 0)),
                   pl.BlockSpec((tm, 3 * D_HEADS), lambda i: (i, 0))],
        out_shape=[jax.ShapeDtypeStruct((t, n), F32),
                   jax.ShapeDtypeStruct((t, 3 * D_HEADS), BF16)],
        compiler_params=_cparams(1),
        name="proj_in",
    )(u, w)


def _top2_gates(logits):
    lane = lax.broadcasted_iota(jnp.int32, logits.shape, 1)
    v1 = jnp.max(logits, axis=-1, keepdims=True)
    i1 = jnp.min(jnp.where(logits == v1, lane, LANES), axis=-1, keepdims=True)
    rest = jnp.where(lane == i1, -jnp.inf, logits)
    v2 = jnp.max(rest, axis=-1, keepdims=True)
    i2 = jnp.min(jnp.where(rest == v2, lane, LANES), axis=-1, keepdims=True)
    e2 = jnp.exp(v2 - v1)
    inv = 1.0 / (1.0 + e2)
    route = jnp.where(lane == 0, i1.astype(F32), 0.0) + jnp.where(lane == 1, i2.astype(F32), 0.0)
    return route + jnp.where(lane == 2, inv, 0.0) + jnp.where(lane == 3, e2 * inv, 0.0)


def _res_ln_kernel(*refs, split, n_first, alpha, next_rows, with_router):
    n_in = len(split)
    a_vals, pos = [], 0
    for two in split:
        if two:
            a_vals.append(jnp.where(pl.program_id(0) < n_first, refs[pos][...], refs[pos + 1][...]))
        else:
            a_vals.append(refs[pos][...])
        pos += 2 if two else 1
    w_refs = refs[pos:pos + n_in]
    x_ref, modg_ref, modn_ref, g_ref, b_ref = refs[pos + n_in:pos + n_in + 5]
    pos += n_in + 5
    if with_router:
        wr_ref = refs[pos]
        pos += 1
    outs = refs[pos:]
    h = jnp.dot(a_vals[0], w_refs[0][...], preferred_element_type=F32)
    for a_val, w_ref in zip(a_vals[1:], w_refs[1:]):
        h = h + jnp.dot(a_val, w_ref[...], preferred_element_type=F32)
    xn = _layer_norm(_gated(alpha * x_ref[...], modg_ref[...], h), g_ref[...], b_ref[...])
    outs[0][...] = xn
    if next_rows is not None:
        u = _modulate(xn, modn_ref[next_rows[0]], modn_ref[next_rows[1]])
        outs[1][...] = u.astype(outs[1].dtype)
        if with_router:
            logits = jnp.dot(u, wr_ref[...], precision=HIGHEST, preferred_element_type=F32)
            lane = lax.broadcasted_iota(jnp.int32, logits.shape, 1)
            outs[2][...] = _top2_gates(jnp.where(lane < N_EXPERTS, logits, -jnp.inf))


def _res_ln(a_list, w_list, x, mod_all, gate_layer, gate_row, next_layer, next_rows, ln_g, ln_b,
            alpha, tm, group_of_tile, w_router=None):
    t, d = x.shape
    row = pl.BlockSpec((tm, d), lambda i: (i, 0))
    vec = pl.BlockSpec((1, d), lambda i: (0, 0))
    split = tuple(isinstance(a, tuple) for a in a_list)
    n_first = a_list[split.index(True)][0].shape[0] // tm if any(split) else 0
    in_specs, flat_a = [], []
    for a in a_list:
        if isinstance(a, tuple):
            in_specs += [pl.BlockSpec((tm, a[0].shape[1]), lambda i: (jnp.minimum(i, n_first - 1), 0)),
                         pl.BlockSpec((tm, a[1].shape[1]), lambda i: (jnp.maximum(i - n_first, 0), 0))]
            flat_a += list(a)
        else:
            in_specs.append(pl.BlockSpec((tm, a.shape[1]), lambda i: (i, 0)))
            flat_a.append(a)
    in_specs += [pl.BlockSpec(w.shape, lambda i: (0, 0)) for w in w_list]
    in_specs += [row,
                 pl.BlockSpec((None, None, None, MOD_ROWS, d),
                              lambda i: (gate_layer, gate_row, group_of_tile(i), 0, 0)),
                 _mod_spec(next_layer if next_rows is not None else gate_layer, group_of_tile),
                 vec, vec]
    args = flat_a + list(w_list) + [x, mod_all, mod_all, ln_g.reshape(1, d), ln_b.reshape(1, d)]
    out_specs = [row]
    out_shape = [jax.ShapeDtypeStruct((t, d), F32)]
    with_router = w_router is not None
    if next_rows is not None:
        out_specs.append(row)
        out_shape.append(jax.ShapeDtypeStruct((t, d), F32 if with_router else BF16))
    if with_router:
        in_specs.append(pl.BlockSpec(w_router.shape, lambda i: (0, 0)))
        args.append(w_router)
        out_specs.append(pl.BlockSpec((tm, LANES), lambda i: (i, 0)))
        out_shape.append(jax.ShapeDtypeStruct((t, LANES), F32))
    return pl.pallas_call(
        functools.partial(_res_ln_kernel, split=split, n_first=n_first, alpha=alpha,
                          next_rows=next_rows, with_router=with_router),
        grid=(t // tm,),
        in_specs=in_specs,
        out_specs=out_specs,
        out_shape=out_shape,
        compiler_params=_cparams(1),
        name="res_ln",
    )(*args)


def _ffn_up_kernel(u_ref, wg_ref, wu_ref, o_ref, *, chunk):
    u = u_ref[...]
    n = o_ref.shape[1]
    for c0 in range(0, n, chunk):
        c1 = min(c0 + chunk, n)
        hg = jnp.dot(u, wg_ref[:, c0:c1], preferred_element_type=F32)
        hu = jnp.dot(u, wu_ref[:, c0:c1], preferred_element_type=F32)
        o_ref[:, c0:c1] = (_silu(hg) * hu).astype(BF16)


def _ffn_up(u, wg, wu, tm, chunk=256):
    t, d = u.shape
    n = wg.shape[1]
    wspec = pl.BlockSpec((d, n), lambda i: (0, 0))
    return pl.pallas_call(
        functools.partial(_ffn_up_kernel, chunk=chunk),
        grid=(t // tm,),
        in_specs=[pl.BlockSpec((tm, d), lambda i: (i, 0)), wspec, wspec],
        out_specs=pl.BlockSpec((tm, n), lambda i: (i, 0)),
        out_shape=jax.ShapeDtypeStruct((t, n), BF16),
        compiler_params=_cparams(1),
        name="ffn_up",
    )(u, wg, wu)


def _row_copy(src_hbm, src_row, dst_ref, r, sem):
    return pltpu.make_async_copy(src_hbm.at[pl.ds(src_row, 1)], dst_ref.at[pl.ds(r, 1)], sem)


def _gather_start(idx_ref, idx_base, src_hbm, dst_ref, sem, n_rows):
    def body(r, carry):
        _row_copy(src_hbm, idx_ref[0, idx_base + r], dst_ref, r, sem).start()
        return carry

    lax.fori_loop(0, n_rows, body, 0, unroll=16)


def _gather_wait(src_hbm, dst_ref, sem):
    pltpu.make_async_copy(src_hbm.at[pl.ds(0, dst_ref.shape[0])], dst_ref, sem).wait()


def _moe_ffn_kernel(te_ref, nv_ref, tos_ref, tos_next_ref, u_hbm, wg_ref, wu_ref, wd_ref, y_ref,
                    xg_ref, xb_ref, acc_ref, sem):
    del te_ref
    i = pl.program_id(0)
    j = pl.program_id(1)
    n_valid = nv_ref[0]
    valid = i < n_valid
    cur = i % 2
    rows = xb_ref.shape[0]

    @pl.when(valid & (i == 0) & (j == 0))
    def _():
        _gather_start(tos_ref, 0, u_hbm, xg_ref.at[0], sem.at[0], rows)

    @pl.when(valid & (j == 0))
    def _():
        _gather_wait(u_hbm, xg_ref.at[cur], sem.at[cur])
        xb_ref[...] = xg_ref[cur].astype(BF16)
        acc_ref[...] = jnp.zeros_like(acc_ref)

    @pl.when((i + 1 < n_valid) & (j == 0))
    def _():
        _gather_start(tos_next_ref, 0, u_hbm, xg_ref.at[1 - cur], sem.at[1 - cur], rows)

    @pl.when(valid)
    def _():
        x = xb_ref[...]
        hg = jnp.dot(x, wg_ref[...].astype(BF16), preferred_element_type=F32)
        hu = jnp.dot(x, wu_ref[...].astype(BF16), preferred_element_type=F32)
        acc_ref[...] += jnp.dot((_silu(hg) * hu).astype(BF16), wd_ref[...].astype(BF16),
                                preferred_element_type=F32)

    last = j == pl.num_programs(1) - 1

    @pl.when(valid & last)
    def _():
        y_ref[...] = acc_ref[...]

    @pl.when(jnp.logical_not(valid) & last)
    def _():
        y_ref[...] = jnp.zeros_like(y_ref)


def _moe_ffn(u, tile_expert, n_valid, token_of_slot, wg, wu, wd, tm, tj):
    n_tiles = tile_expert.shape[0]
    d = u.shape[1]
    f = wg.shape[2]
    assert f % tj == 0
    nj = f // tj

    def jj(i, j, nv):
        return jnp.where(i < nv[0], j, nj - 1)

    grid_spec = pltpu.PrefetchScalarGridSpec(
        num_scalar_prefetch=2,
        grid=(n_tiles, nj),
        in_specs=[pl.BlockSpec((None, 1, tm), lambda i, j, te, nv: (i, 0, 0), memory_space=pltpu.SMEM),
                  pl.BlockSpec((None, 1, tm), lambda i, j, te, nv: (jnp.minimum(i + 1, n_tiles - 1), 0, 0),
                               memory_space=pltpu.SMEM),
                  pl.BlockSpec(memory_space=pl.ANY),
                  pl.BlockSpec((None, d, tj), lambda i, j, te, nv: (te[i], 0, jj(i, j, nv))),
                  pl.BlockSpec((None, d, tj), lambda i, j, te, nv: (te[i], 0, jj(i, j, nv))),
                  pl.BlockSpec((None, tj, d), lambda i, j, te, nv: (te[i], jj(i, j, nv), 0))],
        out_specs=pl.BlockSpec((tm, d), lambda i, j, te, nv: (i, 0)),
        scratch_shapes=[pltpu.VMEM((2, tm, d), F32), pltpu.VMEM((tm, d), BF16),
                        pltpu.VMEM((tm, d), F32), pltpu.SemaphoreType.DMA((2,))])
    return pl.pallas_call(
        _moe_ffn_kernel,
        grid_spec=grid_spec,
        out_shape=jax.ShapeDtypeStruct((n_tiles * tm, d), F32),
        compiler_params=_cparams(2),
        name="moe_ffn",
    )(tile_expert, n_valid, token_of_slot, token_of_slot, u, wg, wu, wd)


def _moe_combine_kernel(slots_ref, slots_next_ref, y_hbm, route_ref, x_ref, modg_ref, g_ref, b_ref,
                        oa_ref, ob_ref, buf_ref, sem, *, alpha, n_first):
    rows = x_ref.shape[0]
    i = pl.program_id(0)
    cur = i % 2

    @pl.when(i == 0)
    def _():
        for k in range(TOP_K):
            _gather_start(slots_ref, k * rows, y_hbm, buf_ref.at[0, k], sem.at[0], rows)

    for k in range(TOP_K):
        _gather_wait(y_hbm, buf_ref.at[cur, k], sem.at[cur])

    @pl.when(i + 1 < pl.num_programs(0))
    def _():
        for k in range(TOP_K):
            _gather_start(slots_next_ref, k * rows, y_hbm, buf_ref.at[1 - cur, k], sem.at[1 - cur], rows)

    route = route_ref[...]
    f = route[:, 2:3] * buf_ref[cur, 0] + route[:, 3:4] * buf_ref[cur, 1]
    o = _layer_norm(_gated(alpha * x_ref[...], modg_ref[...], f), g_ref[...], b_ref[...])

    @pl.when(pl.program_id(0) < n_first)
    def _():
        oa_ref[...] = o

    @pl.when(pl.program_id(0) >= n_first)
    def _():
        ob_ref[...] = o


def _moe_combine(y, slots, route, x, mod_all, gate_layer, gate_row, ln_g, ln_b, alpha, tm,
                 group_of_tile, n_first):
    t, d = x.shape
    row = pl.BlockSpec((tm, d), lambda i: (i, 0))
    vec = pl.BlockSpec((1, d), lambda i: (0, 0))
    return pl.pallas_call(
        functools.partial(_moe_combine_kernel, alpha=alpha, n_first=n_first),
        grid=(t // tm,),
        in_specs=[pl.BlockSpec((None, 1, 2 * tm), lambda i: (i, 0, 0), memory_space=pltpu.SMEM),
                  pl.BlockSpec((None, 1, 2 * tm), lambda i: (jnp.minimum(i + 1, t // tm - 1), 0, 0),
                               memory_space=pltpu.SMEM),
                  pl.BlockSpec(memory_space=pl.ANY),
                  pl.BlockSpec((tm, LANES), lambda i: (i, 0)),
                  row,
                  pl.BlockSpec((None, None, None, MOD_ROWS, d),
                               lambda i: (gate_layer, gate_row, group_of_tile(i), 0, 0)),
                  vec, vec],
        out_specs=[pl.BlockSpec((tm, d), lambda i: (jnp.minimum(i, n_first - 1), 0)),
                   pl.BlockSpec((tm, d), lambda i: (jnp.maximum(i - n_first, 0), 0))],
        out_shape=[jax.ShapeDtypeStruct((n_first * tm, d), F32),
                   jax.ShapeDtypeStruct((t - n_first * tm, d), F32)],
        scratch_shapes=[pltpu.VMEM((2, TOP_K, tm, d), F32), pltpu.SemaphoreType.DMA((2,))],
        compiler_params=_cparams(1),
        name="moe_combine",
    )(slots, slots, y, route, x, mod_all, ln_g.reshape(1, d), ln_b.reshape(1, d))


def _moe_routing(route, tm):
    t = route.shape[0]
    n_assign = TOP_K * t
    n_tiles = -(-n_assign // tm) + N_EXPERTS
    experts = route[:, :TOP_K].astype(jnp.int32).reshape(n_assign)
    onehot = (experts[:, None] == jnp.arange(N_EXPERTS, dtype=jnp.int32)[None, :]).astype(jnp.int32)
    rank = jnp.sum((jnp.cumsum(onehot, axis=0) - onehot) * onehot, axis=1)
    count = jnp.sum(onehot, axis=0)
    padded = (count + tm - 1) // tm * tm
    ends = jnp.cumsum(padded)
    slot = jnp.sum(onehot * (ends - padded)[None, :], axis=1) + rank
    n_valid = ends[-1] // tm
    starts = jnp.arange(n_tiles, dtype=jnp.int32) * tm
    tile_expert = jnp.minimum(jnp.sum((starts[:, None] >= ends[None, :]).astype(jnp.int32), axis=1),
                              N_EXPERTS - 1)
    tile_expert = jnp.where(starts < ends[-1], tile_expert, tile_expert[jnp.maximum(n_valid - 1, 0)])
    token_of_slot = jnp.zeros((n_tiles * tm,), jnp.int32).at[slot].set(
        jnp.arange(n_assign, dtype=jnp.int32) // TOP_K)
    return (tile_expert.astype(jnp.int32), n_valid.reshape(1).astype(jnp.int32),
            token_of_slot.reshape(n_tiles, 1, tm), slot.reshape(t, TOP_K).astype(jnp.int32))


def _rel_bucket_table(max_dist):
    n = np.arange(max_dist + 1)
    max_exact = N_BUCKETS // 2
    ratio = np.log(np.maximum(n, 1).astype(np.float32) / np.float32(max_exact))
    large = max_exact + (ratio / np.float32(math.log(MAX_DISTANCE / max_exact))
                         * np.float32(N_BUCKETS - max_exact)).astype(np.int32)
    large = np.minimum(large, N_BUCKETS - 1)
    return np.where(n < max_exact, n, large).astype(np.int32)


def _bias_lookup(rel_table, bucket_idx):
    onehot = jnp.asarray(np.eye(N_BUCKETS, dtype=np.float32)[bucket_idx.reshape(-1)])
    vals = jnp.dot(onehot, rel_table, precision=HIGHEST)
    return vals.T.reshape((rel_table.shape[1],) + bucket_idx.shape)


def _select_top_blocks(gate, n_valid, ksel):
    blk = lax.broadcasted_iota(jnp.int32, gate.shape, 0)
    nb = gate.shape[0]
    g = jnp.where(blk < n_valid, gate, -jnp.inf)
    keep = blk == n_valid
    for _ in range(ksel):
        mx = jnp.max(g, axis=0, keepdims=True)
        first = jnp.min(jnp.where(g == mx, blk, nb), axis=0, keepdims=True)
        hit = blk == first
        keep = keep | (hit & (blk < n_valid))
        g = jnp.where(hit, -jnp.inf, g)
    return jnp.where(keep, 0.0, NEG)


def _moba_prompt_kernel(q_ref, k_ref, v_ref, bias_ref, o_ref,
                        kmean_ref, vt_ref, qm_ref, sel_ref, m_ref, l_ref, acc_ref, *, ksel):
    qt = pl.program_id(1)
    seq = k_ref.shape[0]
    nb = seq // MOBA_BLOCK
    gw = 4 * HEAD_DIM
    n_groups = N_HEADS // 4
    nt = (((1,), (1,)), ((), ()))

    @pl.when(qt == 0)
    def _():
        row = lax.broadcasted_iota(jnp.int32, (nb, seq), 0)
        col = lax.broadcasted_iota(jnp.int32, (nb, seq), 1)
        pool = jnp.where(col // MOBA_BLOCK == row, 1.0, 0.0).astype(BF16)
        kmean_ref[...] = jnp.dot(pool, k_ref[...], preferred_element_type=F32) * (1.0 / MOBA_BLOCK)
        for n in range(nb):
            vt_ref[n] = v_ref[n * MOBA_BLOCK:(n + 1) * MOBA_BLOCK, :].astype(F32).T.astype(BF16)

    head_of_lane = lax.broadcasted_iota(jnp.int32, (1, gw), 1) // HEAD_DIM
    q = q_ref[...] * (HEAD_DIM ** -0.5)
    kmean = kmean_ref[...].astype(BF16)
    for h in range(N_HEADS):
        g, hh = divmod(h, 4)
        qm = jnp.where(head_of_lane == hh, q[:, g * gw:(g + 1) * gw], 0.0).astype(BF16)
        qm_ref[h] = qm
        gate = lax.dot_general(kmean[:, g * gw:(g + 1) * gw], qm, nt, preferred_element_type=F32)
        sel_ref[h] = _select_top_blocks(gate, qt, ksel)
        m_ref[h] = jnp.full(m_ref.shape[1:], NEG, F32)
        l_ref[h] = jnp.zeros(l_ref.shape[1:], F32)
    acc_ref[...] = jnp.zeros_like(acc_ref)

    def body(step, carry, near):
        n = qt - step
        start = pl.multiple_of(n * MOBA_BLOCK, MOBA_BLOCK)
        kb = k_ref[pl.ds(start, MOBA_BLOCK), :]
        vt = vt_ref[n]
        scores = [lax.dot_general(kb[:, (h // 4) * gw:(h // 4 + 1) * gw], qm_ref[h], nt,
                                  preferred_element_type=F32) for h in range(N_HEADS)]
        probs, alphas = [], []
        for h in range(N_HEADS):
            shift = sel_ref[h, pl.ds(n, 1), :]
            if near:
                s = scores[h] + bias_ref[h, step]
            else:
                s = scores[h]
                shift = shift + bias_ref[h, 2, 0:1, :]
            m_old = m_ref[h]
            m_new = jnp.maximum(m_old, jnp.max(s, axis=0, keepdims=True) + shift)
            alpha = jnp.exp(m_old - m_new)
            p = jnp.exp(s - (m_new - shift))
            l_ref[h] = alpha * l_ref[h] + jnp.sum(p, axis=0, keepdims=True)
            m_ref[h] = m_new
            probs.append(p.astype(BF16))
            alphas.append(alpha)
        outs = [jnp.dot(vt[h * HEAD_DIM:(h + 1) * HEAD_DIM, :], probs[h],
                        preferred_element_type=F32) for h in range(N_HEADS)]
        for h in range(N_HEADS):
            g, hh = divmod(h, 4)
            rows = slice(hh * HEAD_DIM, (hh + 1) * HEAD_DIM)
            acc_ref[g, rows, :] = alphas[h] * acc_ref[g, rows, :] + outs[h]
        return carry

    lax.fori_loop(0, jnp.minimum(qt + 1, 2), functools.partial(body, near=True), 0)
    lax.fori_loop(2, qt + 1, functools.partial(body, near=False), 0)

    for g in range(n_groups):
        o_t = jnp.concatenate(
            [acc_ref[g, hh * HEAD_DIM:(hh + 1) * HEAD_DIM, :] * (1.0 / l_ref[4 * g + hh])
             for hh in range(4)], axis=0)
        o_ref[:, g * gw:(g + 1) * gw] = o_t.T.astype(o_ref.dtype)


def _moba_bias_tables(rel_table):
    bucket = _rel_bucket_table(2 * MOBA_BLOCK)
    i = np.arange(MOBA_BLOCK)[None, :]
    j = np.arange(MOBA_BLOCK)[:, None]
    own = jnp.where(jnp.asarray(i >= j), _bias_lookup(rel_table, bucket[np.maximum(i - j, 0)]), NEG)
    prev = _bias_lookup(rel_table, bucket[MOBA_BLOCK + i - j])
    far = jnp.broadcast_to(rel_table[N_BUCKETS - 1][:, None, None], prev.shape)
    return jnp.stack([own, prev, far], axis=1).astype(F32)


def _moba_prompt(qkv, bias_tab, batch, seq):
    nb = seq // MOBA_BLOCK
    ksel = min(MOBA_TOPK, nb)
    stat = pltpu.VMEM((N_HEADS, 1, MOBA_BLOCK), F32)
    return pl.pallas_call(
        functools.partial(_moba_prompt_kernel, ksel=ksel),
        grid=(batch, nb),
        in_specs=[pl.BlockSpec((MOBA_BLOCK, D_HEADS), lambda b, t: (b * nb + t, 0)),
                  pl.BlockSpec((seq, D_HEADS), lambda b, t: (b, 1)),
                  pl.BlockSpec((seq, D_HEADS), lambda b, t: (b, 2)),
                  pl.BlockSpec(bias_tab.shape, lambda b, t: (0, 0, 0, 0))],
        out_specs=pl.BlockSpec((MOBA_BLOCK, D_HEADS), lambda b, t: (b * nb + t, 0)),
        out_shape=jax.ShapeDtypeStruct((batch * seq, D_HEADS), BF16),
        scratch_shapes=[pltpu.VMEM((nb, D_HEADS), F32),
                        pltpu.VMEM((nb, D_HEADS, MOBA_BLOCK), BF16),
                        pltpu.VMEM((N_HEADS, MOBA_BLOCK, 4 * HEAD_DIM), BF16),
                        pltpu.VMEM((N_HEADS, nb, MOBA_BLOCK), F32),
                        stat, stat,
                        pltpu.VMEM((N_HEADS // 4, 4 * HEAD_DIM, MOBA_BLOCK), F32)],
        compiler_params=_cparams(2),
        name="moba_prompt",
    )(qkv, qkv, qkv, bias_tab)


def _moba_sample_kernel(pt_ref, q_ref, kn_ref, vn_ref, *refs, ksel, n_q, nbp, pages_per_block):
    del pt_ref
    n_pages = nbp * pages_per_block
    k_refs, v_refs = refs[:n_pages], refs[n_pages:2 * n_pages]
    bias_ref, bown_ref, o_ref, m_ref, l_ref, g_ref, acc_ref = refs[2 * n_pages:]
    nt = (((1,), (1,)), ((), ()))
    dh = N_HEADS * HEAD_DIM
    q8 = q_ref[...]
    slots = q8.shape[0]
    lane_head = lax.broadcasted_iota(jnp.int32, (1, dh), 1) // HEAD_DIM
    qbd = jnp.concatenate([jnp.where(lane_head == hp, q8, 0.0) for hp in range(N_HEADS)], axis=0)
    qs = (qbd * (HEAD_DIM ** -0.5)).astype(BF16)
    ones = jnp.ones((8, PAGE_SIZE), BF16)

    kts = [k_ref[...].reshape(dh, PAGE_SIZE).astype(BF16) for k_ref in k_refs]
    s_pages = [jnp.dot(qs, kt, preferred_element_type=F32) for kt in kts]
    ksums = [lax.dot_general(ones, kt, nt, preferred_element_type=F32) for kt in kts]
    probs = []
    for j in range(nbp):
        pages = range(j * pages_per_block, (j + 1) * pages_per_block)
        s = jnp.concatenate([s_pages[pg] for pg in pages], axis=1) + bias_ref[0 if j == nbp - 1 else 1]
        m = jnp.max(s, axis=-1, keepdims=True)
        p32 = jnp.exp(s - m)
        probs.append(p32.astype(BF16))
        m_ref[j] = m
        l_ref[j] = jnp.sum(p32, axis=-1, keepdims=True)
        ksum = functools.reduce(jnp.add, [ksums[pg] for pg in pages])
        g_ref[j] = jnp.sum(qbd * ksum[0:1, :], axis=-1, keepdims=True) * (1.0 / MOBA_BLOCK)
    pv = [lax.dot_general(probs[pg // pages_per_block][:, (pg % pages_per_block) * PAGE_SIZE:
                                                       (pg % pages_per_block + 1) * PAGE_SIZE],
                          v_refs[pg][...].reshape(dh, PAGE_SIZE).astype(BF16), nt,
                          preferred_element_type=F32) for pg in range(n_pages)]
    for j in range(nbp):
        acc_ref[j] = functools.reduce(jnp.add, pv[j * pages_per_block:(j + 1) * pages_per_block])

    s_own = [jnp.sum(qbd * kn_ref[i:i + 1, :], axis=-1, keepdims=True) * (HEAD_DIM ** -0.5)
             + bown_ref[:, i:i + 1] for i in range(n_q)]
    m_o = functools.reduce(jnp.maximum, s_own)
    p_own = [jnp.exp(s_i - m_o) for s_i in s_own]
    l_o = functools.reduce(jnp.add, p_own)
    acc_o = functools.reduce(jnp.add, [p_i * vn_ref[i:i + 1, :] for i, p_i in enumerate(p_own)])

    gg = g_ref[...]
    blk = lax.broadcasted_iota(jnp.int32, gg.shape, 0)
    keep = blk < 0
    for _ in range(ksel):
        mx = jnp.max(gg, axis=0, keepdims=True)
        first = jnp.min(jnp.where(gg == mx, blk, nbp), axis=0, keepdims=True)
        hit = (blk == first) & (gg > -jnp.inf)
        keep = keep | hit
        gg = jnp.where(blk == first, -jnp.inf, gg)
    m_all = m_ref[...]
    m_tot = jnp.maximum(jnp.max(jnp.where(keep, m_all, NEG), axis=0), m_o)
    w = jnp.where(keep, jnp.exp(m_all - m_tot[None]), 0.0)
    w_o = jnp.exp(m_o - m_tot)
    l_tot = jnp.sum(w * l_ref[...], axis=0) + w_o * l_o
    acc_tot = (jnp.sum(w * acc_ref[...], axis=0) + w_o * acc_o) * (1.0 / l_tot)
    row_head = lax.broadcasted_iota(jnp.int32, (N_HEADS * slots, 1), 0) // slots
    acc_tot = jnp.where(row_head == lane_head, acc_tot, 0.0)
    o_ref[...] = functools.reduce(
        jnp.add, [acc_tot[:, hp * HEAD_DIM:(hp + 1) * HEAD_DIM] for hp in range(N_HEADS)])


def _moba_sample_bias(rel_table, n_q):
    slots = SAMPLE_PAD
    bucket = _rel_bucket_table(MOBA_BLOCK + slots)
    rows_h = np.repeat(np.arange(N_HEADS), slots)
    rows_i = np.minimum(np.tile(np.arange(slots), N_HEADS), n_q - 1)
    pos = np.arange(MOBA_BLOCK)[None, :]
    by_head = _bias_lookup(rel_table, bucket[MOBA_BLOCK + rows_i[:, None] - pos])
    pick = jnp.asarray(np.arange(N_HEADS)[:, None, None] == rows_h[None, :, None])
    near = jnp.sum(jnp.where(pick, by_head, 0.0), axis=0)
    far = jnp.broadcast_to(jnp.repeat(rel_table[N_BUCKETS - 1], slots)[:, None], near.shape)
    past = jnp.stack([near, far]).astype(F32)
    new = np.arange(slots)[None, :]
    own_h = _bias_lookup(rel_table, bucket[np.maximum(rows_i[:, None] - new, 0)])
    own = jnp.sum(jnp.where(pick, own_h, 0.0), axis=0)
    ok = jnp.asarray((new <= rows_i[:, None]) & (new < n_q))
    return past, jnp.where(ok, own, NEG).astype(F32)


def _moba_sample(q_rows, k_new, v_new, cache_k, cache_v, page_table, layer, bias_past, bias_own, n_q):
    bsz, slots, dh = q_rows.shape
    hd = HEAD_DIM
    rows = N_HEADS * slots
    pages_per_block = MOBA_BLOCK // PAGE_SIZE
    nbp = page_table.shape[1] // pages_per_block
    ksel = min(MOBA_TOPK, nbp + 1)

    n_pages = nbp * pages_per_block

    def page_spec(page):
        return pl.BlockSpec((None, None, N_HEADS, hd, PAGE_SIZE),
                            lambda b, pt: (layer, pt[b, page], 0, 0, 0))

    per_seq = lambda b, pt: (b, 0, 0)
    stat = pltpu.VMEM((nbp, rows, 1), F32)
    grid_spec = pltpu.PrefetchScalarGridSpec(
        num_scalar_prefetch=1,
        grid=(bsz,),
        in_specs=([pl.BlockSpec((None, slots, dh), per_seq)] * 3
                  + [page_spec(page) for page in range(n_pages)] * 2
                  + [pl.BlockSpec(bias_past.shape, lambda b, pt: (0, 0, 0)),
                     pl.BlockSpec(bias_own.shape, lambda b, pt: (0, 0))]),
        out_specs=pl.BlockSpec((None, rows, hd), per_seq),
        scratch_shapes=[stat, stat, stat, pltpu.VMEM((nbp, rows, dh), F32)])
    return pl.pallas_call(
        functools.partial(_moba_sample_kernel, ksel=ksel, n_q=n_q, nbp=nbp,
                          pages_per_block=pages_per_block),
        grid_spec=grid_spec,
        out_shape=jax.ShapeDtypeStruct((bsz, rows, hd), F32),
        compiler_params=_cparams(1),
        name="moba_sample",
    )(page_table, q_rows, k_new, v_new, *([cache_k] * n_pages), *([cache_v] * n_pages),
      bias_past, bias_own)


def _softplus(x):
    return jnp.maximum(x, 0.0) + jnp.log1p(jnp.exp(-jnp.abs(x)))


def _bmm(a, b):
    return jnp.matmul(a.astype(BF16), b.astype(BF16), preferred_element_type=F32)


def _dot_f32(a, b):
    return jnp.dot(a, b, precision=HIGHEST, preferred_element_type=F32)


def _split_bf16(a):
    hi = a.astype(BF16)
    return hi, (a - hi.astype(F32)).astype(BF16)


def _bmm_3pass(a, b):
    a_hi, a_lo = _split_bf16(a)
    b_hi, b_lo = _split_bf16(b)
    return (jnp.matmul(a_hi, b_hi, preferred_element_type=F32)
            + jnp.matmul(a_hi, b_lo, preferred_element_type=F32)
            + jnp.matmul(a_lo, b_hi, preferred_element_type=F32))


def _unit_lower_inverse(a, row, col):
    c = a.shape[1]
    eye = jnp.where(row == col, 1.0, 0.0)
    in16 = (row // 16) == (col // 16)
    in32 = (row // 32) == (col // 32)
    nil = jnp.where(in16, -a, 0.0)
    x = eye + nil
    p = _bmm_3pass(nil, nil)
    for _ in range(2):
        xp = _bmm_3pass(jnp.concatenate([x, p], axis=1), p)
        x = x + xp[:, :c]
        p = xp[:, c:]
    x = x + _bmm_3pass(x, p)
    for off in (jnp.where(in32 & jnp.logical_not(in16), a, 0.0),
                jnp.where(in32, 0.0, a)):
        x = x - _bmm(x, _bmm(off, x))
    return x


def _gdn_prepare(q, k, v, k_t, g_col, g_row, beta, row, col):
    c = q.shape[1]
    incl = row >= col
    decay = jnp.where(incl, jnp.exp(jnp.where(incl, g_col - g_row, 0.0)), 0.0)
    qk_kk = _bmm(jnp.concatenate([q, k], axis=1), k_t)
    a_mat = jnp.where(row > col, qk_kk[:, c:] * decay * beta, 0.0)
    e_g = jnp.exp(g_col)
    rhs = jnp.concatenate([v * beta, k * beta * e_g], axis=2)
    sol = _bmm(_unit_lower_inverse(a_mat, row, col), rhs)
    dv = v.shape[2]
    g_last = g_col[:, c - 1:c, :]
    w_qd = jnp.concatenate([sol[:, :, dv:], q * e_g], axis=1).astype(BF16)
    qk = jnp.where(incl, qk_kk[:, :c] * decay, 0.0).astype(BF16)
    k_dec_t = (k_t * jnp.exp(g_last - g_row)).astype(BF16)
    return sol[:, :, :dv], w_qd, qk, k_dec_t, jnp.exp(g_last)


def _gdn_step(u, w_qd, qk, k_dec_t, chunk_decay, state):
    c = u.shape[1]
    ws_qs = jnp.matmul(w_qd, state.astype(BF16), preferred_element_type=F32)
    v_new = (u - ws_qs[:, :c]).astype(BF16)
    o = ws_qs[:, c:] + jnp.matmul(qk, v_new, preferred_element_type=F32)
    new_state = state * chunk_decay + jnp.matmul(k_dec_t, v_new, preferred_element_type=F32)
    return o, new_state


def _gdn_prompt_kernel(x_ref, z_ref, gab_ref, ak_ref, av_ref, cw_ref, alog_ref, dtb_ref, nw_ref, seg_ref,
                       o_ref, conv_ref, sout_ref, kt_ref, vt_ref,
                       xbuf_ref, halo_ref, s_ref, u_ref, wqd_ref, qk_ref, kdt_ref, cd_ref, oh_ref):
    t = pl.program_id(1)
    tl = x_ref.shape[0]
    cs = GDN_CHUNK
    n_chunks = tl // cs
    hd = HEAD_DIM
    kt_ref[...] = ak_ref[...].T.reshape(kt_ref.shape)
    vt_ref[...] = av_ref[...].T.reshape(vt_ref.shape)

    @pl.when(t == 0)
    def _():
        halo_ref[...] = jnp.zeros_like(halo_ref)
        s_ref[...] = jnp.zeros_like(s_ref)

    x = x_ref[...]
    xbuf_ref[0:8, :] = halo_ref[...]
    xbuf_ref[8:8 + tl, :] = x
    y = cw_ref[0:1, :] * xbuf_ref[5:5 + tl, :]
    for i in range(1, GDN_CONV):
        y = y + cw_ref[i:i + 1, :] * xbuf_ref[5 + i:5 + i + tl, :]
    y = _silu(y)
    tail = x[tl - 8:, :]
    halo_ref[...] = tail
    conv_ref[...] = tail[8 - (GDN_CONV - 1):, :]

    dh = N_HEADS * hd
    seg = seg_ref[...]

    def l2n(a):
        sq_hi, sq_lo = _split_bf16(a * a)
        ss = (jnp.dot(sq_hi, seg, preferred_element_type=F32)
              + jnp.dot(sq_lo, seg, preferred_element_type=F32))
        return a * lax.rsqrt(ss + 1e-6)

    qn = l2n(y[:, :dh]) * (hd ** -0.5)
    kn = l2n(y[:, dh:2 * dh])
    vv = y[:, 2 * dh:]
    kn_t = kn.T

    gab = gab_ref[...]
    beta = 1.0 / (1.0 + jnp.exp(-gab))
    glog = -jnp.exp(alog_ref[...]) * _softplus(gab + dtb_ref[...])

    row = lax.broadcasted_iota(jnp.int32, (cs, cs), 0)
    col = lax.broadcasted_iota(jnp.int32, (cs, cs), 1)
    tril = jnp.where(row >= col, 1.0, 0.0)
    pick = jnp.where(lax.broadcasted_iota(jnp.int32, (N_HEADS, LANES), 1)
                     == lax.broadcasted_iota(jnp.int32, (N_HEADS, LANES), 0) + N_HEADS, 1.0, 0.0)
    parts = {name: [] for name in ("q", "k", "v", "kt", "gc", "gr", "bb")}
    for c in range(n_chunks):
        rows = slice(c * cs, (c + 1) * cs)
        gcum = _dot_f32(tril, glog[rows])
        g_rows = lax.dot_general(pick, gcum, (((1,), (1,)), ((), ())), precision=HIGHEST,
                                 preferred_element_type=F32)
        for h in range(N_HEADS):
            lanes = slice(h * hd, (h + 1) * hd)
            parts["q"].append(qn[rows, lanes])
            parts["k"].append(kn[rows, lanes])
            parts["v"].append(vv[rows, lanes])
            parts["kt"].append(kn_t[lanes, rows])
            parts["gc"].append(jnp.broadcast_to(gcum[:, N_HEADS + h:N_HEADS + h + 1], (cs, cs)))
            parts["gr"].append(g_rows[h:h + 1, :])
            parts["bb"].append(jnp.broadcast_to(beta[rows, h:h + 1], (cs, cs)))
    st = {name: jnp.stack(vals) for name, vals in parts.items()}
    u_all, w_qd, qk, k_dec_t, chunk_decay = _gdn_prepare(
        st["q"], st["k"], st["v"], st["kt"], st["gc"], st["gr"], st["bb"], row, col)
    u_ref[...] = u_all.reshape(u_ref.shape)
    wqd_ref[...] = w_qd.reshape(wqd_ref.shape)
    qk_ref[...] = qk.reshape(qk_ref.shape)
    kdt_ref[...] = k_dec_t.reshape(kdt_ref.shape)
    cd_ref[...] = chunk_decay.reshape(cd_ref.shape)

    def chunk_body(c, carry):
        o, s_new = _gdn_step(u_ref[c], wqd_ref[c], qk_ref[c], kdt_ref[c], cd_ref[c], s_ref[...])
        oh_ref[c] = o
        s_ref[...] = s_new
        return carry

    lax.fori_loop(0, n_chunks, chunk_body, 0)

    z = z_ref[...]
    nw = nw_ref[...]
    for h in range(N_HEADS):
        lanes = slice(h * hd, (h + 1) * hd)
        o = jnp.concatenate([oh_ref[c, h] for c in range(n_chunks)], axis=0)
        o = o * lax.rsqrt(jnp.mean(o * o, axis=-1, keepdims=True) + 1e-6) * nw
        o_ref[:, lanes] = (o * _silu(z[:, lanes])).astype(o_ref.dtype)

    @pl.when(t == pl.num_programs(1) - 1)
    def _():
        sout_ref[...] = s_ref[...]


def _gdn_prompt(proj, batch, seq, tl, conv_w, a_log, dt_bias, norm_w):
    hd = HEAD_DIM
    cs = GDN_CHUNK
    n_steps = seq // tl
    n_chunks = tl // cs
    col_x, col_z, col_gab = 1, D_IN_MAIN // D_HEADS - 1, D_IN_MAIN // LANES
    lane = np.arange(D_HEADS)
    seg = jnp.asarray((lane[:, None] // hd == lane[None, :] // hd).astype(np.float32)).astype(BF16)
    alog_row = jnp.zeros((1, LANES), F32).at[0, N_HEADS:2 * N_HEADS].set(a_log)
    dtb_row = jnp.zeros((1, LANES), F32).at[0, N_HEADS:2 * N_HEADS].set(dt_bias)
    const2 = lambda b, t: (0, 0)
    per_inst = pltpu.VMEM((n_chunks, N_HEADS, cs, hd), F32)
    per_inst_bf16 = pltpu.VMEM((n_chunks, N_HEADS, hd, cs), BF16)
    return pl.pallas_call(
        _gdn_prompt_kernel,
        grid=(batch, n_steps),
        in_specs=[pl.BlockSpec((tl, D_CONV), lambda b, t: (b * n_steps + t, col_x)),
                  pl.BlockSpec((tl, D_HEADS), lambda b, t: (b * n_steps + t, col_z)),
                  pl.BlockSpec((tl, LANES), lambda b, t: (b * n_steps + t, col_gab)),
                  pl.BlockSpec((tl, D_HEADS), lambda b, t: (b * n_steps + t, 1)),
                  pl.BlockSpec((tl, D_HEADS), lambda b, t: (b * n_steps + t, 2)),
                  pl.BlockSpec((GDN_CONV, D_CONV), const2),
                  pl.BlockSpec((1, LANES), const2),
                  pl.BlockSpec((1, LANES), const2),
                  pl.BlockSpec((1, hd), const2),
                  pl.BlockSpec((D_HEADS, D_HEADS), const2)],
        out_specs=[pl.BlockSpec((tl, D_HEADS), lambda b, t: (b * n_steps + t, 0)),
                   pl.BlockSpec((None, GDN_CONV - 1, D_CONV), lambda b, t: (b, 0, 0)),
                   pl.BlockSpec((None, N_HEADS, hd, hd), lambda b, t: (b, 0, 0, 0)),
                   pl.BlockSpec((None, N_HEADS, hd, tl), lambda b, t: (b, 0, 0, t)),
                   pl.BlockSpec((None, N_HEADS, hd, tl), lambda b, t: (b, 0, 0, t))],
        out_shape=[jax.ShapeDtypeStruct((batch * seq, D_HEADS), BF16),
                   jax.ShapeDtypeStruct((batch, GDN_CONV - 1, D_CONV), F32),
                   jax.ShapeDtypeStruct((batch, N_HEADS, hd, hd), F32),
                   jax.ShapeDtypeStruct((batch, N_HEADS, hd, seq), F32),
                   jax.ShapeDtypeStruct((batch, N_HEADS, hd, seq), F32)],
        scratch_shapes=[pltpu.VMEM((8 + tl + 8, D_CONV), F32),
                        pltpu.VMEM((8, D_CONV), F32),
                        pltpu.VMEM((N_HEADS, hd, hd), F32),
                        per_inst,
                        pltpu.VMEM((n_chunks, N_HEADS, 2 * cs, hd), BF16),
                        per_inst_bf16, per_inst_bf16,
                        pltpu.VMEM((n_chunks, N_HEADS, 1, hd), F32),
                        per_inst],
        compiler_params=_cparams(2),
        name="gdn_prompt",
    )(proj, proj, proj, proj, proj, conv_w, alog_row, dtb_row, norm_w.reshape(1, hd), seg)


def _gdn_sample_kernel(xq_ref, xk_ref, xv_ref, cq_ref, ck_ref, cv_ref, wq_ref, wk_ref, wv_ref,
                       z_ref, gab_ref, alog_ref, dtb_ref, nw_ref, s_ref,
                       o_ref, sout_ref, kq_ref, gt_ref, ot_ref):
    pair = pl.program_id(0)
    n_q = xq_ref.shape[0]
    hd = HEAD_DIM

    def conv_t(x_ref, c_ref, w_ref):
        xp = [c_ref[i] for i in range(GDN_CONV - 1)] + [x_ref[i] for i in range(n_q)]
        out = []
        for i in range(n_q):
            y = w_ref[0:1, :] * xp[i]
            for m in range(1, GDN_CONV):
                y = y + w_ref[m:m + 1, :] * xp[i + m]
            out.append(_silu(y).T)
        return out

    q_t = conv_t(xq_ref, cq_ref, wq_ref)
    k_t = conv_t(xk_ref, ck_ref, wk_ref)
    v_t = conv_t(xv_ref, cv_ref, wv_ref)

    def l2n(a):
        return a * lax.rsqrt(jnp.sum(a * a, axis=0, keepdims=True) + 1e-6)

    for i in range(n_q):
        gab_t = gab_ref[i].T
        gt_ref[0, i] = 1.0 / (1.0 + jnp.exp(-gab_t))
        gt_ref[1, i] = jnp.exp(-jnp.exp(alog_ref[...]) * _softplus(gab_t + dtb_ref[...]))

    for hh in range(2):
        rows = slice(hh * hd, (hh + 1) * hd)
        head = 2 * pair + hh
        for i in range(n_q):
            kq_ref[0] = l2n(k_t[i][rows])
            kq_ref[1] = l2n(q_t[i][rows]) * (hd ** -0.5)
            beta = gt_ref[0, i, pl.ds(head, 1), :]
            decay = gt_ref[1, i, pl.ds(N_HEADS + head, 1), :]
            src = s_ref if i == 0 else sout_ref

            def ks_body(kk, acc):
                return acc + kq_ref[0, pl.ds(kk, 1), :] * src[hh, kk]

            k_s = lax.fori_loop(0, hd, ks_body, jnp.zeros((hd, k_t[i].shape[1]), F32), unroll=8)
            r = beta * (v_t[i][rows] - decay * k_s)

            def upd_body(kk, acc):
                s_new = decay * src[hh, kk] + kq_ref[0, pl.ds(kk, 1), :] * r
                sout_ref[hh, kk] = s_new
                return acc + kq_ref[1, pl.ds(kk, 1), :] * s_new

            o = lax.fori_loop(0, hd, upd_body, jnp.zeros_like(r), unroll=8)
            ot_ref[i, rows, :] = o * lax.rsqrt(jnp.mean(o * o, axis=0, keepdims=True) + 1e-6) * nw_ref[...]

    for i in range(n_q):
        o_ref[i] = (ot_ref[i].T * _silu(z_ref[i])).astype(o_ref.dtype)


def _gdn_sample(proj_s, conv_state, state_t, conv_w, a_log, dt_bias, norm_w):
    n_q, bsz, _ = proj_s.shape
    hd = HEAD_DIM
    pw = 2 * hd
    n_pairs = N_HEADS // 2
    base = D_CONV // pw
    col = jnp.zeros((LANES, 1), F32)
    alog_col = col.at[N_HEADS:2 * N_HEADS, 0].set(a_log)
    dtb_col = col.at[N_HEADS:2 * N_HEADS, 0].set(dt_bias)
    nw_col = norm_w.reshape(hd, 1)

    def xspec(part):
        return pl.BlockSpec((n_q, bsz, pw), lambda p: (0, 0, base + part * n_pairs + p))

    def cspec(rows, part):
        return pl.BlockSpec((rows, bsz, pw) if rows else (GDN_CONV, pw),
                            (lambda p: (0, 0, part * n_pairs + p)) if rows
                            else (lambda p: (0, part * n_pairs + p)))

    const = lambda p: (0, 0)
    state_spec = pl.BlockSpec((2, hd, hd, bsz), lambda p: (p, 0, 0, 0))
    return pl.pallas_call(
        _gdn_sample_kernel,
        grid=(n_pairs,),
        in_specs=[xspec(0), xspec(1), xspec(2),
                  cspec(GDN_CONV - 1, 0), cspec(GDN_CONV - 1, 1), cspec(GDN_CONV - 1, 2),
                  cspec(0, 0), cspec(0, 1), cspec(0, 2),
                  pl.BlockSpec((n_q, bsz, pw), lambda p: (0, 0, D_IN_MAIN // pw - n_pairs + p)),
                  pl.BlockSpec((n_q, bsz, LANES), lambda p: (0, 0, D_IN_MAIN // LANES)),
                  pl.BlockSpec((LANES, 1), const), pl.BlockSpec((LANES, 1), const),
                  pl.BlockSpec((hd, 1), const),
                  state_spec],
        out_specs=[pl.BlockSpec((n_q, bsz, pw), lambda p: (0, 0, p)), state_spec],
        out_shape=[jax.ShapeDtypeStruct((n_q, bsz, D_HEADS), BF16),
                   jax.ShapeDtypeStruct(state_t.shape, F32)],
        scratch_shapes=[pltpu.VMEM((2, hd, bsz), F32),
                        pltpu.VMEM((2, n_q, LANES, bsz), F32),
                        pltpu.VMEM((n_q, pw, bsz), F32)],
        compiler_params=_cparams(1),
        name="gdn_sample",
    )(proj_s, proj_s, proj_s, conv_state, conv_state, conv_state, conv_w, conv_w, conv_w,
      proj_s, proj_s, alog_col, dtb_col, nw_col, state_t)


TM = 512
TM_MOE = 1024
TJ_MOE = 512
GDN_ROWS = 256
SAMPLE_PAD = 8


def kernel(x_prompt, x_sample, cache_k, cache_v, state_conv, state_gdn, page_table, c_prompt, c_sample,
           ln_in_g, ln_in_b, w_mod, b_mod, w_in, conv_w, a_log, dt_bias, gdn_norm_w, w_out, rel_table,
           ln_g, ln_b, ffn_w_gate, ffn_w_up, ffn_w_down, moe_router, moe_w_gate, moe_w_up, moe_w_down):
    bp, seq, d = x_prompt.shape
    bs, n_q, _ = x_sample.shape
    depth = w_in.shape[0]
    tp, ts = bp * seq, bs * n_q
    h, hd = N_HEADS, HEAD_DIM
    assert bs == MOD_ROWS and seq % TM == 0 and ts % TM == 0 and seq % MOBA_BLOCK == 0
    assert depth == 2 and GDN_CONV - 1 <= n_q <= SAMPLE_PAD and GDN_CHUNK == HEAD_DIM
    alpha = (2 * depth) ** 0.25

    def groups(tm):
        return lambda i: jnp.minimum(i // (seq // tm), bp)

    n_c = bp + bs
    c_all = jnp.pad(jnp.concatenate([c_prompt, c_sample]), ((0, (-n_c) % 8), (0, 0)))
    mod = _mod_vectors(c_all, w_mod, b_mod).reshape(depth, -1, 6, d)
    mod_p = jnp.broadcast_to(mod[:, :bp].transpose(0, 2, 1, 3)[:, :, :, None, :],
                             (depth, 6, bp, MOD_ROWS, d))
    mod_s = mod[:, bp:n_c].transpose(0, 2, 1, 3)[:, :, None]
    mod_all = jnp.concatenate([mod_p, mod_s], axis=2)

    x, u = _ln_mod(x_prompt.reshape(tp, d), x_sample.transpose(1, 0, 2).reshape(ts, d),
                   ln_in_g, ln_in_b, mod_all, 0, TM, groups(TM))

    bias_tab = _moba_bias_tables(rel_table)
    assert page_table.shape[1] * PAGE_SIZE % MOBA_BLOCK == 0
    bias_past, bias_own = _moba_sample_bias(rel_table, n_q)
    cache_kt = cache_k.transpose(0, 1, 3, 4, 2)
    cache_vt = cache_v.transpose(0, 1, 3, 4, 2)
    state_t = state_gdn.transpose(0, 2, 3, 4, 1)
    conv_t = state_conv.transpose(0, 2, 1, 3)

    k_p, v_p, conv_p, gdn_p, k_s, v_s, conv_s, gdn_s = [], [], [], [], [], [], [], []
    for layer in range(depth):
        w_l = w_in[layer]
        w_cat = jnp.concatenate(
            [w_l[:, :D_IN_MAIN], jnp.pad(w_l[:, D_IN_MAIN:], ((0, 0), (0, LANES - 2 * h)))],
            axis=1).astype(BF16)
        proj, qkv = _proj_in(u, w_cat, TM)
        proj_s = proj[tp:].reshape(n_q, bs, -1)

        def slots_s(cols):
            return jnp.pad(proj_s[:, :, cols].transpose(1, 0, 2), ((0, 0), (0, SAMPLE_PAD - n_q), (0, 0)))

        k_sr = proj_s[:, :, D_HEADS:2 * D_HEADS].transpose(1, 0, 2).reshape(bs, n_q, h, hd)
        v_sr = proj_s[:, :, 2 * D_HEADS:3 * D_HEADS].transpose(1, 0, 2).reshape(bs, n_q, h, hd)
        att_p = _moba_prompt(qkv, bias_tab, bp, seq)
        att_s = _moba_sample(slots_s(slice(0, D_HEADS)), slots_s(slice(D_HEADS, 2 * D_HEADS)),
                             slots_s(slice(2 * D_HEADS, 3 * D_HEADS)), cache_kt, cache_vt, page_table,
                             layer, bias_past, bias_own, n_q)
        att_s = att_s.reshape(bs, h, SAMPLE_PAD, hd)[:, :, :n_q].transpose(2, 0, 1, 3)
        att = (att_p, att_s.reshape(ts, D_HEADS).astype(BF16))

        o_p, conv_new_p, s_new_p, k_t, v_t = _gdn_prompt(proj, bp, seq, GDN_ROWS, conv_w[layer],
                                                         a_log[layer], dt_bias[layer], gdn_norm_w[layer])
        o_s, s_new_s = _gdn_sample(proj_s, conv_t[layer], state_t[layer], conv_w[layer], a_log[layer],
                                   dt_bias[layer], gdn_norm_w[layer])
        s_new_s = s_new_s.transpose(3, 0, 1, 2)
        conv_new_s = proj_s[n_q - (GDN_CONV - 1):, :, 3 * D_HEADS:3 * D_HEADS + D_CONV].transpose(1, 0, 2)
        gdn_o = (o_p, o_s.reshape(ts, D_HEADS))

        w_o = w_out[layer].astype(BF16)
        i = layer // 2
        if layer % 2 == 0:
            x, u = _res_ln([att, gdn_o], [w_o[:D_HEADS], w_o[D_HEADS:]], x, mod_all, layer, 2, layer,
                           (4, 3), ln_g[layer, 0], ln_b[layer, 0], alpha, TM, groups(TM))
            hdn = _ffn_up(u, ffn_w_gate[i].astype(BF16), ffn_w_up[i].astype(BF16), TM)
            x, u = _res_ln([hdn], [ffn_w_down[i].astype(BF16)], x, mod_all, layer, 5, layer + 1,
                           (1, 0), ln_g[layer, 1], ln_b[layer, 1], alpha, TM, groups(TM))
        else:
            w_r = jnp.pad(moe_router[i], ((0, 0), (0, LANES - N_EXPERTS)))
            x, u, route = _res_ln([att, gdn_o], [w_o[:D_HEADS], w_o[D_HEADS:]], x, mod_all, layer, 2,
                                  layer, (4, 3), ln_g[layer, 0], ln_b[layer, 0], alpha, TM,
                                  groups(TM), w_router=w_r)
            tile_expert, n_valid, token_of_slot, slot = _moe_routing(route, TM_MOE)
            y = _moe_ffn(u, tile_expert, n_valid, token_of_slot, moe_w_gate[i], moe_w_up[i],
                         moe_w_down[i], TM_MOE, TJ_MOE)
            n_t = (tp + ts) // TM
            slots = slot.reshape(n_t, TM, TOP_K).transpose(0, 2, 1).reshape(n_t, 1, TOP_K * TM)
            x_p, x_s = _moe_combine(y, slots, route, x, mod_all, layer, 5, ln_g[layer, 1],
                                    ln_b[layer, 1], alpha, TM, groups(TM), tp // TM)

        k_p.append(k_t)
        v_p.append(v_t)
        conv_p.append(conv_new_p)
        gdn_p.append(s_new_p)
        k_s.append(k_sr)
        v_s.append(v_sr)
        conv_s.append(conv_new_s)
        gdn_s.append(s_new_s)

    y_prompt = x_p.reshape(bp, seq, d)
    y_sample = x_s.reshape(n_q, bs, d).transpose(1, 0, 2)
    k_prompt = jnp.stack(k_p).transpose(0, 1, 4, 2, 3)
    v_prompt = jnp.stack(v_p).transpose(0, 1, 4, 2, 3)
    return (y_prompt, y_sample, k_prompt, v_prompt, jnp.stack(conv_p), jnp.stack(gdn_p),
            jnp.stack(k_s), jnp.stack(v_s), jnp.stack(conv_s), jnp.stack(gdn_s))
```

```python
import functools
import math

import numpy as np
import jax
import jax.numpy as jnp
from jax import lax
from jax.experimental import pallas as pl
from jax.experimental.pallas import tpu as pltpu

F32 = jnp.float32
BF16 = jnp.bfloat16

D_MODEL = 1024
HEAD_DIM = 64
N_HEADS = 8
D_HEADS = N_HEADS * HEAD_DIM
D_CONV = 3 * D_HEADS
D_IN = 3 * D_HEADS + D_CONV + D_HEADS + 2 * N_HEADS
D_IN_MAIN = D_IN - 2 * N_HEADS
LANES = 128
MOBA_BLOCK = 256
MOBA_TOPK = 3
PAGE_SIZE = 128
GDN_CONV = 4
GDN_CHUNK = 64
N_BUCKETS = 32
MAX_DISTANCE = 128
N_EXPERTS = 8
TOP_K = 2
LN_EPS = 1e-5
NEG = -1e30
MOD_ROWS = 128
VMEM_LIMIT = 56 * 1024 * 1024

HIGHEST = lax.Precision.HIGHEST


def _cparams(n_axes):
    return pltpu.CompilerParams(dimension_semantics=("arbitrary",) * n_axes,
                                vmem_limit_bytes=VMEM_LIMIT)


def _silu(x):
    return x * (1.0 / (1.0 + jnp.exp(-x)))


def _layer_norm(x, g, b):
    mu = jnp.mean(x, axis=-1, keepdims=True)
    xc = x - mu
    var = jnp.mean(xc * xc, axis=-1, keepdims=True)
    return xc * lax.rsqrt(var + LN_EPS) * g + b


def _modulate(x, scale, shift):
    tm, d = x.shape
    x3 = x.reshape(tm // MOD_ROWS, MOD_ROWS, d)
    return (x3 * (1.0 + scale[None]) + shift[None]).reshape(tm, d)


def _gated(x, gate, h):
    tm, d = x.shape
    x3 = x.reshape(tm // MOD_ROWS, MOD_ROWS, d)
    h3 = h.reshape(tm // MOD_ROWS, MOD_ROWS, d)
    return (x3 + (1.0 + gate[None]) * h3).reshape(tm, d)


def _mod_kernel(c_ref, w_ref, b_ref, o_ref):
    a = _silu(c_ref[...])
    o_ref[...] = jnp.dot(a, w_ref[...], precision=HIGHEST, preferred_element_type=F32) + b_ref[...]


def _mod_vectors(c_all, w_mod, b_mod, tn=1536):
    depth, d, n = w_mod.shape
    rows = c_all.shape[0]
    return pl.pallas_call(
        _mod_kernel,
        grid=(depth, n // tn),
        in_specs=[pl.BlockSpec((rows, d), lambda l, j: (0, 0)),
                  pl.BlockSpec((None, d, tn), lambda l, j: (l, 0, j)),
                  pl.BlockSpec((None, 1, tn), lambda l, j: (l, 0, j))],
        out_specs=pl.BlockSpec((None, rows, tn), lambda l, j: (l, 0, j)),
        out_shape=jax.ShapeDtypeStruct((depth, rows, n), F32),
        compiler_params=_cparams(2),
        name="mod_vectors",
    )(c_all, w_mod, b_mod.reshape(depth, 1, n))


def _ln_mod_kernel(xa_ref, xb_ref, g_ref, b_ref, mod_ref, xn_ref, u_ref, *, n_first):
    x = jnp.where(pl.program_id(0) < n_first, xa_ref[...], xb_ref[...])
    xn = _layer_norm(x, g_ref[...], b_ref[...])
    xn_ref[...] = xn
    u_ref[...] = _modulate(xn, mod_ref[1], mod_ref[0]).astype(BF16)


def _mod_spec(layer, group_of_tile):
    return pl.BlockSpec((None, 6, None, MOD_ROWS, D_MODEL),
                        lambda i: (layer, 0, group_of_tile(i), 0, 0))


def _ln_mod(x_first, x_rest, g, b, mod_all, layer, tm, group_of_tile):
    d = x_first.shape[1]
    n_first = x_first.shape[0] // tm
    t = x_first.shape[0] + x_rest.shape[0]
    row = pl.BlockSpec((tm, d), lambda i: (i, 0))
    vec = pl.BlockSpec((1, d), lambda i: (0, 0))
    return pl.pallas_call(
        functools.partial(_ln_mod_kernel, n_first=n_first),
        grid=(t // tm,),
        in_specs=[pl.BlockSpec((tm, d), lambda i: (jnp.minimum(i, n_first - 1), 0)),
                  pl.BlockSpec((tm, d), lambda i: (jnp.maximum(i - n_first, 0), 0)),
                  vec, vec, _mod_spec(layer, group_of_tile)],
        out_specs=[row, row],
        out_shape=[jax.ShapeDtypeStruct((t, d), F32), jax.ShapeDtypeStruct((t, d), BF16)],
        compiler_params=_cparams(1),
        name="ln_mod",
    )(x_first, x_rest, g.reshape(1, d), b.reshape(1, d), mod_all)


def _proj_kernel(u_ref, w_ref, proj_ref, qkv_ref):
    u = u_ref[...]
    n = w_ref.shape[1]
    chunk = D_HEADS
    n_qkv = qkv_ref.shape[1]
    for c0 in range(0, n, chunk):
        c1 = min(c0 + chunk, n)
        r = jnp.dot(u, w_ref[:, c0:c1], preferred_element_type=F32)
        proj_ref[:, c0:c1] = r
        if c1 <= n_qkv:
            qkv_ref[:, c0:c1] = r.astype(BF16)


def _proj_in(u, w, tm):
    t, d = u.shape
    n = w.shape[1]
    return pl.pallas_call(
        _proj_kernel,
        grid=(t // tm,),
        in_specs=[pl.BlockSpec((tm, d), lambda i: (i, 0)),
                  pl.BlockSpec((d, n), lambda i: (0, 0))],
        out_specs=[pl.BlockSpec((tm, n), lambda i: (i, 0)),
                   pl.BlockSpec((tm, 3 * D_HEADS), lambda i: (i, 0))],
        out_shape=[jax.ShapeDtypeStruct((t, n), F32),
                   jax.ShapeDtypeStruct((t, 3 * D_HEADS), BF16)],
        compiler_params=_cparams(1),
        name="proj_in",
    )(u, w)


def _top2_gates(logits):
    lane = lax.broadcasted_iota(jnp.int32, logits.shape, 1)
    v1 = jnp.max(logits, axis=-1, keepdims=True)
    i1 = jnp.min(jnp.where(logits == v1, lane, LANES), axis=-1, keepdims=True)
    rest = jnp.where(lane == i1, -jnp.inf, logits)
    v2 = jnp.max(rest, axis=-1, keepdims=True)
    i2 = jnp.min(jnp.where(rest == v2, lane, LANES), axis=-1, keepdims=True)
    e2 = jnp.exp(v2 - v1)
    inv = 1.0 / (1.0 + e2)
    route = jnp.where(lane == 0, i1.astype(F32), 0.0) + jnp.where(lane == 1, i2.astype(F32), 0.0)
    return route + jnp.where(lane == 2, inv, 0.0) + jnp.where(lane == 3, e2 * inv, 0.0)


def _res_ln_kernel(*refs, split, n_first, alpha, next_rows, with_router):
    n_in = len(split)
    a_vals, pos = [], 0
    for two in split:
        if two:
            a_vals.append(jnp.where(pl.program_id(0) < n_first, refs[pos][...], refs[pos + 1][...]))
        else:
            a_vals.append(refs[pos][...])
        pos += 2 if two else 1
    w_refs = refs[pos:pos + n_in]
    x_ref, modg_ref, modn_ref, g_ref, b_ref = refs[pos + n_in:pos + n_in + 5]
    pos += n_in + 5
    if with_router:
        wr_ref = refs[pos]
        pos += 1
    outs = refs[pos:]
    h = jnp.dot(a_vals[0], w_refs[0][...], preferred_element_type=F32)
    for a_val, w_ref in zip(a_vals[1:], w_refs[1:]):
        h = h + jnp.dot(a_val, w_ref[...], preferred_element_type=F32)
    xn = _layer_norm(_gated(alpha * x_ref[...], modg_ref[...], h), g_ref[...], b_ref[...])
    outs[0][...] = xn
    if next_rows is not None:
        u = _modulate(xn, modn_ref[next_rows[0]], modn_ref[next_rows[1]])
        outs[1][...] = u.astype(outs[1].dtype)
        if with_router:
            logits = jnp.dot(u, wr_ref[...], precision=HIGHEST, preferred_element_type=F32)
            lane = lax.broadcasted_iota(jnp.int32, logits.shape, 1)
            outs[2][...] = _top2_gates(jnp.where(lane < N_EXPERTS, logits, -jnp.inf))


def _res_ln(a_list, w_list, x, mod_all, gate_layer, gate_row, next_layer, next_rows, ln_g, ln_b,
            alpha, tm, group_of_tile, w_router=None):
    t, d = x.shape
    row = pl.BlockSpec((tm, d), lambda i: (i, 0))
    vec = pl.BlockSpec((1, d), lambda i: (0, 0))
    split = tuple(isinstance(a, tuple) for a in a_list)
    n_first = a_list[split.index(True)][0].shape[0] // tm if any(split) else 0
    in_specs, flat_a = [], []
    for a in a_list:
        if isinstance(a, tuple):
            in_specs += [pl.BlockSpec((tm, a[0].shape[1]), lambda i: (jnp.minimum(i, n_first - 1), 0)),
                         pl.BlockSpec((tm, a[1].shape[1]), lambda i: (jnp.maximum(i - n_first, 0), 0))]
            flat_a += list(a)
        else:
            in_specs.append(pl.BlockSpec((tm, a.shape[1]), lambda i: (i, 0)))
            flat_a.append(a)
    in_specs += [pl.BlockSpec(w.shape, lambda i: (0, 0)) for w in w_list]
    in_specs += [row,
                 pl.BlockSpec((None, None, None, MOD_ROWS, d),
                              lambda i: (gate_layer, gate_row, group_of_tile(i), 0, 0)),
                 _mod_spec(next_layer if next_rows is not None else gate_layer, group_of_tile),
                 vec, vec]
    args = flat_a + list(w_list) + [x, mod_all, mod_all, ln_g.reshape(1, d), ln_b.reshape(1, d)]
    out_specs = [row]
    out_shape = [jax.ShapeDtypeStruct((t, d), F32)]
    with_router = w_router is not None
    if next_rows is not None:
        out_specs.append(row)
        out_shape.append(jax.ShapeDtypeStruct((t, d), F32 if with_router else BF16))
    if with_router:
        in_specs.append(pl.BlockSpec(w_router.shape, lambda i: (0, 0)))
        args.append(w_router)
        out_specs.append(pl.BlockSpec((tm, LANES), lambda i: (i, 0)))
        out_shape.append(jax.ShapeDtypeStruct((t, LANES), F32))
    return pl.pallas_call(
        functools.partial(_res_ln_kernel, split=split, n_first=n_first, alpha=alpha,
                          next_rows=next_rows, with_router=with_router),
        grid=(t // tm,),
        in_specs=in_specs,
        out_specs=out_specs,
        out_shape=out_shape,
        compiler_params=_cparams(1),
        name="res_ln",
    )(*args)


def _ffn_up_kernel(u_ref, wg_ref, wu_ref, o_ref, *, chunk):
    u = u_ref[...]
    n = o_ref.shape[1]
    for c0 in range(0, n, chunk):
        c1 = min(c0 + chunk, n)
        hg = jnp.dot(u, wg_ref[:, c0:c1], preferred_element_type=F32)
        hu = jnp.dot(u, wu_ref[:, c0:c1], preferred_element_type=F32)
        o_ref[:, c0:c1] = (_silu(hg) * hu).astype(BF16)


def _ffn_up(u, wg, wu, tm, chunk=256):
    t, d = u.shape
    n = wg.shape[1]
    wspec = pl.BlockSpec((d, n), lambda i: (0, 0))
    return pl.pallas_call(
        functools.partial(_ffn_up_kernel, chunk=chunk),
        grid=(t // tm,),
        in_specs=[pl.BlockSpec((tm, d), lambda i: (i, 0)), wspec, wspec],
        out_specs=pl.BlockSpec((tm, n), lambda i: (i, 0)),
        out_shape=jax.ShapeDtypeStruct((t, n), BF16),
        compiler_params=_cparams(1),
        name="ffn_up",
    )(u, wg, wu)


def _row_copy(src_hbm, src_row, dst_ref, r, sem):
    return pltpu.make_async_copy(src_hbm.at[pl.ds(src_row, 1)], dst_ref.at[pl.ds(r, 1)], sem)


def _gather_start(idx_ref, idx_base, src_hbm, dst_ref, sem, n_rows):
    def body(r, carry):
        _row_copy(src_hbm, idx_ref[0, idx_base + r], dst_ref, r, sem).start()
        return carry

    lax.fori_loop(0, n_rows, body, 0, unroll=16)


def _gather_wait(src_hbm, dst_ref, sem):
    pltpu.make_async_copy(src_hbm.at[pl.ds(0, dst_ref.shape[0])], dst_ref, sem).wait()


def _moe_ffn_kernel(te_ref, nv_ref, tos_ref, tos_next_ref, u_hbm, wg_ref, wu_ref, wd_ref, y_ref,
                    xg_ref, xb_ref, acc_ref, sem):
    del te_ref
    i = pl.program_id(0)
    j = pl.program_id(1)
    n_valid = nv_ref[0]
    valid = i < n_valid
    cur = i % 2
    rows = xb_ref.shape[0]

    @pl.when(valid & (i == 0) & (j == 0))
    def _():
        _gather_start(tos_ref, 0, u_hbm, xg_ref.at[0], sem.at[0], rows)

    @pl.when(valid & (j == 0))
    def _():
        _gather_wait(u_hbm, xg_ref.at[cur], sem.at[cur])
        xb_ref[...] = xg_ref[cur].astype(BF16)
        acc_ref[...] = jnp.zeros_like(acc_ref)

    @pl.when((i + 1 < n_valid) & (j == 0))
    def _():
        _gather_start(tos_next_ref, 0, u_hbm, xg_ref.at[1 - cur], sem.at[1 - cur], rows)

    @pl.when(valid)
    def _():
        x = xb_ref[...]
        hg = jnp.dot(x, wg_ref[...].astype(BF16), preferred_element_type=F32)
        hu = jnp.dot(x, wu_ref[...].astype(BF16), preferred_element_type=F32)
        acc_ref[...] += jnp.dot((_silu(hg) * hu).astype(BF16), wd_ref[...].astype(BF16),
                                preferred_element_type=F32)

    last = j == pl.num_programs(1) - 1

    @pl.when(valid & last)
    def _():
        y_ref[...] = acc_ref[...]

    @pl.when(jnp.logical_not(valid) & last)
    def _():
        y_ref[...] = jnp.zeros_like(y_ref)


def _moe_ffn(u, tile_expert, n_valid, token_of_slot, wg, wu, wd, tm, tj):
    n_tiles = tile_expert.shape[0]
    d = u.shape[1]
    f = wg.shape[2]
    assert f % tj == 0
    nj = f // tj

    def jj(i, j, nv):
        return jnp.where(i < nv[0], j, nj - 1)

    grid_spec = pltpu.PrefetchScalarGridSpec(
        num_scalar_prefetch=2,
        grid=(n_tiles, nj),
        in_specs=[pl.BlockSpec((None, 1, tm), lambda i, j, te, nv: (i, 0, 0), memory_space=pltpu.SMEM),
                  pl.BlockSpec((None, 1, tm), lambda i, j, te, nv: (jnp.minimum(i + 1, n_tiles - 1), 0, 0),
                               memory_space=pltpu.SMEM),
                  pl.BlockSpec(memory_space=pl.ANY),
                  pl.BlockSpec((None, d, tj), lambda i, j, te, nv: (te[i], 0, jj(i, j, nv))),
                  pl.BlockSpec((None, d, tj), lambda i, j, te, nv: (te[i], 0, jj(i, j, nv))),
                  pl.BlockSpec((None, tj, d), lambda i, j, te, nv: (te[i], jj(i, j, nv), 0))],
        out_specs=pl.BlockSpec((tm, d), lambda i, j, te, nv: (i, 0)),
        scratch_shapes=[pltpu.VMEM((2, tm, d), F32), pltpu.VMEM((tm, d), BF16),
                        pltpu.VMEM((tm, d), F32), pltpu.SemaphoreType.DMA((2,))])
    return pl.pallas_call(
        _moe_ffn_kernel,
        grid_spec=grid_spec,
        out_shape=jax.ShapeDtypeStruct((n_tiles * tm, d), F32),
        compiler_params=_cparams(2),
        name="moe_ffn",
    )(tile_expert, n_valid, token_of_slot, token_of_slot, u, wg, wu, wd)


def _moe_combine_kernel(slots_ref, slots_next_ref, y_hbm, route_ref, x_ref, modg_ref, g_ref, b_ref,
                        oa_ref, ob_ref, buf_ref, sem, *, alpha, n_first):
    rows = x_ref.shape[0]
    i = pl.program_id(0)
    cur = i % 2

    @pl.when(i == 0)
    def _():
        for k in range(TOP_K):
            _gather_start(slots_ref, k * rows, y_hbm, buf_ref.at[0, k], sem.at[0], rows)

    for k in range(TOP_K):
        _gather_wait(y_hbm, buf_ref.at[cur, k], sem.at[cur])

    @pl.when(i + 1 < pl.num_programs(0))
    def _():
        for k in range(TOP_K):
            _gather_start(slots_next_ref, k * rows, y_hbm, buf_ref.at[1 - cur, k], sem.at[1 - cur], rows)

    route = route_ref[...]
    f = route[:, 2:3] * buf_ref[cur, 0] + route[:, 3:4] * buf_ref[cur, 1]
    o = _layer_norm(_gated(alpha * x_ref[...], modg_ref[...], f), g_ref[...], b_ref[...])

    @pl.when(pl.program_id(0) < n_first)
    def _():
        oa_ref[...] = o

    @pl.when(pl.program_id(0) >= n_first)
    def _():
        ob_ref[...] = o


def _moe_combine(y, slots, route, x, mod_all, gate_layer, gate_row, ln_g, ln_b, alpha, tm,
                 group_of_tile, n_first):
    t, d = x.shape
    row = pl.BlockSpec((tm, d), lambda i: (i, 0))
    vec = pl.BlockSpec((1, d), lambda i: (0, 0))
    return pl.pallas_call(
        functools.partial(_moe_combine_kernel, alpha=alpha, n_first=n_first),
        grid=(t // tm,),
        in_specs=[pl.BlockSpec((None, 1, 2 * tm), lambda i: (i, 0, 0), memory_space=pltpu.SMEM),
                  pl.BlockSpec((None, 1, 2 * tm), lambda i: (jnp.minimum(i + 1, t // tm - 1), 0, 0),
                               memory_space=pltpu.SMEM),
                  pl.BlockSpec(memory_space=pl.ANY),
                  pl.BlockSpec((tm, LANES), lambda i: (i, 0)),
                  row,
                  pl.BlockSpec((None, None, None, MOD_ROWS, d),
                               lambda i: (gate_layer, gate_row, group_of_tile(i), 0, 0)),
                  vec, vec],
        out_specs=[pl.BlockSpec((tm, d), lambda i: (jnp.minimum(i, n_first - 1), 0)),
                   pl.BlockSpec((tm, d), lambda i: (jnp.maximum(i - n_first, 0), 0))],
        out_shape=[jax.ShapeDtypeStruct((n_first * tm, d), F32),
                   jax.ShapeDtypeStruct((t - n_first * tm, d), F32)],
        scratch_shapes=[pltpu.VMEM((2, TOP_K, tm, d), F32), pltpu.SemaphoreType.DMA((2,))],
        compiler_params=_cparams(1),
        name="moe_combine",
    )(slots, slots, y, route, x, mod_all, ln_g.reshape(1, d), ln_b.reshape(1, d))


def _moe_routing(route, tm):
    t = route.shape[0]
    n_assign = TOP_K * t
    n_tiles = -(-n_assign // tm) + N_EXPERTS
    experts = route[:, :TOP_K].astype(jnp.int32).reshape(n_assign)
    onehot = (experts[:, None] == jnp.arange(N_EXPERTS, dtype=jnp.int32)[None, :]).astype(jnp.int32)
    rank = jnp.sum((jnp.cumsum(onehot, axis=0) - onehot) * onehot, axis=1)
    count = jnp.sum(onehot, axis=0)
    padded = (count + tm - 1) // tm * tm
    ends = jnp.cumsum(padded)
    slot = jnp.sum(onehot * (ends - padded)[None, :], axis=1) + rank
    n_valid = ends[-1] // tm
    starts = jnp.arange(n_tiles, dtype=jnp.int32) * tm
    tile_expert = jnp.minimum(jnp.sum((starts[:, None] >= ends[None, :]).astype(jnp.int32), axis=1),
                              N_EXPERTS - 1)
    tile_expert = jnp.where(starts < ends[-1], tile_expert, tile_expert[jnp.maximum(n_valid - 1, 0)])
    token_of_slot = jnp.zeros((n_tiles * tm,), jnp.int32).at[slot].set(
        jnp.arange(n_assign, dtype=jnp.int32) // TOP_K)
    return (tile_expert.astype(jnp.int32), n_valid.reshape(1).astype(jnp.int32),
            token_of_slot.reshape(n_tiles, 1, tm), slot.reshape(t, TOP_K).astype(jnp.int32))


def _rel_bucket_table(max_dist):
    n = np.arange(max_dist + 1)
    max_exact = N_BUCKETS // 2
    ratio = np.log(np.maximum(n, 1).astype(np.float32) / np.float32(max_exact))
    large = max_exact + (ratio / np.float32(math.log(MAX_DISTANCE / max_exact))
                         * np.float32(N_BUCKETS - max_exact)).astype(np.int32)
    large = np.minimum(large, N_BUCKETS - 1)
    return np.where(n < max_exact, n, large).astype(np.int32)


def _bias_lookup(rel_table, bucket_idx):
    onehot = jnp.asarray(np.eye(N_BUCKETS, dtype=np.float32)[bucket_idx.reshape(-1)])
    vals = jnp.dot(onehot, rel_table, precision=HIGHEST)
    return vals.T.reshape((rel_table.shape[1],) + bucket_idx.shape)


def _select_top_blocks(gate, n_valid, ksel):
    blk = lax.broadcasted_iota(jnp.int32, gate.shape, 0)
    nb = gate.shape[0]
    g = jnp.where(blk < n_valid, gate, -jnp.inf)
    keep = blk == n_valid
    for _ in range(ksel):
        mx = jnp.max(g, axis=0, keepdims=True)
        first = jnp.min(jnp.where(g == mx, blk, nb), axis=0, keepdims=True)
        hit = blk == first
        keep = keep | (hit & (blk < n_valid))
        g = jnp.where(hit, -jnp.inf, g)
    return jnp.where(keep, 0.0, NEG)


def _moba_prompt_kernel(q_ref, k_ref, v_ref, bias_ref, o_ref,
                        kmean_ref, vt_ref, qm_ref, sel_ref, m_ref, l_ref, acc_ref, *, ksel):
    qt = pl.program_id(1)
    seq = k_ref.shape[0]
    nb = seq // MOBA_BLOCK
    gw = 4 * HEAD_DIM
    n_groups = N_HEADS // 4
    nt = (((1,), (1,)), ((), ()))

    @pl.when(qt == 0)
    def _():
        row = lax.broadcasted_iota(jnp.int32, (nb, seq), 0)
        col = lax.broadcasted_iota(jnp.int32, (nb, seq), 1)
        pool = jnp.where(col // MOBA_BLOCK == row, 1.0, 0.0).astype(BF16)
        kmean_ref[...] = jnp.dot(pool, k_ref[...], preferred_element_type=F32) * (1.0 / MOBA_BLOCK)
        for n in range(nb):
            vt_ref[n] = v_ref[n * MOBA_BLOCK:(n + 1) * MOBA_BLOCK, :].astype(F32).T.astype(BF16)

    head_of_lane = lax.broadcasted_iota(jnp.int32, (1, gw), 1) // HEAD_DIM
    q = q_ref[...] * (HEAD_DIM ** -0.5)
    kmean = kmean_ref[...].astype(BF16)
    for h in range(N_HEADS):
        g, hh = divmod(h, 4)
        qm = jnp.where(head_of_lane == hh, q[:, g * gw:(g + 1) * gw], 0.0).astype(BF16)
        qm_ref[h] = qm
        gate = lax.dot_general(kmean[:, g * gw:(g + 1) * gw], qm, nt, preferred_element_type=F32)
        sel_ref[h] = _select_top_blocks(gate, qt, ksel)
        m_ref[h] = jnp.full(m_ref.shape[1:], NEG, F32)
        l_ref[h] = jnp.zeros(l_ref.shape[1:], F32)
    acc_ref[...] = jnp.zeros_like(acc_ref)

    def body(step, carry, near):
        n = qt - step
        start = pl.multiple_of(n * MOBA_BLOCK, MOBA_BLOCK)
        kb = k_ref[pl.ds(start, MOBA_BLOCK), :]
        vt = vt_ref[n]
        scores = [lax.dot_general(kb[:, (h // 4) * gw:(h // 4 + 1) * gw], qm_ref[h], nt,
                                  preferred_element_type=F32) for h in range(N_HEADS)]
        probs, alphas = [], []
        for h in range(N_HEADS):
            shift = sel_ref[h, pl.ds(n, 1), :]
            if near:
                s = scores[h] + bias_ref[h, step]
            else:
                s = scores[h]
                shift = shift + bias_ref[h, 2, 0:1, :]
            m_old = m_ref[h]
            m_new = jnp.maximum(m_old, jnp.max(s, axis=0, keepdims=True) + shift)
            alpha = jnp.exp(m_old - m_new)
            p = jnp.exp(s - (m_new - shift))
            l_ref[h] = alpha * l_ref[h] + jnp.sum(p, axis=0, keepdims=True)
            m_ref[h] = m_new
            probs.append(p.astype(BF16))
            alphas.append(alpha)
        outs = [jnp.dot(vt[h * HEAD_DIM:(h + 1) * HEAD_DIM, :], probs[h],
                        preferred_element_type=F32) for h in range(N_HEADS)]
        for h in range(N_HEADS):
            g, hh = divmod(h, 4)
            rows = slice(hh * HEAD_DIM, (hh + 1) * HEAD_DIM)
            acc_ref[g, rows, :] = alphas[h] * acc_ref[g, rows, :] + outs[h]
        return carry

    lax.fori_loop(0, jnp.minimum(qt + 1, 2), functools.partial(body, near=True), 0)
    lax.fori_loop(2, qt + 1, functools.partial(body, near=False), 0)

    for g in range(n_groups):
        o_t = jnp.concatenate(
            [acc_ref[g, hh * HEAD_DIM:(hh + 1) * HEAD_DIM, :] * (1.0 / l_ref[4 * g + hh])
             for hh in range(4)], axis=0)
        o_ref[:, g * gw:(g + 1) * gw] = o_t.T.astype(o_ref.dtype)


def _moba_bias_tables(rel_table):
    bucket = _rel_bucket_table(2 * MOBA_BLOCK)
    i = np.arange(MOBA_BLOCK)[None, :]
    j = np.arange(MOBA_BLOCK)[:, None]
    own = jnp.where(jnp.asarray(i >= j), _bias_lookup(rel_table, bucket[np.maximum(i - j, 0)]), NEG)
    prev = _bias_lookup(rel_table, bucket[MOBA_BLOCK + i - j])
    far = jnp.broadcast_to(rel_table[N_BUCKETS - 1][:, None, None], prev.shape)
    return jnp.stack([own, prev, far], axis=1).astype(F32)


def _moba_prompt(qkv, bias_tab, batch, seq):
    nb = seq // MOBA_BLOCK
    ksel = min(MOBA_TOPK, nb)
    stat = pltpu.VMEM((N_HEADS, 1, MOBA_BLOCK), F32)
    return pl.pallas_call(
        functools.partial(_moba_prompt_kernel, ksel=ksel),
        grid=(batch, nb),
        in_specs=[pl.BlockSpec((MOBA_BLOCK, D_HEADS), lambda b, t: (b * nb + t, 0)),
                  pl.BlockSpec((seq, D_HEADS), lambda b, t: (b, 1)),
                  pl.BlockSpec((seq, D_HEADS), lambda b, t: (b, 2)),
                  pl.BlockSpec(bias_tab.shape, lambda b, t: (0, 0, 0, 0))],
        out_specs=pl.BlockSpec((MOBA_BLOCK, D_HEADS), lambda b, t: (b * nb + t, 0)),
        out_shape=jax.ShapeDtypeStruct((batch * seq, D_HEADS), BF16),
        scratch_shapes=[pltpu.VMEM((nb, D_HEADS), F32),
                        pltpu.VMEM((nb, D_HEADS, MOBA_BLOCK), BF16),
                        pltpu.VMEM((N_HEADS, MOBA_BLOCK, 4 * HEAD_DIM), BF16),
                        pltpu.VMEM((N_HEADS, nb, MOBA_BLOCK), F32),
                        stat, stat,
                        pltpu.VMEM((N_HEADS // 4, 4 * HEAD_DIM, MOBA_BLOCK), F32)],
        compiler_params=_cparams(2),
        name="moba_prompt",
    )(qkv, qkv, qkv, bias_tab)


def _moba_sample_kernel(pt_ref, q_ref, kn_ref, vn_ref, *refs, ksel, n_q, nbp, pages_per_block):
    del pt_ref
    n_pages = nbp * pages_per_block
    k_refs, v_refs = refs[:n_pages], refs[n_pages:2 * n_pages]
    bias_ref, bown_ref, o_ref, m_ref, l_ref, g_ref, acc_ref = refs[2 * n_pages:]
    nt = (((1,), (1,)), ((), ()))
    dh = N_HEADS * HEAD_DIM
    q8 = q_ref[...]
    slots = q8.shape[0]
    lane_head = lax.broadcasted_iota(jnp.int32, (1, dh), 1) // HEAD_DIM
    qbd = jnp.concatenate([jnp.where(lane_head == hp, q8, 0.0) for hp in range(N_HEADS)], axis=0)
    qs = (qbd * (HEAD_DIM ** -0.5)).astype(BF16)
    ones = jnp.ones((8, PAGE_SIZE), BF16)

    kts = [k_ref[...].reshape(dh, PAGE_SIZE).astype(BF16) for k_ref in k_refs]
    s_pages = [jnp.dot(qs, kt, preferred_element_type=F32) for kt in kts]
    ksums = [lax.dot_general(ones, kt, nt, preferred_element_type=F32) for kt in kts]
    probs = []
    for j in range(nbp):
        pages = range(j * pages_per_block, (j + 1) * pages_per_block)
        s = jnp.concatenate([s_pages[pg] for pg in pages], axis=1) + bias_ref[0 if j == nbp - 1 else 1]
        m = jnp.max(s, axis=-1, keepdims=True)
        p32 = jnp.exp(s - m)
        probs.append(p32.astype(BF16))
        m_ref[j] = m
        l_ref[j] = jnp.sum(p32, axis=-1, keepdims=True)
        ksum = functools.reduce(jnp.add, [ksums[pg] for pg in pages])
        g_ref[j] = jnp.sum(qbd * ksum[0:1, :], axis=-1, keepdims=True) * (1.0 / MOBA_BLOCK)
    pv = [lax.dot_general(probs[pg // pages_per_block][:, (pg % pages_per_block) * PAGE_SIZE:
                                                       (pg % pages_per_block + 1) * PAGE_SIZE],
                          v_refs[pg][...].reshape(dh, PAGE_SIZE).astype(BF16), nt,
                          preferred_element_type=F32) for pg in range(n_pages)]
    for j in range(nbp):
        acc_ref[j] = functools.reduce(jnp.add, pv[j * pages_per_block:(j + 1) * pages_per_block])

    s_own = [jnp.sum(qbd * kn_ref[i:i + 1, :], axis=-1, keepdims=True) * (HEAD_DIM ** -0.5)
             + bown_ref[:, i:i + 1] for i in range(n_q)]
    m_o = functools.reduce(jnp.maximum, s_own)
    p_own = [jnp.exp(s_i - m_o) for s_i in s_own]
    l_o = functools.reduce(jnp.add, p_own)
    acc_o = functools.reduce(jnp.add, [p_i * vn_ref[i:i + 1, :] for i, p_i in enumerate(p_own)])

    gg = g_ref[...]
    blk = lax.broadcasted_iota(jnp.int32, gg.shape, 0)
    keep = blk < 0
    for _ in range(ksel):
        mx = jnp.max(gg, axis=0, keepdims=True)
        first = jnp.min(jnp.where(gg == mx, blk, nbp), axis=0, keepdims=True)
        hit = (blk == first) & (gg > -jnp.inf)
        keep = keep | hit
        gg = jnp.where(blk == first, -jnp.inf, gg)
    m_all = m_ref[...]
    m_tot = jnp.maximum(jnp.max(jnp.where(keep, m_all, NEG), axis=0), m_o)
    w = jnp.where(keep, jnp.exp(m_all - m_tot[None]), 0.0)
    w_o = jnp.exp(m_o - m_tot)
    l_tot = jnp.sum(w * l_ref[...], axis=0) + w_o * l_o
    acc_tot = (jnp.sum(w * acc_ref[...], axis=0) + w_o * acc_o) * (1.0 / l_tot)
    row_head = lax.broadcasted_iota(jnp.int32, (N_HEADS * slots, 1), 0) // slots
    acc_tot = jnp.where(row_head == lane_head, acc_tot, 0.0)
    o_ref[...] = functools.reduce(
        jnp.add, [acc_tot[:, hp * HEAD_DIM:(hp + 1) * HEAD_DIM] for hp in range(N_HEADS)])


def _moba_sample_bias(rel_table, n_q):
    slots = SAMPLE_PAD
    bucket = _rel_bucket_table(MOBA_BLOCK + slots)
    rows_h = np.repeat(np.arange(N_HEADS), slots)
    rows_i = np.minimum(np.tile(np.arange(slots), N_HEADS), n_q - 1)
    pos = np.arange(MOBA_BLOCK)[None, :]
    by_head = _bias_lookup(rel_table, bucket[MOBA_BLOCK + rows_i[:, None] - pos])
    pick = jnp.asarray(np.arange(N_HEADS)[:, None, None] == rows_h[None, :, None])
    near = jnp.sum(jnp.where(pick, by_head, 0.0), axis=0)
    far = jnp.broadcast_to(jnp.repeat(rel_table[N_BUCKETS - 1], slots)[:, None], near.shape)
    past = jnp.stack([near, far]).astype(F32)
    new = np.arange(slots)[None, :]
    own_h = _bias_lookup(rel_table, bucket[np.maximum(rows_i[:, None] - new, 0)])
    own = jnp.sum(jnp.where(pick, own_h, 0.0), axis=0)
    ok = jnp.asarray((new <= rows_i[:, None]) & (new < n_q))
    return past, jnp.where(ok, own, NEG).astype(F32)


def _moba_sample(q_rows, k_new, v_new, cache_k, cache_v, page_table, layer, bias_past, bias_own, n_q):
    bsz, slots, dh = q_rows.shape
    hd = HEAD_DIM
    rows = N_HEADS * slots
    pages_per_block = MOBA_BLOCK // PAGE_SIZE
    nbp = page_table.shape[1] // pages_per_block
    ksel = min(MOBA_TOPK, nbp + 1)

    n_pages = nbp * pages_per_block

    def page_spec(page):
        return pl.BlockSpec((None, None, N_HEADS, hd, PAGE_SIZE),
                            lambda b, pt: (layer, pt[b, page], 0, 0, 0))

    per_seq = lambda b, pt: (b, 0, 0)
    stat = pltpu.VMEM((nbp, rows, 1), F32)
    grid_spec = pltpu.PrefetchScalarGridSpec(
        num_scalar_prefetch=1,
        grid=(bsz,),
        in_specs=([pl.BlockSpec((None, slots, dh), per_seq)] * 3
                  + [page_spec(page) for page in range(n_pages)] * 2
                  + [pl.BlockSpec(bias_past.shape, lambda b, pt: (0, 0, 0)),
                     pl.BlockSpec(bias_own.shape, lambda b, pt: (0, 0))]),
        out_specs=pl.BlockSpec((None, rows, hd), per_seq),
        scratch_shapes=[stat, stat, stat, pltpu.VMEM((nbp, rows, dh), F32)])
    return pl.pallas_call(
        functools.partial(_moba_sample_kernel, ksel=ksel, n_q=n_q, nbp=nbp,
                          pages_per_block=pages_per_block),
        grid_spec=grid_spec,
        out_shape=jax.ShapeDtypeStruct((bsz, rows, hd), F32),
        compiler_params=_cparams(1),
        name="moba_sample",
    )(page_table, q_rows, k_new, v_new, *([cache_k] * n_pages), *([cache_v] * n_pages),
      bias_past, bias_own)


def _softplus(x):
    return jnp.maximum(x, 0.0) + jnp.log1p(jnp.exp(-jnp.abs(x)))


def _bmm(a, b):
    return jnp.matmul(a.astype(BF16), b.astype(BF16), preferred_element_type=F32)


def _dot_f32(a, b):
    return jnp.dot(a, b, precision=HIGHEST, preferred_element_type=F32)


def _split_bf16(a):
    hi = a.astype(BF16)
    return hi, (a - hi.astype(F32)).astype(BF16)


def _bmm_3pass(a, b):
    a_hi, a_lo = _split_bf16(a)
    b_hi, b_lo = _split_bf16(b)
    return (jnp.matmul(a_hi, b_hi, preferred_element_type=F32)
            + jnp.matmul(a_hi, b_lo, preferred_element_type=F32)
            + jnp.matmul(a_lo, b_hi, preferred_element_type=F32))


def _unit_lower_inverse(a, row, col):
    c = a.shape[1]
    eye = jnp.where(row == col, 1.0, 0.0)
    in16 = (row // 16) == (col // 16)
    in32 = (row // 32) == (col // 32)
    nil = jnp.where(in16, -a, 0.0)
    x = eye + nil
    p = _bmm_3pass(nil, nil)
    for _ in range(2):
        xp = _bmm_3pass(jnp.concatenate([x, p], axis=1), p)
        x = x + xp[:, :c]
        p = xp[:, c:]
    x = x + _bmm_3pass(x, p)
    for off in (jnp.where(in32 & jnp.logical_not(in16), a, 0.0),
                jnp.where(in32, 0.0, a)):
        x = x - _bmm(x, _bmm(off, x))
    return x


def _gdn_prepare(q, k, v, k_t, g_col, g_row, beta, row, col):
    c = q.shape[1]
    incl = row >= col
    decay = jnp.where(incl, jnp.exp(jnp.where(incl, g_col - g_row, 0.0)), 0.0)
    qk_kk = _bmm(jnp.concatenate([q, k], axis=1), k_t)
    a_mat = jnp.where(row > col, qk_kk[:, c:] * decay * beta, 0.0)
    e_g = jnp.exp(g_col)
    rhs = jnp.concatenate([v * beta, k * beta * e_g], axis=2)
    sol = _bmm(_unit_lower_inverse(a_mat, row, col), rhs)
    dv = v.shape[2]
    g_last = g_col[:, c - 1:c, :]
    w_qd = jnp.concatenate([sol[:, :, dv:], q * e_g], axis=1).astype(BF16)
    qk = jnp.where(incl, qk_kk[:, :c] * decay, 0.0).astype(BF16)
    k_dec_t = (k_t * jnp.exp(g_last - g_row)).astype(BF16)
    return sol[:, :, :dv], w_qd, qk, k_dec_t, jnp.exp(g_last)


def _gdn_step(u, w_qd, qk, k_dec_t, chunk_decay, state):
    c = u.shape[1]
    ws_qs = jnp.matmul(w_qd, state.astype(BF16), preferred_element_type=F32)
    v_new = (u - ws_qs[:, :c]).astype(BF16)
    o = ws_qs[:, c:] + jnp.matmul(qk, v_new, preferred_element_type=F32)
    new_state = state * chunk_decay + jnp.matmul(k_dec_t, v_new, preferred_element_type=F32)
    return o, new_state


def _gdn_prompt_kernel(x_ref, z_ref, gab_ref, ak_ref, av_ref, cw_ref, alog_ref, dtb_ref, nw_ref, seg_ref,
                       o_ref, conv_ref, sout_ref, kt_ref, vt_ref,
                       xbuf_ref, halo_ref, s_ref, u_ref, wqd_ref, qk_ref, kdt_ref, cd_ref, oh_ref):
    t = pl.program_id(1)
    tl = x_ref.shape[0]
    cs = GDN_CHUNK
    n_chunks = tl // cs
    hd = HEAD_DIM
    kt_ref[...] = ak_ref[...].T.reshape(kt_ref.shape)
    vt_ref[...] = av_ref[...].T.reshape(vt_ref.shape)

    @pl.when(t == 0)
    def _():
        halo_ref[...] = jnp.zeros_like(halo_ref)
        s_ref[...] = jnp.zeros_like(s_ref)

    x = x_ref[...]
    xbuf_ref[0:8, :] = halo_ref[...]
    xbuf_ref[8:8 + tl, :] = x
    y = cw_ref[0:1, :] * xbuf_ref[5:5 + tl, :]
    for i in range(1, GDN_CONV):
        y = y + cw_ref[i:i + 1, :] * xbuf_ref[5 + i:5 + i + tl, :]
    y = _silu(y)
    tail = x[tl - 8:, :]
    halo_ref[...] = tail
    conv_ref[...] = tail[8 - (GDN_CONV - 1):, :]

    dh = N_HEADS * hd
    seg = seg_ref[...]

    def l2n(a):
        sq_hi, sq_lo = _split_bf16(a * a)
        ss = (jnp.dot(sq_hi, seg, preferred_element_type=F32)
              + jnp.dot(sq_lo, seg, preferred_element_type=F32))
        return a * lax.rsqrt(ss + 1e-6)

    qn = l2n(y[:, :dh]) * (hd ** -0.5)
    kn = l2n(y[:, dh:2 * dh])
    vv = y[:, 2 * dh:]
    kn_t = kn.T

    gab = gab_ref[...]
    beta = 1.0 / (1.0 + jnp.exp(-gab))
    glog = -jnp.exp(alog_ref[...]) * _softplus(gab + dtb_ref[...])

    row = lax.broadcasted_iota(jnp.int32, (cs, cs), 0)
    col = lax.broadcasted_iota(jnp.int32, (cs, cs), 1)
    tril = jnp.where(row >= col, 1.0, 0.0)
    pick = jnp.where(lax.broadcasted_iota(jnp.int32, (N_HEADS, LANES), 1)
                     == lax.broadcasted_iota(jnp.int32, (N_HEADS, LANES), 0) + N_HEADS, 1.0, 0.0)
    parts = {name: [] for name in ("q", "k", "v", "kt", "gc", "gr", "bb")}
    for c in range(n_chunks):
        rows = slice(c * cs, (c + 1) * cs)
        gcum = _dot_f32(tril, glog[rows])
        g_rows = lax.dot_general(pick, gcum, (((1,), (1,)), ((), ())), precision=HIGHEST,
                                 preferred_element_type=F32)
        for h in range(N_HEADS):
            lanes = slice(h * hd, (h + 1) * hd)
            parts["q"].append(qn[rows, lanes])
            parts["k"].append(kn[rows, lanes])
            parts["v"].append(vv[rows, lanes])
            parts["kt"].append(kn_t[lanes, rows])
            parts["gc"].append(jnp.broadcast_to(gcum[:, N_HEADS + h:N_HEADS + h + 1], (cs, cs)))
            parts["gr"].append(g_rows[h:h + 1, :])
            parts["bb"].append(jnp.broadcast_to(beta[rows, h:h + 1], (cs, cs)))
    st = {name: jnp.stack(vals) for name, vals in parts.items()}
    u_all, w_qd, qk, k_dec_t, chunk_decay = _gdn_prepare(
        st["q"], st["k"], st["v"], st["kt"], st["gc"], st["gr"], st["bb"], row, col)
    u_ref[...] = u_all.reshape(u_ref.shape)
    wqd_ref[...] = w_qd.reshape(wqd_ref.shape)
    qk_ref[...] = qk.reshape(qk_ref.shape)
    kdt_ref[...] = k_dec_t.reshape(kdt_ref.shape)
    cd_ref[...] = chunk_decay.reshape(cd_ref.shape)

    def chunk_body(c, carry):
        o, s_new = _gdn_step(u_ref[c], wqd_ref[c], qk_ref[c], kdt_ref[c], cd_ref[c], s_ref[...])
        oh_ref[c] = o
        s_ref[...] = s_new
        return carry

    lax.fori_loop(0, n_chunks, chunk_body, 0)

    o_all = jnp.concatenate(
        [jnp.concatenate([oh_ref[c, h] for c in range(n_chunks)], axis=0) for h in range(N_HEADS)],
        axis=1)
    sq_hi, sq_lo = _split_bf16(o_all * o_all)
    ss = (jnp.dot(sq_hi, seg, preferred_element_type=F32)
          + jnp.dot(sq_lo, seg, preferred_element_type=F32))
    o_all = o_all * lax.rsqrt(ss * (1.0 / hd) + 1e-6) * nw_ref[...]
    o_ref[...] = (o_all * _silu(z_ref[...])).astype(o_ref.dtype)

    @pl.when(t == pl.num_programs(1) - 1)
    def _():
        sout_ref[...] = s_ref[...]


def _gdn_prompt(proj, batch, seq, tl, conv_w, a_log, dt_bias, norm_w):
    hd = HEAD_DIM
    cs = GDN_CHUNK
    n_steps = seq // tl
    n_chunks = tl // cs
    col_x, col_z, col_gab = 1, D_IN_MAIN // D_HEADS - 1, D_IN_MAIN // LANES
    lane = np.arange(D_HEADS)
    seg = jnp.asarray((lane[:, None] // hd == lane[None, :] // hd).astype(np.float32)).astype(BF16)
    alog_row = jnp.zeros((1, LANES), F32).at[0, N_HEADS:2 * N_HEADS].set(a_log)
    dtb_row = jnp.zeros((1, LANES), F32).at[0, N_HEADS:2 * N_HEADS].set(dt_bias)
    const2 = lambda b, t: (0, 0)
    per_inst = pltpu.VMEM((n_chunks, N_HEADS, cs, hd), F32)
    per_inst_bf16 = pltpu.VMEM((n_chunks, N_HEADS, hd, cs), BF16)
    return pl.pallas_call(
        _gdn_prompt_kernel,
        grid=(batch, n_steps),
        in_specs=[pl.BlockSpec((tl, D_CONV), lambda b, t: (b * n_steps + t, col_x)),
                  pl.BlockSpec((tl, D_HEADS), lambda b, t: (b * n_steps + t, col_z)),
                  pl.BlockSpec((tl, LANES), lambda b, t: (b * n_steps + t, col_gab)),
                  pl.BlockSpec((tl, D_HEADS), lambda b, t: (b * n_steps + t, 1)),
                  pl.BlockSpec((tl, D_HEADS), lambda b, t: (b * n_steps + t, 2)),
                  pl.BlockSpec((GDN_CONV, D_CONV), const2),
                  pl.BlockSpec((1, LANES), const2),
                  pl.BlockSpec((1, LANES), const2),
                  pl.BlockSpec((1, D_HEADS), const2),
                  pl.BlockSpec((D_HEADS, D_HEADS), const2)],
        out_specs=[pl.BlockSpec((tl, D_HEADS), lambda b, t: (b * n_steps + t, 0)),
                   pl.BlockSpec((None, GDN_CONV - 1, D_CONV), lambda b, t: (b, 0, 0)),
                   pl.BlockSpec((None, N_HEADS, hd, hd), lambda b, t: (b, 0, 0, 0)),
                   pl.BlockSpec((None, N_HEADS, hd, tl), lambda b, t: (b, 0, 0, t)),
                   pl.BlockSpec((None, N_HEADS, hd, tl), lambda b, t: (b, 0, 0, t))],
        out_shape=[jax.ShapeDtypeStruct((batch * seq, D_HEADS), BF16),
                   jax.ShapeDtypeStruct((batch, GDN_CONV - 1, D_CONV), F32),
                   jax.ShapeDtypeStruct((batch, N_HEADS, hd, hd), F32),
                   jax.ShapeDtypeStruct((batch, N_HEADS, hd, seq), F32),
                   jax.ShapeDtypeStruct((batch, N_HEADS, hd, seq), F32)],
        scratch_shapes=[pltpu.VMEM((8 + tl + 8, D_CONV), F32),
                        pltpu.VMEM((8, D_CONV), F32),
                        pltpu.VMEM((N_HEADS, hd, hd), F32),
                        per_inst,
                        pltpu.VMEM((n_chunks, N_HEADS, 2 * cs, hd), BF16),
                        per_inst_bf16, per_inst_bf16,
                        pltpu.VMEM((n_chunks, N_HEADS, 1, hd), F32),
                        per_inst],
        compiler_params=_cparams(2),
        name="gdn_prompt",
    )(proj, proj, proj, proj, proj, conv_w, alog_row, dtb_row,
      jnp.tile(norm_w, N_HEADS).reshape(1, D_HEADS), seg)


def _gdn_sample_kernel(xq_ref, xk_ref, xv_ref, cq_ref, ck_ref, cv_ref, wq_ref, wk_ref, wv_ref,
                       z_ref, gab_ref, alog_ref, dtb_ref, nw_ref, s_ref,
                       o_ref, sout_ref, kq_ref, gt_ref, ot_ref):
    pair = pl.program_id(0)
    n_q = xq_ref.shape[0]
    hd = HEAD_DIM

    def conv_t(x_ref, c_ref, w_ref):
        xp = [c_ref[i] for i in range(GDN_CONV - 1)] + [x_ref[i] for i in range(n_q)]
        out = []
        for i in range(n_q):
            y = w_ref[0:1, :] * xp[i]
            for m in range(1, GDN_CONV):
                y = y + w_ref[m:m + 1, :] * xp[i + m]
            out.append(_silu(y).T)
        return out

    q_t = conv_t(xq_ref, cq_ref, wq_ref)
    k_t = conv_t(xk_ref, ck_ref, wk_ref)
    v_t = conv_t(xv_ref, cv_ref, wv_ref)

    def l2n(a):
        return a * lax.rsqrt(jnp.sum(a * a, axis=0, keepdims=True) + 1e-6)

    for i in range(n_q):
        gab_t = gab_ref[i].T
        gt_ref[0, i] = 1.0 / (1.0 + jnp.exp(-gab_t))
        gt_ref[1, i] = jnp.exp(-jnp.exp(alog_ref[...]) * _softplus(gab_t + dtb_ref[...]))

    for hh in range(2):
        rows = slice(hh * hd, (hh + 1) * hd)
        head = 2 * pair + hh
        for i in range(n_q):
            kq_ref[0] = l2n(k_t[i][rows])
            kq_ref[1] = l2n(q_t[i][rows]) * (hd ** -0.5)
            beta = gt_ref[0, i, pl.ds(head, 1), :]
            decay = gt_ref[1, i, pl.ds(N_HEADS + head, 1), :]
            src = s_ref if i == 0 else sout_ref

            def ks_body(kk, acc):
                return acc + kq_ref[0, pl.ds(kk, 1), :] * src[hh, kk]

            k_s = lax.fori_loop(0, hd, ks_body, jnp.zeros((hd, k_t[i].shape[1]), F32), unroll=8)
            r = beta * (v_t[i][rows] - decay * k_s)

            def upd_body(kk, acc):
                s_new = decay * src[hh, kk] + kq_ref[0, pl.ds(kk, 1), :] * r
                sout_ref[hh, kk] = s_new
                return acc + kq_ref[1, pl.ds(kk, 1), :] * s_new

            o = lax.fori_loop(0, hd, upd_body, jnp.zeros_like(r), unroll=8)
            ot_ref[i, rows, :] = o * lax.rsqrt(jnp.mean(o * o, axis=0, keepdims=True) + 1e-6) * nw_ref[...]

    for i in range(n_q):
        o_ref[i] = (ot_ref[i].T * _silu(z_ref[i])).astype(o_ref.dtype)


def _gdn_sample(proj_s, conv_state, state_t, conv_w, a_log, dt_bias, norm_w):
    n_q, bsz, _ = proj_s.shape
    hd = HEAD_DIM
    pw = 2 * hd
    n_pairs = N_HEADS // 2
    base = D_CONV // pw
    col = jnp.zeros((LANES, 1), F32)
    alog_col = col.at[N_HEADS:2 * N_HEADS, 0].set(a_log)
    dtb_col = col.at[N_HEADS:2 * N_HEADS, 0].set(dt_bias)
    nw_col = norm_w.reshape(hd, 1)

    def xspec(part):
        return pl.BlockSpec((n_q, bsz, pw), lambda p: (0, 0, base + part * n_pairs + p))

    def cspec(rows, part):
        return pl.BlockSpec((rows, bsz, pw) if rows else (GDN_CONV, pw),
                            (lambda p: (0, 0, part * n_pairs + p)) if rows
                            else (lambda p: (0, part * n_pairs + p)))

    const = lambda p: (0, 0)
    state_spec = pl.BlockSpec((2, hd, hd, bsz), lambda p: (p, 0, 0, 0))
    return pl.pallas_call(
        _gdn_sample_kernel,
        grid=(n_pairs,),
        in_specs=[xspec(0), xspec(1), xspec(2),
                  cspec(GDN_CONV - 1, 0), cspec(GDN_CONV - 1, 1), cspec(GDN_CONV - 1, 2),
                  cspec(0, 0), cspec(0, 1), cspec(0, 2),
                  pl.BlockSpec((n_q, bsz, pw), lambda p: (0, 0, D_IN_MAIN // pw - n_pairs + p)),
                  pl.BlockSpec((n_q, bsz, LANES), lambda p: (0, 0, D_IN_MAIN // LANES)),
                  pl.BlockSpec((LANES, 1), const), pl.BlockSpec((LANES, 1), const),
                  pl.BlockSpec((hd, 1), const),
                  state_spec],
        out_specs=[pl.BlockSpec((n_q, bsz, pw), lambda p: (0, 0, p)), state_spec],
        out_shape=[jax.ShapeDtypeStruct((n_q, bsz, D_HEADS), BF16),
                   jax.ShapeDtypeStruct(state_t.shape, F32)],
        scratch_shapes=[pltpu.VMEM((2, hd, bsz), F32),
                        pltpu.VMEM((2, n_q, LANES, bsz), F32),
                        pltpu.VMEM((n_q, pw, bsz), F32)],
        compiler_params=_cparams(1),
        name="gdn_sample",
    )(proj_s, proj_s, proj_s, conv_state, conv_state, conv_state, conv_w, conv_w, conv_w,
      proj_s, proj_s, alog_col, dtb_col, nw_col, state_t)


TM = 512
TM_MOE = 1024
TJ_MOE = 512
GDN_ROWS = 256
SAMPLE_PAD = 8


def kernel(x_prompt, x_sample, cache_k, cache_v, state_conv, state_gdn, page_table, c_prompt, c_sample,
           ln_in_g, ln_in_b, w_mod, b_mod, w_in, conv_w, a_log, dt_bias, gdn_norm_w, w_out, rel_table,
           ln_g, ln_b, ffn_w_gate, ffn_w_up, ffn_w_down, moe_router, moe_w_gate, moe_w_up, moe_w_down):
    bp, seq, d = x_prompt.shape
    bs, n_q, _ = x_sample.shape
    depth = w_in.shape[0]
    tp, ts = bp * seq, bs * n_q
    h, hd = N_HEADS, HEAD_DIM
    assert bs == MOD_ROWS and seq % TM == 0 and ts % TM == 0 and seq % MOBA_BLOCK == 0
    assert depth == 2 and GDN_CONV - 1 <= n_q <= SAMPLE_PAD and GDN_CHUNK == HEAD_DIM
    alpha = (2 * depth) ** 0.25

    def groups(tm):
        return lambda i: jnp.minimum(i // (seq // tm), bp)

    n_c = bp + bs
    c_all = jnp.pad(jnp.concatenate([c_prompt, c_sample]), ((0, (-n_c) % 8), (0, 0)))
    mod = _mod_vectors(c_all, w_mod, b_mod).reshape(depth, -1, 6, d)
    mod_p = jnp.broadcast_to(mod[:, :bp].transpose(0, 2, 1, 3)[:, :, :, None, :],
                             (depth, 6, bp, MOD_ROWS, d))
    mod_s = mod[:, bp:n_c].transpose(0, 2, 1, 3)[:, :, None]
    mod_all = jnp.concatenate([mod_p, mod_s], axis=2)

    x, u = _ln_mod(x_prompt.reshape(tp, d), x_sample.transpose(1, 0, 2).reshape(ts, d),
                   ln_in_g, ln_in_b, mod_all, 0, TM, groups(TM))

    bias_tab = _moba_bias_tables(rel_table)
    assert page_table.shape[1] * PAGE_SIZE % MOBA_BLOCK == 0
    bias_past, bias_own = _moba_sample_bias(rel_table, n_q)
    cache_kt = cache_k.transpose(0, 1, 3, 4, 2)
    cache_vt = cache_v.transpose(0, 1, 3, 4, 2)
    state_t = state_gdn.transpose(0, 2, 3, 4, 1)
    conv_t = state_conv.transpose(0, 2, 1, 3)

    k_p, v_p, conv_p, gdn_p, k_s, v_s, conv_s, gdn_s = [], [], [], [], [], [], [], []
    for layer in range(depth):
        w_l = w_in[layer]
        w_cat = jnp.concatenate(
            [w_l[:, :D_IN_MAIN], jnp.pad(w_l[:, D_IN_MAIN:], ((0, 0), (0, LANES - 2 * h)))],
            axis=1).astype(BF16)
        proj, qkv = _proj_in(u, w_cat, TM)
        proj_s = proj[tp:].reshape(n_q, bs, -1)

        def slots_s(cols):
            return jnp.pad(proj_s[:, :, cols].transpose(1, 0, 2), ((0, 0), (0, SAMPLE_PAD - n_q), (0, 0)))

        k_sr = proj_s[:, :, D_HEADS:2 * D_HEADS].transpose(1, 0, 2).reshape(bs, n_q, h, hd)
        v_sr = proj_s[:, :, 2 * D_HEADS:3 * D_HEADS].transpose(1, 0, 2).reshape(bs, n_q, h, hd)
        att_p = _moba_prompt(qkv, bias_tab, bp, seq)
        att_s = _moba_sample(slots_s(slice(0, D_HEADS)), slots_s(slice(D_HEADS, 2 * D_HEADS)),
                             slots_s(slice(2 * D_HEADS, 3 * D_HEADS)), cache_kt, cache_vt, page_table,
                             layer, bias_past, bias_own, n_q)
        att_s = att_s.reshape(bs, h, SAMPLE_PAD, hd)[:, :, :n_q].transpose(2, 0, 1, 3)
        att = (att_p, att_s.reshape(ts, D_HEADS).astype(BF16))

        o_p, conv_new_p, s_new_p, k_t, v_t = _gdn_prompt(proj, bp, seq, GDN_ROWS, conv_w[layer],
                                                         a_log[layer], dt_bias[layer], gdn_norm_w[layer])
        o_s, s_new_s = _gdn_sample(proj_s, conv_t[layer], state_t[layer], conv_w[layer], a_log[layer],
                                   dt_bias[layer], gdn_norm_w[layer])
        s_new_s = s_new_s.transpose(3, 0, 1, 2)
        conv_new_s = proj_s[n_q - (GDN_CONV - 1):, :, 3 * D_HEADS:3 * D_HEADS + D_CONV].transpose(1, 0, 2)
        gdn_o = (o_p, o_s.reshape(ts, D_HEADS))

        w_o = w_out[layer].astype(BF16)
        i = layer // 2
        if layer % 2 == 0:
            x, u = _res_ln([att, gdn_o], [w_o[:D_HEADS], w_o[D_HEADS:]], x, mod_all, layer, 2, layer,
                           (4, 3), ln_g[layer, 0], ln_b[layer, 0], alpha, TM, groups(TM))
            hdn = _ffn_up(u, ffn_w_gate[i].astype(BF16), ffn_w_up[i].astype(BF16), TM)
            x, u = _res_ln([hdn], [ffn_w_down[i].astype(BF16)], x, mod_all, layer, 5, layer + 1,
                           (1, 0), ln_g[layer, 1], ln_b[layer, 1], alpha, TM, groups(TM))
        else:
            w_r = jnp.pad(moe_router[i], ((0, 0), (0, LANES - N_EXPERTS)))
            x, u, route = _res_ln([att, gdn_o], [w_o[:D_HEADS], w_o[D_HEADS:]], x, mod_all, layer, 2,
                                  layer, (4, 3), ln_g[layer, 0], ln_b[layer, 0], alpha, TM,
                                  groups(TM), w_router=w_r)
            tile_expert, n_valid, token_of_slot, slot = _moe_routing(route, TM_MOE)
            y = _moe_ffn(u, tile_expert, n_valid, token_of_slot, moe_w_gate[i], moe_w_up[i],
                         moe_w_down[i], TM_MOE, TJ_MOE)
            n_t = (tp + ts) // TM
            slots = slot.reshape(n_t, TM, TOP_K).transpose(0, 2, 1).reshape(n_t, 1, TOP_K * TM)
            x_p, x_s = _moe_combine(y, slots, route, x, mod_all, layer, 5, ln_g[layer, 1],
                                    ln_b[layer, 1], alpha, TM, groups(TM), tp // TM)

        k_p.append(k_t)
        v_p.append(v_t)
        conv_p.append(conv_new_p)
        gdn_p.append(s_new_p)
        k_s.append(k_sr)
        v_s.append(v_sr)
        conv_s.append(conv_new_s)
        gdn_s.append(s_new_s)

    y_prompt = x_p.reshape(bp, seq, d)
    y_sample = x_s.reshape(n_q, bs, d).transpose(1, 0, 2)
    k_prompt = jnp.stack(k_p).transpose(0, 1, 4, 2, 3)
    v_prompt = jnp.stack(v_p).transpose(0, 1, 4, 2, 3)
    return (y_prompt, y_sample, k_prompt, v_prompt, jnp.stack(conv_p), jnp.stack(gdn_p),
            jnp.stack(k_s), jnp.stack(v_s), jnp.stack(conv_s), jnp.stack(gdn_s))
```

```python
import functools
import math

import numpy as np
import jax
import jax.numpy as jnp
from jax import lax
from jax.experimental import pallas as pl
from jax.experimental.pallas import tpu as pltpu

F32 = jnp.float32
BF16 = jnp.bfloat16

D_MODEL = 1024
HEAD_DIM = 64
N_HEADS = 8
D_HEADS = N_HEADS * HEAD_DIM
D_CONV = 3 * D_HEADS
D_IN = 3 * D_HEADS + D_CONV + D_HEADS + 2 * N_HEADS
D_IN_MAIN = D_IN - 2 * N_HEADS
LANES = 128
MOBA_BLOCK = 256
MOBA_TOPK = 3
PAGE_SIZE = 128
GDN_CONV = 4
GDN_CHUNK = 64
N_BUCKETS = 32
MAX_DISTANCE = 128
N_EXPERTS = 8
TOP_K = 2
LN_EPS = 1e-5
NEG = -1e30
MOD_ROWS = 128
VMEM_LIMIT = 56 * 1024 * 1024

HIGHEST = lax.Precision.HIGHEST


def _cparams(n_axes):
    return pltpu.CompilerParams(dimension_semantics=("arbitrary",) * n_axes,
                                vmem_limit_bytes=VMEM_LIMIT)


def _silu(x):
    return x * (1.0 / (1.0 + jnp.exp(-x)))


def _layer_norm(x, g, b):
    mu = jnp.mean(x, axis=-1, keepdims=True)
    xc = x - mu
    var = jnp.mean(xc * xc, axis=-1, keepdims=True)
    return xc * lax.rsqrt(var + LN_EPS) * g + b


def _modulate(x, scale, shift):
    tm, d = x.shape
    x3 = x.reshape(tm // MOD_ROWS, MOD_ROWS, d)
    return (x3 * (1.0 + scale[None]) + shift[None]).reshape(tm, d)


def _gated(x, gate, h):
    tm, d = x.shape
    x3 = x.reshape(tm // MOD_ROWS, MOD_ROWS, d)
    h3 = h.reshape(tm // MOD_ROWS, MOD_ROWS, d)
    return (x3 + (1.0 + gate[None]) * h3).reshape(tm, d)


def _mod_kernel(c_ref, w_ref, b_ref, o_ref):
    a = _silu(c_ref[...])
    o_ref[...] = jnp.dot(a, w_ref[...], precision=HIGHEST, preferred_element_type=F32) + b_ref[...]


def _mod_vectors(c_all, w_mod, b_mod, tn=1536):
    depth, d, n = w_mod.shape
    rows = c_all.shape[0]
    return pl.pallas_call(
        _mod_kernel,
        grid=(depth, n // tn),
        in_specs=[pl.BlockSpec((rows, d), lambda l, j: (0, 0)),
                  pl.BlockSpec((None, d, tn), lambda l, j: (l, 0, j)),
                  pl.BlockSpec((None, 1, tn), lambda l, j: (l, 0, j))],
        out_specs=pl.BlockSpec((None, rows, tn), lambda l, j: (l, 0, j)),
        out_shape=jax.ShapeDtypeStruct((depth, rows, n), F32),
        compiler_params=_cparams(2),
        name="mod_vectors",
    )(c_all, w_mod, b_mod.reshape(depth, 1, n))


def _ln_mod_kernel(xa_ref, xb_ref, g_ref, b_ref, mod_ref, xn_ref, u_ref, *, n_first):
    x = jnp.where(pl.program_id(0) < n_first, xa_ref[...], xb_ref[...])
    xn = _layer_norm(x, g_ref[...], b_ref[...])
    xn_ref[...] = xn
    u_ref[...] = _modulate(xn, mod_ref[1], mod_ref[0]).astype(BF16)


def _mod_spec(layer, group_of_tile):
    return pl.BlockSpec((None, 6, None, MOD_ROWS, D_MODEL),
                        lambda i: (layer, 0, group_of_tile(i), 0, 0))


def _ln_mod(x_first, x_rest, g, b, mod_all, layer, tm, group_of_tile):
    d = x_first.shape[1]
    n_first = x_first.shape[0] // tm
    t = x_first.shape[0] + x_rest.shape[0]
    row = pl.BlockSpec((tm, d), lambda i: (i, 0))
    vec = pl.BlockSpec((1, d), lambda i: (0, 0))
    return pl.pallas_call(
        functools.partial(_ln_mod_kernel, n_first=n_first),
        grid=(t // tm,),
        in_specs=[pl.BlockSpec((tm, d), lambda i: (jnp.minimum(i, n_first - 1), 0)),
                  pl.BlockSpec((tm, d), lambda i: (jnp.maximum(i - n_first, 0), 0)),
                  vec, vec, _mod_spec(layer, group_of_tile)],
        out_specs=[row, row],
        out_shape=[jax.ShapeDtypeStruct((t, d), F32), jax.ShapeDtypeStruct((t, d), BF16)],
        compiler_params=_cparams(1),
        name="ln_mod",
    )(x_first, x_rest, g.reshape(1, d), b.reshape(1, d), mod_all)


def _proj_kernel(u_ref, w_ref, proj_ref, qkv_ref):
    u = u_ref[...]
    n = w_ref.shape[1]
    chunk = D_HEADS
    n_qkv = qkv_ref.shape[1]
    for c0 in range(0, n, chunk):
        c1 = min(c0 + chunk, n)
        r = jnp.dot(u, w_ref[:, c0:c1], preferred_element_type=F32)
        proj_ref[:, c0:c1] = r
        if c1 <= n_qkv:
            qkv_ref[:, c0:c1] = r.astype(BF16)


def _proj_in(u, w, tm):
    t, d = u.shape
    n = w.shape[1]
    return pl.pallas_call(
        _proj_kernel,
        grid=(t // tm,),
        in_specs=[pl.BlockSpec((tm, d), lambda i: (i, 0)),
                  pl.BlockSpec((d, n), lambda i: (0, 0))],
        out_specs=[pl.BlockSpec((tm, n), lambda i: (i, 0)),
                   pl.BlockSpec((tm, 3 * D_HEADS), lambda i: (i, 0))],
        out_shape=[jax.ShapeDtypeStruct((t, n), F32),
                   jax.ShapeDtypeStruct((t, 3 * D_HEADS), BF16)],
        compiler_params=_cparams(1),
        name="proj_in",
    )(u, w)


def _top2_gates(logits):
    lane = lax.broadcasted_iota(jnp.int32, logits.shape, 1)
    v1 = jnp.max(logits, axis=-1, keepdims=True)
    i1 = jnp.min(jnp.where(logits == v1, lane, LANES), axis=-1, keepdims=True)
    rest = jnp.where(lane == i1, -jnp.inf, logits)
    v2 = jnp.max(rest, axis=-1, keepdims=True)
    i2 = jnp.min(jnp.where(rest == v2, lane, LANES), axis=-1, keepdims=True)
    e2 = jnp.exp(v2 - v1)
    inv = 1.0 / (1.0 + e2)
    route = jnp.where(lane == 0, i1.astype(F32), 0.0) + jnp.where(lane == 1, i2.astype(F32), 0.0)
    return route + jnp.where(lane == 2, inv, 0.0) + jnp.where(lane == 3, e2 * inv, 0.0)


def _res_ln_kernel(*refs, split, n_first, alpha, next_rows, with_router):
    n_in = len(split)
    a_vals, pos = [], 0
    for two in split:
        if two:
            a_vals.append(jnp.where(pl.program_id(0) < n_first, refs[pos][...], refs[pos + 1][...]))
        else:
            a_vals.append(refs[pos][...])
        pos += 2 if two else 1
    w_refs = refs[pos:pos + n_in]
    x_ref, modg_ref, modn_ref, g_ref, b_ref = refs[pos + n_in:pos + n_in + 5]
    pos += n_in + 5
    if with_router:
        wr_ref = refs[pos]
        pos += 1
    outs = refs[pos:]
    h = jnp.dot(a_vals[0], w_refs[0][...], preferred_element_type=F32)
    for a_val, w_ref in zip(a_vals[1:], w_refs[1:]):
        h = h + jnp.dot(a_val, w_ref[...], preferred_element_type=F32)
    xn = _layer_norm(_gated(alpha * x_ref[...], modg_ref[...], h), g_ref[...], b_ref[...])
    outs[0][...] = xn
    if next_rows is not None:
        u = _modulate(xn, modn_ref[next_rows[0]], modn_ref[next_rows[1]])
        outs[1][...] = u.astype(outs[1].dtype)
        if with_router:
            logits = jnp.dot(u, wr_ref[...], precision=HIGHEST, preferred_element_type=F32)
            lane = lax.broadcasted_iota(jnp.int32, logits.shape, 1)
            outs[2][...] = _top2_gates(jnp.where(lane < N_EXPERTS, logits, -jnp.inf))


def _res_ln(a_list, w_list, x, mod_all, gate_layer, gate_row, next_layer, next_rows, ln_g, ln_b,
            alpha, tm, group_of_tile, w_router=None):
    t, d = x.shape
    row = pl.BlockSpec((tm, d), lambda i: (i, 0))
    vec = pl.BlockSpec((1, d), lambda i: (0, 0))
    split = tuple(isinstance(a, tuple) for a in a_list)
    n_first = a_list[split.index(True)][0].shape[0] // tm if any(split) else 0
    in_specs, flat_a = [], []
    for a in a_list:
        if isinstance(a, tuple):
            in_specs += [pl.BlockSpec((tm, a[0].shape[1]), lambda i: (jnp.minimum(i, n_first - 1), 0)),
                         pl.BlockSpec((tm, a[1].shape[1]), lambda i: (jnp.maximum(i - n_first, 0), 0))]
            flat_a += list(a)
        else:
            in_specs.append(pl.BlockSpec((tm, a.shape[1]), lambda i: (i, 0)))
            flat_a.append(a)
    in_specs += [pl.BlockSpec(w.shape, lambda i: (0, 0)) for w in w_list]
    in_specs += [row,
                 pl.BlockSpec((None, None, None, MOD_ROWS, d),
                              lambda i: (gate_layer, gate_row, group_of_tile(i), 0, 0)),
                 _mod_spec(next_layer if next_rows is not None else gate_layer, group_of_tile),
                 vec, vec]
    args = flat_a + list(w_list) + [x, mod_all, mod_all, ln_g.reshape(1, d), ln_b.reshape(1, d)]
    out_specs = [row]
    out_shape = [jax.ShapeDtypeStruct((t, d), F32)]
    with_router = w_router is not None
    if next_rows is not None:
        out_specs.append(row)
        out_shape.append(jax.ShapeDtypeStruct((t, d), F32 if with_router else BF16))
    if with_router:
        in_specs.append(pl.BlockSpec(w_router.shape, lambda i: (0, 0)))
        args.append(w_router)
        out_specs.append(pl.BlockSpec((tm, LANES), lambda i: (i, 0)))
        out_shape.append(jax.ShapeDtypeStruct((t, LANES), F32))
    return pl.pallas_call(
        functools.partial(_res_ln_kernel, split=split, n_first=n_first, alpha=alpha,
                          next_rows=next_rows, with_router=with_router),
        grid=(t // tm,),
        in_specs=in_specs,
        out_specs=out_specs,
        out_shape=out_shape,
        compiler_params=_cparams(1),
        name="res_ln",
    )(*args)


def _ffn_up_kernel(u_ref, wg_ref, wu_ref, o_ref, *, chunk):
    u = u_ref[...]
    n = o_ref.shape[1]
    for c0 in range(0, n, chunk):
        c1 = min(c0 + chunk, n)
        hg = jnp.dot(u, wg_ref[:, c0:c1], preferred_element_type=F32)
        hu = jnp.dot(u, wu_ref[:, c0:c1], preferred_element_type=F32)
        o_ref[:, c0:c1] = (_silu(hg) * hu).astype(BF16)


def _ffn_up(u, wg, wu, tm, chunk=256):
    t, d = u.shape
    n = wg.shape[1]
    wspec = pl.BlockSpec((d, n), lambda i: (0, 0))
    return pl.pallas_call(
        functools.partial(_ffn_up_kernel, chunk=chunk),
        grid=(t // tm,),
        in_specs=[pl.BlockSpec((tm, d), lambda i: (i, 0)), wspec, wspec],
        out_specs=pl.BlockSpec((tm, n), lambda i: (i, 0)),
        out_shape=jax.ShapeDtypeStruct((t, n), BF16),
        compiler_params=_cparams(1),
        name="ffn_up",
    )(u, wg, wu)


def _row_copy(src_hbm, src_row, dst_ref, r, sem):
    return pltpu.make_async_copy(src_hbm.at[pl.ds(src_row, 1)], dst_ref.at[pl.ds(r, 1)], sem)


def _gather_start(idx_ref, idx_base, src_hbm, dst_ref, sem, n_rows):
    def body(pair, carry):
        for priority in range(2):
            r = 2 * pair + priority
            _row_copy(src_hbm, idx_ref[0, idx_base + r], dst_ref, r, sem).start(priority=priority)
        return carry

    lax.fori_loop(0, n_rows // 2, body, 0, unroll=8)


def _gather_wait(src_hbm, dst_ref, sem):
    pltpu.make_async_copy(src_hbm.at[pl.ds(0, dst_ref.shape[0])], dst_ref, sem).wait()


def _moe_ffn_kernel(te_ref, nv_ref, tos_ref, tos_next_ref, u_hbm, wg_ref, wu_ref, wd_ref, y_ref,
                    xg_ref, xb_ref, acc_ref, sem):
    del te_ref
    i = pl.program_id(0)
    j = pl.program_id(1)
    n_valid = nv_ref[0]
    valid = i < n_valid
    cur = i % 2
    rows = xb_ref.shape[0]

    @pl.when(valid & (i == 0) & (j == 0))
    def _():
        _gather_start(tos_ref, 0, u_hbm, xg_ref.at[0], sem.at[0], rows)

    @pl.when(valid & (j == 0))
    def _():
        _gather_wait(u_hbm, xg_ref.at[cur], sem.at[cur])
        xb_ref[...] = xg_ref[cur].astype(BF16)
        acc_ref[...] = jnp.zeros_like(acc_ref)

    @pl.when((i + 1 < n_valid) & (j == 0))
    def _():
        _gather_start(tos_next_ref, 0, u_hbm, xg_ref.at[1 - cur], sem.at[1 - cur], rows)

    @pl.when(valid)
    def _():
        x = xb_ref[...]
        hg = jnp.dot(x, wg_ref[...].astype(BF16), preferred_element_type=F32)
        hu = jnp.dot(x, wu_ref[...].astype(BF16), preferred_element_type=F32)
        acc_ref[...] += jnp.dot((_silu(hg) * hu).astype(BF16), wd_ref[...].astype(BF16),
                                preferred_element_type=F32)

    last = j == pl.num_programs(1) - 1

    @pl.when(valid & last)
    def _():
        y_ref[...] = acc_ref[...]

    @pl.when(jnp.logical_not(valid) & last)
    def _():
        y_ref[...] = jnp.zeros_like(y_ref)


def _moe_ffn(u, tile_expert, n_valid, token_of_slot, wg, wu, wd, tm, tj):
    n_tiles = tile_expert.shape[0]
    d = u.shape[1]
    f = wg.shape[2]
    assert f % tj == 0
    nj = f // tj

    def jj(i, j, nv):
        return jnp.where(i < nv[0], j, nj - 1)

    grid_spec = pltpu.PrefetchScalarGridSpec(
        num_scalar_prefetch=2,
        grid=(n_tiles, nj),
        in_specs=[pl.BlockSpec((None, 1, tm), lambda i, j, te, nv: (i, 0, 0), memory_space=pltpu.SMEM),
                  pl.BlockSpec((None, 1, tm), lambda i, j, te, nv: (jnp.minimum(i + 1, n_tiles - 1), 0, 0),
                               memory_space=pltpu.SMEM),
                  pl.BlockSpec(memory_space=pl.ANY),
                  pl.BlockSpec((None, d, tj), lambda i, j, te, nv: (te[i], 0, jj(i, j, nv))),
                  pl.BlockSpec((None, d, tj), lambda i, j, te, nv: (te[i], 0, jj(i, j, nv))),
                  pl.BlockSpec((None, tj, d), lambda i, j, te, nv: (te[i], jj(i, j, nv), 0))],
        out_specs=pl.BlockSpec((tm, d), lambda i, j, te, nv: (i, 0)),
        scratch_shapes=[pltpu.VMEM((2, tm, d), F32), pltpu.VMEM((tm, d), BF16),
                        pltpu.VMEM((tm, d), F32), pltpu.SemaphoreType.DMA((2,))])
    return pl.pallas_call(
        _moe_ffn_kernel,
        grid_spec=grid_spec,
        out_shape=jax.ShapeDtypeStruct((n_tiles * tm, d), F32),
        compiler_params=_cparams(2),
        name="moe_ffn",
    )(tile_expert, n_valid, token_of_slot, token_of_slot, u, wg, wu, wd)


def _moe_combine_kernel(slots_ref, slots_next_ref, y_hbm, route_ref, x_ref, modg_ref, g_ref, b_ref,
                        oa_ref, ob_ref, buf_ref, sem, *, alpha, n_first):
    rows = x_ref.shape[0]
    i = pl.program_id(0)
    cur = i % 2

    @pl.when(i == 0)
    def _():
        for k in range(TOP_K):
            _gather_start(slots_ref, k * rows, y_hbm, buf_ref.at[0, k], sem.at[0], rows)

    for k in range(TOP_K):
        _gather_wait(y_hbm, buf_ref.at[cur, k], sem.at[cur])

    @pl.when(i + 1 < pl.num_programs(0))
    def _():
        for k in range(TOP_K):
            _gather_start(slots_next_ref, k * rows, y_hbm, buf_ref.at[1 - cur, k], sem.at[1 - cur], rows)

    route = route_ref[...]
    f = route[:, 2:3] * buf_ref[cur, 0] + route[:, 3:4] * buf_ref[cur, 1]
    o = _layer_norm(_gated(alpha * x_ref[...], modg_ref[...], f), g_ref[...], b_ref[...])

    @pl.when(pl.program_id(0) < n_first)
    def _():
        oa_ref[...] = o

    @pl.when(pl.program_id(0) >= n_first)
    def _():
        ob_ref[...] = o


def _moe_combine(y, slots, route, x, mod_all, gate_layer, gate_row, ln_g, ln_b, alpha, tm,
                 group_of_tile, n_first):
    t, d = x.shape
    row = pl.BlockSpec((tm, d), lambda i: (i, 0))
    vec = pl.BlockSpec((1, d), lambda i: (0, 0))
    return pl.pallas_call(
        functools.partial(_moe_combine_kernel, alpha=alpha, n_first=n_first),
        grid=(t // tm,),
        in_specs=[pl.BlockSpec((None, 1, 2 * tm), lambda i: (i, 0, 0), memory_space=pltpu.SMEM),
                  pl.BlockSpec((None, 1, 2 * tm), lambda i: (jnp.minimum(i + 1, t // tm - 1), 0, 0),
                               memory_space=pltpu.SMEM),
                  pl.BlockSpec(memory_space=pl.ANY),
                  pl.BlockSpec((tm, LANES), lambda i: (i, 0)),
                  row,
                  pl.BlockSpec((None, None, None, MOD_ROWS, d),
                               lambda i: (gate_layer, gate_row, group_of_tile(i), 0, 0)),
                  vec, vec],
        out_specs=[pl.BlockSpec((tm, d), lambda i: (jnp.minimum(i, n_first - 1), 0)),
                   pl.BlockSpec((tm, d), lambda i: (jnp.maximum(i - n_first, 0), 0))],
        out_shape=[jax.ShapeDtypeStruct((n_first * tm, d), F32),
                   jax.ShapeDtypeStruct((t - n_first * tm, d), F32)],
        scratch_shapes=[pltpu.VMEM((2, TOP_K, tm, d), F32), pltpu.SemaphoreType.DMA((2,))],
        compiler_params=_cparams(1),
        name="moe_combine",
    )(slots, slots, y, route, x, mod_all, ln_g.reshape(1, d), ln_b.reshape(1, d))


def _moe_routing(route, tm):
    t = route.shape[0]
    n_assign = TOP_K * t
    n_tiles = -(-n_assign // tm) + N_EXPERTS
    experts = route[:, :TOP_K].astype(jnp.int32).reshape(n_assign)
    onehot = (experts[:, None] == jnp.arange(N_EXPERTS, dtype=jnp.int32)[None, :]).astype(jnp.int32)
    rank = jnp.sum((jnp.cumsum(onehot, axis=0) - onehot) * onehot, axis=1)
    count = jnp.sum(onehot, axis=0)
    padded = (count + tm - 1) // tm * tm
    ends = jnp.cumsum(padded)
    slot = jnp.sum(onehot * (ends - padded)[None, :], axis=1) + rank
    n_valid = ends[-1] // tm
    starts = jnp.arange(n_tiles, dtype=jnp.int32) * tm
    tile_expert = jnp.minimum(jnp.sum((starts[:, None] >= ends[None, :]).astype(jnp.int32), axis=1),
                              N_EXPERTS - 1)
    tile_expert = jnp.where(starts < ends[-1], tile_expert, tile_expert[jnp.maximum(n_valid - 1, 0)])
    token_of_slot = jnp.zeros((n_tiles * tm,), jnp.int32).at[slot].set(
        jnp.arange(n_assign, dtype=jnp.int32) // TOP_K)
    return (tile_expert.astype(jnp.int32), n_valid.reshape(1).astype(jnp.int32),
            token_of_slot.reshape(n_tiles, 1, tm), slot.reshape(t, TOP_K).astype(jnp.int32))


def _rel_bucket_table(max_dist):
    n = np.arange(max_dist + 1)
    max_exact = N_BUCKETS // 2
    ratio = np.log(np.maximum(n, 1).astype(np.float32) / np.float32(max_exact))
    large = max_exact + (ratio / np.float32(math.log(MAX_DISTANCE / max_exact))
                         * np.float32(N_BUCKETS - max_exact)).astype(np.int32)
    large = np.minimum(large, N_BUCKETS - 1)
    return np.where(n < max_exact, n, large).astype(np.int32)


def _bias_lookup(rel_table, bucket_idx):
    onehot = jnp.asarray(np.eye(N_BUCKETS, dtype=np.float32)[bucket_idx.reshape(-1)])
    vals = jnp.dot(onehot, rel_table, precision=HIGHEST)
    return vals.T.reshape((rel_table.shape[1],) + bucket_idx.shape)


def _select_top_blocks(gate, n_valid, ksel):
    blk = lax.broadcasted_iota(jnp.int32, gate.shape, 0)
    nb = gate.shape[0]
    g = jnp.where(blk < n_valid, gate, -jnp.inf)
    keep = blk == n_valid
    for _ in range(ksel):
        mx = jnp.max(g, axis=0, keepdims=True)
        first = jnp.min(jnp.where(g == mx, blk, nb), axis=0, keepdims=True)
        hit = blk == first
        keep = keep | (hit & (blk < n_valid))
        g = jnp.where(hit, -jnp.inf, g)
    return jnp.where(keep, 0.0, NEG)


def _moba_prompt_kernel(q_ref, k_ref, v_ref, bias_ref, o_ref,
                        kmean_ref, vt_ref, qm_ref, sel_ref, m_ref, l_ref, acc_ref, *, ksel):
    qt = pl.program_id(1)
    seq = k_ref.shape[0]
    nb = seq // MOBA_BLOCK
    gw = 4 * HEAD_DIM
    n_groups = N_HEADS // 4
    nt = (((1,), (1,)), ((), ()))

    @pl.when(qt == 0)
    def _():
        row = lax.broadcasted_iota(jnp.int32, (nb, seq), 0)
        col = lax.broadcasted_iota(jnp.int32, (nb, seq), 1)
        pool = jnp.where(col // MOBA_BLOCK == row, 1.0, 0.0).astype(BF16)
        kmean_ref[...] = jnp.dot(pool, k_ref[...], preferred_element_type=F32) * (1.0 / MOBA_BLOCK)
        for n in range(nb):
            vt_ref[n] = v_ref[n * MOBA_BLOCK:(n + 1) * MOBA_BLOCK, :].astype(F32).T.astype(BF16)

    head_of_lane = lax.broadcasted_iota(jnp.int32, (1, gw), 1) // HEAD_DIM
    q = q_ref[...] * (HEAD_DIM ** -0.5)
    kmean = kmean_ref[...].astype(BF16)
    for h in range(N_HEADS):
        g, hh = divmod(h, 4)
        qm = jnp.where(head_of_lane == hh, q[:, g * gw:(g + 1) * gw], 0.0).astype(BF16)
        qm_ref[h] = qm
        gate = lax.dot_general(kmean[:, g * gw:(g + 1) * gw], qm, nt, preferred_element_type=F32)
        sel_ref[h] = _select_top_blocks(gate, qt, ksel)
        m_ref[h] = jnp.full(m_ref.shape[1:], NEG, F32)
        l_ref[h] = jnp.zeros(l_ref.shape[1:], F32)
    acc_ref[...] = jnp.zeros_like(acc_ref)

    def body(step, carry, near):
        n = qt - step
        start = pl.multiple_of(n * MOBA_BLOCK, MOBA_BLOCK)
        kb = k_ref[pl.ds(start, MOBA_BLOCK), :]
        vt = vt_ref[n]
        scores = [lax.dot_general(kb[:, (h // 4) * gw:(h // 4 + 1) * gw], qm_ref[h], nt,
                                  preferred_element_type=F32) for h in range(N_HEADS)]
        probs, alphas = [], []
        for h in range(N_HEADS):
            shift = sel_ref[h, pl.ds(n, 1), :]
            if near:
                s = scores[h] + bias_ref[h, step]
            else:
                s = scores[h]
                shift = shift + bias_ref[h, 2, 0:1, :]
            m_old = m_ref[h]
            m_new = jnp.maximum(m_old, jnp.max(s, axis=0, keepdims=True) + shift)
            alpha = jnp.exp(m_old - m_new)
            p = jnp.exp(s - (m_new - shift))
            l_ref[h] = alpha * l_ref[h] + jnp.sum(p, axis=0, keepdims=True)
            m_ref[h] = m_new
            probs.append(p.astype(BF16))
            alphas.append(alpha)
        outs = [jnp.dot(vt[h * HEAD_DIM:(h + 1) * HEAD_DIM, :], probs[h],
                        preferred_element_type=F32) for h in range(N_HEADS)]
        for h in range(N_HEADS):
            g, hh = divmod(h, 4)
            rows = slice(hh * HEAD_DIM, (hh + 1) * HEAD_DIM)
            acc_ref[g, rows, :] = alphas[h] * acc_ref[g, rows, :] + outs[h]
        return carry

    lax.fori_loop(0, jnp.minimum(qt + 1, 2), functools.partial(body, near=True), 0)
    lax.fori_loop(2, qt + 1, functools.partial(body, near=False), 0)

    for g in range(n_groups):
        o_t = jnp.concatenate(
            [acc_ref[g, hh * HEAD_DIM:(hh + 1) * HEAD_DIM, :] * (1.0 / l_ref[4 * g + hh])
             for hh in range(4)], axis=0)
        o_ref[:, g * gw:(g + 1) * gw] = o_t.T.astype(o_ref.dtype)


def _moba_bias_tables(rel_table):
    bucket = _rel_bucket_table(2 * MOBA_BLOCK)
    i = np.arange(MOBA_BLOCK)[None, :]
    j = np.arange(MOBA_BLOCK)[:, None]
    own = jnp.where(jnp.asarray(i >= j), _bias_lookup(rel_table, bucket[np.maximum(i - j, 0)]), NEG)
    prev = _bias_lookup(rel_table, bucket[MOBA_BLOCK + i - j])
    far = jnp.broadcast_to(rel_table[N_BUCKETS - 1][:, None, None], prev.shape)
    return jnp.stack([own, prev, far], axis=1).astype(F32)


def _moba_prompt(qkv, bias_tab, batch, seq):
    nb = seq // MOBA_BLOCK
    ksel = min(MOBA_TOPK, nb)
    stat = pltpu.VMEM((N_HEADS, 1, MOBA_BLOCK), F32)
    return pl.pallas_call(
        functools.partial(_moba_prompt_kernel, ksel=ksel),
        grid=(batch, nb),
        in_specs=[pl.BlockSpec((MOBA_BLOCK, D_HEADS), lambda b, t: (b * nb + t, 0)),
                  pl.BlockSpec((seq, D_HEADS), lambda b, t: (b, 1)),
                  pl.BlockSpec((seq, D_HEADS), lambda b, t: (b, 2)),
                  pl.BlockSpec(bias_tab.shape, lambda b, t: (0, 0, 0, 0))],
        out_specs=pl.BlockSpec((MOBA_BLOCK, D_HEADS), lambda b, t: (b * nb + t, 0)),
        out_shape=jax.ShapeDtypeStruct((batch * seq, D_HEADS), BF16),
        scratch_shapes=[pltpu.VMEM((nb, D_HEADS), F32),
                        pltpu.VMEM((nb, D_HEADS, MOBA_BLOCK), BF16),
                        pltpu.VMEM((N_HEADS, MOBA_BLOCK, 4 * HEAD_DIM), BF16),
                        pltpu.VMEM((N_HEADS, nb, MOBA_BLOCK), F32),
                        stat, stat,
                        pltpu.VMEM((N_HEADS // 4, 4 * HEAD_DIM, MOBA_BLOCK), F32)],
        compiler_params=_cparams(2),
        name="moba_prompt",
    )(qkv, qkv, qkv, bias_tab)


def _moba_sample_kernel(pt_ref, q_ref, kn_ref, vn_ref, *refs, ksel, n_q, nbp, pages_per_block):
    del pt_ref
    n_pages = nbp * pages_per_block
    k_refs, v_refs = refs[:n_pages], refs[n_pages:2 * n_pages]
    bias_ref, bown_ref, o_ref, m_ref, l_ref, g_ref, acc_ref = refs[2 * n_pages:]
    nt = (((1,), (1,)), ((), ()))
    dh = N_HEADS * HEAD_DIM
    q8 = q_ref[...]
    slots = q8.shape[0]
    lane_head = lax.broadcasted_iota(jnp.int32, (1, dh), 1) // HEAD_DIM
    qbd = jnp.concatenate([jnp.where(lane_head == hp, q8, 0.0) for hp in range(N_HEADS)], axis=0)
    qs = (qbd * (HEAD_DIM ** -0.5)).astype(BF16)
    ones = jnp.ones((8, PAGE_SIZE), BF16)

    kts = [k_ref[...].reshape(dh, PAGE_SIZE).astype(BF16) for k_ref in k_refs]
    s_pages = [jnp.dot(qs, kt, preferred_element_type=F32) for kt in kts]
    ksums = [lax.dot_general(ones, kt, nt, preferred_element_type=F32) for kt in kts]
    probs = []
    for j in range(nbp):
        pages = range(j * pages_per_block, (j + 1) * pages_per_block)
        s = jnp.concatenate([s_pages[pg] for pg in pages], axis=1) + bias_ref[0 if j == nbp - 1 else 1]
        m = jnp.max(s, axis=-1, keepdims=True)
        p32 = jnp.exp(s - m)
        probs.append(p32.astype(BF16))
        m_ref[j] = m
        l_ref[j] = jnp.sum(p32, axis=-1, keepdims=True)
        ksum = functools.reduce(jnp.add, [ksums[pg] for pg in pages])
        g_ref[j] = jnp.sum(qbd * ksum[0:1, :], axis=-1, keepdims=True) * (1.0 / MOBA_BLOCK)
    pv = [lax.dot_general(probs[pg // pages_per_block][:, (pg % pages_per_block) * PAGE_SIZE:
                                                       (pg % pages_per_block + 1) * PAGE_SIZE],
                          v_refs[pg][...].reshape(dh, PAGE_SIZE).astype(BF16), nt,
                          preferred_element_type=F32) for pg in range(n_pages)]
    for j in range(nbp):
        acc_ref[j] = functools.reduce(jnp.add, pv[j * pages_per_block:(j + 1) * pages_per_block])

    s_own = [jnp.sum(qbd * kn_ref[i:i + 1, :], axis=-1, keepdims=True) * (HEAD_DIM ** -0.5)
             + bown_ref[:, i:i + 1] for i in range(n_q)]
    m_o = functools.reduce(jnp.maximum, s_own)
    p_own = [jnp.exp(s_i - m_o) for s_i in s_own]
    l_o = functools.reduce(jnp.add, p_own)
    acc_o = functools.reduce(jnp.add, [p_i * vn_ref[i:i + 1, :] for i, p_i in enumerate(p_own)])

    gg = g_ref[...]
    blk = lax.broadcasted_iota(jnp.int32, gg.shape, 0)
    keep = blk < 0
    for _ in range(ksel):
        mx = jnp.max(gg, axis=0, keepdims=True)
        first = jnp.min(jnp.where(gg == mx, blk, nbp), axis=0, keepdims=True)
        hit = (blk == first) & (gg > -jnp.inf)
        keep = keep | hit
        gg = jnp.where(blk == first, -jnp.inf, gg)
    m_all = m_ref[...]
    m_tot = jnp.maximum(jnp.max(jnp.where(keep, m_all, NEG), axis=0), m_o)
    w = jnp.where(keep, jnp.exp(m_all - m_tot[None]), 0.0)
    w_o = jnp.exp(m_o - m_tot)
    l_tot = jnp.sum(w * l_ref[...], axis=0) + w_o * l_o
    acc_tot = (jnp.sum(w * acc_ref[...], axis=0) + w_o * acc_o) * (1.0 / l_tot)
    row_head = lax.broadcasted_iota(jnp.int32, (N_HEADS * slots, 1), 0) // slots
    acc_tot = jnp.where(row_head == lane_head, acc_tot, 0.0)
    o_ref[...] = functools.reduce(
        jnp.add, [acc_tot[:, hp * HEAD_DIM:(hp + 1) * HEAD_DIM] for hp in range(N_HEADS)])


def _moba_sample_bias(rel_table, n_q):
    slots = SAMPLE_PAD
    bucket = _rel_bucket_table(MOBA_BLOCK + slots)
    rows_h = np.repeat(np.arange(N_HEADS), slots)
    rows_i = np.minimum(np.tile(np.arange(slots), N_HEADS), n_q - 1)
    pos = np.arange(MOBA_BLOCK)[None, :]
    by_head = _bias_lookup(rel_table, bucket[MOBA_BLOCK + rows_i[:, None] - pos])
    pick = jnp.asarray(np.arange(N_HEADS)[:, None, None] == rows_h[None, :, None])
    near = jnp.sum(jnp.where(pick, by_head, 0.0), axis=0)
    far = jnp.broadcast_to(jnp.repeat(rel_table[N_BUCKETS - 1], slots)[:, None], near.shape)
    past = jnp.stack([near, far]).astype(F32)
    new = np.arange(slots)[None, :]
    own_h = _bias_lookup(rel_table, bucket[np.maximum(rows_i[:, None] - new, 0)])
    own = jnp.sum(jnp.where(pick, own_h, 0.0), axis=0)
    ok = jnp.asarray((new <= rows_i[:, None]) & (new < n_q))
    return past, jnp.where(ok, own, NEG).astype(F32)


def _moba_sample(q_rows, k_new, v_new, cache_k, cache_v, page_table, layer, bias_past, bias_own, n_q):
    bsz, slots, dh = q_rows.shape
    hd = HEAD_DIM
    rows = N_HEADS * slots
    pages_per_block = MOBA_BLOCK // PAGE_SIZE
    nbp = page_table.shape[1] // pages_per_block
    ksel = min(MOBA_TOPK, nbp + 1)

    n_pages = nbp * pages_per_block

    def page_spec(page):
        return pl.BlockSpec((None, None, N_HEADS, hd, PAGE_SIZE),
                            lambda b, pt: (layer, pt[b, page], 0, 0, 0))

    per_seq = lambda b, pt: (b, 0, 0)
    stat = pltpu.VMEM((nbp, rows, 1), F32)
    grid_spec = pltpu.PrefetchScalarGridSpec(
        num_scalar_prefetch=1,
        grid=(bsz,),
        in_specs=([pl.BlockSpec((None, slots, dh), per_seq)] * 3
                  + [page_spec(page) for page in range(n_pages)] * 2
                  + [pl.BlockSpec(bias_past.shape, lambda b, pt: (0, 0, 0)),
                     pl.BlockSpec(bias_own.shape, lambda b, pt: (0, 0))]),
        out_specs=pl.BlockSpec((None, rows, hd), per_seq),
        scratch_shapes=[stat, stat, stat, pltpu.VMEM((nbp, rows, dh), F32)])
    return pl.pallas_call(
        functools.partial(_moba_sample_kernel, ksel=ksel, n_q=n_q, nbp=nbp,
                          pages_per_block=pages_per_block),
        grid_spec=grid_spec,
        out_shape=jax.ShapeDtypeStruct((bsz, rows, hd), F32),
        compiler_params=_cparams(1),
        name="moba_sample",
    )(page_table, q_rows, k_new, v_new, *([cache_k] * n_pages), *([cache_v] * n_pages),
      bias_past, bias_own)


def _softplus(x):
    return jnp.maximum(x, 0.0) + jnp.log1p(jnp.exp(-jnp.abs(x)))


def _bmm(a, b):
    return jnp.matmul(a.astype(BF16), b.astype(BF16), preferred_element_type=F32)


def _dot_f32(a, b):
    return jnp.dot(a, b, precision=HIGHEST, preferred_element_type=F32)


def _split_bf16(a):
    hi = a.astype(BF16)
    return hi, (a - hi.astype(F32)).astype(BF16)


def _bmm_3pass(a, b):
    a_hi, a_lo = _split_bf16(a)
    b_hi, b_lo = _split_bf16(b)
    return (jnp.matmul(a_hi, b_hi, preferred_element_type=F32)
            + jnp.matmul(a_hi, b_lo, preferred_element_type=F32)
            + jnp.matmul(a_lo, b_hi, preferred_element_type=F32))


def _unit_lower_inverse(a, row, col):
    c = a.shape[1]
    eye = jnp.where(row == col, 1.0, 0.0)
    in16 = (row // 16) == (col // 16)
    in32 = (row // 32) == (col // 32)
    nil = jnp.where(in16, -a, 0.0)
    x = eye + nil
    p = _bmm_3pass(nil, nil)
    for _ in range(2):
        xp = _bmm_3pass(jnp.concatenate([x, p], axis=1), p)
        x = x + xp[:, :c]
        p = xp[:, c:]
    x = x + _bmm_3pass(x, p)
    for off in (jnp.where(in32 & jnp.logical_not(in16), a, 0.0),
                jnp.where(in32, 0.0, a)):
        x = x - _bmm(x, _bmm(off, x))
    return x


def _gdn_prepare(q, k, v, k_t, g_col, g_row, beta, row, col):
    c = q.shape[1]
    incl = row >= col
    decay = jnp.where(incl, jnp.exp(jnp.where(incl, g_col - g_row, 0.0)), 0.0)
    qk_kk = _bmm(jnp.concatenate([q, k], axis=1), k_t)
    a_mat = jnp.where(row > col, qk_kk[:, c:] * decay * beta, 0.0)
    e_g = jnp.exp(g_col)
    rhs = jnp.concatenate([v * beta, k * beta * e_g], axis=2)
    sol = _bmm(_unit_lower_inverse(a_mat, row, col), rhs)
    dv = v.shape[2]
    g_last = g_col[:, c - 1:c, :]
    w_qd = jnp.concatenate([sol[:, :, dv:], q * e_g], axis=1).astype(BF16)
    qk = jnp.where(incl, qk_kk[:, :c] * decay, 0.0).astype(BF16)
    k_dec_t = (k_t * jnp.exp(g_last - g_row)).astype(BF16)
    return sol[:, :, :dv], w_qd, qk, k_dec_t, jnp.exp(g_last)


def _gdn_step(u, w_qd, qk, k_dec_t, chunk_decay, state):
    c = u.shape[1]
    ws_qs = jnp.matmul(w_qd, state.astype(BF16), preferred_element_type=F32)
    v_new = (u - ws_qs[:, :c]).astype(BF16)
    o = ws_qs[:, c:] + jnp.matmul(qk, v_new, preferred_element_type=F32)
    new_state = state * chunk_decay + jnp.matmul(k_dec_t, v_new, preferred_element_type=F32)
    return o, new_state


def _gdn_prompt_kernel(x_ref, z_ref, gab_ref, ak_ref, av_ref, cw_ref, alog_ref, dtb_ref, nw_ref, seg_ref,
                       o_ref, conv_ref, sout_ref, kt_ref, vt_ref,
                       xbuf_ref, halo_ref, s_ref, u_ref, wqd_ref, qk_ref, kdt_ref, cd_ref, oh_ref):
    t = pl.program_id(1)
    tl = x_ref.shape[0]
    cs = GDN_CHUNK
    n_chunks = tl // cs
    hd = HEAD_DIM
    kt_ref[...] = ak_ref[...].T.reshape(kt_ref.shape)
    vt_ref[...] = av_ref[...].T.reshape(vt_ref.shape)

    @pl.when(t == 0)
    def _():
        halo_ref[...] = jnp.zeros_like(halo_ref)
        s_ref[...] = jnp.zeros_like(s_ref)

    x = x_ref[...]
    xbuf_ref[0:8, :] = halo_ref[...]
    xbuf_ref[8:8 + tl, :] = x
    y = cw_ref[0:1, :] * xbuf_ref[5:5 + tl, :]
    for i in range(1, GDN_CONV):
        y = y + cw_ref[i:i + 1, :] * xbuf_ref[5 + i:5 + i + tl, :]
    y = _silu(y)
    tail = x[tl - 8:, :]
    halo_ref[...] = tail
    conv_ref[...] = tail[8 - (GDN_CONV - 1):, :]

    dh = N_HEADS * hd
    seg = seg_ref[...]

    def l2n(a):
        sq_hi, sq_lo = _split_bf16(a * a)
        ss = (jnp.dot(sq_hi, seg, preferred_element_type=F32)
              + jnp.dot(sq_lo, seg, preferred_element_type=F32))
        return a * lax.rsqrt(ss + 1e-6)

    qn = l2n(y[:, :dh]) * (hd ** -0.5)
    kn = l2n(y[:, dh:2 * dh])
    vv = y[:, 2 * dh:]
    kn_t = kn.T

    gab = gab_ref[...]
    beta = 1.0 / (1.0 + jnp.exp(-gab))
    glog = -jnp.exp(alog_ref[...]) * _softplus(gab + dtb_ref[...])

    row = lax.broadcasted_iota(jnp.int32, (cs, cs), 0)
    col = lax.broadcasted_iota(jnp.int32, (cs, cs), 1)
    tril = jnp.where(row >= col, 1.0, 0.0)
    pick = jnp.where(lax.broadcasted_iota(jnp.int32, (N_HEADS, LANES), 1)
                     == lax.broadcasted_iota(jnp.int32, (N_HEADS, LANES), 0) + N_HEADS, 1.0, 0.0)
    parts = {name: [] for name in ("q", "k", "v", "kt", "gc", "gr", "bb")}
    for c in range(n_chunks):
        rows = slice(c * cs, (c + 1) * cs)
        gcum = _dot_f32(tril, glog[rows])
        g_rows = lax.dot_general(pick, gcum, (((1,), (1,)), ((), ())), precision=HIGHEST,
                                 preferred_element_type=F32)
        for h in range(N_HEADS):
            lanes = slice(h * hd, (h + 1) * hd)
            parts["q"].append(qn[rows, lanes])
            parts["k"].append(kn[rows, lanes])
            parts["v"].append(vv[rows, lanes])
            parts["kt"].append(kn_t[lanes, rows])
            parts["gc"].append(jnp.broadcast_to(gcum[:, N_HEADS + h:N_HEADS + h + 1], (cs, cs)))
            parts["gr"].append(g_rows[h:h + 1, :])
            parts["bb"].append(jnp.broadcast_to(beta[rows, h:h + 1], (cs, cs)))
    st = {name: jnp.stack(vals) for name, vals in parts.items()}
    u_all, w_qd, qk, k_dec_t, chunk_decay = _gdn_prepare(
        st["q"], st["k"], st["v"], st["kt"], st["gc"], st["gr"], st["bb"], row, col)
    u_ref[...] = u_all.reshape(u_ref.shape)
    wqd_ref[...] = w_qd.reshape(wqd_ref.shape)
    qk_ref[...] = qk.reshape(qk_ref.shape)
    kdt_ref[...] = k_dec_t.reshape(kdt_ref.shape)
    cd_ref[...] = chunk_decay.reshape(cd_ref.shape)

    def chunk_body(c, carry):
        o, s_new = _gdn_step(u_ref[c], wqd_ref[c], qk_ref[c], kdt_ref[c], cd_ref[c], s_ref[...])
        oh_ref[c] = o
        s_ref[...] = s_new
        return carry

    lax.fori_loop(0, n_chunks, chunk_body, 0)

    o_all = jnp.concatenate(
        [jnp.concatenate([oh_ref[c, h] for c in range(n_chunks)], axis=0) for h in range(N_HEADS)],
        axis=1)
    sq_hi, sq_lo = _split_bf16(o_all * o_all)
    ss = (jnp.dot(sq_hi, seg, preferred_element_type=F32)
          + jnp.dot(sq_lo, seg, preferred_element_type=F32))
    o_all = o_all * lax.rsqrt(ss * (1.0 / hd) + 1e-6) * nw_ref[...]
    o_ref[...] = (o_all * _silu(z_ref[...])).astype(o_ref.dtype)

    @pl.when(t == pl.num_programs(1) - 1)
    def _():
        sout_ref[...] = s_ref[...]


def _gdn_prompt(proj, batch, seq, tl, conv_w, a_log, dt_bias, norm_w):
    hd = HEAD_DIM
    cs = GDN_CHUNK
    n_steps = seq // tl
    n_chunks = tl // cs
    col_x, col_z, col_gab = 1, D_IN_MAIN // D_HEADS - 1, D_IN_MAIN // LANES
    lane = np.arange(D_HEADS)
    seg = jnp.asarray((lane[:, None] // hd == lane[None, :] // hd).astype(np.float32)).astype(BF16)
    alog_row = jnp.zeros((1, LANES), F32).at[0, N_HEADS:2 * N_HEADS].set(a_log)
    dtb_row = jnp.zeros((1, LANES), F32).at[0, N_HEADS:2 * N_HEADS].set(dt_bias)
    const2 = lambda b, t: (0, 0)
    per_inst = pltpu.VMEM((n_chunks, N_HEADS, cs, hd), F32)
    per_inst_bf16 = pltpu.VMEM((n_chunks, N_HEADS, hd, cs), BF16)
    return pl.pallas_call(
        _gdn_prompt_kernel,
        grid=(batch, n_steps),
        in_specs=[pl.BlockSpec((tl, D_CONV), lambda b, t: (b * n_steps + t, col_x)),
                  pl.BlockSpec((tl, D_HEADS), lambda b, t: (b * n_steps + t, col_z)),
                  pl.BlockSpec((tl, LANES), lambda b, t: (b * n_steps + t, col_gab)),
                  pl.BlockSpec((tl, D_HEADS), lambda b, t: (b * n_steps + t, 1)),
                  pl.BlockSpec((tl, D_HEADS), lambda b, t: (b * n_steps + t, 2)),
                  pl.BlockSpec((GDN_CONV, D_CONV), const2),
                  pl.BlockSpec((1, LANES), const2),
                  pl.BlockSpec((1, LANES), const2),
                  pl.BlockSpec((1, D_HEADS), const2),
                  pl.BlockSpec((D_HEADS, D_HEADS), const2)],
        out_specs=[pl.BlockSpec((tl, D_HEADS), lambda b, t: (b * n_steps + t, 0)),
                   pl.BlockSpec((None, GDN_CONV - 1, D_CONV), lambda b, t: (b, 0, 0)),
                   pl.BlockSpec((None, N_HEADS, hd, hd), lambda b, t: (b, 0, 0, 0)),
                   pl.BlockSpec((None, N_HEADS, hd, tl), lambda b, t: (b, 0, 0, t)),
                   pl.BlockSpec((None, N_HEADS, hd, tl), lambda b, t: (b, 0, 0, t))],
        out_shape=[jax.ShapeDtypeStruct((batch * seq, D_HEADS), BF16),
                   jax.ShapeDtypeStruct((batch, GDN_CONV - 1, D_CONV), F32),
                   jax.ShapeDtypeStruct((batch, N_HEADS, hd, hd), F32),
                   jax.ShapeDtypeStruct((batch, N_HEADS, hd, seq), F32),
                   jax.ShapeDtypeStruct((batch, N_HEADS, hd, seq), F32)],
        scratch_shapes=[pltpu.VMEM((8 + tl + 8, D_CONV), F32),
                        pltpu.VMEM((8, D_CONV), F32),
                        pltpu.VMEM((N_HEADS, hd, hd), F32),
                        per_inst,
                        pltpu.VMEM((n_chunks, N_HEADS, 2 * cs, hd), BF16),
                        per_inst_bf16, per_inst_bf16,
                        pltpu.VMEM((n_chunks, N_HEADS, 1, hd), F32),
                        per_inst],
        compiler_params=_cparams(2),
        name="gdn_prompt",
    )(proj, proj, proj, proj, proj, conv_w, alog_row, dtb_row,
      jnp.tile(norm_w, N_HEADS).reshape(1, D_HEADS), seg)


def _gdn_sample_kernel(xq_ref, xk_ref, xv_ref, cq_ref, ck_ref, cv_ref, wq_ref, wk_ref, wv_ref,
                       z_ref, gab_ref, alog_ref, dtb_ref, nw_ref, s_ref,
                       o_ref, sout_ref, kq_ref, gt_ref, ot_ref):
    pair = pl.program_id(0)
    n_q = xq_ref.shape[0]
    hd = HEAD_DIM

    def conv_t(x_ref, c_ref, w_ref):
        xp = [c_ref[i] for i in range(GDN_CONV - 1)] + [x_ref[i] for i in range(n_q)]
        out = []
        for i in range(n_q):
            y = w_ref[0:1, :] * xp[i]
            for m in range(1, GDN_CONV):
                y = y + w_ref[m:m + 1, :] * xp[i + m]
            out.append(_silu(y).T)
        return out

    q_t = conv_t(xq_ref, cq_ref, wq_ref)
    k_t = conv_t(xk_ref, ck_ref, wk_ref)
    v_t = conv_t(xv_ref, cv_ref, wv_ref)

    def l2n(a):
        return a * lax.rsqrt(jnp.sum(a * a, axis=0, keepdims=True) + 1e-6)

    for i in range(n_q):
        gab_t = gab_ref[i].T
        gt_ref[0, i] = 1.0 / (1.0 + jnp.exp(-gab_t))
        gt_ref[1, i] = jnp.exp(-jnp.exp(alog_ref[...]) * _softplus(gab_t + dtb_ref[...]))

    for hh in range(2):
        rows = slice(hh * hd, (hh + 1) * hd)
        head = 2 * pair + hh
        for i in range(n_q):
            kq_ref[0] = l2n(k_t[i][rows])
            kq_ref[1] = l2n(q_t[i][rows]) * (hd ** -0.5)
            beta = gt_ref[0, i, pl.ds(head, 1), :]
            decay = gt_ref[1, i, pl.ds(N_HEADS + head, 1), :]
            src = s_ref if i == 0 else sout_ref

            def ks_body(kk, acc):
                return acc + kq_ref[0, pl.ds(kk, 1), :] * src[hh, kk]

            k_s = lax.fori_loop(0, hd, ks_body, jnp.zeros((hd, k_t[i].shape[1]), F32), unroll=8)
            r = beta * (v_t[i][rows] - decay * k_s)

            def upd_body(kk, acc):
                s_new = decay * src[hh, kk] + kq_ref[0, pl.ds(kk, 1), :] * r
                sout_ref[hh, kk] = s_new
                return acc + kq_ref[1, pl.ds(kk, 1), :] * s_new

            o = lax.fori_loop(0, hd, upd_body, jnp.zeros_like(r), unroll=8)
            ot_ref[i, rows, :] = o * lax.rsqrt(jnp.mean(o * o, axis=0, keepdims=True) + 1e-6) * nw_ref[...]

    for i in range(n_q):
        o_ref[i] = (ot_ref[i].T * _silu(z_ref[i])).astype(o_ref.dtype)


def _gdn_sample(proj_s, conv_state, state_t, conv_w, a_log, dt_bias, norm_w):
    n_q, bsz, _ = proj_s.shape
    hd = HEAD_DIM
    pw = 2 * hd
    n_pairs = N_HEADS // 2
    base = D_CONV // pw
    col = jnp.zeros((LANES, 1), F32)
    alog_col = col.at[N_HEADS:2 * N_HEADS, 0].set(a_log)
    dtb_col = col.at[N_HEADS:2 * N_HEADS, 0].set(dt_bias)
    nw_col = norm_w.reshape(hd, 1)

    def xspec(part):
        return pl.BlockSpec((n_q, bsz, pw), lambda p: (0, 0, base + part * n_pairs + p))

    def cspec(rows, part):
        return pl.BlockSpec((rows, bsz, pw) if rows else (GDN_CONV, pw),
                            (lambda p: (0, 0, part * n_pairs + p)) if rows
                            else (lambda p: (0, part * n_pairs + p)))

    const = lambda p: (0, 0)
    state_spec = pl.BlockSpec((2, hd, hd, bsz), lambda p: (p, 0, 0, 0))
    return pl.pallas_call(
        _gdn_sample_kernel,
        grid=(n_pairs,),
        in_specs=[xspec(0), xspec(1), xspec(2),
                  cspec(GDN_CONV - 1, 0), cspec(GDN_CONV - 1, 1), cspec(GDN_CONV - 1, 2),
                  cspec(0, 0), cspec(0, 1), cspec(0, 2),
                  pl.BlockSpec((n_q, bsz, pw), lambda p: (0, 0, D_IN_MAIN // pw - n_pairs + p)),
                  pl.BlockSpec((n_q, bsz, LANES), lambda p: (0, 0, D_IN_MAIN // LANES)),
                  pl.BlockSpec((LANES, 1), const), pl.BlockSpec((LANES, 1), const),
                  pl.BlockSpec((hd, 1), const),
                  state_spec],
        out_specs=[pl.BlockSpec((n_q, bsz, pw), lambda p: (0, 0, p)), state_spec],
        out_shape=[jax.ShapeDtypeStruct((n_q, bsz, D_HEADS), BF16),
                   jax.ShapeDtypeStruct(state_t.shape, F32)],
        scratch_shapes=[pltpu.VMEM((2, hd, bsz), F32),
                        pltpu.VMEM((2, n_q, LANES, bsz), F32),
                        pltpu.VMEM((n_q, pw, bsz), F32)],
        compiler_params=_cparams(1),
        name="gdn_sample",
    )(proj_s, proj_s, proj_s, conv_state, conv_state, conv_state, conv_w, conv_w, conv_w,
      proj_s, proj_s, alog_col, dtb_col, nw_col, state_t)


TM = 512
TM_MOE = 1024
TJ_MOE = 512
GDN_ROWS = 256
SAMPLE_PAD = 8


def kernel(x_prompt, x_sample, cache_k, cache_v, state_conv, state_gdn, page_table, c_prompt, c_sample,
           ln_in_g, ln_in_b, w_mod, b_mod, w_in, conv_w, a_log, dt_bias, gdn_norm_w, w_out, rel_table,
           ln_g, ln_b, ffn_w_gate, ffn_w_up, ffn_w_down, moe_router, moe_w_gate, moe_w_up, moe_w_down):
    bp, seq, d = x_prompt.shape
    bs, n_q, _ = x_sample.shape
    depth = w_in.shape[0]
    tp, ts = bp * seq, bs * n_q
    h, hd = N_HEADS, HEAD_DIM
    assert bs == MOD_ROWS and seq % TM == 0 and ts % TM == 0 and seq % MOBA_BLOCK == 0
    assert depth == 2 and GDN_CONV - 1 <= n_q <= SAMPLE_PAD and GDN_CHUNK == HEAD_DIM
    alpha = (2 * depth) ** 0.25

    def groups(tm):
        return lambda i: jnp.minimum(i // (seq // tm), bp)

    n_c = bp + bs
    c_all = jnp.pad(jnp.concatenate([c_prompt, c_sample]), ((0, (-n_c) % 8), (0, 0)))
    mod = _mod_vectors(c_all, w_mod, b_mod).reshape(depth, -1, 6, d)
    mod_p = jnp.broadcast_to(mod[:, :bp].transpose(0, 2, 1, 3)[:, :, :, None, :],
                             (depth, 6, bp, MOD_ROWS, d))
    mod_s = mod[:, bp:n_c].transpose(0, 2, 1, 3)[:, :, None]
    mod_all = jnp.concatenate([mod_p, mod_s], axis=2)

    x, u = _ln_mod(x_prompt.reshape(tp, d), x_sample.transpose(1, 0, 2).reshape(ts, d),
                   ln_in_g, ln_in_b, mod_all, 0, TM, groups(TM))

    bias_tab = _moba_bias_tables(rel_table)
    assert page_table.shape[1] * PAGE_SIZE % MOBA_BLOCK == 0
    bias_past, bias_own = _moba_sample_bias(rel_table, n_q)
    cache_kt = cache_k.transpose(0, 1, 3, 4, 2)
    cache_vt = cache_v.transpose(0, 1, 3, 4, 2)
    state_t = state_gdn.transpose(0, 2, 3, 4, 1)
    conv_t = state_conv.transpose(0, 2, 1, 3)

    k_p, v_p, conv_p, gdn_p, k_s, v_s, conv_s, gdn_s = [], [], [], [], [], [], [], []
    for layer in range(depth):
        w_l = w_in[layer]
        w_cat = jnp.concatenate(
            [w_l[:, :D_IN_MAIN], jnp.pad(w_l[:, D_IN_MAIN:], ((0, 0), (0, LANES - 2 * h)))],
            axis=1).astype(BF16)
        proj, qkv = _proj_in(u, w_cat, TM)
        proj_s = proj[tp:].reshape(n_q, bs, -1)

        def slots_s(cols):
            return jnp.pad(proj_s[:, :, cols].transpose(1, 0, 2), ((0, 0), (0, SAMPLE_PAD - n_q), (0, 0)))

        k_sr = proj_s[:, :, D_HEADS:2 * D_HEADS].transpose(1, 0, 2).reshape(bs, n_q, h, hd)
        v_sr = proj_s[:, :, 2 * D_HEADS:3 * D_HEADS].transpose(1, 0, 2).reshape(bs, n_q, h, hd)
        att_p = _moba_prompt(qkv, bias_tab, bp, seq)
        att_s = _moba_sample(slots_s(slice(0, D_HEADS)), slots_s(slice(D_HEADS, 2 * D_HEADS)),
                             slots_s(slice(2 * D_HEADS, 3 * D_HEADS)), cache_kt, cache_vt, page_table,
                             layer, bias_past, bias_own, n_q)
        att_s = att_s.reshape(bs, h, SAMPLE_PAD, hd)[:, :, :n_q].transpose(2, 0, 1, 3)
        att = (att_p, att_s.reshape(ts, D_HEADS).astype(BF16))

        o_p, conv_new_p, s_new_p, k_t, v_t = _gdn_prompt(proj, bp, seq, GDN_ROWS, conv_w[layer],
                                                         a_log[layer], dt_bias[layer], gdn_norm_w[layer])
        o_s, s_new_s = _gdn_sample(proj_s, conv_t[layer], state_t[layer], conv_w[layer], a_log[layer],
                                   dt_bias[layer], gdn_norm_w[layer])
        s_new_s = s_new_s.transpose(3, 0, 1, 2)
        conv_new_s = proj_s[n_q - (GDN_CONV - 1):, :, 3 * D_HEADS:3 * D_HEADS + D_CONV].transpose(1, 0, 2)
        gdn_o = (o_p, o_s.reshape(ts, D_HEADS))

        w_o = w_out[layer].astype(BF16)
        i = layer // 2
        if layer % 2 == 0:
            x, u = _res_ln([att, gdn_o], [w_o[:D_HEADS], w_o[D_HEADS:]], x, mod_all, layer, 2, layer,
                           (4, 3), ln_g[layer, 0], ln_b[layer, 0], alpha, TM, groups(TM))
            hdn = _ffn_up(u, ffn_w_gate[i].astype(BF16), ffn_w_up[i].astype(BF16), TM)
            x, u = _res_ln([hdn], [ffn_w_down[i].astype(BF16)], x, mod_all, layer, 5, layer + 1,
                           (1, 0), ln_g[layer, 1], ln_b[layer, 1], alpha, TM, groups(TM))
        else:
            w_r = jnp.pad(moe_router[i], ((0, 0), (0, LANES - N_EXPERTS)))
            x, u, route = _res_ln([att, gdn_o], [w_o[:D_HEADS], w_o[D_HEADS:]], x, mod_all, layer, 2,
                                  layer, (4, 3), ln_g[layer, 0], ln_b[layer, 0], alpha, TM,
                                  groups(TM), w_router=w_r)
            tile_expert, n_valid, token_of_slot, slot = _moe_routing(route, TM_MOE)
            y = _moe_ffn(u, tile_expert, n_valid, token_of_slot, moe_w_gate[i], moe_w_up[i],
                         moe_w_down[i], TM_MOE, TJ_MOE)
            n_t = (tp + ts) // TM
            slots = slot.reshape(n_t, TM, TOP_K).transpose(0, 2, 1).reshape(n_t, 1, TOP_K * TM)
            x_p, x_s = _moe_combine(y, slots, route, x, mod_all, layer, 5, ln_g[layer, 1],
                                    ln_b[layer, 1], alpha, TM, groups(TM), tp // TM)

        k_p.append(k_t)
        v_p.append(v_t)
        conv_p.append(conv_new_p)
        gdn_p.append(s_new_p)
        k_s.append(k_sr)
        v_s.append(v_sr)
        conv_s.append(conv_new_s)
        gdn_s.append(s_new_s)

    y_prompt = x_p.reshape(bp, seq, d)
    y_sample = x_s.reshape(n_q, bs, d).transpose(1, 0, 2)
    k_prompt = jnp.stack(k_p).transpose(0, 1, 4, 2, 3)
    v_prompt = jnp.stack(v_p).transpose(0, 1, 4, 2, 3)
    return (y_prompt, y_sample, k_prompt, v_prompt, jnp.stack(conv_p), jnp.stack(gdn_p),
            jnp.stack(k_s), jnp.stack(v_s), jnp.stack(conv_s), jnp.stack(gdn_s))
```
